```python
import jax, jax.numpy as jnp
from jax import lax
import numpy as np

D_MODEL = 1024
BATCH = 8
SEQ = 16384
DEPTH = 4

HEAD_DIM = 128
DILATED_GROUPS = ((128, 1), (512, 4), (2048, 16))
HEADS_PER_GROUP = 4
N_ATTN_HEADS = HEADS_PER_GROUP * len(DILATED_GROUPS)
ATTN_WIDTH = N_ATTN_HEADS * HEAD_DIM
ATTN_OUT = HEADS_PER_GROUP * HEAD_DIM
BLOCK = 128
ROPE_THETA = 500000.0
ROT_DIM = HEAD_DIM // 4
D_CONV = D_MODEL
CONV_K = 31
D_FF = 256 * ((8 * D_MODEL // 3 + 255) // 256)
FFN_K = 3
IN_WIDTH = 3 * ATTN_WIDTH + 2 * D_CONV + 2 * D_MODEL
EPS = 1e-6

kernel_name = "hybrid_dilated_attn_conformer_convffn_adaln"


def rms_norm(x, g):
    x32 = x.astype(jnp.float32)
    y = x32 * lax.rsqrt(jnp.mean(x32 * x32, axis=-1, keepdims=True) + EPS)
    return (y * g.astype(jnp.float32)).astype(x.dtype)


def layer_norm(x, g, b):
    x32 = x.astype(jnp.float32)
    mu = jnp.mean(x32, axis=-1, keepdims=True)
    xc = x32 - mu
    y = xc * lax.rsqrt(jnp.mean(xc * xc, axis=-1, keepdims=True) + EPS)
    return (y * g.astype(jnp.float32) + b.astype(jnp.float32)).astype(x.dtype)


def causal_dwconv(x, w, b):
    K, C = w.shape
    y = lax.conv_general_dilated(x, w[:, None, :].astype(x.dtype), window_strides=(1,),
                                 padding=[(K - 1, 0)], dimension_numbers=("NWC", "WIO", "NWC"),
                                 feature_group_count=C)
    return y + b.astype(x.dtype)


def partial_rope(t, cos, sin):
    half = ROT_DIM // 2
    t1, t2 = t[..., :half], t[..., half:ROT_DIM]
    return jnp.concatenate([t1 * cos - t2 * sin, t2 * cos + t1 * sin, t[..., ROT_DIM:]], axis=-1)


def dilated_band_attention(q, k, v, window, dilation):
    B, S, H, Dh = q.shape
    band = window // dilation
    sub_len = -(-S // (dilation * BLOCK)) * BLOCK
    pad = sub_len * dilation - S
    nb = sub_len // BLOCK

    def to_sub(t):
        t = jnp.pad(t, ((0, 0), (0, pad), (0, 0), (0, 0)))
        t = t.reshape(B, sub_len, dilation, H, Dh).transpose(0, 2, 3, 1, 4)
        return t.reshape(B, dilation, H, nb, BLOCK, Dh)

    def with_prev(t):
        prev = jnp.pad(t[:, :, :, :-1], ((0, 0), (0, 0), (0, 0), (1, 0), (0, 0), (0, 0)))
        return jnp.concatenate([prev, t], axis=4)

    qb = to_sub(q)
    kw = with_prev(to_sub(k))
    vw = with_prev(to_sub(v))
    s = jnp.einsum('brhnqd,brhnkd->brhnqk', qb, kw) * (Dh ** -0.5)
    qi = jnp.arange(BLOCK)[:, None]
    kj = jnp.arange(2 * BLOCK)[None, :]
    dist = qi + BLOCK - kj
    valid = (dist >= 0) & (dist <= band)
    not_first = jnp.arange(nb)[:, None, None] > 0
    valid = valid[None] & (not_first | (kj >= BLOCK)[None])
    s = jnp.where(valid, s, -jnp.inf)
    m = jnp.max(s, axis=-1, keepdims=True)
    p = jnp.exp(s - m)
    l = jnp.sum(p, axis=-1, keepdims=True)
    o = jnp.einsum('brhnqk,brhnkd->brhnqd', p, vw) / l
    lse = (m + jnp.log(l))[..., 0]
    o = o.reshape(B, dilation, H, sub_len, Dh).transpose(0, 3, 1, 2, 4)
    o = o.reshape(B, sub_len * dilation, H, Dh)[:, :S]
    lse = lse.reshape(B, dilation, H, sub_len).transpose(0, 3, 1, 2)
    lse = lse.reshape(B, sub_len * dilation, H)[:, :S]
    return o, lse


def _fwd_setup_inputs(seed: int = 0) -> dict:
    key = jax.random.key(seed)
    ks = jax.random.split(key, 24)
    f32 = jnp.float32

    def nrm(k, shape, fan_in, s=1.0):
        return jax.random.normal(k, shape, f32) * (s * fan_in ** -0.5)

    def gain(k, shape):
        return 1.0 + 0.05 * jax.random.normal(k, shape, f32)

    def bias(k, shape):
        return 0.02 * jax.random.normal(k, shape, f32)

    return {
        "x": jax.random.normal(ks[0], (BATCH, SEQ, D_MODEL), f32),
        "c": jax.random.normal(ks[1], (BATCH, D_MODEL), f32),
        "positions": jnp.broadcast_to(jnp.arange(SEQ, dtype=jnp.int32), (BATCH, SEQ)),
        "w_ada": nrm(ks[2], (DEPTH, D_MODEL, 6 * D_MODEL), D_MODEL, 0.5),
        "b_ada": bias(ks[3], (DEPTH, 6 * D_MODEL)),
        "g_norm1": gain(ks[4], (DEPTH, D_MODEL)),
        "w_in": nrm(ks[5], (DEPTH, D_MODEL, IN_WIDTH), D_MODEL),
        "g_q": gain(ks[6], (DEPTH, HEAD_DIM)),
        "g_k": gain(ks[7], (DEPTH, HEAD_DIM)),
        "w_attn_proj": nrm(ks[8], (DEPTH, ATTN_OUT, D_MODEL), ATTN_OUT),
        "w_conv_dw": nrm(ks[9], (DEPTH, CONV_K, D_CONV), CONV_K),
        "b_conv_dw": bias(ks[10], (DEPTH, D_CONV)),
        "g_conv_ln": gain(ks[11], (DEPTH, D_CONV)),
        "b_conv_ln": bias(ks[12], (DEPTH, D_CONV)),
        "w_conv_out": nrm(ks[13], (DEPTH, D_CONV, D_MODEL), D_CONV),
        "w_o": nrm(ks[14], (DEPTH, D_MODEL, D_MODEL), D_MODEL),
        "g_norm2": gain(ks[15], (DEPTH, D_MODEL)),
        "w_ffn_in": nrm(ks[16], (DEPTH, D_MODEL, 2 * D_FF), D_MODEL),
        "w_ffn_dw": nrm(ks[17], (DEPTH, FFN_K, D_FF), FFN_K),
        "b_ffn_dw": bias(ks[18], (DEPTH, D_FF)),
        "w_ffn_down": nrm(ks[19], (DEPTH, D_FF, D_MODEL), D_FF),
    }


def _fwd_reference(x, c, positions, w_ada, b_ada, g_norm1, w_in, g_q, g_k, w_attn_proj,
              w_conv_dw, b_conv_dw, g_conv_ln, b_conv_ln, w_conv_out, w_o, g_norm2,
              w_ffn_in, w_ffn_dw, b_ffn_dw, w_ffn_down):
    f32 = jnp.float32
    B, S, _ = x.shape
    inv_freq = ROPE_THETA ** (-jnp.arange(0, ROT_DIM, 2, dtype=f32) / ROT_DIM)
    ang = positions.astype(f32)[..., None] * inv_freq
    cos = jnp.cos(ang)[:, :, None, :]
    sin = jnp.sin(ang)[:, :, None, :]
    c_act = jax.nn.silu(c)
    split_at = [ATTN_WIDTH, 2 * ATTN_WIDTH, 3 * ATTN_WIDTH,
                3 * ATTN_WIDTH + D_CONV, 3 * ATTN_WIDTH + 2 * D_CONV,
                3 * ATTN_WIDTH + 2 * D_CONV + D_MODEL]

    for l in range(DEPTH):
        mod = (c_act @ w_ada[l] + b_ada[l])[:, None, :]
        sh1, sc1, gt1, sh2, sc2, gt2 = jnp.split(mod, 6, axis=-1)

        h = rms_norm(x, g_norm1[l]) * (1.0 + sc1) + sh1
        z = h @ w_in[l]
        q, k, v, c_val, c_gate, gate_a, gate_b = jnp.split(z, split_at, axis=-1)

        q = partial_rope(rms_norm(q.astype(f32).reshape(B, S, N_ATTN_HEADS, HEAD_DIM), g_q[l]), cos, sin)
        k = partial_rope(rms_norm(k.astype(f32).reshape(B, S, N_ATTN_HEADS, HEAD_DIM), g_k[l]), cos, sin)
        v = v.astype(f32).reshape(B, S, N_ATTN_HEADS, HEAD_DIM)
        outs, lses = [], []
        for gi, (win, dil) in enumerate(DILATED_GROUPS):
            hs = slice(gi * HEADS_PER_GROUP, (gi + 1) * HEADS_PER_GROUP)
            o_g, lse_g = dilated_band_attention(q[:, :, hs], k[:, :, hs], v[:, :, hs], win, dil)
            outs.append(o_g)
            lses.append(lse_g)
        wts = jax.nn.softmax(jnp.stack(lses, axis=0), axis=0)
        attn = jnp.sum(wts[..., None] * jnp.stack(outs, axis=0), axis=0)
        y_a = attn.reshape(B, S, ATTN_OUT).astype(x.dtype) @ w_attn_proj[l]

        u = c_val * jax.nn.sigmoid(c_gate)
        u = causal_dwconv(u, w_conv_dw[l], b_conv_dw[l])
        u = jax.nn.silu(layer_norm(u, g_conv_ln[l], b_conv_ln[l]))
        y_b = u @ w_conv_out[l]

        merged = jax.nn.sigmoid(gate_a) * y_a + jax.nn.sigmoid(gate_b) * y_b
        x = x + gt1 * (merged @ w_o[l])

        h2 = rms_norm(x, g_norm2[l]) * (1.0 + sc2) + sh2
        gu = h2 @ w_ffn_in[l]
        g_path, u_path = jnp.split(gu, 2, axis=-1)
        g_path = causal_dwconv(g_path, w_ffn_dw[l], b_ffn_dw[l])
        x = x + gt2 * ((jax.nn.silu(g_path) * u_path) @ w_ffn_down[l])

    return x


import jax as _jax
import jax.numpy as _jnp

TWIN_FORMAT = 'train_step'
FWD_PARAMS = ['x', 'c', 'positions', 'w_ada', 'b_ada', 'g_norm1', 'w_in', 'g_q', 'g_k', 'w_attn_proj', 'w_conv_dw', 'b_conv_dw', 'g_conv_ln', 'b_conv_ln', 'w_conv_out', 'w_o', 'g_norm2', 'w_ffn_in', 'w_ffn_dw', 'b_ffn_dw', 'w_ffn_down']
TWIN_WEIGHTS = ['w_ada', 'b_ada', 'g_norm1', 'w_in', 'g_q', 'g_k', 'w_attn_proj', 'w_conv_dw', 'b_conv_dw', 'g_conv_ln', 'b_conv_ln', 'w_conv_out', 'w_o', 'g_norm2', 'w_ffn_in', 'w_ffn_dw', 'b_ffn_dw', 'w_ffn_down']
TWIN_DIFF_INPUT = 'x'
TWIN_INPUTS = ['x', 'c', 'positions', 'w_ada', 'b_ada', 'g_norm1', 'w_in', 'g_q', 'g_k', 'w_attn_proj', 'w_conv_dw', 'b_conv_dw', 'g_conv_ln', 'b_conv_ln', 'w_conv_out', 'w_o', 'g_norm2', 'w_ffn_in', 'w_ffn_dw', 'b_ffn_dw', 'w_ffn_down', 'loss_target', 'm_w_ada', 'm_b_ada', 'm_g_norm1', 'm_w_in', 'm_g_q', 'm_g_k', 'm_w_attn_proj', 'm_w_conv_dw', 'm_b_conv_dw', 'm_g_conv_ln', 'm_b_conv_ln', 'm_w_conv_out', 'm_w_o', 'm_g_norm2', 'm_w_ffn_in', 'm_w_ffn_dw', 'm_b_ffn_dw', 'm_w_ffn_down', 'v_w_ada', 'v_b_ada', 'v_g_norm1', 'v_w_in', 'v_g_q', 'v_g_k', 'v_w_attn_proj', 'v_w_conv_dw', 'v_b_conv_dw', 'v_g_conv_ln', 'v_b_conv_ln', 'v_w_conv_out', 'v_w_o', 'v_g_norm2', 'v_w_ffn_in', 'v_w_ffn_dw', 'v_b_ffn_dw', 'v_w_ffn_down']
TWIN_OUTPUTS = ['loss', 'grad_x', 'grad_w_ada', 'grad_b_ada', 'grad_g_norm1', 'grad_w_in', 'grad_g_q', 'grad_g_k', 'grad_w_attn_proj', 'grad_w_conv_dw', 'grad_b_conv_dw', 'grad_g_conv_ln', 'grad_b_conv_ln', 'grad_w_conv_out', 'grad_w_o', 'grad_g_norm2', 'grad_w_ffn_in', 'grad_w_ffn_dw', 'grad_b_ffn_dw', 'grad_w_ffn_down', 'delta_w_ada', 'delta_b_ada', 'delta_g_norm1', 'delta_w_in', 'delta_g_q', 'delta_g_k', 'delta_w_attn_proj', 'delta_w_conv_dw', 'delta_b_conv_dw', 'delta_g_conv_ln', 'delta_b_conv_ln', 'delta_w_conv_out', 'delta_w_o', 'delta_g_norm2', 'delta_w_ffn_in', 'delta_w_ffn_dw', 'delta_b_ffn_dw', 'delta_w_ffn_down', 'new_m_w_ada', 'new_m_b_ada', 'new_m_g_norm1', 'new_m_w_in', 'new_m_g_q', 'new_m_g_k', 'new_m_w_attn_proj', 'new_m_w_conv_dw', 'new_m_b_conv_dw', 'new_m_g_conv_ln', 'new_m_b_conv_ln', 'new_m_w_conv_out', 'new_m_w_o', 'new_m_g_norm2', 'new_m_w_ffn_in', 'new_m_w_ffn_dw', 'new_m_b_ffn_dw', 'new_m_w_ffn_down', 'new_v_w_ada', 'new_v_b_ada', 'new_v_g_norm1', 'new_v_w_in', 'new_v_g_q', 'new_v_g_k', 'new_v_w_attn_proj', 'new_v_w_conv_dw', 'new_v_b_conv_dw', 'new_v_g_conv_ln', 'new_v_b_conv_ln', 'new_v_w_conv_out', 'new_v_w_o', 'new_v_g_norm2', 'new_v_w_ffn_in', 'new_v_w_ffn_dw', 'new_v_b_ffn_dw', 'new_v_w_ffn_down']
TWIN_LEAF_KINDS = {'loss': 'loss', 'grad_x': 'grad_x', 'grad_w_ada': 'grad_w', 'grad_b_ada': 'grad_w', 'grad_g_norm1': 'grad_w', 'grad_w_in': 'grad_w', 'grad_g_q': 'grad_w', 'grad_g_k': 'grad_w', 'grad_w_attn_proj': 'grad_w', 'grad_w_conv_dw': 'grad_w', 'grad_b_conv_dw': 'grad_w', 'grad_g_conv_ln': 'grad_w', 'grad_b_conv_ln': 'grad_w', 'grad_w_conv_out': 'grad_w', 'grad_w_o': 'grad_w', 'grad_g_norm2': 'grad_w', 'grad_w_ffn_in': 'grad_w', 'grad_w_ffn_dw': 'grad_w', 'grad_b_ffn_dw': 'grad_w', 'grad_w_ffn_down': 'grad_w', 'delta_w_ada': 'delta_w', 'delta_b_ada': 'delta_w', 'delta_g_norm1': 'delta_w', 'delta_w_in': 'delta_w', 'delta_g_q': 'delta_w', 'delta_g_k': 'delta_w', 'delta_w_attn_proj': 'delta_w', 'delta_w_conv_dw': 'delta_w', 'delta_b_conv_dw': 'delta_w', 'delta_g_conv_ln': 'delta_w', 'delta_b_conv_ln': 'delta_w', 'delta_w_conv_out': 'delta_w', 'delta_w_o': 'delta_w', 'delta_g_norm2': 'delta_w', 'delta_w_ffn_in': 'delta_w', 'delta_w_ffn_dw': 'delta_w', 'delta_b_ffn_dw': 'delta_w', 'delta_w_ffn_down': 'delta_w', 'new_m_w_ada': 'new_m', 'new_m_b_ada': 'new_m', 'new_m_g_norm1': 'new_m', 'new_m_w_in': 'new_m', 'new_m_g_q': 'new_m', 'new_m_g_k': 'new_m', 'new_m_w_attn_proj': 'new_m', 'new_m_w_conv_dw': 'new_m', 'new_m_b_conv_dw': 'new_m', 'new_m_g_conv_ln': 'new_m', 'new_m_b_conv_ln': 'new_m', 'new_m_w_conv_out': 'new_m', 'new_m_w_o': 'new_m', 'new_m_g_norm2': 'new_m', 'new_m_w_ffn_in': 'new_m', 'new_m_w_ffn_dw': 'new_m', 'new_m_b_ffn_dw': 'new_m', 'new_m_w_ffn_down': 'new_m', 'new_v_w_ada': 'new_v', 'new_v_b_ada': 'new_v', 'new_v_g_norm1': 'new_v', 'new_v_w_in': 'new_v', 'new_v_g_q': 'new_v', 'new_v_g_k': 'new_v', 'new_v_w_attn_proj': 'new_v', 'new_v_w_conv_dw': 'new_v', 'new_v_b_conv_dw': 'new_v', 'new_v_g_conv_ln': 'new_v', 'new_v_b_conv_ln': 'new_v', 'new_v_w_conv_out': 'new_v', 'new_v_w_o': 'new_v', 'new_v_g_norm2': 'new_v', 'new_v_w_ffn_in': 'new_v', 'new_v_w_ffn_dw': 'new_v', 'new_v_b_ffn_dw': 'new_v', 'new_v_w_ffn_down': 'new_v'}


def _forward(args):
    return _fwd_reference(*[args[k] for k in FWD_PARAMS])


def _output_shape():
    def fwd():
        inp = _fwd_setup_inputs(0)
        return _fwd_reference(*[inp[k] for k in FWD_PARAMS])
    out = _jax.eval_shape(fwd)
    return out.shape, out.dtype

N_MICROBATCH = 1
ADAM_LR = 0.001
ADAM_B1 = 0.9
ADAM_B2 = 0.999
ADAM_EPS = 1e-08
ADAM_WD = 0.01
ADAM_STEP = 10
PER_EXAMPLE_BATCH_AXIS = {'x': 0, 'c': 0, 'positions': 0, 'loss_target': 0}
SHARED_INPUTS = []
_WEIGHT_DTYPES = {'w_ada': _jnp.float32, 'b_ada': _jnp.float32, 'g_norm1': _jnp.float32, 'w_in': _jnp.float32, 'g_q': _jnp.float32, 'g_k': _jnp.float32, 'w_attn_proj': _jnp.float32, 'w_conv_dw': _jnp.float32, 'b_conv_dw': _jnp.float32, 'g_conv_ln': _jnp.float32, 'b_conv_ln': _jnp.float32, 'w_conv_out': _jnp.float32, 'w_o': _jnp.float32, 'g_norm2': _jnp.float32, 'w_ffn_in': _jnp.float32, 'w_ffn_dw': _jnp.float32, 'b_ffn_dw': _jnp.float32, 'w_ffn_down': _jnp.float32}
MOMENT_SCALE = {'w_ada': 2.351910e+00, 'b_ada': 6.588925e+00, 'g_norm1': 1.167718e-01, 'w_in': 5.555833e-02, 'g_q': 1.164082e-01, 'g_k': 1.157847e-01, 'w_attn_proj': 1.228706e-01, 'w_conv_dw': 1.197740e-01, 'b_conv_dw': 9.756925e-01, 'g_conv_ln': 1.561379e+00, 'b_conv_ln': 1.129241e+00, 'w_conv_out': 2.387789e-01, 'w_o': 2.246314e-01, 'g_norm2': 1.314127e+01, 'w_ffn_in': 2.080916e-01, 'w_ffn_dw': 1.525365e+00, 'b_ffn_dw': 1.703250e+00, 'w_ffn_down': 2.061842e-01}


def _to_microbatches(a, axis):
    t = _jnp.moveaxis(a, axis, 0)
    t = t.reshape((N_MICROBATCH, t.shape[0] // N_MICROBATCH) + t.shape[1:])
    return _jnp.moveaxis(t, 1, axis + 1)


def setup_inputs(seed: int = 0) -> dict:
    inp = _fwd_setup_inputs(seed)
    key = _jax.random.fold_in(_jax.random.key(seed), 7919)
    shape, _ = _output_shape()
    out = dict(inp)
    out["loss_target"] = _jax.random.normal(_jax.random.fold_in(key, 0), shape, _jnp.float32)
    for i, name in enumerate(TWIN_WEIGHTS):
        w = inp[name].astype(_jnp.float32)
        if MOMENT_SCALE is None:
            s = _jnp.sqrt(_jnp.mean(_jnp.square(w)) + 1e-30)
        else:
            s = MOMENT_SCALE[name]
        km, kv = _jax.random.split(_jax.random.fold_in(key, i + 1))
        out[name] = w
        out["m_" + name] = s * _jax.random.normal(km, w.shape, _jnp.float32)
        out["v_" + name] = (s * s) * _jax.random.uniform(kv, w.shape, _jnp.float32, 0.5, 1.5)
    if N_MICROBATCH > 1:
        for name, axis in PER_EXAMPLE_BATCH_AXIS.items():
            out[name] = _to_microbatches(out[name], axis)
    return {'x': out['x'], 'c': out['c'], 'positions': out['positions'], 'w_ada': out['w_ada'], 'b_ada': out['b_ada'], 'g_norm1': out['g_norm1'], 'w_in': out['w_in'], 'g_q': out['g_q'], 'g_k': out['g_k'], 'w_attn_proj': out['w_attn_proj'], 'w_conv_dw': out['w_conv_dw'], 'b_conv_dw': out['b_conv_dw'], 'g_conv_ln': out['g_conv_ln'], 'b_conv_ln': out['b_conv_ln'], 'w_conv_out': out['w_conv_out'], 'w_o': out['w_o'], 'g_norm2': out['g_norm2'], 'w_ffn_in': out['w_ffn_in'], 'w_ffn_dw': out['w_ffn_dw'], 'b_ffn_dw': out['b_ffn_dw'], 'w_ffn_down': out['w_ffn_down'], 'loss_target': out['loss_target'], 'm_w_ada': out['m_w_ada'], 'm_b_ada': out['m_b_ada'], 'm_g_norm1': out['m_g_norm1'], 'm_w_in': out['m_w_in'], 'm_g_q': out['m_g_q'], 'm_g_k': out['m_g_k'], 'm_w_attn_proj': out['m_w_attn_proj'], 'm_w_conv_dw': out['m_w_conv_dw'], 'm_b_conv_dw': out['m_b_conv_dw'], 'm_g_conv_ln': out['m_g_conv_ln'], 'm_b_conv_ln': out['m_b_conv_ln'], 'm_w_conv_out': out['m_w_conv_out'], 'm_w_o': out['m_w_o'], 'm_g_norm2': out['m_g_norm2'], 'm_w_ffn_in': out['m_w_ffn_in'], 'm_w_ffn_dw': out['m_w_ffn_dw'], 'm_b_ffn_dw': out['m_b_ffn_dw'], 'm_w_ffn_down': out['m_w_ffn_down'], 'v_w_ada': out['v_w_ada'], 'v_b_ada': out['v_b_ada'], 'v_g_norm1': out['v_g_norm1'], 'v_w_in': out['v_w_in'], 'v_g_q': out['v_g_q'], 'v_g_k': out['v_g_k'], 'v_w_attn_proj': out['v_w_attn_proj'], 'v_w_conv_dw': out['v_w_conv_dw'], 'v_b_conv_dw': out['v_b_conv_dw'], 'v_g_conv_ln': out['v_g_conv_ln'], 'v_b_conv_ln': out['v_b_conv_ln'], 'v_w_conv_out': out['v_w_conv_out'], 'v_w_o': out['v_w_o'], 'v_g_norm2': out['v_g_norm2'], 'v_w_ffn_in': out['v_w_ffn_in'], 'v_w_ffn_dw': out['v_w_ffn_dw'], 'v_b_ffn_dw': out['v_b_ffn_dw'], 'v_w_ffn_down': out['v_w_ffn_down']}


def _loss(weights, diff, rest, loss_target):
    with _jax.named_scope("forward"):
        args = {**rest, TWIN_DIFF_INPUT: diff, **{k: w.astype(_WEIGHT_DTYPES[k]) for k, w in weights.items()}}
        y = _forward(args)
    with _jax.named_scope("loss_head"):
        err = _jnp.square(y.astype(_jnp.float32) - loss_target)
        return 0.5 * _jnp.sum(_jnp.mean(err, axis=-1)) if err.ndim else 0.5 * err


def _adamw(w, g, m, v):
    m = ADAM_B1 * m + (1.0 - ADAM_B1) * g
    v = ADAM_B2 * v + (1.0 - ADAM_B2) * _jnp.square(g)
    m_hat = m / (1.0 - ADAM_B1 ** ADAM_STEP)
    v_hat = v / (1.0 - ADAM_B2 ** ADAM_STEP)
    delta = -ADAM_LR * (m_hat / (_jnp.sqrt(v_hat) + ADAM_EPS) + ADAM_WD * w)
    return delta, m, v


def reference(x, c, positions, w_ada, b_ada, g_norm1, w_in, g_q, g_k, w_attn_proj, w_conv_dw, b_conv_dw, g_conv_ln, b_conv_ln, w_conv_out, w_o, g_norm2, w_ffn_in, w_ffn_dw, b_ffn_dw, w_ffn_down, loss_target, m_w_ada, m_b_ada, m_g_norm1, m_w_in, m_g_q, m_g_k, m_w_attn_proj, m_w_conv_dw, m_b_conv_dw, m_g_conv_ln, m_b_conv_ln, m_w_conv_out, m_w_o, m_g_norm2, m_w_ffn_in, m_w_ffn_dw, m_b_ffn_dw, m_w_ffn_down, v_w_ada, v_b_ada, v_g_norm1, v_w_in, v_g_q, v_g_k, v_w_attn_proj, v_w_conv_dw, v_b_conv_dw, v_g_conv_ln, v_b_conv_ln, v_w_conv_out, v_w_o, v_g_norm2, v_w_ffn_in, v_w_ffn_dw, v_b_ffn_dw, v_w_ffn_down):
    given = dict(x=x, c=c, positions=positions, w_ada=w_ada, b_ada=b_ada, g_norm1=g_norm1, w_in=w_in, g_q=g_q, g_k=g_k, w_attn_proj=w_attn_proj, w_conv_dw=w_conv_dw, b_conv_dw=b_conv_dw, g_conv_ln=g_conv_ln, b_conv_ln=b_conv_ln, w_conv_out=w_conv_out, w_o=w_o, g_norm2=g_norm2, w_ffn_in=w_ffn_in, w_ffn_dw=w_ffn_dw, b_ffn_dw=b_ffn_dw, w_ffn_down=w_ffn_down, loss_target=loss_target, m_w_ada=m_w_ada, m_b_ada=m_b_ada, m_g_norm1=m_g_norm1, m_w_in=m_w_in, m_g_q=m_g_q, m_g_k=m_g_k, m_w_attn_proj=m_w_attn_proj, m_w_conv_dw=m_w_conv_dw, m_b_conv_dw=m_b_conv_dw, m_g_conv_ln=m_g_conv_ln, m_b_conv_ln=m_b_conv_ln, m_w_conv_out=m_w_conv_out, m_w_o=m_w_o, m_g_norm2=m_g_norm2, m_w_ffn_in=m_w_ffn_in, m_w_ffn_dw=m_w_ffn_dw, m_b_ffn_dw=m_b_ffn_dw, m_w_ffn_down=m_w_ffn_down, v_w_ada=v_w_ada, v_b_ada=v_b_ada, v_g_norm1=v_g_norm1, v_w_in=v_w_in, v_g_q=v_g_q, v_g_k=v_g_k, v_w_attn_proj=v_w_attn_proj, v_w_conv_dw=v_w_conv_dw, v_b_conv_dw=v_b_conv_dw, v_g_conv_ln=v_g_conv_ln, v_b_conv_ln=v_b_conv_ln, v_w_conv_out=v_w_conv_out, v_w_o=v_w_o, v_g_norm2=v_g_norm2, v_w_ffn_in=v_w_ffn_in, v_w_ffn_dw=v_w_ffn_dw, v_b_ffn_dw=v_b_ffn_dw, v_w_ffn_down=v_w_ffn_down)
    weights = {n: given[n] for n in TWIN_WEIGHTS}
    shared = {n: given[n] for n in SHARED_INPUTS}
    per_example = {n: given[n] for n in ['x', 'c', 'positions']}
    grad_fn = _jax.value_and_grad(_loss, argnums=(0, 1))

    def one_microbatch(ex, loss_target):
        ex = dict(ex)
        diff = ex.pop(TWIN_DIFF_INPUT)
        return grad_fn(weights, diff, {**shared, **ex}, loss_target)

    if N_MICROBATCH == 1:
        loss, (grad_w, grad_x) = one_microbatch(per_example, given["loss_target"])
    else:
        def body(carry, xs):
            loss_sum, grad_sum = carry
            l_k, (gw_k, gx_k) = one_microbatch(xs[0], xs[1])
            with _jax.named_scope("update"):
                return (loss_sum + l_k, _jax.tree.map(_jnp.add, grad_sum, gw_k)), gx_k

        init = (_jnp.zeros((), _jnp.float32), _jax.tree.map(_jnp.zeros_like, weights))
        (loss, grad_w), grad_x = _jax.lax.scan(body, init, (per_example, given["loss_target"]))
    with _jax.named_scope("update"):
        delta_w, new_m, new_v = {}, {}, {}
        for n in TWIN_WEIGHTS:
            delta_w[n], new_m[n], new_v[n] = _adamw(weights[n], grad_w[n], given["m_" + n], given["v_" + n])
    return (loss, grad_x, *[grad_w[n] for n in TWIN_WEIGHTS], *[delta_w[n] for n in TWIN_WEIGHTS],
            *[new_m[n] for n in TWIN_WEIGHTS], *[new_v[n] for n in TWIN_WEIGHTS])
```

```python
import functools
import math

import jax
import jax.numpy as jnp
from jax import lax
from jax.experimental import pallas as pl
from jax.experimental.pallas import tpu as pltpu

F32 = jnp.float32
BF16 = jnp.bfloat16
MESH = pl.DeviceIdType.MESH
N_DEV = 8

EPS = 1e-6
HEAD_DIM = 128
BLOCK = 128
DILATIONS = (1, 4, 16)
HEADS_PER_GROUP = 4
N_HEADS = 12
ATTN_W = N_HEADS * HEAD_DIM
ROT_DIM = 32
ROPE_THETA = 500000.0
CONV_K = 31
CONV_HALO = 32
FFN_K = 3
FFN_HALO = 8
NEG = -1e30

ADAM_LR, ADAM_B1, ADAM_B2, ADAM_EPS, ADAM_WD, ADAM_STEP = 0.001, 0.9, 0.999, 1e-08, 0.01, 10

LANE = 128
VMEM_LIMIT = 56 * 1024 * 1024
ROW_TILE = 512


def _pcall(body, **kw):
    return pl.pallas_call(body, **kw)


def _cparams(*sem):
    return pltpu.CompilerParams(dimension_semantics=sem, vmem_limit_bytes=VMEM_LIMIT)


def _sig(v):
    return 1.0 / (1.0 + jnp.exp(-v))


def _divtile(dim, target):
    best = None
    for t in range(LANE, min(dim, target) + 1, LANE):
        if dim % t == 0:
            best = t
    return best or dim


def _rows(t, c, col=0):
    return pl.BlockSpec((t, c), lambda i: (i, col))


def _full(shape):
    nd = len(shape)
    return pl.BlockSpec(shape, lambda i: (0,) * nd)


def _acc_rows(ref, val, i):
    @pl.when(i == 0)
    def _():
        ref[...] = jnp.zeros_like(ref)
    r = val.shape[0]
    ref[0:r, :] += val


def mm(a, b, mode, *, name, out_dtype=F32, c_in=None, b_off=0, b_len=None):
    if mode == "nn":
        M, K = a.shape
        N = b.shape[1]
    elif mode == "nt":
        M, K = a.shape
        N = b_len if b_len is not None else b.shape[0]
    else:
        K, M = a.shape
        N = b.shape[1]
    tm = _divtile(M, 1024)
    g_n = math.gcd(N, b_off) if (mode == "nt" and b_off) else N
    g_k = math.gcd(K, b_off) if (mode == "nn" and b_off) else K
    tn = _divtile(g_n, 1024 if mode != "tn" else 1024)
    tk = _divtile(g_k, 1024)
    nk = K // tk

    def body(*refs):
        if c_in is not None:
            a_ref, b_ref, c_ref, o_ref, acc = refs
        else:
            a_ref, b_ref, o_ref, acc = refs
            c_ref = None
        k = pl.program_id(2)

        @pl.when(k == 0)
        def _():
            if c_ref is None:
                acc[...] = jnp.zeros_like(acc)
            else:
                acc[...] = c_ref[...].astype(F32)

        av = a_ref[...].astype(BF16)
        bv = b_ref[...].astype(BF16)
        if mode == "nn":
            dims = (((1,), (0,)), ((), ()))
        elif mode == "nt":
            dims = (((1,), (1,)), ((), ()))
        else:
            dims = (((0,), (0,)), ((), ()))
        acc[...] += lax.dot_general(av, bv, dims, preferred_element_type=F32)

        @pl.when(k == nk - 1)
        def _():
            o_ref[...] = acc[...].astype(out_dtype)

    if mode == "nn":
        a_spec = pl.BlockSpec((tm, tk), lambda i, j, k: (i, k))
        ob = b_off // tk
        b_spec = pl.BlockSpec((tk, tn), lambda i, j, k: (k + ob, j))
    elif mode == "nt":
        a_spec = pl.BlockSpec((tm, tk), lambda i, j, k: (i, k))
        ob = b_off // tn
        b_spec = pl.BlockSpec((tn, tk), lambda i, j, k: (j + ob, k))
    else:
        a_spec = pl.BlockSpec((tk, tm), lambda i, j, k: (k, i))
        b_spec = pl.BlockSpec((tk, tn), lambda i, j, k: (k, j))
    o_spec = pl.BlockSpec((tm, tn), lambda i, j, k: (i, j))
    in_specs = [a_spec, b_spec]
    args = [a, b]
    if c_in is not None:
        in_specs.append(o_spec)
        args.append(c_in)
    return _pcall(
        body, name=name, grid=(M // tm, N // tn, nk),
        in_specs=in_specs, out_specs=o_spec,
        out_shape=jax.ShapeDtypeStruct((M, N), out_dtype),
        scratch_shapes=[pltpu.VMEM((tm, tn), F32)],
        compiler_params=_cparams("parallel", "parallel", "arbitrary"),
    )(*args)


def norm_mod_fwd(x_prev, delta, gt, g, sc, sh, *, name):
    S, D = x_prev.shape
    T = ROW_TILE
    has_delta = delta is not None

    def body(*refs):
        if has_delta:
            xp, dl, gt_r, g_r, sc_r, sh_r, x_out, h_out = refs
            xv = xp[...] + gt_r[...] * dl[...]
            x_out[...] = xv
        else:
            xp, g_r, sc_r, sh_r, h_out = refs
            xv = xp[...]
        r = lax.rsqrt(jnp.mean(xv * xv, axis=-1, keepdims=True) + EPS)
        h_out[...] = ((xv * r) * g_r[...] * (1.0 + sc_r[...]) + sh_r[...]).astype(BF16)

    vec = _full((1, D))
    if has_delta:
        ins, specs = [x_prev, delta, gt, g, sc, sh], [_rows(T, D), _rows(T, D), vec, vec, vec, vec]
        outs = (jax.ShapeDtypeStruct((S, D), F32), jax.ShapeDtypeStruct((S, D), BF16))
        ospecs = (_rows(T, D), _rows(T, D))
    else:
        ins, specs = [x_prev, g, sc, sh], [_rows(T, D), vec, vec, vec]
        outs = jax.ShapeDtypeStruct((S, D), BF16)
        ospecs = _rows(T, D)
    res = _pcall(body, name=name, grid=(S // T,), in_specs=specs, out_specs=ospecs, out_shape=outs,
                 compiler_params=_cparams("parallel"))(*ins)
    return res if has_delta else (x_prev, res)


def norm_mod_bwd(x, dh, g, sc, sh, dx_res, *, name):
    S, D = x.shape
    T = ROW_TILE

    def body(x_r, dh_r, g_r, sc_r, sh_r, dr_r, dx_o, dg_o, dsc_o, dsh_o):
        i = pl.program_id(0)
        xv = x_r[...]
        dh_v = dh_r[...]
        r = lax.rsqrt(jnp.mean(xv * xv, axis=-1, keepdims=True) + EPS)
        xh = xv * r
        dn = dh_v * (1.0 + sc_r[...])
        dxh = dn * g_r[...]
        dx_o[...] = dr_r[...] + r * (dxh - xh * jnp.mean(dxh * xh, axis=-1, keepdims=True))
        _acc_rows(dg_o, jnp.sum(dn * xh, axis=0, keepdims=True), i)
        _acc_rows(dsc_o, jnp.sum(dh_v * (xh * g_r[...]), axis=0, keepdims=True), i)
        _acc_rows(dsh_o, jnp.sum(dh_v, axis=0, keepdims=True), i)

    vec = _full((1, D))
    part = jax.ShapeDtypeStruct((8, D), F32)
    return _pcall(
        body, name=name, grid=(S // T,),
        in_specs=[_rows(T, D), _rows(T, D), vec, vec, vec, _rows(T, D)],
        out_specs=(_rows(T, D), _full((8, D)), _full((8, D)), _full((8, D))),
        out_shape=(jax.ShapeDtypeStruct((S, D), F32), part, part, part),
        compiler_params=_cparams("arbitrary"),
    )(x, dh, g, sc, sh, dx_res)


def scale_bwd(dx, delta, gt, *, name):
    S, D = dx.shape
    T = ROW_TILE

    def body(dx_r, dl_r, gt_r, dd_o, dgt_o):
        i = pl.program_id(0)
        dv = dx_r[...]
        dd_o[...] = (dv * gt_r[...]).astype(BF16)
        _acc_rows(dgt_o, jnp.sum(dv * dl_r[...], axis=0, keepdims=True), i)

    return _pcall(
        body, name=name, grid=(S // T,),
        in_specs=[_rows(T, D), _rows(T, D), _full((1, D))],
        out_specs=(_rows(T, D), _full((8, D))),
        out_shape=(jax.ShapeDtypeStruct((S, D), BF16), jax.ShapeDtypeStruct((8, D), F32)),
        compiler_params=_cparams("arbitrary"),
    )(dx, delta, gt)


def loss_head(x_mid, ffo, gt, target, *, name):
    S, D = x_mid.shape
    T = ROW_TILE

    def body(x_r, f_r, gt_r, t_r, dy_o, l_o):
        i = pl.program_id(0)
        e = x_r[...] + gt_r[...] * f_r[...] - t_r[...]
        dy_o[...] = e * (1.0 / D)
        _acc_rows(l_o, jnp.sum(e * e, axis=0, keepdims=True), i)

    return _pcall(
        body, name=name, grid=(S // T,),
        in_specs=[_rows(T, D), _rows(T, D), _full((1, D)), _rows(T, D)],
        out_specs=(_rows(T, D), _full((8, D))),
        out_shape=(jax.ShapeDtypeStruct((S, D), F32), jax.ShapeDtypeStruct((8, D), F32)),
        compiler_params=_cparams("arbitrary"),
    )(x_mid, ffo, gt, target)


def _rope(t, c_t, s1_t, s2_t):
    return t * c_t + pltpu.roll(t, LANE - ROT_DIM // 2, 1) * s1_t + pltpu.roll(t, ROT_DIM // 2, 1) * s2_t


def _rope_t(d, c_t, s1_t, s2_t):
    return d * c_t + pltpu.roll(d * s1_t, ROT_DIM // 2, 1) + pltpu.roll(d * s2_t, LANE - ROT_DIM // 2, 1)


def qk_prep_fwd(zq, g_q, g_k, tabs, *, name):
    S = zq.shape[0]
    T = ROW_TILE

    def body(q_r, k_r, gq_r, gk_r, c_r, s1_r, s2_r, qn_o, kn_o):
        c_t, s1_t, s2_t = c_r[...], s1_r[...], s2_r[...]
        for src, g_r, dst in ((q_r, gq_r, qn_o), (k_r, gk_r, kn_o)):
            for h in range(N_HEADS):
                cols = slice(h * HEAD_DIM, (h + 1) * HEAD_DIM)
                t = src[:, cols]
                r = lax.rsqrt(jnp.mean(t * t, axis=-1, keepdims=True) + EPS)
                dst[:, cols] = _rope(t * r * g_r[...], c_t, s1_t, s2_t)

    tab = _rows(T, LANE)
    shp = jax.ShapeDtypeStruct((S, ATTN_W), F32)
    return _pcall(
        body, name=name, grid=(S // T,),
        in_specs=[_rows(T, ATTN_W, 0), _rows(T, ATTN_W, 1), _full((1, LANE)), _full((1, LANE)), tab, tab, tab],
        out_specs=(_rows(T, ATTN_W), _rows(T, ATTN_W)), out_shape=(shp, shp),
        compiler_params=_cparams("parallel"),
    )(zq, zq, g_q, g_k, *tabs)


def attn_bwd_post(zq, g_q, g_k, tabs, dq, dkc, dkp, dvc, dvp, *, name):
    S = zq.shape[0]
    T = BLOCK
    nblk = S // T
    GW = HEADS_PER_GROUP * HEAD_DIM

    def body(*refs):
        q_r, k_r, gq_r, gk_r, c_r, s1_r, s2_r = refs[:7]
        grp = refs[7:7 + 15]
        dz_o, dgq_o, dgk_o = refs[22:]
        i = pl.program_id(0)
        c_t, s1_t, s2_t = c_r[...], s1_r[...], s2_r[...]
        dgq = jnp.zeros((1, LANE), F32)
        dgk = jnp.zeros((1, LANE), F32)
        for gi, d in enumerate(DILATIONS):
            dq_r, dkc_r, dkp_r, dvc_r, dvp_r = grp[5 * gi:5 * gi + 5]
            live = jnp.where(i + d < nblk, 1.0, 0.0)
            for hh in range(HEADS_PER_GROUP):
                h = gi * HEADS_PER_GROUP + hh
                cols = slice(h * HEAD_DIM, (h + 1) * HEAD_DIM)
                gc = slice(hh * HEAD_DIM, (hh + 1) * HEAD_DIM)
                dk_v = dkc_r[:, gc] + live * dkp_r[:, gc]
                dv_v = dvc_r[:, gc] + live * dvp_r[:, gc]
                dz_o[:, 2 * ATTN_W + h * HEAD_DIM:2 * ATTN_W + (h + 1) * HEAD_DIM] = dv_v.astype(BF16)
                for which, (src, g_r, d_out) in enumerate(((q_r, gq_r, dq_r[:, gc]), (k_r, gk_r, dk_v))):
                    t = src[:, cols]
                    r = lax.rsqrt(jnp.mean(t * t, axis=-1, keepdims=True) + EPS)
                    xh = t * r
                    dtn = _rope_t(d_out, c_t, s1_t, s2_t)
                    dxh = dtn * g_r[...]
                    dt = r * (dxh - xh * jnp.mean(dxh * xh, axis=-1, keepdims=True))
                    dz_o[:, which * ATTN_W + h * HEAD_DIM:which * ATTN_W + (h + 1) * HEAD_DIM] = dt.astype(BF16)
                    part = jnp.sum(dtn * xh, axis=0, keepdims=True)
                    if which == 0:
                        dgq = dgq + part
                    else:
                        dgk = dgk + part
        _acc_rows(dgq_o, dgq, i)
        _acc_rows(dgk_o, dgk, i)

    tab = _rows(T, LANE)
    specs = [_rows(T, ATTN_W, 0), _rows(T, ATTN_W, 1), _full((1, LANE)), _full((1, LANE)), tab, tab, tab]
    args = [zq, zq, g_q, g_k, *tabs]
    for gi, d in enumerate(DILATIONS):
        cur = _rows(T, GW)
        nxt = pl.BlockSpec((T, GW), functools.partial(lambda i, d: (jnp.minimum(i + d, nblk - 1), 0), d=d))
        specs += [cur, cur, nxt, cur, nxt]
        args += [dq[gi], dkc[gi], dkp[gi], dvc[gi], dvp[gi]]
    part = jax.ShapeDtypeStruct((8, LANE), F32)
    return _pcall(
        body, name=name, grid=(nblk,), in_specs=specs,
        out_specs=(_rows(T, 3 * ATTN_W), _full((8, LANE)), _full((8, LANE))),
        out_shape=(jax.ShapeDtypeStruct((S, 3 * ATTN_W), BF16), part, part),
        compiler_params=_cparams("arbitrary"),
    )(*args)


def _attn_geometry(d, S):
    R = 4 * BLOCK * d if d < 16 else BLOCK * d
    R = min(R, S)
    return R, R // (BLOCK * d), S // R


def _sub_rows(j, r, d):
    if d == 1:
        return pl.ds(j * BLOCK, BLOCK)
    return pl.ds(j * BLOCK * d + r, BLOCK, stride=d)


def _dot_nt(a, b):
    return lax.dot_general(a, b, (((1,), (1,)), ((), ())), preferred_element_type=F32)


def _dot_tn(a, b):
    return lax.dot_general(a, b, (((0,), (0,)), ((), ())), preferred_element_type=F32)


def _attn_specs(gi, R):
    h0 = gi * HEADS_PER_GROUP
    vcol = 2 * N_HEADS + h0
    cur = lambda off: pl.BlockSpec((R, HEAD_DIM), lambda h, n: (n, off + h))
    prev = lambda off: pl.BlockSpec((R, HEAD_DIM), lambda h, n: (jnp.maximum(n - 1, 0), off + h))
    return [cur(h0), cur(h0), prev(h0), cur(vcol), prev(vcol)]


def attn_fwd(qn, kn, zq, gi, *, name):
    S = qn.shape[0]
    d = DILATIONS[gi]
    R, J, nblk = _attn_geometry(d, S)
    scale = HEAD_DIM ** -0.5
    GW = HEADS_PER_GROUP * HEAD_DIM

    def body(q_r, kc_r, kp_r, vc_r, vp_r, o_o, l_o):
        n = pl.program_id(1)
        qi = lax.broadcasted_iota(jnp.int32, (BLOCK, BLOCK), 0)
        kj = lax.broadcasted_iota(jnp.int32, (BLOCK, BLOCK), 1)
        m_cur = kj <= qi
        m_prev = kj >= qi

        def unit(j, r):
            rq = _sub_rows(j, r, d)
            q = q_r[rq, :].astype(BF16)
            kc = kc_r[rq, :].astype(BF16)
            vc = vc_r[rq, :].astype(BF16)
            if j > 0:
                rp = _sub_rows(j - 1, r, d)
                kp = kc_r[rp, :].astype(BF16)
                vp = vc_r[rp, :].astype(BF16)
                pen = 0.0
            else:
                rp = _sub_rows(J - 1, r, d)
                kp = kp_r[rp, :].astype(BF16)
                vp = vp_r[rp, :].astype(BF16)
                pen = jnp.where(n > 0, 0.0, NEG)
            s_c = jnp.where(m_cur, _dot_nt(q, kc) * scale, NEG)
            s_p = jnp.where(m_prev, _dot_nt(q, kp) * scale, NEG) + pen
            m = jnp.maximum(jnp.max(s_c, axis=-1, keepdims=True), jnp.max(s_p, axis=-1, keepdims=True))
            p_c = jnp.exp(s_c - m)
            p_p = jnp.exp(s_p - m)
            l = jnp.sum(p_c, axis=-1, keepdims=True) + jnp.sum(p_p, axis=-1, keepdims=True)
            o = (jnp.dot(p_c.astype(BF16), vc, preferred_element_type=F32)
                 + jnp.dot(p_p.astype(BF16), vp, preferred_element_type=F32)) / l
            o_o[rq, :] = o
            l_o[rq, :] = jnp.broadcast_to(m + jnp.log(l), (BLOCK, HEAD_DIM))

        for j in range(J):
            if d == 1:
                unit(j, 0)
            else:
                def step(r, carry, j=j):
                    unit(j, r)
                    return carry
                lax.fori_loop(0, d, step, 0)

    ospec = pl.BlockSpec((R, HEAD_DIM), lambda h, n: (n, h))
    shp = jax.ShapeDtypeStruct((S, GW), F32)
    return _pcall(
        body, name=name, grid=(HEADS_PER_GROUP, nblk), in_specs=_attn_specs(gi, R),
        out_specs=(ospec, ospec), out_shape=(shp, shp),
        compiler_params=_cparams("parallel", "arbitrary"),
    )(qn, kn, kn, zq, zq)


def attn_bwd(qn, kn, zq, do, lse, cc, gi, *, name):
    S = qn.shape[0]
    d = DILATIONS[gi]
    R, J, nblk = _attn_geometry(d, S)
    scale = HEAD_DIM ** -0.5
    GW = HEADS_PER_GROUP * HEAD_DIM

    def body(q_r, kc_r, kp_r, vc_r, vp_r, do_r, l_r, c_r, dq_o, dkc_o, dkp_o, dvc_o, dvp_o):
        n = pl.program_id(1)
        qi = lax.broadcasted_iota(jnp.int32, (BLOCK, BLOCK), 0)
        kj = lax.broadcasted_iota(jnp.int32, (BLOCK, BLOCK), 1)
        m_cur = kj <= qi
        m_prev = kj >= qi

        def unit(j, r):
            rq = _sub_rows(j, r, d)
            q = q_r[rq, :].astype(BF16)
            kc = kc_r[rq, :].astype(BF16)
            vc = vc_r[rq, :].astype(BF16)
            if j > 0:
                rp = _sub_rows(j - 1, r, d)
                kp = kc_r[rp, :].astype(BF16)
                vp = vc_r[rp, :].astype(BF16)
                pen = 0.0
            else:
                rp = _sub_rows(J - 1, r, d)
                kp = kp_r[rp, :].astype(BF16)
                vp = vp_r[rp, :].astype(BF16)
                pen = jnp.where(n > 0, 0.0, NEG)
            dov = do_r[rq, :]
            dob = dov.astype(BF16)
            lv = l_r[rq, :]
            cv = c_r[rq, :]
            p_c = jnp.exp(jnp.where(m_cur, _dot_nt(q, kc) * scale, NEG) - lv)
            p_p = jnp.exp(jnp.where(m_prev, _dot_nt(q, kp) * scale, NEG) + pen - lv)
            ds_c = (p_c * (_dot_nt(dob, vc) + cv)).astype(BF16)
            ds_p = (p_p * (_dot_nt(dob, vp) + cv)).astype(BF16)
            dq_o[rq, :] = (jnp.dot(ds_c, kc, preferred_element_type=F32)
                           + jnp.dot(ds_p, kp, preferred_element_type=F32)) * scale
            dkc_o[rq, :] = _dot_tn(ds_c, q) * scale
            dkp_o[rq, :] = _dot_tn(ds_p, q) * scale
            dvc_o[rq, :] = _dot_tn(p_c.astype(BF16), dob)
            dvp_o[rq, :] = _dot_tn(p_p.astype(BF16), dob)

        for j in range(J):
            if d == 1:
                unit(j, 0)
            else:
                def step(r, carry, j=j):
                    unit(j, r)
                    return carry
                lax.fori_loop(0, d, step, 0)

    ospec = pl.BlockSpec((R, HEAD_DIM), lambda h, n: (n, h))
    shp = jax.ShapeDtypeStruct((S, GW), F32)
    return _pcall(
        body, name=name, grid=(HEADS_PER_GROUP, nblk),
        in_specs=_attn_specs(gi, R) + [ospec, ospec, ospec],
        out_specs=(ospec,) * 5, out_shape=(shp,) * 5,
        compiler_params=_cparams("parallel", "arbitrary"),
    )(qn, kn, kn, zq, zq, do, lse, cc)


def combine_fwd(o, lse, *, name):
    S, GW = o[0].shape
    T = ROW_TILE

    def body(o0, o1, o2, l0, l1, l2, a_o):
        m = jnp.maximum(jnp.maximum(l0[...], l1[...]), l2[...])
        e0, e1, e2 = jnp.exp(l0[...] - m), jnp.exp(l1[...] - m), jnp.exp(l2[...] - m)
        a_o[...] = ((e0 * o0[...] + e1 * o1[...] + e2 * o2[...]) / (e0 + e1 + e2)).astype(BF16)

    return _pcall(
        body, name=name, grid=(S // T,), in_specs=[_rows(T, GW)] * 6, out_specs=_rows(T, GW),
        out_shape=jax.ShapeDtypeStruct((S, GW), BF16), compiler_params=_cparams("parallel"),
    )(*o, *lse)


def combine_bwd(d_attn, o, lse, *, name):
    S, GW = d_attn.shape
    T = ROW_TILE

    def body(da_r, o0, o1, o2, l0, l1, l2, d0, d1, d2, c0, c1, c2):
        m = jnp.maximum(jnp.maximum(l0[...], l1[...]), l2[...])
        e0, e1, e2 = jnp.exp(l0[...] - m), jnp.exp(l1[...] - m), jnp.exp(l2[...] - m)
        inv = 1.0 / (e0 + e1 + e2)
        w = (e0 * inv, e1 * inv, e2 * inv)
        da = da_r[...]
        attn = w[0] * o0[...] + w[1] * o1[...] + w[2] * o2[...]
        prod = da * attn
        for hh in range(HEADS_PER_GROUP):
            cols = slice(hh * HEAD_DIM, (hh + 1) * HEAD_DIM)
            a_h = jnp.sum(prod[:, cols], axis=-1, keepdims=True)
            for w_g, d_o, c_o in zip(w, (d0, d1, d2), (c0, c1, c2)):
                d_o[:, cols] = w_g[:, cols] * da[:, cols]
                c_o[:, cols] = -w_g[:, cols] * a_h

    shp = jax.ShapeDtypeStruct((S, GW), F32)
    return _pcall(
        body, name=name, grid=(S // T,), in_specs=[_rows(T, GW)] * 7, out_specs=(_rows(T, GW),) * 6,
        out_shape=(shp,) * 6, compiler_params=_cparams("parallel"),
    )(d_attn, *o, *lse)


def _halo_prev(T, H, C, col):
    k = T // H
    return pl.BlockSpec((H, C), lambda i: (jnp.maximum(i * k - 1, 0), col))


def _halo_next(T, H, C, col, n_rows):
    k = T // H
    last = n_rows // H - 1
    return pl.BlockSpec((H, C), lambda i: (jnp.minimum((i + 1) * k, last), col))


def _conv_branch_fwd_tile(cv, cg, hv, hg, w_r, b_r, ubuf, i, C):
    T = cv.shape[0]
    live = jnp.where(i > 0, 1.0, 0.0)
    ubuf[0:CONV_HALO, :] = live * (hv[...] * _sig(hg[...]))
    ubuf[CONV_HALO:CONV_HALO + T, :] = cv[...] * _sig(cg[...])
    y = jnp.zeros((T, C), F32) + b_r[...]
    for k in range(CONV_K):
        off = CONV_HALO - (CONV_K - 1) + k
        y = y + w_r[k:k + 1, :] * ubuf[off:off + T, :]
    return y


def convb_fwd(zc, w, b, g_ln, b_ln, *, name):
    S = zc.shape[0]
    C = zc.shape[1] // 2
    T = ROW_TILE

    def body(cv, cg, hv, hg, w_r, b_r, g_r, bl_r, u_o, ubuf):
        i = pl.program_id(0)
        y = _conv_branch_fwd_tile(cv, cg, hv, hg, w_r, b_r, ubuf, i, C)
        mu = jnp.mean(y, axis=-1, keepdims=True)
        yc = y - mu
        rs = lax.rsqrt(jnp.mean(yc * yc, axis=-1, keepdims=True) + EPS)
        v = yc * rs * g_r[...] + bl_r[...]
        u_o[...] = (v * _sig(v)).astype(BF16)

    vec = _full((1, C))
    return _pcall(
        body, name=name, grid=(S // T,),
        in_specs=[_rows(T, C, 0), _rows(T, C, 1), _halo_prev(T, CONV_HALO, C, 0), _halo_prev(T, CONV_HALO, C, 1),
                  _full((CONV_K, C)), vec, vec, vec],
        out_specs=_rows(T, C), out_shape=jax.ShapeDtypeStruct((S, C), BF16),
        scratch_shapes=[pltpu.VMEM((CONV_HALO + T, C), F32)],
        compiler_params=_cparams("parallel"),
    )(zc, zc, zc, zc, w, b, g_ln, b_ln)


def convb_bwd1(d_u2, zc, w, b, g_ln, b_ln, *, name):
    S = zc.shape[0]
    C = zc.shape[1] // 2
    T = ROW_TILE

    def body(du_r, cv, cg, hv, hg, w_r, b_r, g_r, bl_r, dy_o, dw_o, db_o, dg_o, dbl_o, ubuf):
        i = pl.program_id(0)
        y = _conv_branch_fwd_tile(cv, cg, hv, hg, w_r, b_r, ubuf, i, C)
        mu = jnp.mean(y, axis=-1, keepdims=True)
        yc = y - mu
        rs = lax.rsqrt(jnp.mean(yc * yc, axis=-1, keepdims=True) + EPS)
        yn = yc * rs
        v = yn * g_r[...] + bl_r[...]
        sg = _sig(v)
        dv = du_r[...] * (sg * (1.0 + v * (1.0 - sg)))
        dyn = dv * g_r[...]
        dy = rs * (dyn - jnp.mean(dyn, axis=-1, keepdims=True) - yn * jnp.mean(dyn * yn, axis=-1, keepdims=True))
        dy_o[...] = dy
        _acc_rows(dg_o, jnp.sum(dv * yn, axis=0, keepdims=True), i)
        _acc_rows(dbl_o, jnp.sum(dv, axis=0, keepdims=True), i)
        _acc_rows(db_o, jnp.sum(dy, axis=0, keepdims=True), i)

        @pl.when(i == 0)
        def _():
            dw_o[...] = jnp.zeros_like(dw_o)
        for k in range(CONV_K):
            off = CONV_HALO - (CONV_K - 1) + k
            dw_o[k:k + 1, :] += jnp.sum(dy * ubuf[off:off + T, :], axis=0, keepdims=True)

    vec = _full((1, C))
    part = jax.ShapeDtypeStruct((8, C), F32)
    return _pcall(
        body, name=name, grid=(S // T,),
        in_specs=[_rows(T, C), _rows(T, C, 0), _rows(T, C, 1), _halo_prev(T, CONV_HALO, C, 0),
                  _halo_prev(T, CONV_HALO, C, 1), _full((CONV_K, C)), vec, vec, vec],
        out_specs=(_rows(T, C), _full((32, C)), _full((8, C)), _full((8, C)), _full((8, C))),
        out_shape=(jax.ShapeDtypeStruct((S, C), F32), jax.ShapeDtypeStruct((32, C), F32), part, part, part),
        scratch_shapes=[pltpu.VMEM((CONV_HALO + T, C), F32)],
        compiler_params=_cparams("arbitrary"),
    )(d_u2, zc, zc, zc, zc, w, b, g_ln, b_ln)


def convb_bwd2(dy, zc, w, *, name):
    S = zc.shape[0]
    C = zc.shape[1] // 2
    T = ROW_TILE
    nblk = S // T

    def body(dy_r, dyn_r, cv, cg, w_r, dz_o, dbuf):
        i = pl.program_id(0)
        live = jnp.where(i < nblk - 1, 1.0, 0.0)
        dbuf[0:T, :] = dy_r[...]
        dbuf[T:T + CONV_HALO, :] = live * dyn_r[...]
        du = jnp.zeros((T, C), F32)
        for k in range(CONV_K):
            off = CONV_K - 1 - k
            du = du + w_r[k:k + 1, :] * dbuf[off:off + T, :]
        sg = _sig(cg[...])
        dz_o[:, 0:C] = (du * sg).astype(BF16)
        dz_o[:, C:2 * C] = (du * cv[...] * sg * (1.0 - sg)).astype(BF16)

    return _pcall(
        body, name=name, grid=(nblk,),
        in_specs=[_rows(T, C), _halo_next(T, CONV_HALO, C, 0, S), _rows(T, C, 0), _rows(T, C, 1), _full((CONV_K, C))],
        out_specs=_rows(T, 2 * C), out_shape=jax.ShapeDtypeStruct((S, 2 * C), BF16),
        scratch_shapes=[pltpu.VMEM((T + CONV_HALO, C), F32)],
        compiler_params=_cparams("parallel"),
    )(dy, dy, zc, zc, w)


def merge_fwd(y_a, y_b, zg, *, name):
    S, D = y_a.shape
    T = ROW_TILE

    def body(a_r, b_r, ga_r, gb_r, m_o):
        m_o[...] = (_sig(ga_r[...]) * a_r[...] + _sig(gb_r[...]) * b_r[...]).astype(BF16)

    return _pcall(
        body, name=name, grid=(S // T,),
        in_specs=[_rows(T, D), _rows(T, D), _rows(T, D, 0), _rows(T, D, 1)],
        out_specs=_rows(T, D), out_shape=jax.ShapeDtypeStruct((S, D), BF16),
        compiler_params=_cparams("parallel"),
    )(y_a, y_b, zg, zg)


def merge_bwd(d_m, y_a, y_b, zg, *, name):
    S, D = y_a.shape
    T = ROW_TILE

    def body(dm_r, a_r, b_r, ga_r, gb_r, da_o, db_o, dz_o):
        dm = dm_r[...]
        sa, sb = _sig(ga_r[...]), _sig(gb_r[...])
        da_o[...] = (dm * sa).astype(BF16)
        db_o[...] = (dm * sb).astype(BF16)
        dz_o[:, 0:D] = (dm * a_r[...] * sa * (1.0 - sa)).astype(BF16)
        dz_o[:, D:2 * D] = (dm * b_r[...] * sb * (1.0 - sb)).astype(BF16)

    shp = jax.ShapeDtypeStruct((S, D), BF16)
    return _pcall(
        body, name=name, grid=(S // T,),
        in_specs=[_rows(T, D), _rows(T, D), _rows(T, D), _rows(T, D, 0), _rows(T, D, 1)],
        out_specs=(_rows(T, D), _rows(T, D), _rows(T, 2 * D)),
        out_shape=(shp, shp, jax.ShapeDtypeStruct((S, 2 * D), BF16)),
        compiler_params=_cparams("parallel"),
    )(d_m, y_a, y_b, zg, zg)


def _ffn_gate_tile(g_r, hg_r, w_r, b_r, gbuf, i):
    T, F = g_r.shape
    live = jnp.where(i > 0, 1.0, 0.0)
    gbuf[0:FFN_HALO, :] = live * hg_r[...]
    gbuf[FFN_HALO:FFN_HALO + T, :] = g_r[...]
    gp = jnp.zeros((T, F), F32) + b_r[...]
    for k in range(FFN_K):
        off = FFN_HALO - (FFN_K - 1) + k
        gp = gp + w_r[k:k + 1, :] * gbuf[off:off + T, :]
    return gp


def ffn_act_fwd(gu, w, b, *, name):
    S = gu.shape[0]
    F = gu.shape[1] // 2
    T = ROW_TILE // 2

    def body(g_r, u_r, hg_r, w_r, b_r, a_o, gbuf):
        gp = _ffn_gate_tile(g_r, hg_r, w_r, b_r, gbuf, pl.program_id(0))
        a_o[...] = (gp * _sig(gp) * u_r[...]).astype(BF16)

    return _pcall(
        body, name=name, grid=(S // T,),
        in_specs=[_rows(T, F, 0), _rows(T, F, 1), _halo_prev(T, FFN_HALO, F, 0), _full((FFN_K, F)), _full((1, F))],
        out_specs=_rows(T, F), out_shape=jax.ShapeDtypeStruct((S, F), BF16),
        scratch_shapes=[pltpu.VMEM((FFN_HALO + T, F), F32)],
        compiler_params=_cparams("parallel"),
    )(gu, gu, gu, w, b)


def ffn_act_bwd1(d_a, gu, w, b, *, name):
    S = gu.shape[0]
    F = gu.shape[1] // 2
    T = ROW_TILE // 2

    def body(da_r, g_r, u_r, hg_r, w_r, b_r, dgp_o, dup_o, dw_o, db_o, gbuf):
        i = pl.program_id(0)
        gp = _ffn_gate_tile(g_r, hg_r, w_r, b_r, gbuf, i)
        sg = _sig(gp)
        da = da_r[...]
        dup_o[...] = (da * gp * sg).astype(BF16)
        dgp = da * u_r[...] * (sg * (1.0 + gp * (1.0 - sg)))
        dgp_o[...] = dgp
        _acc_rows(db_o, jnp.sum(dgp, axis=0, keepdims=True), i)

        @pl.when(i == 0)
        def _():
            dw_o[...] = jnp.zeros_like(dw_o)
        for k in range(FFN_K):
            off = FFN_HALO - (FFN_K - 1) + k
            dw_o[k:k + 1, :] += jnp.sum(dgp * gbuf[off:off + T, :], axis=0, keepdims=True)

    part = jax.ShapeDtypeStruct((8, F), F32)
    return _pcall(
        body, name=name, grid=(S // T,),
        in_specs=[_rows(T, F), _rows(T, F, 0), _rows(T, F, 1), _halo_prev(T, FFN_HALO, F, 0),
                  _full((FFN_K, F)), _full((1, F))],
        out_specs=(_rows(T, F), _rows(T, F), _full((8, F)), _full((8, F))),
        out_shape=(jax.ShapeDtypeStruct((S, F), F32), jax.ShapeDtypeStruct((S, F), BF16), part, part),
        scratch_shapes=[pltpu.VMEM((FFN_HALO + T, F), F32)],
        compiler_params=_cparams("arbitrary"),
    )(d_a, gu, gu, gu, w, b)


def ffn_act_bwd2(dgp, dup, w, *, name):
    S, F = dgp.shape
    T = ROW_TILE // 2
    nblk = S // T

    def body(d_r, dn_r, up_r, w_r, o_o, dbuf):
        i = pl.program_id(0)
        live = jnp.where(i < nblk - 1, 1.0, 0.0)
        dbuf[0:T, :] = d_r[...]
        dbuf[T:T + FFN_HALO, :] = live * dn_r[...]
        dg = jnp.zeros((T, F), F32)
        for k in range(FFN_K):
            off = FFN_K - 1 - k
            dg = dg + w_r[k:k + 1, :] * dbuf[off:off + T, :]
        o_o[:, 0:F] = dg.astype(BF16)
        o_o[:, F:2 * F] = up_r[...]

    return _pcall(
        body, name=name, grid=(nblk,),
        in_specs=[_rows(T, F), _halo_next(T, FFN_HALO, F, 0, S), _rows(T, F), _full((FFN_K, F))],
        out_specs=_rows(T, 2 * F), out_shape=jax.ShapeDtypeStruct((S, 2 * F), BF16),
        scratch_shapes=[pltpu.VMEM((T + FFN_HALO, F), F32)],
        compiler_params=_cparams("parallel"),
    )(dgp, dgp, dup, w)


def _row_tile(R, target=512):
    if R <= target:
        return R
    for t in range(target, 7, -8):
        if R % t == 0:
            return t
    return R


def sum_slots(land, *, name):
    _, R, C = land.shape
    T = _row_tile(R)

    def body(l_r, o_o):
        acc = l_r[0]
        for q in range(1, N_DEV):
            acc = acc + l_r[q]
        o_o[...] = acc

    return _pcall(
        body, name=name, grid=(R // T,),
        in_specs=[pl.BlockSpec((N_DEV, T, C), lambda i: (0, i, 0))],
        out_specs=_rows(T, C), out_shape=jax.ShapeDtypeStruct((R, C), F32),
        compiler_params=_cparams("parallel"),
    )(land)


def adamw(w, g, m, v, *, name):
    shape = w.shape
    C = shape[-1]
    R = math.prod(shape[:-1])
    w2, g2, m2, v2 = (t.reshape(R, C) for t in (w, g, m, v))
    T = _row_tile(R)
    c1 = 1.0 - ADAM_B1 ** ADAM_STEP
    c2 = 1.0 - ADAM_B2 ** ADAM_STEP

    def body(w_r, g_r, m_r, v_r, d_o, m_o, v_o):
        gv = g_r[...]
        mn = ADAM_B1 * m_r[...] + (1.0 - ADAM_B1) * gv
        vn = ADAM_B2 * v_r[...] + (1.0 - ADAM_B2) * (gv * gv)
        m_o[...] = mn
        v_o[...] = vn
        d_o[...] = -ADAM_LR * ((mn / c1) / (jnp.sqrt(vn / c2) + ADAM_EPS) + ADAM_WD * w_r[...])

    shp = jax.ShapeDtypeStruct((R, C), F32)
    d, mn, vn = _pcall(
        body, name=name, grid=(R // T,), in_specs=[_rows(T, C)] * 4, out_specs=(_rows(T, C),) * 3,
        out_shape=(shp,) * 3, compiler_params=_cparams("parallel"),
    )(w2, g2, m2, v2)
    return d.reshape(shape), mn.reshape(shape), vn.reshape(shape)


def _my_pos():
    return lax.axis_index("x"), lax.axis_index("y"), lax.axis_index("c")


def all_gather(xs, *, name):
    L, R, C = xs.shape
    HBM = pl.BlockSpec(memory_space=pl.ANY)

    def body(x_ref, out_ref, send_sems, recv_sems, local_sems):
        x, y, c = _my_pos()
        me, sibling = (x, y, c), (x, y, 1 - c)
        chips = [(1 - x, y), (x, 1 - y), (1 - x, 1 - y)]

        def slot(l, p):
            return out_ref.at[l, 4 * p[0] + 2 * p[1] + p[2]]

        def copy(k, l, block, to, src=None):
            return pltpu.make_async_remote_copy(
                src_ref=slot(l, block) if src is None else src, dst_ref=slot(l, block),
                send_sem=send_sems.at[k, l], recv_sem=recv_sems.at[k, l],
                device_id=to, device_id_type=MESH)

        mine = [pltpu.make_async_copy(x_ref.at[l], slot(l, me), local_sems.at[l]) for l in range(L)]
        for cp in mine:
            cp.start()
        first = []
        for l in range(L):
            first.append(copy(0, l, me, sibling, src=x_ref.at[l]))
            first += [copy(1 + j, l, me, (*chip, c), src=x_ref.at[l]) for j, chip in enumerate(chips)]
        for cp in first:
            cp.start()
        passed = []
        for j, chip in enumerate(chips):
            for l in range(L):
                copy(1 + j, l, (*chip, c), me).wait_recv()
                cp = copy(4 + j, l, (*chip, c), sibling)
                cp.start()
                passed.append(cp)
        for l in range(L):
            copy(0, l, sibling, me).wait_recv()
            for j, chip in enumerate(chips):
                copy(4 + j, l, (*chip, 1 - c), me).wait_recv()
        for cp in first + passed:
            cp.wait_send()
        for cp in mine:
            cp.wait()

    return _pcall(
        body, name=name, in_specs=[HBM], out_specs=HBM,
        out_shape=jax.ShapeDtypeStruct((L, N_DEV, R, C), xs.dtype),
        scratch_shapes=[pltpu.SemaphoreType.DMA((7, L)), pltpu.SemaphoreType.DMA((7, L)),
                        pltpu.SemaphoreType.DMA((L,))],
    )(xs)


def scatter_slots(g, *, name):
    L, _, R, C = g.shape
    HBM = pl.BlockSpec(memory_space=pl.ANY)

    def body(g_ref, land_ref, send_sems, recv_sems, local_sems):
        x, y, c = _my_pos()
        me_id = 4 * x + 2 * y + c
        mine = [pltpu.make_async_copy(g_ref.at[l, me_id], land_ref.at[l, me_id], local_sems.at[l])
                for l in range(L)]
        for cp in mine:
            cp.start()
        sends = []
        for msk in range(1, N_DEV):
            px = 1 - x if msk & 4 else x
            py = 1 - y if msk & 2 else y
            pc = 1 - c if msk & 1 else c
            pid = 4 * px + 2 * py + pc
            for l in range(L):
                cp = pltpu.make_async_remote_copy(
                    src_ref=g_ref.at[l, pid], dst_ref=land_ref.at[l, me_id],
                    send_sem=send_sems.at[msk - 1, l], recv_sem=recv_sems.at[msk - 1, l],
                    device_id=(px, py, pc), device_id_type=MESH)
                cp.start()
                sends.append((cp, pltpu.make_async_remote_copy(
                    src_ref=g_ref.at[l, pid], dst_ref=land_ref.at[l, pid],
                    send_sem=send_sems.at[msk - 1, l], recv_sem=recv_sems.at[msk - 1, l],
                    device_id=(px, py, pc), device_id_type=MESH)))
        for cp, rv in sends:
            rv.wait_recv()
        for cp, rv in sends:
            cp.wait_send()
        for cp in mine:
            cp.wait()

    return _pcall(
        body, name=name, in_specs=[HBM], out_specs=HBM,
        out_shape=jax.ShapeDtypeStruct(g.shape, g.dtype),
        scratch_shapes=[pltpu.SemaphoreType.DMA((7, L)), pltpu.SemaphoreType.DMA((7, L)),
                        pltpu.SemaphoreType.DMA((L,))],
    )(g)


def _rope_tables(positions):
    half = ROT_DIM // 2
    inv_freq = ROPE_THETA ** (-jnp.arange(0, ROT_DIM, 2, dtype=F32) / ROT_DIM)
    ang = positions.astype(F32)[:, None] * inv_freq
    cos, sin = jnp.cos(ang), jnp.sin(ang)
    S = positions.shape[0]
    c_t = jnp.concatenate([cos, cos, jnp.ones((S, LANE - ROT_DIM), F32)], axis=1)
    s1_t = jnp.concatenate([-sin, jnp.zeros((S, LANE - half), F32)], axis=1)
    s2_t = jnp.concatenate([jnp.zeros((S, half), F32), sin, jnp.zeros((S, LANE - ROT_DIM), F32)], axis=1)
    return c_t, s1_t, s2_t


def _row(v):
    return v.reshape(1, -1)


def kernel(x, c, positions, w_ada, b_ada, g_norm1, w_in, g_q, g_k, w_attn_proj, w_conv_dw, b_conv_dw, g_conv_ln, b_conv_ln, w_conv_out, w_o, g_norm2, w_ffn_in, w_ffn_dw, b_ffn_dw, w_ffn_down, loss_target, m_w_ada, m_b_ada, m_g_norm1, m_w_in, m_g_q, m_g_k, m_w_attn_proj, m_w_conv_dw, m_b_conv_dw, m_g_conv_ln, m_b_conv_ln, m_w_conv_out, m_w_o, m_g_norm2, m_w_ffn_in, m_w_ffn_dw, m_b_ffn_dw, m_w_ffn_down, v_w_ada, v_b_ada, v_g_norm1, v_w_in, v_g_q, v_g_k, v_w_attn_proj, v_w_conv_dw, v_b_conv_dw, v_g_conv_ln, v_b_conv_ln, v_w_conv_out, v_w_o, v_g_norm2, v_w_ffn_in, v_w_ffn_dw, v_b_ffn_dw, v_w_ffn_down):
    L = w_in.shape[0]
    S, D = x.shape[1], x.shape[2]
    FF = w_ffn_down.shape[1] * N_DEV
    xi, yi, ci = _my_pos()
    me = 4 * xi + 2 * yi + ci
    x0 = x[0]
    tabs = _rope_tables(positions[0])

    c_act = c * _sig(c)
    c_all = all_gather(jnp.pad(c_act, ((0, 7), (0, 0)))[None], name="ag_c")[0][:, 0, :]
    c_all16 = jnp.pad(c_all, ((0, 8), (0, 0)))
    m_part = jnp.stack([mm(c_all16, w_ada[l], "nn", name="mod_mm") for l in range(L)])
    m_all = all_gather(m_part, name="ag_mod")
    mod = lax.dynamic_index_in_dim(m_all, me, axis=2, keepdims=False).reshape(L, 6 * D) + b_ada
    mod = mod.reshape(L, 6, 1, D)

    def gathered(w_sh):
        g = all_gather(w_sh, name="ag_w")
        return g.reshape(L, N_DEV * w_sh.shape[1], w_sh.shape[2])

    wt_in = gathered(jnp.transpose(w_in, (0, 2, 1)).astype(BF16))
    wt_fi = gathered(jnp.transpose(w_ffn_in, (0, 2, 1)).astype(BF16))
    wt_ap = gathered(jnp.transpose(w_attn_proj, (0, 2, 1)).astype(BF16))
    w_co = gathered(w_conv_out.astype(BF16))
    w_oo = gathered(w_o.astype(BF16))
    w_dn = gathered(w_ffn_down.astype(BF16))
    cdw = all_gather(jnp.pad(w_conv_dw, ((0, 0), (0, 1), (0, 0))), name="ag_cdw")
    cdw = jnp.transpose(cdw, (0, 2, 1, 3)).reshape(L, 32, D)[:, :CONV_K]
    fsh = w_ffn_dw.shape[2]
    fpad = -fsh % LANE
    fdw = all_gather(jnp.pad(w_ffn_dw, ((0, 0), (0, 8 - FFN_K), (0, fpad))), name="ag_fdw")
    fdw = jnp.transpose(fdw[:, :, :FFN_K, :fsh], (0, 2, 1, 3)).reshape(L, FFN_K, FF)

    QKV, CW = 3 * ATTN_W, 2 * D
    seg = ((0, QKV), (QKV, CW), (QKV + CW, 2 * D))

    saved = []
    x_prev, delta, gt_prev = x0, None, None
    for l in range(L):
        sh1, sc1, gt1, sh2, sc2, gt2 = (mod[l, i] for i in range(6))
        x_l, h = norm_mod_fwd(x_prev, delta, gt_prev, _row(g_norm1[l]), sc1, sh1, name="norm_fwd")
        zq, zc, zg = (mm(h, wt_in[l], "nt", b_off=o, b_len=n, name="z_mm") for o, n in seg)
        gq, gk = _row(g_q[l]), _row(g_k[l])
        qn, kn = qk_prep_fwd(zq, gq, gk, tabs, name="qk_prep")
        o_g, lse_g = [], []
        for gi in range(3):
            o_i, l_i = attn_fwd(qn, kn, zq, gi, name="attn_fwd%d" % gi)
            o_g.append(o_i)
            lse_g.append(l_i)
        attn = combine_fwd(o_g, lse_g, name="combine_fwd")
        y_a = mm(attn, wt_ap[l], "nt", name="ya_mm")
        cw, cb = cdw[l], _row(b_conv_dw[l])
        cg, cbl = _row(g_conv_ln[l]), _row(b_conv_ln[l])
        u2 = convb_fwd(zc, cw, cb, cg, cbl, name="convb_fwd")
        y_b = mm(u2, w_co[l], "nn", name="yb_mm")
        merged = merge_fwd(y_a, y_b, zg, name="merge_fwd")
        mo = mm(merged, w_oo[l], "nn", name="mo_mm")
        x_mid, h2 = norm_mod_fwd(x_l, mo, gt1, _row(g_norm2[l]), sc2, sh2, name="norm_fwd")
        gu = mm(h2, wt_fi[l], "nt", name="gu_mm")
        fw, fb = fdw[l], _row(b_ffn_dw[l])
        act = ffn_act_fwd(gu, fw, fb, name="ffn_act")
        ffo = mm(act, w_dn[l], "nn", name="ffo_mm")
        saved.append(dict(x=x_l, h=h, zq=zq, zc=zc, zg=zg, qn=qn, kn=kn, o=o_g, lse=lse_g, attn=attn, y_a=y_a,
                          u2=u2, y_b=y_b, merged=merged, mo=mo, x_mid=x_mid, h2=h2, gu=gu, act=act, ffo=ffo))
        x_prev, delta, gt_prev = x_mid, ffo, gt2

    dx, lpart = loss_head(x_prev, delta, gt_prev, loss_target[0], name="loss_head")
    loss = lax.psum(0.5 / D * jnp.sum(lpart[0]), ("x", "y", "c"))

    big = {k: [None] * L for k in ("in", "fi", "ap", "co", "o", "dn")}
    small_rows = []
    for l in reversed(range(L)):
        sv = saved[l]
        sh1, sc1, gt1, sh2, sc2, gt2 = (mod[l, i] for i in range(6))
        d_ffo, p_gt2 = scale_bwd(dx, sv["ffo"], gt2, name="scale_bwd")
        d_act = mm(d_ffo, w_dn[l], "nt", name="dact_mm")
        big["dn"][l] = mm(sv["act"], d_ffo, "tn", name="dwdn_mm")
        fw, fb = fdw[l], _row(b_ffn_dw[l])
        dgp, dup, p_fw, p_fb = ffn_act_bwd1(d_act, sv["gu"], fw, fb, name="ffn_bwd1")
        dgu = ffn_act_bwd2(dgp, dup, fw, name="ffn_bwd2")
        dh2 = mm(dgu, wt_fi[l], "nn", name="dh2_mm")
        big["fi"][l] = mm(dgu, sv["h2"], "tn", name="dwfi_mm")
        dx, p_g2, p_sc2, p_sh2 = norm_mod_bwd(sv["x_mid"], dh2, _row(g_norm2[l]), sc2, sh2, dx, name="norm_bwd")
        d_mo, p_gt1 = scale_bwd(dx, sv["mo"], gt1, name="scale_bwd")
        d_merged = mm(d_mo, w_oo[l], "nt", name="dmerged_mm")
        big["o"][l] = mm(sv["merged"], d_mo, "tn", name="dwo_mm")
        d_ya, d_yb, dzg = merge_bwd(d_merged, sv["y_a"], sv["y_b"], sv["zg"], name="merge_bwd")
        d_attn = mm(d_ya, wt_ap[l], "nn", name="dattn_mm")
        big["ap"][l] = mm(d_ya, sv["attn"], "tn", name="dwap_mm")
        d_u2 = mm(d_yb, w_co[l], "nt", name="du2_mm")
        big["co"][l] = mm(sv["u2"], d_yb, "tn", name="dwco_mm")
        cw, cb = cdw[l], _row(b_conv_dw[l])
        cg, cbl = _row(g_conv_ln[l]), _row(b_conv_ln[l])
        dy, p_cw, p_cb, p_cg, p_cbl = convb_bwd1(d_u2, sv["zc"], cw, cb, cg, cbl, name="convb_bwd1")
        dzc = convb_bwd2(dy, sv["zc"], cw, name="convb_bwd2")
        dd = combine_bwd(d_attn, sv["o"], sv["lse"], name="combine_bwd")
        do_g, cc_g = dd[:3], dd[3:]
        parts = [attn_bwd(sv["qn"], sv["kn"], sv["zq"], do_g[gi], sv["lse"][gi], cc_g[gi], gi,
                          name="attn_bwd%d" % gi) for gi in range(3)]
        gq, gk = _row(g_q[l]), _row(g_k[l])
        dzq, p_gq, p_gk = attn_bwd_post(sv["zq"], gq, gk, tabs, *[[p[i] for p in parts] for i in range(5)],
                                        name="attn_post")
        dh = None
        for dz_s, (o, n) in zip((dzq, dzc, dzg), seg):
            dh = mm(dz_s, wt_in[l], "nn", b_off=o, c_in=dh, name="dh_mm")
        big["in"][l] = jnp.concatenate([mm(dz_s, sv["h"], "tn", name="dwin_mm") for dz_s in (dzq, dzc, dzg)], axis=0)
        dx, p_g1, p_sc1, p_sh1 = norm_mod_bwd(sv["x"], dh, _row(g_norm1[l]), sc1, sh1, dx, name="norm_bwd")

        def row1k(p):
            v = p[0]
            pad = -v.shape[0] % D
            return jnp.pad(v, (0, pad)).reshape(-1, D)

        rows = [row1k(p) for p in (p_sh1, p_sc1, p_gt1, p_sh2, p_sc2, p_gt2, p_g1, p_g2)]
        rows.append(row1k(jnp.concatenate([p_gq, p_gk], axis=1)))
        rows += [row1k(p) for p in (p_cb, p_cg, p_cbl, p_fb)]
        rows.append(p_cw[:CONV_K])
        rows += [row1k(p_fw[k:k + 1]) for k in range(FFN_K)]
        blk = jnp.concatenate(rows, axis=0)
        small_rows.append(jnp.pad(blk, ((0, -blk.shape[0] % 8), (0, 0))))
    small_rows = small_rows[::-1]
    n_small = small_rows[0].shape[0]
    ff_rows = -(-FF // D)

    small = jnp.concatenate(small_rows, axis=0)[None]
    small_all = all_gather(small, name="ag_small")[0]
    small_sum = sum_slots(small_all, name="sum_small").reshape(L, n_small, D)
    small_all = small_all.reshape(N_DEV, L, n_small, D)

    g_b_ada = small_sum[:, 0:6].reshape(L, 6 * D)
    g_g1, g_g2 = small_sum[:, 6], small_sum[:, 7]
    g_gq, g_gk = small_sum[:, 8, 0:LANE], small_sum[:, 8, LANE:2 * LANE]
    g_cb, g_cg, g_cbl = small_sum[:, 9], small_sum[:, 10], small_sum[:, 11]
    r0 = 12
    g_fb = small_sum[:, r0:r0 + ff_rows].reshape(L, -1)[:, :FF]
    r0 += ff_rows
    g_cw_full = small_sum[:, r0:r0 + CONV_K]
    r0 += CONV_K
    g_fw_full = small_sum[:, r0:r0 + FFN_K * ff_rows].reshape(L, FFN_K, -1)[:, :, :FF]
    csh = w_conv_dw.shape[2]
    g_cw = lax.dynamic_slice_in_dim(g_cw_full, me * csh, csh, axis=2)
    g_fw = lax.dynamic_slice_in_dim(g_fw_full, me * fsh, fsh, axis=2)

    ash = w_ada.shape[2]
    dmod_all = small_all[:, :, 0:6].reshape(N_DEV, L, 6 * D)
    dmod_mine = lax.dynamic_slice_in_dim(dmod_all, me * ash, ash, axis=2)
    g_w_ada = jnp.stack([mm(c_all16, jnp.pad(dmod_mine[:, l], ((0, 8), (0, 0))), "tn", name="dwada_mm")
                         for l in range(L)])

    def reduced(key, transposed):
        outs = []
        for g in big[key]:
            R = g.shape[0] // N_DEV
            land = scatter_slots(g.reshape(1, N_DEV, R, g.shape[1]), name="rs_" + key)
            outs.append(sum_slots(land[0], name="sum_" + key))
        out = jnp.stack(outs)
        return jnp.transpose(out, (0, 2, 1)) if transposed else out

    g_w_in = reduced("in", True)
    g_w_fi = reduced("fi", True)
    g_w_ap = reduced("ap", True)
    g_w_co = reduced("co", False)
    g_w_o = reduced("o", False)
    g_w_dn = reduced("dn", False)

    grads = [g_w_ada, g_b_ada, g_g1, g_w_in, g_gq, g_gk, g_w_ap, g_cw, g_cb, g_cg, g_cbl, g_w_co, g_w_o, g_g2,
             g_w_fi, g_fw, g_fb, g_w_dn]
    ws = [w_ada, b_ada, g_norm1, w_in, g_q, g_k, w_attn_proj, w_conv_dw, b_conv_dw, g_conv_ln, b_conv_ln,
          w_conv_out, w_o, g_norm2, w_ffn_in, w_ffn_dw, b_ffn_dw, w_ffn_down]
    ms = [m_w_ada, m_b_ada, m_g_norm1, m_w_in, m_g_q, m_g_k, m_w_attn_proj, m_w_conv_dw, m_b_conv_dw, m_g_conv_ln,
          m_b_conv_ln, m_w_conv_out, m_w_o, m_g_norm2, m_w_ffn_in, m_w_ffn_dw, m_b_ffn_dw, m_w_ffn_down]
    vs = [v_w_ada, v_b_ada, v_g_norm1, v_w_in, v_g_q, v_g_k, v_w_attn_proj, v_w_conv_dw, v_b_conv_dw, v_g_conv_ln,
          v_b_conv_ln, v_w_conv_out, v_w_o, v_g_norm2, v_w_ffn_in, v_w_ffn_dw, v_b_ffn_dw, v_w_ffn_down]
    deltas, new_m, new_v = [], [], []
    for w_i, g_i, m_i, v_i in zip(ws, grads, ms, vs):
        d_i, mn_i, vn_i = adamw(w_i, g_i, m_i, v_i, name="adamw")
        deltas.append(d_i)
        new_m.append(mn_i)
        new_v.append(vn_i)
    return (loss, dx[None], *grads, *deltas, *new_m, *new_v)
```

```python
import functools
import math

import jax
import jax.numpy as jnp
from jax import lax
from jax.experimental import pallas as pl
from jax.experimental.pallas import tpu as pltpu

F32 = jnp.float32
BF16 = jnp.bfloat16
MESH = pl.DeviceIdType.MESH
N_DEV = 8

EPS = 1e-6
HEAD_DIM = 128
BLOCK = 128
DILATIONS = (1, 4, 16)
HEADS_PER_GROUP = 4
N_HEADS = 12
ATTN_W = N_HEADS * HEAD_DIM
ROT_DIM = 32
ROPE_THETA = 500000.0
CONV_K = 31
CONV_HALO = 32
FFN_K = 3
FFN_HALO = 8
NEG = -1e30

ADAM_LR, ADAM_B1, ADAM_B2, ADAM_EPS, ADAM_WD, ADAM_STEP = 0.001, 0.9, 0.999, 1e-08, 0.01, 10

LANE = 128
VMEM_LIMIT = 56 * 1024 * 1024
ROW_TILE = 512


def _pcall(body, **kw):
    return pl.pallas_call(body, **kw)


def _cparams(*sem):
    return pltpu.CompilerParams(dimension_semantics=sem, vmem_limit_bytes=VMEM_LIMIT)


def _sig(v):
    return 1.0 / (1.0 + jnp.exp(-v))


def _divtile(dim, target):
    best = None
    for t in range(LANE, min(dim, target) + 1, LANE):
        if dim % t == 0:
            best = t
    return best or dim


def _rows(t, c, col=0):
    return pl.BlockSpec((t, c), lambda i: (i, col))


def _full(shape):
    nd = len(shape)
    return pl.BlockSpec(shape, lambda i: (0,) * nd)


def _acc_rows(ref, val, i):
    @pl.when(i == 0)
    def _():
        ref[...] = jnp.zeros_like(ref)
    r = val.shape[0]
    ref[0:r, :] += val


def mm(a, b, mode, *, name, out_dtype=F32, c_in=None, b_off=0, b_len=None):
    if mode == "nn":
        M, K = a.shape
        N = b.shape[1]
    elif mode == "nt":
        M, K = a.shape
        N = b_len if b_len is not None else b.shape[0]
    else:
        K, M = a.shape
        N = b.shape[1]
    tm = _divtile(M, 1024)
    g_n = math.gcd(N, b_off) if (mode == "nt" and b_off) else N
    g_k = math.gcd(K, b_off) if (mode == "nn" and b_off) else K
    tn = _divtile(g_n, 1024 if mode != "tn" else 1024)
    tk = _divtile(g_k, 1024)
    nk = K // tk

    def body(*refs):
        if c_in is not None:
            a_ref, b_ref, c_ref, o_ref, acc = refs
        else:
            a_ref, b_ref, o_ref, acc = refs
            c_ref = None
        k = pl.program_id(2)

        @pl.when(k == 0)
        def _():
            if c_ref is None:
                acc[...] = jnp.zeros_like(acc)
            else:
                acc[...] = c_ref[...].astype(F32)

        av = a_ref[...].astype(BF16)
        bv = b_ref[...].astype(BF16)
        if mode == "nn":
            dims = (((1,), (0,)), ((), ()))
        elif mode == "nt":
            dims = (((1,), (1,)), ((), ()))
        else:
            dims = (((0,), (0,)), ((), ()))
        acc[...] += lax.dot_general(av, bv, dims, preferred_element_type=F32)

        @pl.when(k == nk - 1)
        def _():
            o_ref[...] = acc[...].astype(out_dtype)

    if mode == "nn":
        a_spec = pl.BlockSpec((tm, tk), lambda i, j, k: (i, k))
        ob = b_off // tk
        b_spec = pl.BlockSpec((tk, tn), lambda i, j, k: (k + ob, j))
    elif mode == "nt":
        a_spec = pl.BlockSpec((tm, tk), lambda i, j, k: (i, k))
        ob = b_off // tn
        b_spec = pl.BlockSpec((tn, tk), lambda i, j, k: (j + ob, k))
    else:
        a_spec = pl.BlockSpec((tk, tm), lambda i, j, k: (k, i))
        b_spec = pl.BlockSpec((tk, tn), lambda i, j, k: (k, j))
    o_spec = pl.BlockSpec((tm, tn), lambda i, j, k: (i, j))
    in_specs = [a_spec, b_spec]
    args = [a, b]
    if c_in is not None:
        in_specs.append(o_spec)
        args.append(c_in)
    return _pcall(
        body, name=name, grid=(M // tm, N // tn, nk),
        in_specs=in_specs, out_specs=o_spec,
        out_shape=jax.ShapeDtypeStruct((M, N), out_dtype),
        scratch_shapes=[pltpu.VMEM((tm, tn), F32)],
        compiler_params=_cparams("parallel", "parallel", "arbitrary"),
    )(*args)


def norm_mod_fwd(x_prev, delta, gt, g, sc, sh, *, name):
    S, D = x_prev.shape
    T = ROW_TILE
    has_delta = delta is not None

    def body(*refs):
        if has_delta:
            xp, dl, gt_r, g_r, sc_r, sh_r, x_out, h_out = refs
            xv = xp[...] + gt_r[...] * dl[...]
            x_out[...] = xv
        else:
            xp, g_r, sc_r, sh_r, h_out = refs
            xv = xp[...]
        r = lax.rsqrt(jnp.mean(xv * xv, axis=-1, keepdims=True) + EPS)
        h_out[...] = ((xv * r) * g_r[...] * (1.0 + sc_r[...]) + sh_r[...]).astype(BF16)

    vec = _full((1, D))
    if has_delta:
        ins, specs = [x_prev, delta, gt, g, sc, sh], [_rows(T, D), _rows(T, D), vec, vec, vec, vec]
        outs = (jax.ShapeDtypeStruct((S, D), F32), jax.ShapeDtypeStruct((S, D), BF16))
        ospecs = (_rows(T, D), _rows(T, D))
    else:
        ins, specs = [x_prev, g, sc, sh], [_rows(T, D), vec, vec, vec]
        outs = jax.ShapeDtypeStruct((S, D), BF16)
        ospecs = _rows(T, D)
    res = _pcall(body, name=name, grid=(S // T,), in_specs=specs, out_specs=ospecs, out_shape=outs,
                 compiler_params=_cparams("parallel"))(*ins)
    return res if has_delta else (x_prev, res)


def norm_mod_bwd(x, dh, g, sc, sh, dx_res, *, name):
    S, D = x.shape
    T = ROW_TILE

    def body(x_r, dh_r, g_r, sc_r, sh_r, dr_r, dx_o, dg_o, dsc_o, dsh_o):
        i = pl.program_id(0)
        xv = x_r[...]
        dh_v = dh_r[...]
        r = lax.rsqrt(jnp.mean(xv * xv, axis=-1, keepdims=True) + EPS)
        xh = xv * r
        dn = dh_v * (1.0 + sc_r[...])
        dxh = dn * g_r[...]
        dx_o[...] = dr_r[...] + r * (dxh - xh * jnp.mean(dxh * xh, axis=-1, keepdims=True))
        _acc_rows(dg_o, jnp.sum(dn * xh, axis=0, keepdims=True), i)
        _acc_rows(dsc_o, jnp.sum(dh_v * (xh * g_r[...]), axis=0, keepdims=True), i)
        _acc_rows(dsh_o, jnp.sum(dh_v, axis=0, keepdims=True), i)

    vec = _full((1, D))
    part = jax.ShapeDtypeStruct((8, D), F32)
    return _pcall(
        body, name=name, grid=(S // T,),
        in_specs=[_rows(T, D), _rows(T, D), vec, vec, vec, _rows(T, D)],
        out_specs=(_rows(T, D), _full((8, D)), _full((8, D)), _full((8, D))),
        out_shape=(jax.ShapeDtypeStruct((S, D), F32), part, part, part),
        compiler_params=_cparams("arbitrary"),
    )(x, dh, g, sc, sh, dx_res)


def scale_bwd(dx, delta, gt, *, name):
    S, D = dx.shape
    T = ROW_TILE

    def body(dx_r, dl_r, gt_r, dd_o, dgt_o):
        i = pl.program_id(0)
        dv = dx_r[...]
        dd_o[...] = (dv * gt_r[...]).astype(BF16)
        _acc_rows(dgt_o, jnp.sum(dv * dl_r[...], axis=0, keepdims=True), i)

    return _pcall(
        body, name=name, grid=(S // T,),
        in_specs=[_rows(T, D), _rows(T, D), _full((1, D))],
        out_specs=(_rows(T, D), _full((8, D))),
        out_shape=(jax.ShapeDtypeStruct((S, D), BF16), jax.ShapeDtypeStruct((8, D), F32)),
        compiler_params=_cparams("arbitrary"),
    )(dx, delta, gt)


def loss_head(x_mid, ffo, gt, target, *, name):
    S, D = x_mid.shape
    T = ROW_TILE

    def body(x_r, f_r, gt_r, t_r, dy_o, l_o):
        i = pl.program_id(0)
        e = x_r[...] + gt_r[...] * f_r[...] - t_r[...]
        dy_o[...] = e * (1.0 / D)
        _acc_rows(l_o, jnp.sum(e * e, axis=0, keepdims=True), i)

    return _pcall(
        body, name=name, grid=(S // T,),
        in_specs=[_rows(T, D), _rows(T, D), _full((1, D)), _rows(T, D)],
        out_specs=(_rows(T, D), _full((8, D))),
        out_shape=(jax.ShapeDtypeStruct((S, D), F32), jax.ShapeDtypeStruct((8, D), F32)),
        compiler_params=_cparams("arbitrary"),
    )(x_mid, ffo, gt, target)


def _rope(t, c_t, s1_t, s2_t):
    return t * c_t + pltpu.roll(t, LANE - ROT_DIM // 2, 1) * s1_t + pltpu.roll(t, ROT_DIM // 2, 1) * s2_t


def _rope_t(d, c_t, s1_t, s2_t):
    return d * c_t + pltpu.roll(d * s1_t, ROT_DIM // 2, 1) + pltpu.roll(d * s2_t, LANE - ROT_DIM // 2, 1)


def qk_prep_fwd(zq, g_q, g_k, tabs, *, name):
    S = zq.shape[0]
    T = ROW_TILE

    def body(q_r, k_r, gq_r, gk_r, c_r, s1_r, s2_r, qn_o, kn_o):
        c_t, s1_t, s2_t = c_r[...], s1_r[...], s2_r[...]
        for src, g_r, dst in ((q_r, gq_r, qn_o), (k_r, gk_r, kn_o)):
            for h in range(N_HEADS):
                cols = slice(h * HEAD_DIM, (h + 1) * HEAD_DIM)
                t = src[:, cols]
                r = lax.rsqrt(jnp.mean(t * t, axis=-1, keepdims=True) + EPS)
                dst[:, cols] = _rope(t * r * g_r[...], c_t, s1_t, s2_t)

    tab = _rows(T, LANE)
    shp = jax.ShapeDtypeStruct((S, ATTN_W), F32)
    return _pcall(
        body, name=name, grid=(S // T,),
        in_specs=[_rows(T, ATTN_W, 0), _rows(T, ATTN_W, 1), _full((1, LANE)), _full((1, LANE)), tab, tab, tab],
        out_specs=(_rows(T, ATTN_W), _rows(T, ATTN_W)), out_shape=(shp, shp),
        compiler_params=_cparams("parallel"),
    )(zq, zq, g_q, g_k, *tabs)


def attn_bwd_post(zq, g_q, g_k, tabs, dq, dkc, dkp, dvc, dvp, *, name):
    S = zq.shape[0]
    T = BLOCK
    nblk = S // T
    GW = HEADS_PER_GROUP * HEAD_DIM

    def body(*refs):
        q_r, k_r, gq_r, gk_r, c_r, s1_r, s2_r = refs[:7]
        grp = refs[7:7 + 15]
        dz_o, dgq_o, dgk_o = refs[22:]
        i = pl.program_id(0)
        c_t, s1_t, s2_t = c_r[...], s1_r[...], s2_r[...]
        dgq = jnp.zeros((1, LANE), F32)
        dgk = jnp.zeros((1, LANE), F32)
        for gi, d in enumerate(DILATIONS):
            dq_r, dkc_r, dkp_r, dvc_r, dvp_r = grp[5 * gi:5 * gi + 5]
            live = jnp.where(i + d < nblk, 1.0, 0.0)
            for hh in range(HEADS_PER_GROUP):
                h = gi * HEADS_PER_GROUP + hh
                cols = slice(h * HEAD_DIM, (h + 1) * HEAD_DIM)
                gc = slice(hh * HEAD_DIM, (hh + 1) * HEAD_DIM)
                dk_v = dkc_r[:, gc] + live * dkp_r[:, gc]
                dv_v = dvc_r[:, gc] + live * dvp_r[:, gc]
                dz_o[:, 2 * ATTN_W + h * HEAD_DIM:2 * ATTN_W + (h + 1) * HEAD_DIM] = dv_v.astype(BF16)
                for which, (src, g_r, d_out) in enumerate(((q_r, gq_r, dq_r[:, gc]), (k_r, gk_r, dk_v))):
                    t = src[:, cols]
                    r = lax.rsqrt(jnp.mean(t * t, axis=-1, keepdims=True) + EPS)
                    xh = t * r
                    dtn = _rope_t(d_out, c_t, s1_t, s2_t)
                    dxh = dtn * g_r[...]
                    dt = r * (dxh - xh * jnp.mean(dxh * xh, axis=-1, keepdims=True))
                    dz_o[:, which * ATTN_W + h * HEAD_DIM:which * ATTN_W + (h + 1) * HEAD_DIM] = dt.astype(BF16)
                    part = jnp.sum(dtn * xh, axis=0, keepdims=True)
                    if which == 0:
                        dgq = dgq + part
                    else:
                        dgk = dgk + part
        _acc_rows(dgq_o, dgq, i)
        _acc_rows(dgk_o, dgk, i)

    tab = _rows(T, LANE)
    specs = [_rows(T, ATTN_W, 0), _rows(T, ATTN_W, 1), _full((1, LANE)), _full((1, LANE)), tab, tab, tab]
    args = [zq, zq, g_q, g_k, *tabs]
    for gi, d in enumerate(DILATIONS):
        cur = _rows(T, GW)
        nxt = pl.BlockSpec((T, GW), functools.partial(lambda i, d: (jnp.minimum(i + d, nblk - 1), 0), d=d))
        specs += [cur, cur, nxt, cur, nxt]
        args += [dq[gi], dkc[gi], dkp[gi], dvc[gi], dvp[gi]]
    part = jax.ShapeDtypeStruct((8, LANE), F32)
    return _pcall(
        body, name=name, grid=(nblk,), in_specs=specs,
        out_specs=(_rows(T, 3 * ATTN_W), _full((8, LANE)), _full((8, LANE))),
        out_shape=(jax.ShapeDtypeStruct((S, 3 * ATTN_W), BF16), part, part),
        compiler_params=_cparams("arbitrary"),
    )(*args)


def _attn_geometry(d, S):
    R = 4 * BLOCK * d if d < 16 else BLOCK * d
    R = min(R, S)
    return R, R // (BLOCK * d), S // R


def _sub_rows(j, r, d):
    if d == 1:
        return pl.ds(j * BLOCK, BLOCK)
    return pl.ds(j * BLOCK * d + r, BLOCK, stride=d)


def _dot_nt(a, b):
    return lax.dot_general(a, b, (((1,), (1,)), ((), ())), preferred_element_type=F32)


def _dot_tn(a, b):
    return lax.dot_general(a, b, (((0,), (0,)), ((), ())), preferred_element_type=F32)


def _attn_specs(gi, R):
    h0 = gi * HEADS_PER_GROUP
    vcol = 2 * N_HEADS + h0
    cur = lambda off: pl.BlockSpec((R, HEAD_DIM), lambda h, n: (n, off + h))
    prev = lambda off: pl.BlockSpec((R, HEAD_DIM), lambda h, n: (jnp.maximum(n - 1, 0), off + h))
    return [cur(h0), cur(h0), prev(h0), cur(vcol), prev(vcol)]


def attn_fwd(qn, kn, zq, gi, *, name):
    S = qn.shape[0]
    d = DILATIONS[gi]
    R, J, nblk = _attn_geometry(d, S)
    scale = HEAD_DIM ** -0.5
    GW = HEADS_PER_GROUP * HEAD_DIM

    def body(q_r, kc_r, kp_r, vc_r, vp_r, o_o, l_o):
        n = pl.program_id(1)
        qi = lax.broadcasted_iota(jnp.int32, (BLOCK, BLOCK), 0)
        kj = lax.broadcasted_iota(jnp.int32, (BLOCK, BLOCK), 1)
        m_cur = kj <= qi
        m_prev = kj >= qi

        def unit(j, r):
            rq = _sub_rows(j, r, d)
            q = q_r[rq, :].astype(BF16)
            kc = kc_r[rq, :].astype(BF16)
            vc = vc_r[rq, :].astype(BF16)
            if j > 0:
                rp = _sub_rows(j - 1, r, d)
                kp = kc_r[rp, :].astype(BF16)
                vp = vc_r[rp, :].astype(BF16)
                pen = 0.0
            else:
                rp = _sub_rows(J - 1, r, d)
                kp = kp_r[rp, :].astype(BF16)
                vp = vp_r[rp, :].astype(BF16)
                pen = jnp.where(n > 0, 0.0, NEG)
            s_c = jnp.where(m_cur, _dot_nt(q, kc) * scale, NEG)
            s_p = jnp.where(m_prev, _dot_nt(q, kp) * scale, NEG) + pen
            m = jnp.maximum(jnp.max(s_c, axis=-1, keepdims=True), jnp.max(s_p, axis=-1, keepdims=True))
            p_c = jnp.exp(s_c - m)
            p_p = jnp.exp(s_p - m)
            l = jnp.sum(p_c, axis=-1, keepdims=True) + jnp.sum(p_p, axis=-1, keepdims=True)
            o = (jnp.dot(p_c.astype(BF16), vc, preferred_element_type=F32)
                 + jnp.dot(p_p.astype(BF16), vp, preferred_element_type=F32)) / l
            o_o[rq, :] = o
            l_o[rq, :] = jnp.broadcast_to(m + jnp.log(l), (BLOCK, HEAD_DIM))

        for j in range(J):
            if d == 1:
                unit(j, 0)
            else:
                def step(r, carry, j=j):
                    unit(j, r)
                    return carry
                lax.fori_loop(0, d, step, 0)

    ospec = pl.BlockSpec((R, HEAD_DIM), lambda h, n: (n, h))
    shp = jax.ShapeDtypeStruct((S, GW), F32)
    return _pcall(
        body, name=name, grid=(HEADS_PER_GROUP, nblk), in_specs=_attn_specs(gi, R),
        out_specs=(ospec, ospec), out_shape=(shp, shp),
        compiler_params=_cparams("parallel", "arbitrary"),
    )(qn, kn, kn, zq, zq)


def attn_bwd(qn, kn, zq, do, lse, cc, gi, *, name):
    S = qn.shape[0]
    d = DILATIONS[gi]
    R, J, nblk = _attn_geometry(d, S)
    scale = HEAD_DIM ** -0.5
    GW = HEADS_PER_GROUP * HEAD_DIM

    def body(q_r, kc_r, kp_r, vc_r, vp_r, do_r, l_r, c_r, dq_o, dkc_o, dkp_o, dvc_o, dvp_o):
        n = pl.program_id(1)
        qi = lax.broadcasted_iota(jnp.int32, (BLOCK, BLOCK), 0)
        kj = lax.broadcasted_iota(jnp.int32, (BLOCK, BLOCK), 1)
        m_cur = kj <= qi
        m_prev = kj >= qi

        def unit(j, r):
            rq = _sub_rows(j, r, d)
            q = q_r[rq, :].astype(BF16)
            kc = kc_r[rq, :].astype(BF16)
            vc = vc_r[rq, :].astype(BF16)
            if j > 0:
                rp = _sub_rows(j - 1, r, d)
                kp = kc_r[rp, :].astype(BF16)
                vp = vc_r[rp, :].astype(BF16)
                pen = 0.0
            else:
                rp = _sub_rows(J - 1, r, d)
                kp = kp_r[rp, :].astype(BF16)
                vp = vp_r[rp, :].astype(BF16)
                pen = jnp.where(n > 0, 0.0, NEG)
            dov = do_r[rq, :]
            dob = dov.astype(BF16)
            lv = l_r[rq, :]
            cv = c_r[rq, :]
            p_c = jnp.exp(jnp.where(m_cur, _dot_nt(q, kc) * scale, NEG) - lv)
            p_p = jnp.exp(jnp.where(m_prev, _dot_nt(q, kp) * scale, NEG) + pen - lv)
            ds_c = (p_c * (_dot_nt(dob, vc) + cv)).astype(BF16)
            ds_p = (p_p * (_dot_nt(dob, vp) + cv)).astype(BF16)
            dq_o[rq, :] = (jnp.dot(ds_c, kc, preferred_element_type=F32)
                           + jnp.dot(ds_p, kp, preferred_element_type=F32)) * scale
            dkc_o[rq, :] = _dot_tn(ds_c, q) * scale
            dkp_o[rq, :] = _dot_tn(ds_p, q) * scale
            dvc_o[rq, :] = _dot_tn(p_c.astype(BF16), dob)
            dvp_o[rq, :] = _dot_tn(p_p.astype(BF16), dob)

        for j in range(J):
            if d == 1:
                unit(j, 0)
            else:
                def step(r, carry, j=j):
                    unit(j, r)
                    return carry
                lax.fori_loop(0, d, step, 0)

    ospec = pl.BlockSpec((R, HEAD_DIM), lambda h, n: (n, h))
    shp = jax.ShapeDtypeStruct((S, GW), F32)
    return _pcall(
        body, name=name, grid=(HEADS_PER_GROUP, nblk),
        in_specs=_attn_specs(gi, R) + [ospec, ospec, ospec],
        out_specs=(ospec,) * 5, out_shape=(shp,) * 5,
        compiler_params=_cparams("parallel", "arbitrary"),
    )(qn, kn, kn, zq, zq, do, lse, cc)


def combine_fwd(o, lse, *, name):
    S, GW = o[0].shape
    T = ROW_TILE

    def body(o0, o1, o2, l0, l1, l2, a_o):
        m = jnp.maximum(jnp.maximum(l0[...], l1[...]), l2[...])
        e0, e1, e2 = jnp.exp(l0[...] - m), jnp.exp(l1[...] - m), jnp.exp(l2[...] - m)
        a_o[...] = ((e0 * o0[...] + e1 * o1[...] + e2 * o2[...]) / (e0 + e1 + e2)).astype(BF16)

    return _pcall(
        body, name=name, grid=(S // T,), in_specs=[_rows(T, GW)] * 6, out_specs=_rows(T, GW),
        out_shape=jax.ShapeDtypeStruct((S, GW), BF16), compiler_params=_cparams("parallel"),
    )(*o, *lse)


def combine_bwd(d_attn, o, lse, *, name):
    S, GW = d_attn.shape
    T = ROW_TILE

    def body(da_r, o0, o1, o2, l0, l1, l2, d0, d1, d2, c0, c1, c2):
        m = jnp.maximum(jnp.maximum(l0[...], l1[...]), l2[...])
        e0, e1, e2 = jnp.exp(l0[...] - m), jnp.exp(l1[...] - m), jnp.exp(l2[...] - m)
        inv = 1.0 / (e0 + e1 + e2)
        w = (e0 * inv, e1 * inv, e2 * inv)
        da = da_r[...]
        attn = w[0] * o0[...] + w[1] * o1[...] + w[2] * o2[...]
        prod = da * attn
        for hh in range(HEADS_PER_GROUP):
            cols = slice(hh * HEAD_DIM, (hh + 1) * HEAD_DIM)
            a_h = jnp.sum(prod[:, cols], axis=-1, keepdims=True)
            for w_g, d_o, c_o in zip(w, (d0, d1, d2), (c0, c1, c2)):
                d_o[:, cols] = w_g[:, cols] * da[:, cols]
                c_o[:, cols] = -w_g[:, cols] * a_h

    shp = jax.ShapeDtypeStruct((S, GW), F32)
    return _pcall(
        body, name=name, grid=(S // T,), in_specs=[_rows(T, GW)] * 7, out_specs=(_rows(T, GW),) * 6,
        out_shape=(shp,) * 6, compiler_params=_cparams("parallel"),
    )(d_attn, *o, *lse)


def _halo_prev(T, H, C, col):
    k = T // H
    return pl.BlockSpec((H, C), lambda i: (jnp.maximum(i * k - 1, 0), col))


def _halo_next(T, H, C, col, n_rows):
    k = T // H
    last = n_rows // H - 1
    return pl.BlockSpec((H, C), lambda i: (jnp.minimum((i + 1) * k, last), col))


CONV_RC = 64
SUBLANES = 8


def _tap_groups(offs):
    groups = {}
    for k, off in offs:
        groups.setdefault(off % SUBLANES, []).append((k, off))
    return [taps for _, taps in sorted(groups.items())]


def _for_taps(src, r0, lanes, offs, fn):
    for taps in _tap_groups(offs):
        lo = min(off for _, off in taps)
        hi = max(off for _, off in taps)
        sb = src[r0 + lo:r0 + hi + CONV_RC, lanes]
        for k, off in taps:
            fn(k, sb[off - lo:off - lo + CONV_RC])


def _dwconv(src, w_r, offs, T, C, bias_r, dst):
    for rc in range(T // CONV_RC):
        for cc in range(C // LANE):
            lanes = slice(cc * LANE, (cc + 1) * LANE)
            r0 = rc * CONV_RC
            acc = None if bias_r is None else jnp.zeros((CONV_RC, LANE), F32) + bias_r[:, lanes]
            for taps in _tap_groups(offs):
                lo = min(off for _, off in taps)
                hi = max(off for _, off in taps)
                sb = src[r0 + lo:r0 + hi + CONV_RC, lanes]
                g_acc = None
                for k, off in taps:
                    term = w_r[k:k + 1, lanes] * sb[off - lo:off - lo + CONV_RC]
                    g_acc = term if g_acc is None else g_acc + term
                acc = g_acc if acc is None else acc + g_acc
            dst[r0:r0 + CONV_RC, lanes] = acc


def _fill_glu(cv, cg, hv, hg, ubuf, i):
    T = cv.shape[0]
    live = jnp.where(i > 0, 1.0, 0.0)
    ubuf[0:CONV_HALO, :] = live * (hv[...] * _sig(hg[...]))
    ubuf[CONV_HALO:CONV_HALO + T, :] = cv[...] * _sig(cg[...])


_CONV_FWD_OFFS = [(k, CONV_HALO - (CONV_K - 1) + k) for k in range(CONV_K)]
_CONV_BWD_OFFS = [(k, CONV_K - 1 - k) for k in range(CONV_K)]


def convb_fwd(zc, w, b, g_ln, b_ln, *, name):
    S = zc.shape[0]
    C = zc.shape[1] // 2
    T = ROW_TILE

    def body(cv, cg, hv, hg, w_r, b_r, g_r, bl_r, u_o, y_o, ubuf):
        _fill_glu(cv, cg, hv, hg, ubuf, pl.program_id(0))
        _dwconv(ubuf, w_r, _CONV_FWD_OFFS, T, C, b_r, y_o)
        y = y_o[...]
        mu = jnp.mean(y, axis=-1, keepdims=True)
        yc = y - mu
        rs = lax.rsqrt(jnp.mean(yc * yc, axis=-1, keepdims=True) + EPS)
        v = yc * rs * g_r[...] + bl_r[...]
        u_o[...] = (v * _sig(v)).astype(BF16)

    vec = _full((1, C))
    return _pcall(
        body, name=name, grid=(S // T,),
        in_specs=[_rows(T, C, 0), _rows(T, C, 1), _halo_prev(T, CONV_HALO, C, 0), _halo_prev(T, CONV_HALO, C, 1),
                  _full((CONV_K, C)), vec, vec, vec],
        out_specs=(_rows(T, C), _rows(T, C)),
        out_shape=(jax.ShapeDtypeStruct((S, C), BF16), jax.ShapeDtypeStruct((S, C), F32)),
        scratch_shapes=[pltpu.VMEM((CONV_HALO + T, C), F32)],
        compiler_params=_cparams("parallel"),
    )(zc, zc, zc, zc, w, b, g_ln, b_ln)


def convb_bwd1(d_u2, y_conv, zc, g_ln, b_ln, *, name):
    S = zc.shape[0]
    C = zc.shape[1] // 2
    T = ROW_TILE

    def body(du_r, y_r, cv, cg, hv, hg, g_r, bl_r, dy_o, dw_o, db_o, dg_o, dbl_o, ubuf):
        i = pl.program_id(0)
        _fill_glu(cv, cg, hv, hg, ubuf, i)
        y = y_r[...]
        mu = jnp.mean(y, axis=-1, keepdims=True)
        yc = y - mu
        rs = lax.rsqrt(jnp.mean(yc * yc, axis=-1, keepdims=True) + EPS)
        yn = yc * rs
        v = yn * g_r[...] + bl_r[...]
        sg = _sig(v)
        dv = du_r[...] * (sg * (1.0 + v * (1.0 - sg)))
        dyn = dv * g_r[...]
        dy = rs * (dyn - jnp.mean(dyn, axis=-1, keepdims=True) - yn * jnp.mean(dyn * yn, axis=-1, keepdims=True))
        dy_o[...] = dy
        _acc_rows(dg_o, jnp.sum(dv * yn, axis=0, keepdims=True), i)
        _acc_rows(dbl_o, jnp.sum(dv, axis=0, keepdims=True), i)
        _acc_rows(db_o, jnp.sum(dy, axis=0, keepdims=True), i)

        @pl.when(i == 0)
        def _():
            dw_o[...] = jnp.zeros_like(dw_o)
        for cc in range(C // LANE):
            lanes = slice(cc * LANE, (cc + 1) * LANE)
            parts = [jnp.zeros((SUBLANES, LANE), F32) for _ in range(CONV_K)]
            for rc in range(T // CONV_RC):
                r0 = rc * CONV_RC
                dyc = dy_o[r0:r0 + CONV_RC, lanes]

                def tap(k, chunk, parts=parts, dyc=dyc):
                    prod = (dyc * chunk).reshape(CONV_RC // SUBLANES, SUBLANES, LANE)
                    parts[k] = parts[k] + jnp.sum(prod, axis=0)

                _for_taps(ubuf, r0, lanes, _CONV_FWD_OFFS, tap)
            for k in range(CONV_K):
                dw_o[k * SUBLANES:(k + 1) * SUBLANES, lanes] += parts[k]

    vec = _full((1, C))
    part = jax.ShapeDtypeStruct((8, C), F32)
    return _pcall(
        body, name=name, grid=(S // T,),
        in_specs=[_rows(T, C), _rows(T, C), _rows(T, C, 0), _rows(T, C, 1), _halo_prev(T, CONV_HALO, C, 0),
                  _halo_prev(T, CONV_HALO, C, 1), vec, vec],
        out_specs=(_rows(T, C), _full((CONV_K * SUBLANES, C)), _full((8, C)), _full((8, C)), _full((8, C))),
        out_shape=(jax.ShapeDtypeStruct((S, C), F32), jax.ShapeDtypeStruct((CONV_K * SUBLANES, C), F32),
                   part, part, part),
        scratch_shapes=[pltpu.VMEM((CONV_HALO + T, C), F32)],
        compiler_params=_cparams("arbitrary"),
    )(d_u2, y_conv, zc, zc, zc, zc, g_ln, b_ln)


def convb_bwd2(dy, zc, w, *, name):
    S = zc.shape[0]
    C = zc.shape[1] // 2
    T = ROW_TILE
    nblk = S // T

    def body(dy_r, dyn_r, cv, cg, w_r, dz_o, dbuf, dubuf):
        i = pl.program_id(0)
        live = jnp.where(i < nblk - 1, 1.0, 0.0)
        dbuf[0:T, :] = dy_r[...]
        dbuf[T:T + CONV_HALO, :] = live * dyn_r[...]
        _dwconv(dbuf, w_r, _CONV_BWD_OFFS, T, C, None, dubuf)
        du = dubuf[...]
        sg = _sig(cg[...])
        dz_o[:, 0:C] = (du * sg).astype(BF16)
        dz_o[:, C:2 * C] = (du * cv[...] * sg * (1.0 - sg)).astype(BF16)

    return _pcall(
        body, name=name, grid=(nblk,),
        in_specs=[_rows(T, C), _halo_next(T, CONV_HALO, C, 0, S), _rows(T, C, 0), _rows(T, C, 1), _full((CONV_K, C))],
        out_specs=_rows(T, 2 * C), out_shape=jax.ShapeDtypeStruct((S, 2 * C), BF16),
        scratch_shapes=[pltpu.VMEM((T + CONV_HALO, C), F32), pltpu.VMEM((T, C), F32)],
        compiler_params=_cparams("parallel"),
    )(dy, dy, zc, zc, w)


def merge_fwd(y_a, y_b, zg, *, name):
    S, D = y_a.shape
    T = ROW_TILE

    def body(a_r, b_r, ga_r, gb_r, m_o):
        m_o[...] = (_sig(ga_r[...]) * a_r[...] + _sig(gb_r[...]) * b_r[...]).astype(BF16)

    return _pcall(
        body, name=name, grid=(S // T,),
        in_specs=[_rows(T, D), _rows(T, D), _rows(T, D, 0), _rows(T, D, 1)],
        out_specs=_rows(T, D), out_shape=jax.ShapeDtypeStruct((S, D), BF16),
        compiler_params=_cparams("parallel"),
    )(y_a, y_b, zg, zg)


def merge_bwd(d_m, y_a, y_b, zg, *, name):
    S, D = y_a.shape
    T = ROW_TILE

    def body(dm_r, a_r, b_r, ga_r, gb_r, da_o, db_o, dz_o):
        dm = dm_r[...]
        sa, sb = _sig(ga_r[...]), _sig(gb_r[...])
        da_o[...] = (dm * sa).astype(BF16)
        db_o[...] = (dm * sb).astype(BF16)
        dz_o[:, 0:D] = (dm * a_r[...] * sa * (1.0 - sa)).astype(BF16)
        dz_o[:, D:2 * D] = (dm * b_r[...] * sb * (1.0 - sb)).astype(BF16)

    shp = jax.ShapeDtypeStruct((S, D), BF16)
    return _pcall(
        body, name=name, grid=(S // T,),
        in_specs=[_rows(T, D), _rows(T, D), _rows(T, D), _rows(T, D, 0), _rows(T, D, 1)],
        out_specs=(_rows(T, D), _rows(T, D), _rows(T, 2 * D)),
        out_shape=(shp, shp, jax.ShapeDtypeStruct((S, 2 * D), BF16)),
        compiler_params=_cparams("parallel"),
    )(d_m, y_a, y_b, zg, zg)


def _ffn_gate_tile(g_r, hg_r, w_r, b_r, gbuf, i):
    T, F = g_r.shape
    live = jnp.where(i > 0, 1.0, 0.0)
    gbuf[0:FFN_HALO, :] = live * hg_r[...]
    gbuf[FFN_HALO:FFN_HALO + T, :] = g_r[...]
    gp = jnp.zeros((T, F), F32) + b_r[...]
    for k in range(FFN_K):
        off = FFN_HALO - (FFN_K - 1) + k
        gp = gp + w_r[k:k + 1, :] * gbuf[off:off + T, :]
    return gp


def ffn_act_fwd(gu, w, b, *, name):
    S = gu.shape[0]
    F = gu.shape[1] // 2
    T = ROW_TILE // 2

    def body(g_r, u_r, hg_r, w_r, b_r, a_o, gbuf):
        gp = _ffn_gate_tile(g_r, hg_r, w_r, b_r, gbuf, pl.program_id(0))
        a_o[...] = (gp * _sig(gp) * u_r[...]).astype(BF16)

    return _pcall(
        body, name=name, grid=(S // T,),
        in_specs=[_rows(T, F, 0), _rows(T, F, 1), _halo_prev(T, FFN_HALO, F, 0), _full((FFN_K, F)), _full((1, F))],
        out_specs=_rows(T, F), out_shape=jax.ShapeDtypeStruct((S, F), BF16),
        scratch_shapes=[pltpu.VMEM((FFN_HALO + T, F), F32)],
        compiler_params=_cparams("parallel"),
    )(gu, gu, gu, w, b)


def ffn_act_bwd1(d_a, gu, w, b, *, name):
    S = gu.shape[0]
    F = gu.shape[1] // 2
    T = ROW_TILE // 2

    def body(da_r, g_r, u_r, hg_r, w_r, b_r, dgp_o, dup_o, dw_o, db_o, gbuf):
        i = pl.program_id(0)
        gp = _ffn_gate_tile(g_r, hg_r, w_r, b_r, gbuf, i)
        sg = _sig(gp)
        da = da_r[...]
        dup_o[...] = (da * gp * sg).astype(BF16)
        dgp = da * u_r[...] * (sg * (1.0 + gp * (1.0 - sg)))
        dgp_o[...] = dgp
        _acc_rows(db_o, jnp.sum(dgp, axis=0, keepdims=True), i)

        @pl.when(i == 0)
        def _():
            dw_o[...] = jnp.zeros_like(dw_o)
        for k in range(FFN_K):
            off = FFN_HALO - (FFN_K - 1) + k
            dw_o[k:k + 1, :] += jnp.sum(dgp * gbuf[off:off + T, :], axis=0, keepdims=True)

    part = jax.ShapeDtypeStruct((8, F), F32)
    return _pcall(
        body, name=name, grid=(S // T,),
        in_specs=[_rows(T, F), _rows(T, F, 0), _rows(T, F, 1), _halo_prev(T, FFN_HALO, F, 0),
                  _full((FFN_K, F)), _full((1, F))],
        out_specs=(_rows(T, F), _rows(T, F), _full((8, F)), _full((8, F))),
        out_shape=(jax.ShapeDtypeStruct((S, F), F32), jax.ShapeDtypeStruct((S, F), BF16), part, part),
        scratch_shapes=[pltpu.VMEM((FFN_HALO + T, F), F32)],
        compiler_params=_cparams("arbitrary"),
    )(d_a, gu, gu, gu, w, b)


def ffn_act_bwd2(dgp, dup, w, *, name):
    S, F = dgp.shape
    T = ROW_TILE // 2
    nblk = S // T

    def body(d_r, dn_r, up_r, w_r, o_o, dbuf):
        i = pl.program_id(0)
        live = jnp.where(i < nblk - 1, 1.0, 0.0)
        dbuf[0:T, :] = d_r[...]
        dbuf[T:T + FFN_HALO, :] = live * dn_r[...]
        dg = jnp.zeros((T, F), F32)
        for k in range(FFN_K):
            off = FFN_K - 1 - k
            dg = dg + w_r[k:k + 1, :] * dbuf[off:off + T, :]
        o_o[:, 0:F] = dg.astype(BF16)
        o_o[:, F:2 * F] = up_r[...]

    return _pcall(
        body, name=name, grid=(nblk,),
        in_specs=[_rows(T, F), _halo_next(T, FFN_HALO, F, 0, S), _rows(T, F), _full((FFN_K, F))],
        out_specs=_rows(T, 2 * F), out_shape=jax.ShapeDtypeStruct((S, 2 * F), BF16),
        scratch_shapes=[pltpu.VMEM((T + FFN_HALO, F), F32)],
        compiler_params=_cparams("parallel"),
    )(dgp, dgp, dup, w)


def _row_tile(R, target=512):
    if R <= target:
        return R
    for t in range(target, 7, -8):
        if R % t == 0:
            return t
    return R


def sum_slots(land, *, name):
    _, R, C = land.shape
    T = _row_tile(R)

    def body(l_r, o_o):
        acc = l_r[0].astype(F32)
        for q in range(1, N_DEV):
            acc = acc + l_r[q].astype(F32)
        o_o[...] = acc

    return _pcall(
        body, name=name, grid=(R // T,),
        in_specs=[pl.BlockSpec((N_DEV, T, C), lambda i: (0, i, 0))],
        out_specs=_rows(T, C), out_shape=jax.ShapeDtypeStruct((R, C), F32),
        compiler_params=_cparams("parallel"),
    )(land)


def adamw(w, g, m, v, *, name):
    shape = w.shape
    C = shape[-1]
    R = math.prod(shape[:-1])
    w2, g2, m2, v2 = (t.reshape(R, C) for t in (w, g, m, v))
    T = _row_tile(R)
    c1 = 1.0 - ADAM_B1 ** ADAM_STEP
    c2 = 1.0 - ADAM_B2 ** ADAM_STEP

    def body(w_r, g_r, m_r, v_r, d_o, m_o, v_o):
        gv = g_r[...]
        mn = ADAM_B1 * m_r[...] + (1.0 - ADAM_B1) * gv
        vn = ADAM_B2 * v_r[...] + (1.0 - ADAM_B2) * (gv * gv)
        m_o[...] = mn
        v_o[...] = vn
        d_o[...] = -ADAM_LR * ((mn / c1) / (jnp.sqrt(vn / c2) + ADAM_EPS) + ADAM_WD * w_r[...])

    shp = jax.ShapeDtypeStruct((R, C), F32)
    d, mn, vn = _pcall(
        body, name=name, grid=(R // T,), in_specs=[_rows(T, C)] * 4, out_specs=(_rows(T, C),) * 3,
        out_shape=(shp,) * 3, compiler_params=_cparams("parallel"),
    )(w2, g2, m2, v2)
    return d.reshape(shape), mn.reshape(shape), vn.reshape(shape)


def _my_pos():
    return lax.axis_index("x"), lax.axis_index("y"), lax.axis_index("c")


def all_gather(xs, *, name):
    L, R, C = xs.shape
    HBM = pl.BlockSpec(memory_space=pl.ANY)

    def body(x_ref, out_ref, send_sems, recv_sems, local_sems):
        x, y, c = _my_pos()
        me, sibling = (x, y, c), (x, y, 1 - c)
        chips = [(1 - x, y), (x, 1 - y), (1 - x, 1 - y)]

        def slot(l, p):
            return out_ref.at[l, 4 * p[0] + 2 * p[1] + p[2]]

        def copy(k, l, block, to, src=None):
            return pltpu.make_async_remote_copy(
                src_ref=slot(l, block) if src is None else src, dst_ref=slot(l, block),
                send_sem=send_sems.at[k, l], recv_sem=recv_sems.at[k, l],
                device_id=to, device_id_type=MESH)

        mine = [pltpu.make_async_copy(x_ref.at[l], slot(l, me), local_sems.at[l]) for l in range(L)]
        for cp in mine:
            cp.start()
        first = []
        for l in range(L):
            first.append(copy(0, l, me, sibling, src=x_ref.at[l]))
            first += [copy(1 + j, l, me, (*chip, c), src=x_ref.at[l]) for j, chip in enumerate(chips)]
        for cp in first:
            cp.start()
        passed = []
        for j, chip in enumerate(chips):
            for l in range(L):
                copy(1 + j, l, (*chip, c), me).wait_recv()
                cp = copy(4 + j, l, (*chip, c), sibling)
                cp.start()
                passed.append(cp)
        for l in range(L):
            copy(0, l, sibling, me).wait_recv()
            for j, chip in enumerate(chips):
                copy(4 + j, l, (*chip, 1 - c), me).wait_recv()
        for cp in first + passed:
            cp.wait_send()
        for cp in mine:
            cp.wait()

    return _pcall(
        body, name=name, in_specs=[HBM], out_specs=HBM,
        out_shape=jax.ShapeDtypeStruct((L, N_DEV, R, C), xs.dtype),
        scratch_shapes=[pltpu.SemaphoreType.DMA((7, L)), pltpu.SemaphoreType.DMA((7, L)),
                        pltpu.SemaphoreType.DMA((L,))],
    )(xs)


def scatter_slots(g, *, name):
    L, _, R, C = g.shape
    HBM = pl.BlockSpec(memory_space=pl.ANY)

    def body(g_ref, land_ref, send_sems, recv_sems, local_sems):
        x, y, c = _my_pos()
        me_id = 4 * x + 2 * y + c
        mine = [pltpu.make_async_copy(g_ref.at[l, me_id], land_ref.at[l, me_id], local_sems.at[l])
                for l in range(L)]
        for cp in mine:
            cp.start()
        sends = []
        for msk in range(1, N_DEV):
            px = 1 - x if msk & 4 else x
            py = 1 - y if msk & 2 else y
            pc = 1 - c if msk & 1 else c
            pid = 4 * px + 2 * py + pc
            for l in range(L):
                cp = pltpu.make_async_remote_copy(
                    src_ref=g_ref.at[l, pid], dst_ref=land_ref.at[l, me_id],
                    send_sem=send_sems.at[msk - 1, l], recv_sem=recv_sems.at[msk - 1, l],
                    device_id=(px, py, pc), device_id_type=MESH)
                cp.start()
                sends.append((cp, pltpu.make_async_remote_copy(
                    src_ref=g_ref.at[l, pid], dst_ref=land_ref.at[l, pid],
                    send_sem=send_sems.at[msk - 1, l], recv_sem=recv_sems.at[msk - 1, l],
                    device_id=(px, py, pc), device_id_type=MESH)))
        for cp, rv in sends:
            rv.wait_recv()
        for cp, rv in sends:
            cp.wait_send()
        for cp in mine:
            cp.wait()

    return _pcall(
        body, name=name, in_specs=[HBM], out_specs=HBM,
        out_shape=jax.ShapeDtypeStruct(g.shape, g.dtype),
        scratch_shapes=[pltpu.SemaphoreType.DMA((7, L)), pltpu.SemaphoreType.DMA((7, L)),
                        pltpu.SemaphoreType.DMA((L,))],
    )(g)


def _rope_tables(positions):
    half = ROT_DIM // 2
    inv_freq = ROPE_THETA ** (-jnp.arange(0, ROT_DIM, 2, dtype=F32) / ROT_DIM)
    ang = positions.astype(F32)[:, None] * inv_freq
    cos, sin = jnp.cos(ang), jnp.sin(ang)
    S = positions.shape[0]
    c_t = jnp.concatenate([cos, cos, jnp.ones((S, LANE - ROT_DIM), F32)], axis=1)
    s1_t = jnp.concatenate([-sin, jnp.zeros((S, LANE - half), F32)], axis=1)
    s2_t = jnp.concatenate([jnp.zeros((S, half), F32), sin, jnp.zeros((S, LANE - ROT_DIM), F32)], axis=1)
    return c_t, s1_t, s2_t


def _row(v):
    return v.reshape(1, -1)


def kernel(x, c, positions, w_ada, b_ada, g_norm1, w_in, g_q, g_k, w_attn_proj, w_conv_dw, b_conv_dw, g_conv_ln, b_conv_ln, w_conv_out, w_o, g_norm2, w_ffn_in, w_ffn_dw, b_ffn_dw, w_ffn_down, loss_target, m_w_ada, m_b_ada, m_g_norm1, m_w_in, m_g_q, m_g_k, m_w_attn_proj, m_w_conv_dw, m_b_conv_dw, m_g_conv_ln, m_b_conv_ln, m_w_conv_out, m_w_o, m_g_norm2, m_w_ffn_in, m_w_ffn_dw, m_b_ffn_dw, m_w_ffn_down, v_w_ada, v_b_ada, v_g_norm1, v_w_in, v_g_q, v_g_k, v_w_attn_proj, v_w_conv_dw, v_b_conv_dw, v_g_conv_ln, v_b_conv_ln, v_w_conv_out, v_w_o, v_g_norm2, v_w_ffn_in, v_w_ffn_dw, v_b_ffn_dw, v_w_ffn_down):
    L = w_in.shape[0]
    S, D = x.shape[1], x.shape[2]
    FF = w_ffn_down.shape[1] * N_DEV
    xi, yi, ci = _my_pos()
    me = 4 * xi + 2 * yi + ci
    x0 = x[0]
    tabs = _rope_tables(positions[0])

    c_act = c * _sig(c)
    c_all = all_gather(jnp.pad(c_act, ((0, 7), (0, 0)))[None], name="ag_c")[0][:, 0, :]
    c_all16 = jnp.pad(c_all, ((0, 8), (0, 0)))
    m_part = jnp.stack([mm(c_all16, w_ada[l], "nn", name="mod_mm") for l in range(L)])
    m_all = all_gather(m_part, name="ag_mod")
    mod = lax.dynamic_index_in_dim(m_all, me, axis=2, keepdims=False).reshape(L, 6 * D) + b_ada
    mod = mod.reshape(L, 6, 1, D)

    def gathered(w_sh):
        g = all_gather(w_sh, name="ag_w")
        return g.reshape(L, N_DEV * w_sh.shape[1], w_sh.shape[2])

    wt_in = gathered(jnp.transpose(w_in, (0, 2, 1)).astype(BF16))
    wt_fi = gathered(jnp.transpose(w_ffn_in, (0, 2, 1)).astype(BF16))
    wt_ap = gathered(jnp.transpose(w_attn_proj, (0, 2, 1)).astype(BF16))
    w_co = gathered(w_conv_out.astype(BF16))
    w_oo = gathered(w_o.astype(BF16))
    w_dn = gathered(w_ffn_down.astype(BF16))
    cdw = all_gather(jnp.pad(w_conv_dw, ((0, 0), (0, 1), (0, 0))), name="ag_cdw")
    cdw = jnp.transpose(cdw, (0, 2, 1, 3)).reshape(L, 32, D)[:, :CONV_K]
    fsh = w_ffn_dw.shape[2]
    fpad = -fsh % LANE
    fdw = all_gather(jnp.pad(w_ffn_dw, ((0, 0), (0, 8 - FFN_K), (0, fpad))), name="ag_fdw")
    fdw = jnp.transpose(fdw[:, :, :FFN_K, :fsh], (0, 2, 1, 3)).reshape(L, FFN_K, FF)

    QKV, CW = 3 * ATTN_W, 2 * D
    seg = ((0, QKV), (QKV, CW), (QKV + CW, 2 * D))

    saved = []
    x_prev, delta, gt_prev = x0, None, None
    for l in range(L):
        sh1, sc1, gt1, sh2, sc2, gt2 = (mod[l, i] for i in range(6))
        x_l, h = norm_mod_fwd(x_prev, delta, gt_prev, _row(g_norm1[l]), sc1, sh1, name="norm_fwd")
        zq, zc, zg = (mm(h, wt_in[l], "nt", b_off=o, b_len=n, name="z_mm") for o, n in seg)
        gq, gk = _row(g_q[l]), _row(g_k[l])
        qn, kn = qk_prep_fwd(zq, gq, gk, tabs, name="qk_prep")
        o_g, lse_g = [], []
        for gi in range(3):
            o_i, l_i = attn_fwd(qn, kn, zq, gi, name="attn_fwd%d" % gi)
            o_g.append(o_i)
            lse_g.append(l_i)
        attn = combine_fwd(o_g, lse_g, name="combine_fwd")
        y_a = mm(attn, wt_ap[l], "nt", name="ya_mm")
        cw, cb = cdw[l], _row(b_conv_dw[l])
        cg, cbl = _row(g_conv_ln[l]), _row(b_conv_ln[l])
        u2, y_conv = convb_fwd(zc, cw, cb, cg, cbl, name="convb_fwd")
        y_b = mm(u2, w_co[l], "nn", name="yb_mm")
        merged = merge_fwd(y_a, y_b, zg, name="merge_fwd")
        mo = mm(merged, w_oo[l], "nn", name="mo_mm")
        x_mid, h2 = norm_mod_fwd(x_l, mo, gt1, _row(g_norm2[l]), sc2, sh2, name="norm_fwd")
        gu = mm(h2, wt_fi[l], "nt", name="gu_mm")
        fw, fb = fdw[l], _row(b_ffn_dw[l])
        act = ffn_act_fwd(gu, fw, fb, name="ffn_act")
        ffo = mm(act, w_dn[l], "nn", name="ffo_mm")
        saved.append(dict(x=x_l, h=h, zq=zq, zc=zc, zg=zg, qn=qn, kn=kn, o=o_g, lse=lse_g, attn=attn, y_a=y_a,
                          u2=u2, y_conv=y_conv, y_b=y_b, merged=merged, mo=mo, x_mid=x_mid, h2=h2, gu=gu, act=act, ffo=ffo))
        x_prev, delta, gt_prev = x_mid, ffo, gt2

    dx, lpart = loss_head(x_prev, delta, gt_prev, loss_target[0], name="loss_head")
    loss = lax.psum(0.5 / D * jnp.sum(lpart[0]), ("x", "y", "c"))

    big = {k: [None] * L for k in ("in", "fi", "ap", "co", "o", "dn")}
    small_rows = []
    for l in reversed(range(L)):
        sv = saved[l]
        sh1, sc1, gt1, sh2, sc2, gt2 = (mod[l, i] for i in range(6))
        d_ffo, p_gt2 = scale_bwd(dx, sv["ffo"], gt2, name="scale_bwd")
        d_act = mm(d_ffo, w_dn[l], "nt", name="dact_mm")
        big["dn"][l] = mm(sv["act"], d_ffo, "tn", out_dtype=BF16, name="dwdn_mm")
        fw, fb = fdw[l], _row(b_ffn_dw[l])
        dgp, dup, p_fw, p_fb = ffn_act_bwd1(d_act, sv["gu"], fw, fb, name="ffn_bwd1")
        dgu = ffn_act_bwd2(dgp, dup, fw, name="ffn_bwd2")
        dh2 = mm(dgu, wt_fi[l], "nn", name="dh2_mm")
        big["fi"][l] = mm(dgu, sv["h2"], "tn", out_dtype=BF16, name="dwfi_mm")
        dx, p_g2, p_sc2, p_sh2 = norm_mod_bwd(sv["x_mid"], dh2, _row(g_norm2[l]), sc2, sh2, dx, name="norm_bwd")
        d_mo, p_gt1 = scale_bwd(dx, sv["mo"], gt1, name="scale_bwd")
        d_merged = mm(d_mo, w_oo[l], "nt", name="dmerged_mm")
        big["o"][l] = mm(sv["merged"], d_mo, "tn", out_dtype=BF16, name="dwo_mm")
        d_ya, d_yb, dzg = merge_bwd(d_merged, sv["y_a"], sv["y_b"], sv["zg"], name="merge_bwd")
        d_attn = mm(d_ya, wt_ap[l], "nn", name="dattn_mm")
        big["ap"][l] = mm(d_ya, sv["attn"], "tn", out_dtype=BF16, name="dwap_mm")
        d_u2 = mm(d_yb, w_co[l], "nt", name="du2_mm")
        big["co"][l] = mm(sv["u2"], d_yb, "tn", out_dtype=BF16, name="dwco_mm")
        cw, cb = cdw[l], _row(b_conv_dw[l])
        cg, cbl = _row(g_conv_ln[l]), _row(b_conv_ln[l])
        dy, p_cw, p_cb, p_cg, p_cbl = convb_bwd1(d_u2, sv["y_conv"], sv["zc"], cg, cbl, name="convb_bwd1")
        dzc = convb_bwd2(dy, sv["zc"], cw, name="convb_bwd2")
        dd = combine_bwd(d_attn, sv["o"], sv["lse"], name="combine_bwd")
        do_g, cc_g = dd[:3], dd[3:]
        parts = [attn_bwd(sv["qn"], sv["kn"], sv["zq"], do_g[gi], sv["lse"][gi], cc_g[gi], gi,
                          name="attn_bwd%d" % gi) for gi in range(3)]
        gq, gk = _row(g_q[l]), _row(g_k[l])
        dzq, p_gq, p_gk = attn_bwd_post(sv["zq"], gq, gk, tabs, *[[p[i] for p in parts] for i in range(5)],
                                        name="attn_post")
        dh = None
        for dz_s, (o, n) in zip((dzq, dzc, dzg), seg):
            dh = mm(dz_s, wt_in[l], "nn", b_off=o, c_in=dh, name="dh_mm")
        big["in"][l] = jnp.concatenate([mm(dz_s, sv["h"], "tn", out_dtype=BF16, name="dwin_mm")
                                        for dz_s in (dzq, dzc, dzg)], axis=0)
        dx, p_g1, p_sc1, p_sh1 = norm_mod_bwd(sv["x"], dh, _row(g_norm1[l]), sc1, sh1, dx, name="norm_bwd")

        def row1k(p):
            v = p[0]
            pad = -v.shape[0] % D
            return jnp.pad(v, (0, pad)).reshape(-1, D)

        rows = [row1k(p) for p in (p_sh1, p_sc1, p_gt1, p_sh2, p_sc2, p_gt2, p_g1, p_g2)]
        rows.append(row1k(jnp.concatenate([p_gq, p_gk], axis=1)))
        rows += [row1k(p) for p in (p_cb, p_cg, p_cbl, p_fb)]
        rows.append(jnp.sum(p_cw.reshape(CONV_K, SUBLANES, D), axis=1))
        rows += [row1k(p_fw[k:k + 1]) for k in range(FFN_K)]
        blk = jnp.concatenate(rows, axis=0)
        small_rows.append(jnp.pad(blk, ((0, -blk.shape[0] % 8), (0, 0))))
    small_rows = small_rows[::-1]
    n_small = small_rows[0].shape[0]
    ff_rows = -(-FF // D)

    small = jnp.concatenate(small_rows, axis=0)[None]
    small_all = all_gather(small, name="ag_small")[0]
    small_sum = sum_slots(small_all, name="sum_small").reshape(L, n_small, D)
    small_all = small_all.reshape(N_DEV, L, n_small, D)

    g_b_ada = small_sum[:, 0:6].reshape(L, 6 * D)
    g_g1, g_g2 = small_sum[:, 6], small_sum[:, 7]
    g_gq, g_gk = small_sum[:, 8, 0:LANE], small_sum[:, 8, LANE:2 * LANE]
    g_cb, g_cg, g_cbl = small_sum[:, 9], small_sum[:, 10], small_sum[:, 11]
    r0 = 12
    g_fb = small_sum[:, r0:r0 + ff_rows].reshape(L, -1)[:, :FF]
    r0 += ff_rows
    g_cw_full = small_sum[:, r0:r0 + CONV_K]
    r0 += CONV_K
    g_fw_full = small_sum[:, r0:r0 + FFN_K * ff_rows].reshape(L, FFN_K, -1)[:, :, :FF]
    csh = w_conv_dw.shape[2]
    g_cw = lax.dynamic_slice_in_dim(g_cw_full, me * csh, csh, axis=2)
    g_fw = lax.dynamic_slice_in_dim(g_fw_full, me * fsh, fsh, axis=2)

    ash = w_ada.shape[2]
    dmod_all = small_all[:, :, 0:6].reshape(N_DEV, L, 6 * D)
    dmod_mine = lax.dynamic_slice_in_dim(dmod_all, me * ash, ash, axis=2)
    g_w_ada = jnp.stack([mm(c_all16, jnp.pad(dmod_mine[:, l], ((0, 8), (0, 0))), "tn", name="dwada_mm")
                         for l in range(L)])

    def reduced(key, transposed):
        outs = []
        for g in big[key]:
            R = g.shape[0] // N_DEV
            land = scatter_slots(g.reshape(1, N_DEV, R, g.shape[1]), name="rs_" + key)
            outs.append(sum_slots(land[0], name="sum_" + key))
        out = jnp.stack(outs)
        return jnp.transpose(out, (0, 2, 1)) if transposed else out

    g_w_in = reduced("in", True)
    g_w_fi = reduced("fi", True)
    g_w_ap = reduced("ap", True)
    g_w_co = reduced("co", False)
    g_w_o = reduced("o", False)
    g_w_dn = reduced("dn", False)

    grads = [g_w_ada, g_b_ada, g_g1, g_w_in, g_gq, g_gk, g_w_ap, g_cw, g_cb, g_cg, g_cbl, g_w_co, g_w_o, g_g2,
             g_w_fi, g_fw, g_fb, g_w_dn]
    ws = [w_ada, b_ada, g_norm1, w_in, g_q, g_k, w_attn_proj, w_conv_dw, b_conv_dw, g_conv_ln, b_conv_ln,
          w_conv_out, w_o, g_norm2, w_ffn_in, w_ffn_dw, b_ffn_dw, w_ffn_down]
    ms = [m_w_ada, m_b_ada, m_g_norm1, m_w_in, m_g_q, m_g_k, m_w_attn_proj, m_w_conv_dw, m_b_conv_dw, m_g_conv_ln,
          m_b_conv_ln, m_w_conv_out, m_w_o, m_g_norm2, m_w_ffn_in, m_w_ffn_dw, m_b_ffn_dw, m_w_ffn_down]
    vs = [v_w_ada, v_b_ada, v_g_norm1, v_w_in, v_g_q, v_g_k, v_w_attn_proj, v_w_conv_dw, v_b_conv_dw, v_g_conv_ln,
          v_b_conv_ln, v_w_conv_out, v_w_o, v_g_norm2, v_w_ffn_in, v_w_ffn_dw, v_b_ffn_dw, v_w_ffn_down]
    deltas, new_m, new_v = [], [], []
    for w_i, g_i, m_i, v_i in zip(ws, grads, ms, vs):
        d_i, mn_i, vn_i = adamw(w_i, g_i, m_i, v_i, name="adamw")
        deltas.append(d_i)
        new_m.append(mn_i)
        new_v.append(vn_i)
    return (loss, dx[None], *grads, *deltas, *new_m, *new_v)
```

```python
import functools
import math

import jax
import jax.numpy as jnp
from jax import lax
from jax.experimental import pallas as pl
from jax.experimental.pallas import tpu as pltpu

F32 = jnp.float32
BF16 = jnp.bfloat16
MESH = pl.DeviceIdType.MESH
N_DEV = 8

EPS = 1e-6
HEAD_DIM = 128
BLOCK = 128
DILATIONS = (1, 4, 16)
HEADS_PER_GROUP = 4
N_HEADS = 12
ATTN_W = N_HEADS * HEAD_DIM
ROT_DIM = 32
ROPE_THETA = 500000.0
CONV_K = 31
CONV_HALO = 32
FFN_K = 3
FFN_HALO = 8
NEG = -1e30

ADAM_LR, ADAM_B1, ADAM_B2, ADAM_EPS, ADAM_WD, ADAM_STEP = 0.001, 0.9, 0.999, 1e-08, 0.01, 10

LANE = 128
VMEM_LIMIT = 56 * 1024 * 1024
ROW_TILE = 512


def _pcall(body, **kw):
    return pl.pallas_call(body, **kw)


def _cparams(*sem):
    return pltpu.CompilerParams(dimension_semantics=sem, vmem_limit_bytes=VMEM_LIMIT)


def _sig(v):
    return 1.0 / (1.0 + jnp.exp(-v))


def _divtile(dim, target):
    best = None
    for t in range(LANE, min(dim, target) + 1, LANE):
        if dim % t == 0:
            best = t
    return best or dim


def _rows(t, c, col=0):
    return pl.BlockSpec((t, c), lambda i: (i, col))


def _full(shape):
    nd = len(shape)
    return pl.BlockSpec(shape, lambda i: (0,) * nd)


def _acc_rows(ref, val, i):
    @pl.when(i == 0)
    def _():
        ref[...] = jnp.zeros_like(ref)
    r = val.shape[0]
    ref[0:r, :] += val


MM_TILE = 1536


def mm(a, b, mode, *, name, out_dtype=F32, c_in=None, b_off=0, b_len=None, comm=None):
    if mode == "nn":
        M, K = a.shape
        N = b.shape[1]
    elif mode == "nt":
        M, K = a.shape
        N = b_len if b_len is not None else b.shape[0]
    else:
        K, M = a.shape
        N = b.shape[1]
    g_n = math.gcd(N, b_off) if (mode == "nt" and b_off) else N
    g_k = math.gcd(K, b_off) if (mode == "nn" and b_off) else K
    tn = _divtile(g_n, MM_TILE if c_in is None else 1024)
    tk = _divtile(g_k, MM_TILE if mode != "tn" else 1024)
    tm = _divtile(M, MM_TILE if mode == "tn" else (2048 if (tn <= 512 and c_in is None) else 1024))
    gm, gn, nk = M // tm, N // tn, K // tk
    n_ci = 0 if comm is None else len(comm.inputs)
    n_co = 0 if comm is None else len(comm.out_shapes)
    n_x = 2 + (c_in is not None)
    if mode == "nn":
        dims = (((1,), (0,)), ((), ()))
    elif mode == "nt":
        dims = (((1,), (1,)), ((), ()))
    else:
        dims = (((0,), (0,)), ((), ()))

    def body(*refs):
        a_ref, b_ref = refs[0], refs[1]
        c_ref = refs[2] if c_in is not None else None
        c_ins = refs[n_x:n_x + n_ci]
        o_ref = refs[n_x + n_ci]
        c_outs = refs[n_x + n_ci + 1:n_x + n_ci + 1 + n_co]
        rest = refs[n_x + n_ci + 1 + n_co:]
        acc = rest[0] if nk > 1 else None
        sems = rest[1:] if nk > 1 else rest
        i, j, k = pl.program_id(0), pl.program_id(1), pl.program_id(2)

        if comm is not None:
            @pl.when(jnp.logical_and(jnp.logical_and(i == 0, j == 0), k == 0))
            def _():
                comm.start(c_ins, c_outs, sems)

        prod = lax.dot_general(a_ref[...].astype(BF16), b_ref[...].astype(BF16), dims, preferred_element_type=F32)
        if nk == 1:
            if c_ref is not None:
                prod = prod + c_ref[...].astype(F32)
            o_ref[...] = prod.astype(out_dtype)
        else:
            @pl.when(k == 0)
            def _():
                if c_ref is None:
                    acc[...] = prod
                else:
                    acc[...] = prod + c_ref[...].astype(F32)

            @pl.when(k > 0)
            def _():
                acc[...] += prod

            @pl.when(k == nk - 1)
            def _():
                o_ref[...] = acc[...].astype(out_dtype)

        if comm is not None:
            @pl.when(jnp.logical_and(jnp.logical_and(i == gm - 1, j == gn - 1), k == nk - 1))
            def _():
                comm.finish(c_ins, c_outs, sems)

    if mode == "nn":
        a_spec = pl.BlockSpec((tm, tk), lambda i, j, k: (i, k))
        ob = b_off // tk
        b_spec = pl.BlockSpec((tk, tn), lambda i, j, k: (k + ob, j))
    elif mode == "nt":
        a_spec = pl.BlockSpec((tm, tk), lambda i, j, k: (i, k))
        ob = b_off // tn
        b_spec = pl.BlockSpec((tn, tk), lambda i, j, k: (j + ob, k))
    else:
        a_spec = pl.BlockSpec((tk, tm), lambda i, j, k: (k, i))
        b_spec = pl.BlockSpec((tk, tn), lambda i, j, k: (k, j))
    o_spec = pl.BlockSpec((tm, tn), lambda i, j, k: (i, j))
    HBM = pl.BlockSpec(memory_space=pl.ANY)
    in_specs = [a_spec, b_spec]
    args = [a, b]
    if c_in is not None:
        in_specs.append(o_spec)
        args.append(c_in)
    scratch = [pltpu.VMEM((tm, tn), F32)] if nk > 1 else []
    o_shape = jax.ShapeDtypeStruct((M, N), out_dtype)
    if comm is None:
        return _pcall(
            body, name=name, grid=(gm, gn, nk), in_specs=in_specs, out_specs=o_spec, out_shape=o_shape,
            scratch_shapes=scratch, compiler_params=_cparams("parallel", "parallel", "arbitrary"),
        )(*args)
    res = _pcall(
        body, name=name, grid=(gm, gn, nk), in_specs=in_specs + [HBM] * n_ci,
        out_specs=(o_spec, *[HBM] * n_co), out_shape=(o_shape, *comm.out_shapes),
        scratch_shapes=scratch + comm.sems, compiler_params=_cparams("arbitrary", "arbitrary", "arbitrary"),
    )(*args, *comm.inputs)
    return res[0], list(res[1:])


def norm_mod_fwd(x_prev, delta, gt, g, sc, sh, *, name):
    S, D = x_prev.shape
    T = ROW_TILE
    has_delta = delta is not None

    def body(*refs):
        if has_delta:
            xp, dl, gt_r, g_r, sc_r, sh_r, x_out, h_out = refs
            xv = xp[...] + gt_r[...] * dl[...]
            x_out[...] = xv
        else:
            xp, g_r, sc_r, sh_r, h_out = refs
            xv = xp[...]
        r = lax.rsqrt(jnp.mean(xv * xv, axis=-1, keepdims=True) + EPS)
        h_out[...] = ((xv * r) * g_r[...] * (1.0 + sc_r[...]) + sh_r[...]).astype(BF16)

    vec = _full((1, D))
    if has_delta:
        ins, specs = [x_prev, delta, gt, g, sc, sh], [_rows(T, D), _rows(T, D), vec, vec, vec, vec]
        outs = (jax.ShapeDtypeStruct((S, D), F32), jax.ShapeDtypeStruct((S, D), BF16))
        ospecs = (_rows(T, D), _rows(T, D))
    else:
        ins, specs = [x_prev, g, sc, sh], [_rows(T, D), vec, vec, vec]
        outs = jax.ShapeDtypeStruct((S, D), BF16)
        ospecs = _rows(T, D)
    res = _pcall(body, name=name, grid=(S // T,), in_specs=specs, out_specs=ospecs, out_shape=outs,
                 compiler_params=_cparams("parallel"))(*ins)
    return res if has_delta else (x_prev, res)


def norm_mod_bwd(x, dh, g, sc, sh, dx_res, *, name):
    S, D = x.shape
    T = ROW_TILE

    def body(x_r, dh_r, g_r, sc_r, sh_r, dr_r, dx_o, dg_o, dsc_o, dsh_o):
        i = pl.program_id(0)
        xv = x_r[...]
        dh_v = dh_r[...]
        r = lax.rsqrt(jnp.mean(xv * xv, axis=-1, keepdims=True) + EPS)
        xh = xv * r
        dn = dh_v * (1.0 + sc_r[...])
        dxh = dn * g_r[...]
        dx_o[...] = dr_r[...] + r * (dxh - xh * jnp.mean(dxh * xh, axis=-1, keepdims=True))
        _acc_rows(dg_o, jnp.sum(dn * xh, axis=0, keepdims=True), i)
        _acc_rows(dsc_o, jnp.sum(dh_v * (xh * g_r[...]), axis=0, keepdims=True), i)
        _acc_rows(dsh_o, jnp.sum(dh_v, axis=0, keepdims=True), i)

    vec = _full((1, D))
    part = jax.ShapeDtypeStruct((8, D), F32)
    return _pcall(
        body, name=name, grid=(S // T,),
        in_specs=[_rows(T, D), _rows(T, D), vec, vec, vec, _rows(T, D)],
        out_specs=(_rows(T, D), _full((8, D)), _full((8, D)), _full((8, D))),
        out_shape=(jax.ShapeDtypeStruct((S, D), F32), part, part, part),
        compiler_params=_cparams("arbitrary"),
    )(x, dh, g, sc, sh, dx_res)


def scale_bwd(dx, delta, gt, *, name):
    S, D = dx.shape
    T = ROW_TILE

    def body(dx_r, dl_r, gt_r, dd_o, dgt_o):
        i = pl.program_id(0)
        dv = dx_r[...]
        dd_o[...] = (dv * gt_r[...]).astype(BF16)
        _acc_rows(dgt_o, jnp.sum(dv * dl_r[...], axis=0, keepdims=True), i)

    return _pcall(
        body, name=name, grid=(S // T,),
        in_specs=[_rows(T, D), _rows(T, D), _full((1, D))],
        out_specs=(_rows(T, D), _full((8, D))),
        out_shape=(jax.ShapeDtypeStruct((S, D), BF16), jax.ShapeDtypeStruct((8, D), F32)),
        compiler_params=_cparams("arbitrary"),
    )(dx, delta, gt)


def loss_head(x_mid, ffo, gt, target, *, name):
    S, D = x_mid.shape
    T = ROW_TILE

    def body(x_r, f_r, gt_r, t_r, dy_o, l_o):
        i = pl.program_id(0)
        e = x_r[...] + gt_r[...] * f_r[...] - t_r[...]
        dy_o[...] = e * (1.0 / D)
        _acc_rows(l_o, jnp.sum(e * e, axis=0, keepdims=True), i)

    return _pcall(
        body, name=name, grid=(S // T,),
        in_specs=[_rows(T, D), _rows(T, D), _full((1, D)), _rows(T, D)],
        out_specs=(_rows(T, D), _full((8, D))),
        out_shape=(jax.ShapeDtypeStruct((S, D), F32), jax.ShapeDtypeStruct((8, D), F32)),
        compiler_params=_cparams("arbitrary"),
    )(x_mid, ffo, gt, target)


def _rope(t, c_t, s1_t, s2_t):
    return t * c_t + pltpu.roll(t, LANE - ROT_DIM // 2, 1) * s1_t + pltpu.roll(t, ROT_DIM // 2, 1) * s2_t


def _rope_t(d, c_t, s1_t, s2_t):
    return d * c_t + pltpu.roll(d * s1_t, ROT_DIM // 2, 1) + pltpu.roll(d * s2_t, LANE - ROT_DIM // 2, 1)


def qk_prep_fwd(zq, g_q, g_k, tabs, *, name):
    S = zq.shape[0]
    T = ROW_TILE

    def body(q_r, k_r, gq_r, gk_r, c_r, s1_r, s2_r, qn_o, kn_o):
        c_t, s1_t, s2_t = c_r[...], s1_r[...], s2_r[...]
        for src, g_r, dst in ((q_r, gq_r, qn_o), (k_r, gk_r, kn_o)):
            for h in range(N_HEADS):
                cols = slice(h * HEAD_DIM, (h + 1) * HEAD_DIM)
                t = src[:, cols]
                r = lax.rsqrt(jnp.mean(t * t, axis=-1, keepdims=True) + EPS)
                dst[:, cols] = _rope(t * r * g_r[...], c_t, s1_t, s2_t)

    tab = _rows(T, LANE)
    shp = jax.ShapeDtypeStruct((S, ATTN_W), F32)
    return _pcall(
        body, name=name, grid=(S // T,),
        in_specs=[_rows(T, ATTN_W, 0), _rows(T, ATTN_W, 1), _full((1, LANE)), _full((1, LANE)), tab, tab, tab],
        out_specs=(_rows(T, ATTN_W), _rows(T, ATTN_W)), out_shape=(shp, shp),
        compiler_params=_cparams("parallel"),
    )(zq, zq, g_q, g_k, *tabs)


def attn_bwd_post(zq, g_q, g_k, tabs, dq, dkc, dkp, dvc, dvp, *, name):
    S = zq.shape[0]
    T = BLOCK
    nblk = S // T
    GW = HEADS_PER_GROUP * HEAD_DIM

    def body(*refs):
        q_r, k_r, gq_r, gk_r, c_r, s1_r, s2_r = refs[:7]
        grp = refs[7:7 + 15]
        dz_o, dgq_o, dgk_o = refs[22:]
        i = pl.program_id(0)
        c_t, s1_t, s2_t = c_r[...], s1_r[...], s2_r[...]
        dgq = jnp.zeros((1, LANE), F32)
        dgk = jnp.zeros((1, LANE), F32)
        for gi, d in enumerate(DILATIONS):
            dq_r, dkc_r, dkp_r, dvc_r, dvp_r = grp[5 * gi:5 * gi + 5]
            live = jnp.where(i + d < nblk, 1.0, 0.0)
            for hh in range(HEADS_PER_GROUP):
                h = gi * HEADS_PER_GROUP + hh
                cols = slice(h * HEAD_DIM, (h + 1) * HEAD_DIM)
                gc = slice(hh * HEAD_DIM, (hh + 1) * HEAD_DIM)
                dk_v = dkc_r[:, gc] + live * dkp_r[:, gc]
                dv_v = dvc_r[:, gc] + live * dvp_r[:, gc]
                dz_o[:, 2 * ATTN_W + h * HEAD_DIM:2 * ATTN_W + (h + 1) * HEAD_DIM] = dv_v.astype(BF16)
                for which, (src, g_r, d_out) in enumerate(((q_r, gq_r, dq_r[:, gc]), (k_r, gk_r, dk_v))):
                    t = src[:, cols]
                    r = lax.rsqrt(jnp.mean(t * t, axis=-1, keepdims=True) + EPS)
                    xh = t * r
                    dtn = _rope_t(d_out, c_t, s1_t, s2_t)
                    dxh = dtn * g_r[...]
                    dt = r * (dxh - xh * jnp.mean(dxh * xh, axis=-1, keepdims=True))
                    dz_o[:, which * ATTN_W + h * HEAD_DIM:which * ATTN_W + (h + 1) * HEAD_DIM] = dt.astype(BF16)
                    part = jnp.sum(dtn * xh, axis=0, keepdims=True)
                    if which == 0:
                        dgq = dgq + part
                    else:
                        dgk = dgk + part
        _acc_rows(dgq_o, dgq, i)
        _acc_rows(dgk_o, dgk, i)

    tab = _rows(T, LANE)
    specs = [_rows(T, ATTN_W, 0), _rows(T, ATTN_W, 1), _full((1, LANE)), _full((1, LANE)), tab, tab, tab]
    args = [zq, zq, g_q, g_k, *tabs]
    for gi, d in enumerate(DILATIONS):
        cur = _rows(T, GW)
        nxt = pl.BlockSpec((T, GW), functools.partial(lambda i, d: (jnp.minimum(i + d, nblk - 1), 0), d=d))
        specs += [cur, cur, nxt, cur, nxt]
        args += [dq[gi], dkc[gi], dkp[gi], dvc[gi], dvp[gi]]
    part = jax.ShapeDtypeStruct((8, LANE), F32)
    return _pcall(
        body, name=name, grid=(nblk,), in_specs=specs,
        out_specs=(_rows(T, 3 * ATTN_W), _full((8, LANE)), _full((8, LANE))),
        out_shape=(jax.ShapeDtypeStruct((S, 3 * ATTN_W), BF16), part, part),
        compiler_params=_cparams("arbitrary"),
    )(*args)


def _attn_geometry(d, S):
    R = 4 * BLOCK * d if d < 16 else BLOCK * d
    R = min(R, S)
    return R, R // (BLOCK * d), S // R


def _sub_rows(j, r, d):
    if d == 1:
        return pl.ds(j * BLOCK, BLOCK)
    return pl.ds(j * BLOCK * d + r, BLOCK, stride=d)


def _dot_nt(a, b):
    return lax.dot_general(a, b, (((1,), (1,)), ((), ())), preferred_element_type=F32)


def _dot_tn(a, b):
    return lax.dot_general(a, b, (((0,), (0,)), ((), ())), preferred_element_type=F32)


def _attn_specs(gi, R):
    h0 = gi * HEADS_PER_GROUP
    vcol = 2 * N_HEADS + h0
    cur = lambda off: pl.BlockSpec((R, HEAD_DIM), lambda h, n: (n, off + h))
    prev = lambda off: pl.BlockSpec((R, HEAD_DIM), lambda h, n: (jnp.maximum(n - 1, 0), off + h))
    return [cur(h0), cur(h0), prev(h0), cur(vcol), prev(vcol)]


def attn_fwd(qn, kn, zq, gi, *, name):
    S = qn.shape[0]
    d = DILATIONS[gi]
    R, J, nblk = _attn_geometry(d, S)
    scale = HEAD_DIM ** -0.5
    GW = HEADS_PER_GROUP * HEAD_DIM

    def body(q_r, kc_r, kp_r, vc_r, vp_r, o_o, l_o):
        n = pl.program_id(1)
        qi = lax.broadcasted_iota(jnp.int32, (BLOCK, BLOCK), 0)
        kj = lax.broadcasted_iota(jnp.int32, (BLOCK, BLOCK), 1)
        m_cur = kj <= qi
        m_prev = kj >= qi

        def unit(j, r):
            rq = _sub_rows(j, r, d)
            q = q_r[rq, :].astype(BF16)
            kc = kc_r[rq, :].astype(BF16)
            vc = vc_r[rq, :].astype(BF16)
            if j > 0:
                rp = _sub_rows(j - 1, r, d)
                kp = kc_r[rp, :].astype(BF16)
                vp = vc_r[rp, :].astype(BF16)
                pen = 0.0
            else:
                rp = _sub_rows(J - 1, r, d)
                kp = kp_r[rp, :].astype(BF16)
                vp = vp_r[rp, :].astype(BF16)
                pen = jnp.where(n > 0, 0.0, NEG)
            s_c = jnp.where(m_cur, _dot_nt(q, kc) * scale, NEG)
            s_p = jnp.where(m_prev, _dot_nt(q, kp) * scale, NEG) + pen
            m = jnp.maximum(jnp.max(s_c, axis=-1, keepdims=True), jnp.max(s_p, axis=-1, keepdims=True))
            p_c = jnp.exp(s_c - m)
            p_p = jnp.exp(s_p - m)
            l = jnp.sum(p_c, axis=-1, keepdims=True) + jnp.sum(p_p, axis=-1, keepdims=True)
            o = (jnp.dot(p_c.astype(BF16), vc, preferred_element_type=F32)
                 + jnp.dot(p_p.astype(BF16), vp, preferred_element_type=F32)) / l
            o_o[rq, :] = o
            l_o[rq, :] = jnp.broadcast_to(m + jnp.log(l), (BLOCK, HEAD_DIM))

        for j in range(J):
            if d == 1:
                unit(j, 0)
            else:
                def step(r, carry, j=j):
                    unit(j, r)
                    return carry
                lax.fori_loop(0, d, step, 0)

    ospec = pl.BlockSpec((R, HEAD_DIM), lambda h, n: (n, h))
    shp = jax.ShapeDtypeStruct((S, GW), F32)
    return _pcall(
        body, name=name, grid=(HEADS_PER_GROUP, nblk), in_specs=_attn_specs(gi, R),
        out_specs=(ospec, ospec), out_shape=(shp, shp),
        compiler_params=_cparams("parallel", "arbitrary"),
    )(qn, kn, kn, zq, zq)


def attn_bwd(qn, kn, zq, do, lse, cc, gi, *, name):
    S = qn.shape[0]
    d = DILATIONS[gi]
    R, J, nblk = _attn_geometry(d, S)
    scale = HEAD_DIM ** -0.5
    GW = HEADS_PER_GROUP * HEAD_DIM

    def body(q_r, kc_r, kp_r, vc_r, vp_r, do_r, l_r, c_r, dq_o, dkc_o, dkp_o, dvc_o, dvp_o):
        n = pl.program_id(1)
        qi = lax.broadcasted_iota(jnp.int32, (BLOCK, BLOCK), 0)
        kj = lax.broadcasted_iota(jnp.int32, (BLOCK, BLOCK), 1)
        m_cur = kj <= qi
        m_prev = kj >= qi

        def unit(j, r):
            rq = _sub_rows(j, r, d)
            q = q_r[rq, :].astype(BF16)
            kc = kc_r[rq, :].astype(BF16)
            vc = vc_r[rq, :].astype(BF16)
            if j > 0:
                rp = _sub_rows(j - 1, r, d)
                kp = kc_r[rp, :].astype(BF16)
                vp = vc_r[rp, :].astype(BF16)
                pen = 0.0
            else:
                rp = _sub_rows(J - 1, r, d)
                kp = kp_r[rp, :].astype(BF16)
                vp = vp_r[rp, :].astype(BF16)
                pen = jnp.where(n > 0, 0.0, NEG)
            dov = do_r[rq, :]
            dob = dov.astype(BF16)
            lv = l_r[rq, :]
            cv = c_r[rq, :]
            p_c = jnp.exp(jnp.where(m_cur, _dot_nt(q, kc) * scale, NEG) - lv)
            p_p = jnp.exp(jnp.where(m_prev, _dot_nt(q, kp) * scale, NEG) + pen - lv)
            ds_c = (p_c * (_dot_nt(dob, vc) + cv)).astype(BF16)
            ds_p = (p_p * (_dot_nt(dob, vp) + cv)).astype(BF16)
            dq_o[rq, :] = (jnp.dot(ds_c, kc, preferred_element_type=F32)
                           + jnp.dot(ds_p, kp, preferred_element_type=F32)) * scale
            dkc_o[rq, :] = _dot_tn(ds_c, q) * scale
            dkp_o[rq, :] = _dot_tn(ds_p, q) * scale
            dvc_o[rq, :] = _dot_tn(p_c.astype(BF16), dob)
            dvp_o[rq, :] = _dot_tn(p_p.astype(BF16), dob)

        for j in range(J):
            if d == 1:
                unit(j, 0)
            else:
                def step(r, carry, j=j):
                    unit(j, r)
                    return carry
                lax.fori_loop(0, d, step, 0)

    ospec = pl.BlockSpec((R, HEAD_DIM), lambda h, n: (n, h))
    shp = jax.ShapeDtypeStruct((S, GW), F32)
    return _pcall(
        body, name=name, grid=(HEADS_PER_GROUP, nblk),
        in_specs=_attn_specs(gi, R) + [ospec, ospec, ospec],
        out_specs=(ospec,) * 5, out_shape=(shp,) * 5,
        compiler_params=_cparams("parallel", "arbitrary"),
    )(qn, kn, kn, zq, zq, do, lse, cc)


def combine_fwd(o, lse, *, name):
    S, GW = o[0].shape
    T = ROW_TILE

    def body(o0, o1, o2, l0, l1, l2, a_o):
        m = jnp.maximum(jnp.maximum(l0[...], l1[...]), l2[...])
        e0, e1, e2 = jnp.exp(l0[...] - m), jnp.exp(l1[...] - m), jnp.exp(l2[...] - m)
        a_o[...] = ((e0 * o0[...] + e1 * o1[...] + e2 * o2[...]) / (e0 + e1 + e2)).astype(BF16)

    return _pcall(
        body, name=name, grid=(S // T,), in_specs=[_rows(T, GW)] * 6, out_specs=_rows(T, GW),
        out_shape=jax.ShapeDtypeStruct((S, GW), BF16), compiler_params=_cparams("parallel"),
    )(*o, *lse)


def combine_bwd(d_attn, o, lse, *, name):
    S, GW = d_attn.shape
    T = ROW_TILE

    def body(da_r, o0, o1, o2, l0, l1, l2, d0, d1, d2, c0, c1, c2):
        m = jnp.maximum(jnp.maximum(l0[...], l1[...]), l2[...])
        e0, e1, e2 = jnp.exp(l0[...] - m), jnp.exp(l1[...] - m), jnp.exp(l2[...] - m)
        inv = 1.0 / (e0 + e1 + e2)
        w = (e0 * inv, e1 * inv, e2 * inv)
        da = da_r[...]
        attn = w[0] * o0[...] + w[1] * o1[...] + w[2] * o2[...]
        prod = da * attn
        for hh in range(HEADS_PER_GROUP):
            cols = slice(hh * HEAD_DIM, (hh + 1) * HEAD_DIM)
            a_h = jnp.sum(prod[:, cols], axis=-1, keepdims=True)
            for w_g, d_o, c_o in zip(w, (d0, d1, d2), (c0, c1, c2)):
                d_o[:, cols] = w_g[:, cols] * da[:, cols]
                c_o[:, cols] = -w_g[:, cols] * a_h

    shp = jax.ShapeDtypeStruct((S, GW), F32)
    return _pcall(
        body, name=name, grid=(S // T,), in_specs=[_rows(T, GW)] * 7, out_specs=(_rows(T, GW),) * 6,
        out_shape=(shp,) * 6, compiler_params=_cparams("parallel"),
    )(d_attn, *o, *lse)


def _halo_prev(T, H, C, col):
    k = T // H
    return pl.BlockSpec((H, C), lambda i: (jnp.maximum(i * k - 1, 0), col))


def _halo_next(T, H, C, col, n_rows):
    k = T // H
    last = n_rows // H - 1
    return pl.BlockSpec((H, C), lambda i: (jnp.minimum((i + 1) * k, last), col))


CONV_RC = 64
SUBLANES = 8


def _tap_groups(offs):
    groups = {}
    for k, off in offs:
        groups.setdefault(off % SUBLANES, []).append((k, off))
    return [taps for _, taps in sorted(groups.items())]


def _for_taps(src, r0, lanes, offs, fn):
    for taps in _tap_groups(offs):
        lo = min(off for _, off in taps)
        hi = max(off for _, off in taps)
        sb = src[r0 + lo:r0 + hi + CONV_RC, lanes]
        for k, off in taps:
            fn(k, sb[off - lo:off - lo + CONV_RC])


def _dwconv(src, w_r, offs, T, C, bias_r, dst):
    for rc in range(T // CONV_RC):
        for cc in range(C // LANE):
            lanes = slice(cc * LANE, (cc + 1) * LANE)
            r0 = rc * CONV_RC
            acc = None if bias_r is None else jnp.zeros((CONV_RC, LANE), F32) + bias_r[:, lanes]
            for taps in _tap_groups(offs):
                lo = min(off for _, off in taps)
                hi = max(off for _, off in taps)
                sb = src[r0 + lo:r0 + hi + CONV_RC, lanes]
                g_acc = None
                for k, off in taps:
                    term = w_r[k:k + 1, lanes] * sb[off - lo:off - lo + CONV_RC]
                    g_acc = term if g_acc is None else g_acc + term
                acc = g_acc if acc is None else acc + g_acc
            dst[r0:r0 + CONV_RC, lanes] = acc


def _fill_glu(cv, cg, hv, hg, ubuf, i):
    T = cv.shape[0]
    live = jnp.where(i > 0, 1.0, 0.0)
    ubuf[0:CONV_HALO, :] = live * (hv[...] * _sig(hg[...]))
    ubuf[CONV_HALO:CONV_HALO + T, :] = cv[...] * _sig(cg[...])


_CONV_FWD_OFFS = [(k, CONV_HALO - (CONV_K - 1) + k) for k in range(CONV_K)]
_CONV_BWD_OFFS = [(k, CONV_K - 1 - k) for k in range(CONV_K)]


def convb_fwd(zc, w, b, g_ln, b_ln, *, name):
    S = zc.shape[0]
    C = zc.shape[1] // 2
    T = ROW_TILE

    def body(cv, cg, hv, hg, w_r, b_r, g_r, bl_r, u_o, y_o, ubuf):
        _fill_glu(cv, cg, hv, hg, ubuf, pl.program_id(0))
        _dwconv(ubuf, w_r, _CONV_FWD_OFFS, T, C, b_r, y_o)
        y = y_o[...]
        mu = jnp.mean(y, axis=-1, keepdims=True)
        yc = y - mu
        rs = lax.rsqrt(jnp.mean(yc * yc, axis=-1, keepdims=True) + EPS)
        v = yc * rs * g_r[...] + bl_r[...]
        u_o[...] = (v * _sig(v)).astype(BF16)

    vec = _full((1, C))
    return _pcall(
        body, name=name, grid=(S // T,),
        in_specs=[_rows(T, C, 0), _rows(T, C, 1), _halo_prev(T, CONV_HALO, C, 0), _halo_prev(T, CONV_HALO, C, 1),
                  _full((CONV_K, C)), vec, vec, vec],
        out_specs=(_rows(T, C), _rows(T, C)),
        out_shape=(jax.ShapeDtypeStruct((S, C), BF16), jax.ShapeDtypeStruct((S, C), F32)),
        scratch_shapes=[pltpu.VMEM((CONV_HALO + T, C), F32)],
        compiler_params=_cparams("parallel"),
    )(zc, zc, zc, zc, w, b, g_ln, b_ln)


def convb_bwd1(d_u2, y_conv, zc, g_ln, b_ln, *, name):
    S = zc.shape[0]
    C = zc.shape[1] // 2
    T = ROW_TILE

    def body(du_r, y_r, cv, cg, hv, hg, g_r, bl_r, dy_o, dw_o, db_o, dg_o, dbl_o, ubuf):
        i = pl.program_id(0)
        _fill_glu(cv, cg, hv, hg, ubuf, i)
        y = y_r[...]
        mu = jnp.mean(y, axis=-1, keepdims=True)
        yc = y - mu
        rs = lax.rsqrt(jnp.mean(yc * yc, axis=-1, keepdims=True) + EPS)
        yn = yc * rs
        v = yn * g_r[...] + bl_r[...]
        sg = _sig(v)
        dv = du_r[...] * (sg * (1.0 + v * (1.0 - sg)))
        dyn = dv * g_r[...]
        dy = rs * (dyn - jnp.mean(dyn, axis=-1, keepdims=True) - yn * jnp.mean(dyn * yn, axis=-1, keepdims=True))
        dy_o[...] = dy
        _acc_rows(dg_o, jnp.sum(dv * yn, axis=0, keepdims=True), i)
        _acc_rows(dbl_o, jnp.sum(dv, axis=0, keepdims=True), i)
        _acc_rows(db_o, jnp.sum(dy, axis=0, keepdims=True), i)

        @pl.when(i == 0)
        def _():
            dw_o[...] = jnp.zeros_like(dw_o)
        for cc in range(C // LANE):
            lanes = slice(cc * LANE, (cc + 1) * LANE)
            parts = [jnp.zeros((SUBLANES, LANE), F32) for _ in range(CONV_K)]
            for rc in range(T // CONV_RC):
                r0 = rc * CONV_RC
                dyc = dy_o[r0:r0 + CONV_RC, lanes]

                def tap(k, chunk, parts=parts, dyc=dyc):
                    prod = (dyc * chunk).reshape(CONV_RC // SUBLANES, SUBLANES, LANE)
                    parts[k] = parts[k] + jnp.sum(prod, axis=0)

                _for_taps(ubuf, r0, lanes, _CONV_FWD_OFFS, tap)
            for k in range(CONV_K):
                dw_o[k * SUBLANES:(k + 1) * SUBLANES, lanes] += parts[k]

    vec = _full((1, C))
    part = jax.ShapeDtypeStruct((8, C), F32)
    return _pcall(
        body, name=name, grid=(S // T,),
        in_specs=[_rows(T, C), _rows(T, C), _rows(T, C, 0), _rows(T, C, 1), _halo_prev(T, CONV_HALO, C, 0),
                  _halo_prev(T, CONV_HALO, C, 1), vec, vec],
        out_specs=(_rows(T, C), _full((CONV_K * SUBLANES, C)), _full((8, C)), _full((8, C)), _full((8, C))),
        out_shape=(jax.ShapeDtypeStruct((S, C), F32), jax.ShapeDtypeStruct((CONV_K * SUBLANES, C), F32),
                   part, part, part),
        scratch_shapes=[pltpu.VMEM((CONV_HALO + T, C), F32)],
        compiler_params=_cparams("arbitrary"),
    )(d_u2, y_conv, zc, zc, zc, zc, g_ln, b_ln)


def convb_bwd2(dy, zc, w, *, name):
    S = zc.shape[0]
    C = zc.shape[1] // 2
    T = ROW_TILE
    nblk = S // T

    def body(dy_r, dyn_r, cv, cg, w_r, dz_o, dbuf, dubuf):
        i = pl.program_id(0)
        live = jnp.where(i < nblk - 1, 1.0, 0.0)
        dbuf[0:T, :] = dy_r[...]
        dbuf[T:T + CONV_HALO, :] = live * dyn_r[...]
        _dwconv(dbuf, w_r, _CONV_BWD_OFFS, T, C, None, dubuf)
        du = dubuf[...]
        sg = _sig(cg[...])
        dz_o[:, 0:C] = (du * sg).astype(BF16)
        dz_o[:, C:2 * C] = (du * cv[...] * sg * (1.0 - sg)).astype(BF16)

    return _pcall(
        body, name=name, grid=(nblk,),
        in_specs=[_rows(T, C), _halo_next(T, CONV_HALO, C, 0, S), _rows(T, C, 0), _rows(T, C, 1), _full((CONV_K, C))],
        out_specs=_rows(T, 2 * C), out_shape=jax.ShapeDtypeStruct((S, 2 * C), BF16),
        scratch_shapes=[pltpu.VMEM((T + CONV_HALO, C), F32), pltpu.VMEM((T, C), F32)],
        compiler_params=_cparams("parallel"),
    )(dy, dy, zc, zc, w)


def merge_fwd(y_a, y_b, zg, *, name):
    S, D = y_a.shape
    T = ROW_TILE

    def body(a_r, b_r, ga_r, gb_r, m_o):
        m_o[...] = (_sig(ga_r[...]) * a_r[...] + _sig(gb_r[...]) * b_r[...]).astype(BF16)

    return _pcall(
        body, name=name, grid=(S // T,),
        in_specs=[_rows(T, D), _rows(T, D), _rows(T, D, 0), _rows(T, D, 1)],
        out_specs=_rows(T, D), out_shape=jax.ShapeDtypeStruct((S, D), BF16),
        compiler_params=_cparams("parallel"),
    )(y_a, y_b, zg, zg)


def merge_bwd(d_m, y_a, y_b, zg, *, name):
    S, D = y_a.shape
    T = ROW_TILE

    def body(dm_r, a_r, b_r, ga_r, gb_r, da_o, db_o, dz_o):
        dm = dm_r[...]
        sa, sb = _sig(ga_r[...]), _sig(gb_r[...])
        da_o[...] = (dm * sa).astype(BF16)
        db_o[...] = (dm * sb).astype(BF16)
        dz_o[:, 0:D] = (dm * a_r[...] * sa * (1.0 - sa)).astype(BF16)
        dz_o[:, D:2 * D] = (dm * b_r[...] * sb * (1.0 - sb)).astype(BF16)

    shp = jax.ShapeDtypeStruct((S, D), BF16)
    return _pcall(
        body, name=name, grid=(S // T,),
        in_specs=[_rows(T, D), _rows(T, D), _rows(T, D), _rows(T, D, 0), _rows(T, D, 1)],
        out_specs=(_rows(T, D), _rows(T, D), _rows(T, 2 * D)),
        out_shape=(shp, shp, jax.ShapeDtypeStruct((S, 2 * D), BF16)),
        compiler_params=_cparams("parallel"),
    )(d_m, y_a, y_b, zg, zg)


def _ffn_gate_tile(g_r, hg_r, w_r, b_r, gbuf, i):
    T, F = g_r.shape
    live = jnp.where(i > 0, 1.0, 0.0)
    gbuf[0:FFN_HALO, :] = live * hg_r[...]
    gbuf[FFN_HALO:FFN_HALO + T, :] = g_r[...]
    gp = jnp.zeros((T, F), F32) + b_r[...]
    for k in range(FFN_K):
        off = FFN_HALO - (FFN_K - 1) + k
        gp = gp + w_r[k:k + 1, :] * gbuf[off:off + T, :]
    return gp


def ffn_act_fwd(gu, w, b, *, name):
    S = gu.shape[0]
    F = gu.shape[1] // 2
    T = ROW_TILE // 2

    def body(g_r, u_r, hg_r, w_r, b_r, a_o, gbuf):
        gp = _ffn_gate_tile(g_r, hg_r, w_r, b_r, gbuf, pl.program_id(0))
        a_o[...] = (gp * _sig(gp) * u_r[...]).astype(BF16)

    return _pcall(
        body, name=name, grid=(S // T,),
        in_specs=[_rows(T, F, 0), _rows(T, F, 1), _halo_prev(T, FFN_HALO, F, 0), _full((FFN_K, F)), _full((1, F))],
        out_specs=_rows(T, F), out_shape=jax.ShapeDtypeStruct((S, F), BF16),
        scratch_shapes=[pltpu.VMEM((FFN_HALO + T, F), F32)],
        compiler_params=_cparams("parallel"),
    )(gu, gu, gu, w, b)


def ffn_act_bwd1(d_a, gu, w, b, *, name):
    S = gu.shape[0]
    F = gu.shape[1] // 2
    T = ROW_TILE // 2

    def body(da_r, g_r, u_r, hg_r, w_r, b_r, dgp_o, dup_o, dw_o, db_o, gbuf):
        i = pl.program_id(0)
        gp = _ffn_gate_tile(g_r, hg_r, w_r, b_r, gbuf, i)
        sg = _sig(gp)
        da = da_r[...]
        dup_o[...] = (da * gp * sg).astype(BF16)
        dgp = da * u_r[...] * (sg * (1.0 + gp * (1.0 - sg)))
        dgp_o[...] = dgp
        _acc_rows(db_o, jnp.sum(dgp, axis=0, keepdims=True), i)

        @pl.when(i == 0)
        def _():
            dw_o[...] = jnp.zeros_like(dw_o)
        for k in range(FFN_K):
            off = FFN_HALO - (FFN_K - 1) + k
            dw_o[k:k + 1, :] += jnp.sum(dgp * gbuf[off:off + T, :], axis=0, keepdims=True)

    part = jax.ShapeDtypeStruct((8, F), F32)
    return _pcall(
        body, name=name, grid=(S // T,),
        in_specs=[_rows(T, F), _rows(T, F, 0), _rows(T, F, 1), _halo_prev(T, FFN_HALO, F, 0),
                  _full((FFN_K, F)), _full((1, F))],
        out_specs=(_rows(T, F), _rows(T, F), _full((8, F)), _full((8, F))),
        out_shape=(jax.ShapeDtypeStruct((S, F), F32), jax.ShapeDtypeStruct((S, F), BF16), part, part),
        scratch_shapes=[pltpu.VMEM((FFN_HALO + T, F), F32)],
        compiler_params=_cparams("arbitrary"),
    )(d_a, gu, gu, gu, w, b)


def ffn_act_bwd2(dgp, dup, w, *, name):
    S, F = dgp.shape
    T = ROW_TILE // 2
    nblk = S // T

    def body(d_r, dn_r, up_r, w_r, o_o, dbuf):
        i = pl.program_id(0)
        live = jnp.where(i < nblk - 1, 1.0, 0.0)
        dbuf[0:T, :] = d_r[...]
        dbuf[T:T + FFN_HALO, :] = live * dn_r[...]
        dg = jnp.zeros((T, F), F32)
        for k in range(FFN_K):
            off = FFN_K - 1 - k
            dg = dg + w_r[k:k + 1, :] * dbuf[off:off + T, :]
        o_o[:, 0:F] = dg.astype(BF16)
        o_o[:, F:2 * F] = up_r[...]

    return _pcall(
        body, name=name, grid=(nblk,),
        in_specs=[_rows(T, F), _halo_next(T, FFN_HALO, F, 0, S), _rows(T, F), _full((FFN_K, F))],
        out_specs=_rows(T, 2 * F), out_shape=jax.ShapeDtypeStruct((S, 2 * F), BF16),
        scratch_shapes=[pltpu.VMEM((T + FFN_HALO, F), F32)],
        compiler_params=_cparams("parallel"),
    )(dgp, dgp, dup, w)


def _row_tile(R, target=512):
    if R <= target:
        return R
    for t in range(target, 7, -8):
        if R % t == 0:
            return t
    return R


def sum_slots(land, *, name):
    _, R, C = land.shape
    T = _row_tile(R)

    def body(l_r, o_o):
        acc = l_r[0].astype(F32)
        for q in range(1, N_DEV):
            acc = acc + l_r[q].astype(F32)
        o_o[...] = acc

    return _pcall(
        body, name=name, grid=(R // T,),
        in_specs=[pl.BlockSpec((N_DEV, T, C), lambda i: (0, i, 0))],
        out_specs=_rows(T, C), out_shape=jax.ShapeDtypeStruct((R, C), F32),
        compiler_params=_cparams("parallel"),
    )(land)


def adamw(w, g, m, v, *, name):
    shape = w.shape
    C = shape[-1]
    R = math.prod(shape[:-1])
    w2, g2, m2, v2 = (t.reshape(R, C) for t in (w, g, m, v))
    T = _row_tile(R)
    c1 = 1.0 - ADAM_B1 ** ADAM_STEP
    c2 = 1.0 - ADAM_B2 ** ADAM_STEP

    def body(w_r, g_r, m_r, v_r, d_o, m_o, v_o):
        gv = g_r[...]
        mn = ADAM_B1 * m_r[...] + (1.0 - ADAM_B1) * gv
        vn = ADAM_B2 * v_r[...] + (1.0 - ADAM_B2) * (gv * gv)
        m_o[...] = mn
        v_o[...] = vn
        d_o[...] = -ADAM_LR * ((mn / c1) / (jnp.sqrt(vn / c2) + ADAM_EPS) + ADAM_WD * w_r[...])

    shp = jax.ShapeDtypeStruct((R, C), F32)
    d, mn, vn = _pcall(
        body, name=name, grid=(R // T,), in_specs=[_rows(T, C)] * 4, out_specs=(_rows(T, C),) * 3,
        out_shape=(shp,) * 3, compiler_params=_cparams("parallel"),
    )(w2, g2, m2, v2)
    return d.reshape(shape), mn.reshape(shape), vn.reshape(shape)


def _my_pos():
    return lax.axis_index("x"), lax.axis_index("y"), lax.axis_index("c")


class _Exchange:
    def __init__(self, inputs, out_shapes, sems, start, finish):
        self.inputs, self.out_shapes, self.sems, self.start, self.finish = inputs, out_shapes, sems, start, finish


def gather_exchange(shards):
    n = len(shards)

    def plan(ins, outs, sems):
        send_sems, recv_sems, local_sems = sems
        x, y, c = _my_pos()
        me, sibling = (x, y, c), (x, y, 1 - c)
        chips = [(1 - x, y), (x, 1 - y), (1 - x, 1 - y)]

        def slot(i, p):
            return outs[i].at[4 * p[0] + 2 * p[1] + p[2]]

        def copy(k, i, block, to, src=None):
            return pltpu.make_async_remote_copy(
                src_ref=slot(i, block) if src is None else src, dst_ref=slot(i, block),
                send_sem=send_sems.at[k, i], recv_sem=recv_sems.at[k, i], device_id=to, device_id_type=MESH)

        mine = [pltpu.make_async_copy(ins[i], slot(i, me), local_sems.at[i]) for i in range(n)]
        first = []
        for i in range(n):
            first.append(copy(0, i, me, sibling, src=ins[i]))
            first += [copy(1 + j, i, me, (*chip, c), src=ins[i]) for j, chip in enumerate(chips)]
        return me, sibling, chips, c, copy, mine, first

    def start(ins, outs, sems):
        _, _, _, _, _, mine, first = plan(ins, outs, sems)
        for cp in mine + first:
            cp.start()

    def finish(ins, outs, sems):
        me, sibling, chips, c, copy, mine, first = plan(ins, outs, sems)
        passed = []
        for j, chip in enumerate(chips):
            for i in range(n):
                copy(1 + j, i, (*chip, c), me).wait_recv()
                cp = copy(4 + j, i, (*chip, c), sibling)
                cp.start()
                passed.append(cp)
        for i in range(n):
            copy(0, i, sibling, me).wait_recv()
            for j, chip in enumerate(chips):
                copy(4 + j, i, (*chip, 1 - c), me).wait_recv()
        for cp in first + passed:
            cp.wait_send()
        for cp in mine:
            cp.wait()

    outs = [jax.ShapeDtypeStruct((N_DEV,) + s.shape, s.dtype) for s in shards]
    sems = [pltpu.SemaphoreType.DMA((7, n)), pltpu.SemaphoreType.DMA((7, n)), pltpu.SemaphoreType.DMA((n,))]
    return _Exchange(list(shards), outs, sems, start, finish)


def scatter_exchange(gs):
    n = len(gs)

    def plan(ins, outs, sems):
        send_sems, recv_sems, local_sems = sems
        x, y, c = _my_pos()
        me_id = 4 * x + 2 * y + c
        mine = [pltpu.make_async_copy(ins[i].at[me_id], outs[i].at[me_id], local_sems.at[i]) for i in range(n)]
        sends, recvs = [], []
        for msk in range(1, N_DEV):
            px = 1 - x if msk & 4 else x
            py = 1 - y if msk & 2 else y
            pc = 1 - c if msk & 1 else c
            pid = 4 * px + 2 * py + pc
            for i in range(n):
                sends.append(pltpu.make_async_remote_copy(
                    src_ref=ins[i].at[pid], dst_ref=outs[i].at[me_id],
                    send_sem=send_sems.at[msk - 1, i], recv_sem=recv_sems.at[msk - 1, i],
                    device_id=(px, py, pc), device_id_type=MESH))
                recvs.append(pltpu.make_async_remote_copy(
                    src_ref=ins[i].at[pid], dst_ref=outs[i].at[pid],
                    send_sem=send_sems.at[msk - 1, i], recv_sem=recv_sems.at[msk - 1, i],
                    device_id=(px, py, pc), device_id_type=MESH))
        return mine, sends, recvs

    def start(ins, outs, sems):
        mine, sends, _ = plan(ins, outs, sems)
        for cp in mine + sends:
            cp.start()

    def finish(ins, outs, sems):
        mine, sends, recvs = plan(ins, outs, sems)
        for rv in recvs:
            rv.wait_recv()
        for cp in sends:
            cp.wait_send()
        for cp in mine:
            cp.wait()

    outs = [jax.ShapeDtypeStruct(g.shape, g.dtype) for g in gs]
    sems = [pltpu.SemaphoreType.DMA((7, n)), pltpu.SemaphoreType.DMA((7, n)), pltpu.SemaphoreType.DMA((n,))]
    return _Exchange(list(gs), outs, sems, start, finish)


def run_exchange(ex, *, name):
    HBM = pl.BlockSpec(memory_space=pl.ANY)
    n_in, n_out = len(ex.inputs), len(ex.out_shapes)

    def body(*refs):
        ins, outs, sems = refs[:n_in], refs[n_in:n_in + n_out], refs[n_in + n_out:]
        ex.start(ins, outs, sems)
        ex.finish(ins, outs, sems)

    return list(_pcall(body, name=name, in_specs=[HBM] * n_in, out_specs=tuple([HBM] * n_out),
                       out_shape=tuple(ex.out_shapes), scratch_shapes=ex.sems)(*ex.inputs))


def all_gather(xs, *, name):
    outs = run_exchange(gather_exchange([xs[l] for l in range(xs.shape[0])]), name=name)
    return jnp.stack(outs)


def _slots(g):
    return g.reshape(N_DEV, g.shape[0] // N_DEV, g.shape[1])


def _rope_tables(positions):
    half = ROT_DIM // 2
    inv_freq = ROPE_THETA ** (-jnp.arange(0, ROT_DIM, 2, dtype=F32) / ROT_DIM)
    ang = positions.astype(F32)[:, None] * inv_freq
    cos, sin = jnp.cos(ang), jnp.sin(ang)
    S = positions.shape[0]
    c_t = jnp.concatenate([cos, cos, jnp.ones((S, LANE - ROT_DIM), F32)], axis=1)
    s1_t = jnp.concatenate([-sin, jnp.zeros((S, LANE - half), F32)], axis=1)
    s2_t = jnp.concatenate([jnp.zeros((S, half), F32), sin, jnp.zeros((S, LANE - ROT_DIM), F32)], axis=1)
    return c_t, s1_t, s2_t


def _row(v):
    return v.reshape(1, -1)


def kernel(x, c, positions, w_ada, b_ada, g_norm1, w_in, g_q, g_k, w_attn_proj, w_conv_dw, b_conv_dw, g_conv_ln, b_conv_ln, w_conv_out, w_o, g_norm2, w_ffn_in, w_ffn_dw, b_ffn_dw, w_ffn_down, loss_target, m_w_ada, m_b_ada, m_g_norm1, m_w_in, m_g_q, m_g_k, m_w_attn_proj, m_w_conv_dw, m_b_conv_dw, m_g_conv_ln, m_b_conv_ln, m_w_conv_out, m_w_o, m_g_norm2, m_w_ffn_in, m_w_ffn_dw, m_b_ffn_dw, m_w_ffn_down, v_w_ada, v_b_ada, v_g_norm1, v_w_in, v_g_q, v_g_k, v_w_attn_proj, v_w_conv_dw, v_b_conv_dw, v_g_conv_ln, v_b_conv_ln, v_w_conv_out, v_w_o, v_g_norm2, v_w_ffn_in, v_w_ffn_dw, v_b_ffn_dw, v_w_ffn_down):
    L = w_in.shape[0]
    S, D = x.shape[1], x.shape[2]
    FF = w_ffn_down.shape[1] * N_DEV
    xi, yi, ci = _my_pos()
    me = 4 * xi + 2 * yi + ci
    x0 = x[0]
    tabs = _rope_tables(positions[0])

    c_act = c * _sig(c)
    c_all = all_gather(jnp.pad(c_act, ((0, 7), (0, 0)))[None], name="ag_c")[0][:, 0, :]
    c_all16 = jnp.pad(c_all, ((0, 8), (0, 0)))
    m_part = jnp.stack([mm(c_all16, w_ada[l], "nn", name="mod_mm") for l in range(L)])
    m_all = all_gather(m_part, name="ag_mod")
    mod = lax.dynamic_index_in_dim(m_all, me, axis=2, keepdims=False).reshape(L, 6 * D) + b_ada
    mod = mod.reshape(L, 6, 1, D)

    sh_in = jnp.transpose(w_in, (0, 2, 1)).astype(BF16)
    sh_fi = jnp.transpose(w_ffn_in, (0, 2, 1)).astype(BF16)
    sh_ap = jnp.transpose(w_attn_proj, (0, 2, 1)).astype(BF16)
    sh_co, sh_oo, sh_dn = w_conv_out.astype(BF16), w_o.astype(BF16), w_ffn_down.astype(BF16)

    def rowcat(g):
        return g.reshape(N_DEV * g.shape[1], g.shape[2])

    wt_in, wt_fi, wt_ap, w_co, w_oo, w_dn = ([None] * L for _ in range(6))
    got = run_exchange(gather_exchange([sh_in[0], sh_fi[0], sh_ap[0], sh_co[0], sh_oo[0], sh_dn[0]]), name="ag_w0")
    wt_in[0], wt_fi[0], wt_ap[0], w_co[0], w_oo[0], w_dn[0] = (rowcat(g) for g in got)
    cdw = all_gather(jnp.pad(w_conv_dw, ((0, 0), (0, 1), (0, 0))), name="ag_cdw")
    cdw = jnp.transpose(cdw, (0, 2, 1, 3)).reshape(L, 32, D)[:, :CONV_K]
    fsh = w_ffn_dw.shape[2]
    fpad = -fsh % LANE
    fdw = all_gather(jnp.pad(w_ffn_dw, ((0, 0), (0, 8 - FFN_K), (0, fpad))), name="ag_fdw")
    fdw = jnp.transpose(fdw[:, :, :FFN_K, :fsh], (0, 2, 1, 3)).reshape(L, FFN_K, FF)

    QKV, CW = 3 * ATTN_W, 2 * D
    seg = ((0, QKV), (QKV, CW), (QKV + CW, 2 * D))

    saved = []
    x_prev, delta, gt_prev = x0, None, None
    for l in range(L):
        sh1, sc1, gt1, sh2, sc2, gt2 = (mod[l, i] for i in range(6))
        x_l, h = norm_mod_fwd(x_prev, delta, gt_prev, _row(g_norm1[l]), sc1, sh1, name="norm_fwd")
        nxt = l + 1 < L
        if nxt:
            zq, got = mm(h, wt_in[l], "nt", b_off=0, b_len=QKV, name="z_mm_ag", comm=gather_exchange([sh_in[l + 1]]))
            wt_in[l + 1] = rowcat(got[0])
        else:
            zq = mm(h, wt_in[l], "nt", b_off=0, b_len=QKV, name="z_mm")
        zc, zg = (mm(h, wt_in[l], "nt", b_off=o, b_len=n, name="z_mm") for o, n in seg[1:])
        gq, gk = _row(g_q[l]), _row(g_k[l])
        qn, kn = qk_prep_fwd(zq, gq, gk, tabs, name="qk_prep")
        o_g, lse_g = [], []
        for gi in range(3):
            o_i, l_i = attn_fwd(qn, kn, zq, gi, name="attn_fwd%d" % gi)
            o_g.append(o_i)
            lse_g.append(l_i)
        attn = combine_fwd(o_g, lse_g, name="combine_fwd")
        y_a = mm(attn, wt_ap[l], "nt", name="ya_mm")
        cw, cb = cdw[l], _row(b_conv_dw[l])
        cg, cbl = _row(g_conv_ln[l]), _row(b_conv_ln[l])
        u2, y_conv = convb_fwd(zc, cw, cb, cg, cbl, name="convb_fwd")
        y_b = mm(u2, w_co[l], "nn", name="yb_mm")
        merged = merge_fwd(y_a, y_b, zg, name="merge_fwd")
        mo = mm(merged, w_oo[l], "nn", name="mo_mm")
        x_mid, h2 = norm_mod_fwd(x_l, mo, gt1, _row(g_norm2[l]), sc2, sh2, name="norm_fwd")
        if nxt:
            gu, got = mm(h2, wt_fi[l], "nt", name="gu_mm_ag", comm=gather_exchange([sh_fi[l + 1]]))
            wt_fi[l + 1] = rowcat(got[0])
        else:
            gu = mm(h2, wt_fi[l], "nt", name="gu_mm")
        fw, fb = fdw[l], _row(b_ffn_dw[l])
        act = ffn_act_fwd(gu, fw, fb, name="ffn_act")
        if nxt:
            ffo, got = mm(act, w_dn[l], "nn", name="ffo_mm_ag",
                          comm=gather_exchange([sh_ap[l + 1], sh_co[l + 1], sh_oo[l + 1], sh_dn[l + 1]]))
            wt_ap[l + 1], w_co[l + 1], w_oo[l + 1], w_dn[l + 1] = (rowcat(g) for g in got)
        else:
            ffo = mm(act, w_dn[l], "nn", name="ffo_mm")
        saved.append(dict(x=x_l, h=h, zq=zq, zc=zc, zg=zg, qn=qn, kn=kn, o=o_g, lse=lse_g, attn=attn, y_a=y_a,
                          u2=u2, y_conv=y_conv, y_b=y_b, merged=merged, mo=mo, x_mid=x_mid, h2=h2, gu=gu, act=act, ffo=ffo))
        x_prev, delta, gt_prev = x_mid, ffo, gt2

    dx, lpart = loss_head(x_prev, delta, gt_prev, loss_target[0], name="loss_head")
    loss = lax.psum(0.5 / D * jnp.sum(lpart[0]), ("x", "y", "c"))

    land = {k: [None] * L for k in ("in", "fi", "ap", "co", "o", "dn")}
    small_rows = []
    g_in_pending = None
    for l in reversed(range(L)):
        sv = saved[l]
        sh1, sc1, gt1, sh2, sc2, gt2 = (mod[l, i] for i in range(6))
        d_ffo, p_gt2 = scale_bwd(dx, sv["ffo"], gt2, name="scale_bwd")
        if g_in_pending is not None:
            d_act, got = mm(d_ffo, w_dn[l], "nt", name="dact_mm_rs", comm=scatter_exchange([_slots(g_in_pending)]))
            land["in"][l + 1] = got[0]
        else:
            d_act = mm(d_ffo, w_dn[l], "nt", name="dact_mm")
        g_dn = mm(sv["act"], d_ffo, "tn", out_dtype=BF16, name="dwdn_mm")
        fw, fb = fdw[l], _row(b_ffn_dw[l])
        dgp, dup, p_fw, p_fb = ffn_act_bwd1(d_act, sv["gu"], fw, fb, name="ffn_bwd1")
        dgu = ffn_act_bwd2(dgp, dup, fw, name="ffn_bwd2")
        dh2, got = mm(dgu, wt_fi[l], "nn", name="dh2_mm_rs", comm=scatter_exchange([_slots(g_dn)]))
        land["dn"][l] = got[0]
        g_fi = mm(dgu, sv["h2"], "tn", out_dtype=BF16, name="dwfi_mm")
        dx, p_g2, p_sc2, p_sh2 = norm_mod_bwd(sv["x_mid"], dh2, _row(g_norm2[l]), sc2, sh2, dx, name="norm_bwd")
        d_mo, p_gt1 = scale_bwd(dx, sv["mo"], gt1, name="scale_bwd")
        d_merged = mm(d_mo, w_oo[l], "nt", name="dmerged_mm")
        g_o = mm(sv["merged"], d_mo, "tn", out_dtype=BF16, name="dwo_mm")
        d_ya, d_yb, dzg = merge_bwd(d_merged, sv["y_a"], sv["y_b"], sv["zg"], name="merge_bwd")
        d_attn = mm(d_ya, wt_ap[l], "nn", name="dattn_mm")
        g_ap = mm(d_ya, sv["attn"], "tn", out_dtype=BF16, name="dwap_mm")
        d_u2 = mm(d_yb, w_co[l], "nt", name="du2_mm")
        g_co = mm(sv["u2"], d_yb, "tn", out_dtype=BF16, name="dwco_mm")
        cw, cb = cdw[l], _row(b_conv_dw[l])
        cg, cbl = _row(g_conv_ln[l]), _row(b_conv_ln[l])
        dy, p_cw, p_cb, p_cg, p_cbl = convb_bwd1(d_u2, sv["y_conv"], sv["zc"], cg, cbl, name="convb_bwd1")
        dzc = convb_bwd2(dy, sv["zc"], cw, name="convb_bwd2")
        dd = combine_bwd(d_attn, sv["o"], sv["lse"], name="combine_bwd")
        do_g, cc_g = dd[:3], dd[3:]
        parts = [attn_bwd(sv["qn"], sv["kn"], sv["zq"], do_g[gi], sv["lse"][gi], cc_g[gi], gi,
                          name="attn_bwd%d" % gi) for gi in range(3)]
        gq, gk = _row(g_q[l]), _row(g_k[l])
        dzq, p_gq, p_gk = attn_bwd_post(sv["zq"], gq, gk, tabs, *[[p[i] for p in parts] for i in range(5)],
                                        name="attn_post")
        dh, got = mm(dzq, wt_in[l], "nn", b_off=0, name="dh_mm_rs", comm=scatter_exchange([_slots(g_fi)]))
        land["fi"][l] = got[0]
        for dz_s, (o, n) in zip((dzc, dzg), seg[1:]):
            dh = mm(dz_s, wt_in[l], "nn", b_off=o, c_in=dh, name="dh_mm")
        g_in_q, got = mm(dzq, sv["h"], "tn", out_dtype=BF16, name="dwin_mm_rs",
                         comm=scatter_exchange([_slots(g_o), _slots(g_ap), _slots(g_co)]))
        land["o"][l], land["ap"][l], land["co"][l] = got
        g_in_pending = jnp.concatenate([g_in_q] + [mm(dz_s, sv["h"], "tn", out_dtype=BF16, name="dwin_mm")
                                                   for dz_s in (dzc, dzg)], axis=0)
        dx, p_g1, p_sc1, p_sh1 = norm_mod_bwd(sv["x"], dh, _row(g_norm1[l]), sc1, sh1, dx, name="norm_bwd")

        def row1k(p):
            v = p[0]
            pad = -v.shape[0] % D
            return jnp.pad(v, (0, pad)).reshape(-1, D)

        rows = [row1k(p) for p in (p_sh1, p_sc1, p_gt1, p_sh2, p_sc2, p_gt2, p_g1, p_g2)]
        rows.append(row1k(jnp.concatenate([p_gq, p_gk], axis=1)))
        rows += [row1k(p) for p in (p_cb, p_cg, p_cbl, p_fb)]
        rows.append(jnp.sum(p_cw.reshape(CONV_K, SUBLANES, D), axis=1))
        rows += [row1k(p_fw[k:k + 1]) for k in range(FFN_K)]
        blk = jnp.concatenate(rows, axis=0)
        small_rows.append(jnp.pad(blk, ((0, -blk.shape[0] % 8), (0, 0))))
    land["in"][0] = run_exchange(scatter_exchange([_slots(g_in_pending)]), name="rs_in0")[0]
    small_rows = small_rows[::-1]
    n_small = small_rows[0].shape[0]
    ff_rows = -(-FF // D)

    small = jnp.concatenate(small_rows, axis=0)[None]
    small_all = all_gather(small, name="ag_small")[0]
    small_sum = sum_slots(small_all, name="sum_small").reshape(L, n_small, D)
    small_all = small_all.reshape(N_DEV, L, n_small, D)

    g_b_ada = small_sum[:, 0:6].reshape(L, 6 * D)
    g_g1, g_g2 = small_sum[:, 6], small_sum[:, 7]
    g_gq, g_gk = small_sum[:, 8, 0:LANE], small_sum[:, 8, LANE:2 * LANE]
    g_cb, g_cg, g_cbl = small_sum[:, 9], small_sum[:, 10], small_sum[:, 11]
    r0 = 12
    g_fb = small_sum[:, r0:r0 + ff_rows].reshape(L, -1)[:, :FF]
    r0 += ff_rows
    g_cw_full = small_sum[:, r0:r0 + CONV_K]
    r0 += CONV_K
    g_fw_full = small_sum[:, r0:r0 + FFN_K * ff_rows].reshape(L, FFN_K, -1)[:, :, :FF]
    csh = w_conv_dw.shape[2]
    g_cw = lax.dynamic_slice_in_dim(g_cw_full, me * csh, csh, axis=2)
    g_fw = lax.dynamic_slice_in_dim(g_fw_full, me * fsh, fsh, axis=2)

    ash = w_ada.shape[2]
    dmod_all = small_all[:, :, 0:6].reshape(N_DEV, L, 6 * D)
    dmod_mine = lax.dynamic_slice_in_dim(dmod_all, me * ash, ash, axis=2)
    g_w_ada = jnp.stack([mm(c_all16, jnp.pad(dmod_mine[:, l], ((0, 8), (0, 0))), "tn", name="dwada_mm")
                         for l in range(L)])

    def reduced(key, transposed):
        out = jnp.stack([sum_slots(slots, name="sum_" + key) for slots in land[key]])
        return jnp.transpose(out, (0, 2, 1)) if transposed else out

    g_w_in = reduced("in", True)
    g_w_fi = reduced("fi", True)
    g_w_ap = reduced("ap", True)
    g_w_co = reduced("co", False)
    g_w_o = reduced("o", False)
    g_w_dn = reduced("dn", False)

    grads = [g_w_ada, g_b_ada, g_g1, g_w_in, g_gq, g_gk, g_w_ap, g_cw, g_cb, g_cg, g_cbl, g_w_co, g_w_o, g_g2,
             g_w_fi, g_fw, g_fb, g_w_dn]
    ws = [w_ada, b_ada, g_norm1, w_in, g_q, g_k, w_attn_proj, w_conv_dw, b_conv_dw, g_conv_ln, b_conv_ln,
          w_conv_out, w_o, g_norm2, w_ffn_in, w_ffn_dw, b_ffn_dw, w_ffn_down]
    ms = [m_w_ada, m_b_ada, m_g_norm1, m_w_in, m_g_q, m_g_k, m_w_attn_proj, m_w_conv_dw, m_b_conv_dw, m_g_conv_ln,
          m_b_conv_ln, m_w_conv_out, m_w_o, m_g_norm2, m_w_ffn_in, m_w_ffn_dw, m_b_ffn_dw, m_w_ffn_down]
    vs = [v_w_ada, v_b_ada, v_g_norm1, v_w_in, v_g_q, v_g_k, v_w_attn_proj, v_w_conv_dw, v_b_conv_dw, v_g_conv_ln,
          v_b_conv_ln, v_w_conv_out, v_w_o, v_g_norm2, v_w_ffn_in, v_w_ffn_dw, v_b_ffn_dw, v_w_ffn_down]
    deltas, new_m, new_v = [], [], []
    for w_i, g_i, m_i, v_i in zip(ws, grads, ms, vs):
        d_i, mn_i, vn_i = adamw(w_i, g_i, m_i, v_i, name="adamw")
        deltas.append(d_i)
        new_m.append(mn_i)
        new_v.append(vn_i)
    return (loss, dx[None], *grads, *deltas, *new_m, *new_v)
```

```python
import functools
import math

import jax
import jax.numpy as jnp
from jax import lax
from jax.experimental import pallas as pl
from jax.experimental.pallas import tpu as pltpu

F32 = jnp.float32
BF16 = jnp.bfloat16
MESH = pl.DeviceIdType.MESH
N_DEV = 8

EPS = 1e-6
HEAD_DIM = 128
BLOCK = 128
DILATIONS = (1, 4, 16)
HEADS_PER_GROUP = 4
N_HEADS = 12
ATTN_W = N_HEADS * HEAD_DIM
ROT_DIM = 32
ROPE_THETA = 500000.0
CONV_K = 31
CONV_HALO = 32
FFN_K = 3
FFN_HALO = 8
NEG = -1e30

ADAM_LR, ADAM_B1, ADAM_B2, ADAM_EPS, ADAM_WD, ADAM_STEP = 0.001, 0.9, 0.999, 1e-08, 0.01, 10

LANE = 128
VMEM_LIMIT = 56 * 1024 * 1024
ROW_TILE = 512
POST_TILE = 256


def _pcall(body, **kw):
    return pl.pallas_call(body, **kw)


def _cparams(*sem):
    return pltpu.CompilerParams(dimension_semantics=sem, vmem_limit_bytes=VMEM_LIMIT)


def _sig(v):
    return 1.0 / (1.0 + jnp.exp(-v))


def _divtile(dim, target):
    best = None
    for t in range(LANE, min(dim, target) + 1, LANE):
        if dim % t == 0:
            best = t
    return best or dim


def _rows(t, c, col=0):
    return pl.BlockSpec((t, c), lambda i: (i, col))


def _full(shape):
    nd = len(shape)
    return pl.BlockSpec(shape, lambda i: (0,) * nd)


def _acc_rows(ref, val, i):
    @pl.when(i == 0)
    def _():
        ref[...] = jnp.zeros_like(ref)
    r = val.shape[0]
    ref[0:r, :] += val


MM_TILE = 1536


def mm(a, b, mode, *, name, out_dtype=F32, c_in=None, b_off=0, b_len=None, comm=None):
    if mode == "nn":
        M, K = a.shape
        N = b.shape[1]
    elif mode == "nt":
        M, K = a.shape
        N = b_len if b_len is not None else b.shape[0]
    else:
        K, M = a.shape
        N = b.shape[1]
    g_n = math.gcd(N, b_off) if (mode == "nt" and b_off) else N
    g_k = math.gcd(K, b_off) if (mode == "nn" and b_off) else K
    tn = _divtile(g_n, MM_TILE if c_in is None else 1024)
    tk = _divtile(g_k, MM_TILE if mode != "tn" else 1024)
    tm = _divtile(M, MM_TILE if mode == "tn" else (2048 if (tn <= 512 and c_in is None) else 1024))
    gm, gn, nk = M // tm, N // tn, K // tk
    n_ci = 0 if comm is None else len(comm.inputs)
    n_co = 0 if comm is None else len(comm.out_shapes)
    n_x = 2 + (c_in is not None)
    if mode == "nn":
        dims = (((1,), (0,)), ((), ()))
    elif mode == "nt":
        dims = (((1,), (1,)), ((), ()))
    else:
        dims = (((0,), (0,)), ((), ()))

    def body(*refs):
        a_ref, b_ref = refs[0], refs[1]
        c_ref = refs[2] if c_in is not None else None
        c_ins = refs[n_x:n_x + n_ci]
        o_ref = refs[n_x + n_ci]
        c_outs = refs[n_x + n_ci + 1:n_x + n_ci + 1 + n_co]
        rest = refs[n_x + n_ci + 1 + n_co:]
        acc = rest[0] if nk > 1 else None
        sems = rest[1:] if nk > 1 else rest
        i, j, k = pl.program_id(0), pl.program_id(1), pl.program_id(2)

        if comm is not None:
            @pl.when(jnp.logical_and(jnp.logical_and(i == 0, j == 0), k == 0))
            def _():
                comm.start(c_ins, c_outs, sems)

        prod = lax.dot_general(a_ref[...].astype(BF16), b_ref[...].astype(BF16), dims, preferred_element_type=F32)
        if nk == 1:
            if c_ref is not None:
                prod = prod + c_ref[...].astype(F32)
            o_ref[...] = prod.astype(out_dtype)
        else:
            @pl.when(k == 0)
            def _():
                if c_ref is None:
                    acc[...] = prod
                else:
                    acc[...] = prod + c_ref[...].astype(F32)

            @pl.when(k > 0)
            def _():
                acc[...] += prod

            @pl.when(k == nk - 1)
            def _():
                o_ref[...] = acc[...].astype(out_dtype)

        if comm is not None:
            @pl.when(jnp.logical_and(jnp.logical_and(i == gm - 1, j == gn - 1), k == nk - 1))
            def _():
                comm.finish(c_ins, c_outs, sems)

    if mode == "nn":
        a_spec = pl.BlockSpec((tm, tk), lambda i, j, k: (i, k))
        ob = b_off // tk
        b_spec = pl.BlockSpec((tk, tn), lambda i, j, k: (k + ob, j))
    elif mode == "nt":
        a_spec = pl.BlockSpec((tm, tk), lambda i, j, k: (i, k))
        ob = b_off // tn
        b_spec = pl.BlockSpec((tn, tk), lambda i, j, k: (j + ob, k))
    else:
        a_spec = pl.BlockSpec((tk, tm), lambda i, j, k: (k, i))
        b_spec = pl.BlockSpec((tk, tn), lambda i, j, k: (k, j))
    o_spec = pl.BlockSpec((tm, tn), lambda i, j, k: (i, j))
    HBM = pl.BlockSpec(memory_space=pl.ANY)
    in_specs = [a_spec, b_spec]
    args = [a, b]
    if c_in is not None:
        in_specs.append(o_spec)
        args.append(c_in)
    scratch = [pltpu.VMEM((tm, tn), F32)] if nk > 1 else []
    o_shape = jax.ShapeDtypeStruct((M, N), out_dtype)
    if comm is None:
        return _pcall(
            body, name=name, grid=(gm, gn, nk), in_specs=in_specs, out_specs=o_spec, out_shape=o_shape,
            scratch_shapes=scratch, compiler_params=_cparams("parallel", "parallel", "arbitrary"),
        )(*args)
    res = _pcall(
        body, name=name, grid=(gm, gn, nk), in_specs=in_specs + [HBM] * n_ci,
        out_specs=(o_spec, *[HBM] * n_co), out_shape=(o_shape, *comm.out_shapes),
        scratch_shapes=scratch + comm.sems, compiler_params=_cparams("arbitrary", "arbitrary", "arbitrary"),
    )(*args, *comm.inputs)
    return res[0], list(res[1:])


def norm_mod_fwd(x_prev, delta, gt, g, sc, sh, *, name):
    S, D = x_prev.shape
    T = ROW_TILE
    has_delta = delta is not None

    def body(*refs):
        if has_delta:
            xp, dl, gt_r, g_r, sc_r, sh_r, x_out, h_out = refs
            xv = xp[...] + gt_r[...] * dl[...]
            x_out[...] = xv
        else:
            xp, g_r, sc_r, sh_r, h_out = refs
            xv = xp[...]
        r = lax.rsqrt(jnp.mean(xv * xv, axis=-1, keepdims=True) + EPS)
        h_out[...] = ((xv * r) * g_r[...] * (1.0 + sc_r[...]) + sh_r[...]).astype(BF16)

    vec = _full((1, D))
    if has_delta:
        ins, specs = [x_prev, delta, gt, g, sc, sh], [_rows(T, D), _rows(T, D), vec, vec, vec, vec]
        outs = (jax.ShapeDtypeStruct((S, D), F32), jax.ShapeDtypeStruct((S, D), BF16))
        ospecs = (_rows(T, D), _rows(T, D))
    else:
        ins, specs = [x_prev, g, sc, sh], [_rows(T, D), vec, vec, vec]
        outs = jax.ShapeDtypeStruct((S, D), BF16)
        ospecs = _rows(T, D)
    res = _pcall(body, name=name, grid=(S // T,), in_specs=specs, out_specs=ospecs, out_shape=outs,
                 compiler_params=_cparams("parallel"))(*ins)
    return res if has_delta else (x_prev, res)


def norm_mod_bwd(x, dh, g, sc, sh, dx_res, *, name):
    S, D = x.shape
    T = ROW_TILE

    def body(x_r, dh_r, g_r, sc_r, sh_r, dr_r, dx_o, dg_o, dsc_o, dsh_o):
        i = pl.program_id(0)
        xv = x_r[...]
        dh_v = dh_r[...]
        r = lax.rsqrt(jnp.mean(xv * xv, axis=-1, keepdims=True) + EPS)
        xh = xv * r
        dn = dh_v * (1.0 + sc_r[...])
        dxh = dn * g_r[...]
        dx_o[...] = dr_r[...] + r * (dxh - xh * jnp.mean(dxh * xh, axis=-1, keepdims=True))
        _acc_rows(dg_o, jnp.sum(dn * xh, axis=0, keepdims=True), i)
        _acc_rows(dsc_o, jnp.sum(dh_v * (xh * g_r[...]), axis=0, keepdims=True), i)
        _acc_rows(dsh_o, jnp.sum(dh_v, axis=0, keepdims=True), i)

    vec = _full((1, D))
    part = jax.ShapeDtypeStruct((8, D), F32)
    return _pcall(
        body, name=name, grid=(S // T,),
        in_specs=[_rows(T, D), _rows(T, D), vec, vec, vec, _rows(T, D)],
        out_specs=(_rows(T, D), _full((8, D)), _full((8, D)), _full((8, D))),
        out_shape=(jax.ShapeDtypeStruct((S, D), F32), part, part, part),
        compiler_params=_cparams("arbitrary"),
    )(x, dh, g, sc, sh, dx_res)


def scale_bwd(dx, delta, gt, *, name):
    S, D = dx.shape
    T = ROW_TILE

    def body(dx_r, dl_r, gt_r, dd_o, dgt_o):
        i = pl.program_id(0)
        dv = dx_r[...]
        dd_o[...] = (dv * gt_r[...]).astype(BF16)
        _acc_rows(dgt_o, jnp.sum(dv * dl_r[...], axis=0, keepdims=True), i)

    return _pcall(
        body, name=name, grid=(S // T,),
        in_specs=[_rows(T, D), _rows(T, D), _full((1, D))],
        out_specs=(_rows(T, D), _full((8, D))),
        out_shape=(jax.ShapeDtypeStruct((S, D), BF16), jax.ShapeDtypeStruct((8, D), F32)),
        compiler_params=_cparams("arbitrary"),
    )(dx, delta, gt)


def loss_head(x_mid, ffo, gt, target, *, name):
    S, D = x_mid.shape
    T = ROW_TILE

    def body(x_r, f_r, gt_r, t_r, dy_o, l_o):
        i = pl.program_id(0)
        e = x_r[...] + gt_r[...] * f_r[...] - t_r[...]
        dy_o[...] = e * (1.0 / D)
        _acc_rows(l_o, jnp.sum(e * e, axis=0, keepdims=True), i)

    return _pcall(
        body, name=name, grid=(S // T,),
        in_specs=[_rows(T, D), _rows(T, D), _full((1, D)), _rows(T, D)],
        out_specs=(_rows(T, D), _full((8, D))),
        out_shape=(jax.ShapeDtypeStruct((S, D), F32), jax.ShapeDtypeStruct((8, D), F32)),
        compiler_params=_cparams("arbitrary"),
    )(x_mid, ffo, gt, target)


def _rope(t, c_t, s1_t, s2_t):
    return t * c_t + pltpu.roll(t, LANE - ROT_DIM // 2, 1) * s1_t + pltpu.roll(t, ROT_DIM // 2, 1) * s2_t


def _rope_t(d, c_t, s1_t, s2_t):
    return d * c_t + pltpu.roll(d * s1_t, ROT_DIM // 2, 1) + pltpu.roll(d * s2_t, LANE - ROT_DIM // 2, 1)


def qk_prep_fwd(zq, g_q, g_k, tabs, *, name):
    S = zq.shape[0]
    T = ROW_TILE

    def body(q_r, k_r, gq_r, gk_r, c_r, s1_r, s2_r, qn_o, kn_o):
        c_t, s1_t, s2_t = c_r[...], s1_r[...], s2_r[...]
        for src, g_r, dst in ((q_r, gq_r, qn_o), (k_r, gk_r, kn_o)):
            for h in range(N_HEADS):
                cols = slice(h * HEAD_DIM, (h + 1) * HEAD_DIM)
                t = src[:, cols]
                r = lax.rsqrt(jnp.mean(t * t, axis=-1, keepdims=True) + EPS)
                dst[:, cols] = _rope(t * r * g_r[...], c_t, s1_t, s2_t)

    tab = _rows(T, LANE)
    shp = jax.ShapeDtypeStruct((S, ATTN_W), F32)
    return _pcall(
        body, name=name, grid=(S // T,),
        in_specs=[_rows(T, ATTN_W, 0), _rows(T, ATTN_W, 1), _full((1, LANE)), _full((1, LANE)), tab, tab, tab],
        out_specs=(_rows(T, ATTN_W), _rows(T, ATTN_W)), out_shape=(shp, shp),
        compiler_params=_cparams("parallel"),
    )(zq, zq, g_q, g_k, *tabs)


def attn_bwd_post(zq, g_q, g_k, tabs, dq, dkc, dkp, dvc, dvp, *, name):
    S = zq.shape[0]
    T = min(POST_TILE, S)
    nblk = S // T
    GW = HEADS_PER_GROUP * HEAD_DIM
    n_ref = [7 if BLOCK * d < T else 5 for d in DILATIONS]

    def body(*refs):
        q_r, k_r, gq_r, gk_r, c_r, s1_r, s2_r = refs[:7]
        grp = refs[7:7 + sum(n_ref)]
        dz_o, dgq_o, dgk_o = refs[7 + sum(n_ref):]
        i = pl.program_id(0)
        c_t, s1_t, s2_t = c_r[...], s1_r[...], s2_r[...]
        dgq = jnp.zeros((1, LANE), F32)
        dgk = jnp.zeros((1, LANE), F32)
        at = 0
        for gi, d in enumerate(DILATIONS):
            g_refs = grp[at:at + n_ref[gi]]
            at += n_ref[gi]
            sh = BLOCK * d
            if sh < T:
                dq_r, dkc_r, dkp_r, dkpn_r, dvc_r, dvp_r, dvpn_r = g_refs
                live = jnp.where(i + 1 < nblk, 1.0, 0.0)

                def shifted(cur_r, nxt_r, gc, live=live, sh=sh):
                    return jnp.concatenate([cur_r[sh:T, gc], live * nxt_r[:, gc]], axis=0)
            else:
                dq_r, dkc_r, dkp_r, dvc_r, dvp_r = g_refs
                dkpn_r = dvpn_r = None
                live = jnp.where(i + sh // T < nblk, 1.0, 0.0)

                def shifted(cur_r, nxt_r, gc, live=live):
                    return live * cur_r[:, gc]
            for hh in range(HEADS_PER_GROUP):
                h = gi * HEADS_PER_GROUP + hh
                cols = slice(h * HEAD_DIM, (h + 1) * HEAD_DIM)
                gc = slice(hh * HEAD_DIM, (hh + 1) * HEAD_DIM)
                dk_v = dkc_r[:, gc] + shifted(dkp_r, dkpn_r, gc)
                dv_v = dvc_r[:, gc] + shifted(dvp_r, dvpn_r, gc)
                dz_o[:, 2 * ATTN_W + h * HEAD_DIM:2 * ATTN_W + (h + 1) * HEAD_DIM] = dv_v.astype(BF16)
                for which, (src, g_r, d_out) in enumerate(((q_r, gq_r, dq_r[:, gc]), (k_r, gk_r, dk_v))):
                    t = src[:, cols]
                    r = lax.rsqrt(jnp.mean(t * t, axis=-1, keepdims=True) + EPS)
                    xh = t * r
                    dtn = _rope_t(d_out, c_t, s1_t, s2_t)
                    dxh = dtn * g_r[...]
                    dt = r * (dxh - xh * jnp.mean(dxh * xh, axis=-1, keepdims=True))
                    dz_o[:, which * ATTN_W + h * HEAD_DIM:which * ATTN_W + (h + 1) * HEAD_DIM] = dt.astype(BF16)
                    part = jnp.sum(dtn * xh, axis=0, keepdims=True)
                    if which == 0:
                        dgq = dgq + part
                    else:
                        dgk = dgk + part
        _acc_rows(dgq_o, dgq, i)
        _acc_rows(dgk_o, dgk, i)

    tab = _rows(T, LANE)
    specs = [_rows(T, ATTN_W, 0), _rows(T, ATTN_W, 1), _full((1, LANE)), _full((1, LANE)), tab, tab, tab]
    args = [zq, zq, g_q, g_k, *tabs]
    for gi, d in enumerate(DILATIONS):
        cur = _rows(T, GW)
        sh = BLOCK * d
        if sh < T:
            head = pl.BlockSpec((sh, GW), functools.partial(
                lambda i, k, last: (jnp.minimum((i + 1) * k, last), 0), k=T // sh, last=S // sh - 1))
            specs += [cur, cur, cur, head, cur, cur, head]
            args += [dq[gi], dkc[gi], dkp[gi], dkp[gi], dvc[gi], dvp[gi], dvp[gi]]
        else:
            nxt = pl.BlockSpec((T, GW), functools.partial(lambda i, s: (jnp.minimum(i + s, nblk - 1), 0), s=sh // T))
            specs += [cur, cur, nxt, cur, nxt]
            args += [dq[gi], dkc[gi], dkp[gi], dvc[gi], dvp[gi]]
    part = jax.ShapeDtypeStruct((8, LANE), F32)
    return _pcall(
        body, name=name, grid=(nblk,), in_specs=specs,
        out_specs=(_rows(T, 3 * ATTN_W), _full((8, LANE)), _full((8, LANE))),
        out_shape=(jax.ShapeDtypeStruct((S, 3 * ATTN_W), BF16), part, part),
        compiler_params=_cparams("arbitrary"),
    )(*args)


def _attn_geometry(d, S):
    R = 4 * BLOCK * d if d < 16 else BLOCK * d
    R = min(R, S)
    return R, R // (BLOCK * d), S // R


def _sub_rows(j, r, d):
    if d == 1:
        return pl.ds(j * BLOCK, BLOCK)
    return pl.ds(j * BLOCK * d + r, BLOCK, stride=d)


def _dot_nt(a, b):
    return lax.dot_general(a, b, (((1,), (1,)), ((), ())), preferred_element_type=F32)


def _dot_tn(a, b):
    return lax.dot_general(a, b, (((0,), (0,)), ((), ())), preferred_element_type=F32)


def _attn_specs(gi, R):
    h0 = gi * HEADS_PER_GROUP
    vcol = 2 * N_HEADS + h0
    cur = lambda off: pl.BlockSpec((R, HEAD_DIM), lambda h, n: (n, off + h))
    prev = lambda off: pl.BlockSpec((R, HEAD_DIM), lambda h, n: (jnp.maximum(n - 1, 0), off + h))
    return [cur(h0), cur(h0), prev(h0), cur(vcol), prev(vcol)]


ATTN_UNROLL = 4


def _attn_masks(n):
    qi = lax.broadcasted_iota(jnp.int32, (BLOCK, 2 * BLOCK), 0)
    kj = lax.broadcasted_iota(jnp.int32, (BLOCK, 2 * BLOCK), 1)
    band = jnp.where(jnp.logical_and(kj >= qi, kj <= qi + BLOCK), 0.0, NEG)
    no_prev = band + jnp.where(kj < BLOCK, 1.0, 0.0) * jnp.where(n > 0, 0.0, NEG)
    return band, no_prev


def _attn_keys(kc_r, kp_r, vc_r, vp_r, j, r, d, J):
    rq = _sub_rows(j, r, d)
    if j > 0:
        rp = _sub_rows(j - 1, r, d)
        kp, vp = kc_r[rp, :], vc_r[rp, :]
    else:
        rp = _sub_rows(J - 1, r, d)
        kp, vp = kp_r[rp, :], vp_r[rp, :]
    kk = jnp.concatenate([kp, kc_r[rq, :]], axis=0).astype(BF16)
    vv = jnp.concatenate([vp, vc_r[rq, :]], axis=0).astype(BF16)
    return kk, vv


def _attn_units(unit, d, J):
    for j in range(J):
        if d == 1:
            unit(j, 0)
        else:
            def step(r, carry, j=j):
                unit(j, r)
                return carry
            lax.fori_loop(0, d, step, 0, unroll=ATTN_UNROLL)


def attn_fwd(qn, kn, zq, gi, *, name):
    S = qn.shape[0]
    d = DILATIONS[gi]
    R, J, nblk = _attn_geometry(d, S)
    scale = HEAD_DIM ** -0.5
    GW = HEADS_PER_GROUP * HEAD_DIM

    def body(q_r, kc_r, kp_r, vc_r, vp_r, o_o, l_o):
        band, no_prev = _attn_masks(pl.program_id(1))

        def unit(j, r):
            rq = _sub_rows(j, r, d)
            q = q_r[rq, :].astype(BF16)
            kk, vv = _attn_keys(kc_r, kp_r, vc_r, vp_r, j, r, d, J)
            s = _dot_nt(q, kk) * scale + (band if j > 0 else no_prev)
            m = jnp.max(s, axis=-1, keepdims=True)
            p = jnp.exp(s - m)
            l = jnp.sum(p, axis=-1, keepdims=True)
            o_o[rq, :] = jnp.dot(p.astype(BF16), vv, preferred_element_type=F32) / l
            l_o[rq, :] = jnp.broadcast_to(m + jnp.log(l), (BLOCK, HEAD_DIM))

        _attn_units(unit, d, J)

    ospec = pl.BlockSpec((R, HEAD_DIM), lambda h, n: (n, h))
    shp = jax.ShapeDtypeStruct((S, GW), F32)
    return _pcall(
        body, name=name, grid=(HEADS_PER_GROUP, nblk), in_specs=_attn_specs(gi, R),
        out_specs=(ospec, ospec), out_shape=(shp, shp),
        compiler_params=_cparams("parallel", "arbitrary"),
    )(qn, kn, kn, zq, zq)


def attn_bwd(qn, kn, zq, do, lse, cc, gi, *, name):
    S = qn.shape[0]
    d = DILATIONS[gi]
    R, J, nblk = _attn_geometry(d, S)
    scale = HEAD_DIM ** -0.5
    GW = HEADS_PER_GROUP * HEAD_DIM

    def body(q_r, kc_r, kp_r, vc_r, vp_r, do_r, l_r, c_r, dq_o, dkc_o, dkp_o, dvc_o, dvp_o):
        band, no_prev = _attn_masks(pl.program_id(1))

        def unit(j, r):
            rq = _sub_rows(j, r, d)
            q = q_r[rq, :].astype(BF16)
            kk, vv = _attn_keys(kc_r, kp_r, vc_r, vp_r, j, r, d, J)
            s_mask = band if j > 0 else no_prev
            dob = do_r[rq, :].astype(BF16)
            lv = l_r[rq, :]
            cv = c_r[rq, :]
            lv2 = jnp.concatenate([lv, lv], axis=1)
            cv2 = jnp.concatenate([cv, cv], axis=1)
            p = jnp.exp(_dot_nt(q, kk) * scale + s_mask - lv2)
            ds = (p * (_dot_nt(dob, vv) + cv2)).astype(BF16)
            dq_o[rq, :] = jnp.dot(ds, kk, preferred_element_type=F32) * scale
            dk2 = _dot_tn(ds, q) * scale
            dv2 = _dot_tn(p.astype(BF16), dob)
            dkp_o[rq, :] = dk2[0:BLOCK]
            dkc_o[rq, :] = dk2[BLOCK:2 * BLOCK]
            dvp_o[rq, :] = dv2[0:BLOCK]
            dvc_o[rq, :] = dv2[BLOCK:2 * BLOCK]

        _attn_units(unit, d, J)

    ospec = pl.BlockSpec((R, HEAD_DIM), lambda h, n: (n, h))
    shp = jax.ShapeDtypeStruct((S, GW), F32)
    return _pcall(
        body, name=name, grid=(HEADS_PER_GROUP, nblk),
        in_specs=_attn_specs(gi, R) + [ospec, ospec, ospec],
        out_specs=(ospec,) * 5, out_shape=(shp,) * 5,
        compiler_params=_cparams("parallel", "arbitrary"),
    )(qn, kn, kn, zq, zq, do, lse, cc)


def combine_fwd(o, lse, *, name):
    S, GW = o[0].shape
    T = ROW_TILE

    def body(o0, o1, o2, l0, l1, l2, a_o):
        m = jnp.maximum(jnp.maximum(l0[...], l1[...]), l2[...])
        e0, e1, e2 = jnp.exp(l0[...] - m), jnp.exp(l1[...] - m), jnp.exp(l2[...] - m)
        a_o[...] = ((e0 * o0[...] + e1 * o1[...] + e2 * o2[...]) / (e0 + e1 + e2)).astype(BF16)

    return _pcall(
        body, name=name, grid=(S // T,), in_specs=[_rows(T, GW)] * 6, out_specs=_rows(T, GW),
        out_shape=jax.ShapeDtypeStruct((S, GW), BF16), compiler_params=_cparams("parallel"),
    )(*o, *lse)


def combine_bwd(d_attn, o, lse, *, name):
    S, GW = d_attn.shape
    T = ROW_TILE

    def body(da_r, o0, o1, o2, l0, l1, l2, d0, d1, d2, c0, c1, c2):
        m = jnp.maximum(jnp.maximum(l0[...], l1[...]), l2[...])
        e0, e1, e2 = jnp.exp(l0[...] - m), jnp.exp(l1[...] - m), jnp.exp(l2[...] - m)
        inv = 1.0 / (e0 + e1 + e2)
        w = (e0 * inv, e1 * inv, e2 * inv)
        da = da_r[...]
        attn = w[0] * o0[...] + w[1] * o1[...] + w[2] * o2[...]
        prod = da * attn
        for hh in range(HEADS_PER_GROUP):
            cols = slice(hh * HEAD_DIM, (hh + 1) * HEAD_DIM)
            a_h = jnp.sum(prod[:, cols], axis=-1, keepdims=True)
            for w_g, d_o, c_o in zip(w, (d0, d1, d2), (c0, c1, c2)):
                d_o[:, cols] = w_g[:, cols] * da[:, cols]
                c_o[:, cols] = -w_g[:, cols] * a_h

    shp = jax.ShapeDtypeStruct((S, GW), F32)
    return _pcall(
        body, name=name, grid=(S // T,), in_specs=[_rows(T, GW)] * 7, out_specs=(_rows(T, GW),) * 6,
        out_shape=(shp,) * 6, compiler_params=_cparams("parallel"),
    )(d_attn, *o, *lse)


def _halo_prev(T, H, C, col):
    k = T // H
    return pl.BlockSpec((H, C), lambda i: (jnp.maximum(i * k - 1, 0), col))


def _halo_next(T, H, C, col, n_rows):
    k = T // H
    last = n_rows // H - 1
    return pl.BlockSpec((H, C), lambda i: (jnp.minimum((i + 1) * k, last), col))


CONV_RC = 64
SUBLANES = 8


def _tap_groups(offs):
    groups = {}
    for k, off in offs:
        groups.setdefault(off % SUBLANES, []).append((k, off))
    return [taps for _, taps in sorted(groups.items())]


def _for_taps(src, r0, lanes, offs, fn):
    for taps in _tap_groups(offs):
        lo = min(off for _, off in taps)
        hi = max(off for _, off in taps)
        sb = src[r0 + lo:r0 + hi + CONV_RC, lanes]
        for k, off in taps:
            fn(k, sb[off - lo:off - lo + CONV_RC])


def _dwconv(src, w_r, offs, T, C, bias_r, dst):
    for rc in range(T // CONV_RC):
        for cc in range(C // LANE):
            lanes = slice(cc * LANE, (cc + 1) * LANE)
            r0 = rc * CONV_RC
            acc = None if bias_r is None else jnp.zeros((CONV_RC, LANE), F32) + bias_r[:, lanes]
            for taps in _tap_groups(offs):
                lo = min(off for _, off in taps)
                hi = max(off for _, off in taps)
                sb = src[r0 + lo:r0 + hi + CONV_RC, lanes]
                g_acc = None
                for k, off in taps:
                    term = w_r[k:k + 1, lanes] * sb[off - lo:off - lo + CONV_RC]
                    g_acc = term if g_acc is None else g_acc + term
                acc = g_acc if acc is None else acc + g_acc
            dst[r0:r0 + CONV_RC, lanes] = acc


def _fill_glu(cv, cg, hv, hg, ubuf, i):
    T = cv.shape[0]
    live = jnp.where(i > 0, 1.0, 0.0)
    ubuf[0:CONV_HALO, :] = live * (hv[...] * _sig(hg[...]))
    ubuf[CONV_HALO:CONV_HALO + T, :] = cv[...] * _sig(cg[...])


_CONV_FWD_OFFS = [(k, CONV_HALO - (CONV_K - 1) + k) for k in range(CONV_K)]
_CONV_BWD_OFFS = [(k, CONV_K - 1 - k) for k in range(CONV_K)]


def convb_fwd(zc, w, b, g_ln, b_ln, *, name):
    S = zc.shape[0]
    C = zc.shape[1] // 2
    T = ROW_TILE

    def body(cv, cg, hv, hg, w_r, b_r, g_r, bl_r, u_o, y_o, ubuf):
        _fill_glu(cv, cg, hv, hg, ubuf, pl.program_id(0))
        _dwconv(ubuf, w_r, _CONV_FWD_OFFS, T, C, b_r, y_o)
        y = y_o[...]
        mu = jnp.mean(y, axis=-1, keepdims=True)
        yc = y - mu
        rs = lax.rsqrt(jnp.mean(yc * yc, axis=-1, keepdims=True) + EPS)
        v = yc * rs * g_r[...] + bl_r[...]
        u_o[...] = (v * _sig(v)).astype(BF16)

    vec = _full((1, C))
    return _pcall(
        body, name=name, grid=(S // T,),
        in_specs=[_rows(T, C, 0), _rows(T, C, 1), _halo_prev(T, CONV_HALO, C, 0), _halo_prev(T, CONV_HALO, C, 1),
                  _full((CONV_K, C)), vec, vec, vec],
        out_specs=(_rows(T, C), _rows(T, C)),
        out_shape=(jax.ShapeDtypeStruct((S, C), BF16), jax.ShapeDtypeStruct((S, C), F32)),
        scratch_shapes=[pltpu.VMEM((CONV_HALO + T, C), F32)],
        compiler_params=_cparams("parallel"),
    )(zc, zc, zc, zc, w, b, g_ln, b_ln)


def convb_bwd1(d_u2, y_conv, zc, g_ln, b_ln, *, name):
    S = zc.shape[0]
    C = zc.shape[1] // 2
    T = ROW_TILE

    def body(du_r, y_r, cv, cg, hv, hg, g_r, bl_r, dy_o, dw_o, db_o, dg_o, dbl_o, ubuf):
        i = pl.program_id(0)
        _fill_glu(cv, cg, hv, hg, ubuf, i)
        y = y_r[...]
        mu = jnp.mean(y, axis=-1, keepdims=True)
        yc = y - mu
        rs = lax.rsqrt(jnp.mean(yc * yc, axis=-1, keepdims=True) + EPS)
        yn = yc * rs
        v = yn * g_r[...] + bl_r[...]
        sg = _sig(v)
        dv = du_r[...] * (sg * (1.0 + v * (1.0 - sg)))
        dyn = dv * g_r[...]
        dy = rs * (dyn - jnp.mean(dyn, axis=-1, keepdims=True) - yn * jnp.mean(dyn * yn, axis=-1, keepdims=True))
        dy_o[...] = dy
        _acc_rows(dg_o, jnp.sum(dv * yn, axis=0, keepdims=True), i)
        _acc_rows(dbl_o, jnp.sum(dv, axis=0, keepdims=True), i)
        _acc_rows(db_o, jnp.sum(dy, axis=0, keepdims=True), i)

        @pl.when(i == 0)
        def _():
            dw_o[...] = jnp.zeros_like(dw_o)
        for cc in range(C // LANE):
            lanes = slice(cc * LANE, (cc + 1) * LANE)
            parts = [jnp.zeros((SUBLANES, LANE), F32) for _ in range(CONV_K)]
            for rc in range(T // CONV_RC):
                r0 = rc * CONV_RC
                dyc = dy_o[r0:r0 + CONV_RC, lanes]

                def tap(k, chunk, parts=parts, dyc=dyc):
                    prod = (dyc * chunk).reshape(CONV_RC // SUBLANES, SUBLANES, LANE)
                    parts[k] = parts[k] + jnp.sum(prod, axis=0)

                _for_taps(ubuf, r0, lanes, _CONV_FWD_OFFS, tap)
            for k in range(CONV_K):
                dw_o[k * SUBLANES:(k + 1) * SUBLANES, lanes] += parts[k]

    vec = _full((1, C))
    part = jax.ShapeDtypeStruct((8, C), F32)
    return _pcall(
        body, name=name, grid=(S // T,),
        in_specs=[_rows(T, C), _rows(T, C), _rows(T, C, 0), _rows(T, C, 1), _halo_prev(T, CONV_HALO, C, 0),
                  _halo_prev(T, CONV_HALO, C, 1), vec, vec],
        out_specs=(_rows(T, C), _full((CONV_K * SUBLANES, C)), _full((8, C)), _full((8, C)), _full((8, C))),
        out_shape=(jax.ShapeDtypeStruct((S, C), F32), jax.ShapeDtypeStruct((CONV_K * SUBLANES, C), F32),
                   part, part, part),
        scratch_shapes=[pltpu.VMEM((CONV_HALO + T, C), F32)],
        compiler_params=_cparams("arbitrary"),
    )(d_u2, y_conv, zc, zc, zc, zc, g_ln, b_ln)


def convb_bwd2(dy, zc, w, *, name):
    S = zc.shape[0]
    C = zc.shape[1] // 2
    T = ROW_TILE
    nblk = S // T

    def body(dy_r, dyn_r, cv, cg, w_r, dz_o, dbuf, dubuf):
        i = pl.program_id(0)
        live = jnp.where(i < nblk - 1, 1.0, 0.0)
        dbuf[0:T, :] = dy_r[...]
        dbuf[T:T + CONV_HALO, :] = live * dyn_r[...]
        _dwconv(dbuf, w_r, _CONV_BWD_OFFS, T, C, None, dubuf)
        du = dubuf[...]
        sg = _sig(cg[...])
        dz_o[:, 0:C] = (du * sg).astype(BF16)
        dz_o[:, C:2 * C] = (du * cv[...] * sg * (1.0 - sg)).astype(BF16)

    return _pcall(
        body, name=name, grid=(nblk,),
        in_specs=[_rows(T, C), _halo_next(T, CONV_HALO, C, 0, S), _rows(T, C, 0), _rows(T, C, 1), _full((CONV_K, C))],
        out_specs=_rows(T, 2 * C), out_shape=jax.ShapeDtypeStruct((S, 2 * C), BF16),
        scratch_shapes=[pltpu.VMEM((T + CONV_HALO, C), F32), pltpu.VMEM((T, C), F32)],
        compiler_params=_cparams("parallel"),
    )(dy, dy, zc, zc, w)


def merge_fwd(y_a, y_b, zg, *, name):
    S, D = y_a.shape
    T = ROW_TILE

    def body(a_r, b_r, ga_r, gb_r, m_o):
        m_o[...] = (_sig(ga_r[...]) * a_r[...] + _sig(gb_r[...]) * b_r[...]).astype(BF16)

    return _pcall(
        body, name=name, grid=(S // T,),
        in_specs=[_rows(T, D), _rows(T, D), _rows(T, D, 0), _rows(T, D, 1)],
        out_specs=_rows(T, D), out_shape=jax.ShapeDtypeStruct((S, D), BF16),
        compiler_params=_cparams("parallel"),
    )(y_a, y_b, zg, zg)


def merge_bwd(d_m, y_a, y_b, zg, *, name):
    S, D = y_a.shape
    T = ROW_TILE

    def body(dm_r, a_r, b_r, ga_r, gb_r, da_o, db_o, dz_o):
        dm = dm_r[...]
        sa, sb = _sig(ga_r[...]), _sig(gb_r[...])
        da_o[...] = (dm * sa).astype(BF16)
        db_o[...] = (dm * sb).astype(BF16)
        dz_o[:, 0:D] = (dm * a_r[...] * sa * (1.0 - sa)).astype(BF16)
        dz_o[:, D:2 * D] = (dm * b_r[...] * sb * (1.0 - sb)).astype(BF16)

    shp = jax.ShapeDtypeStruct((S, D), BF16)
    return _pcall(
        body, name=name, grid=(S // T,),
        in_specs=[_rows(T, D), _rows(T, D), _rows(T, D), _rows(T, D, 0), _rows(T, D, 1)],
        out_specs=(_rows(T, D), _rows(T, D), _rows(T, 2 * D)),
        out_shape=(shp, shp, jax.ShapeDtypeStruct((S, 2 * D), BF16)),
        compiler_params=_cparams("parallel"),
    )(d_m, y_a, y_b, zg, zg)


FFN_RC = 64


def _ffn_chunks(T, F):
    for cc in range(F // LANE):
        for rc in range(T // FFN_RC):
            yield rc * FFN_RC, slice(rc * FFN_RC, (rc + 1) * FFN_RC), slice(cc * LANE, (cc + 1) * LANE)


def _ffn_fill(g_r, hg_r, gbuf, i):
    T = g_r.shape[0]
    live = jnp.where(i > 0, 1.0, 0.0)
    gbuf[0:FFN_HALO, :] = live * hg_r[...]
    gbuf[FFN_HALO:FFN_HALO + T, :] = g_r[...]


def _ffn_gate_chunk(gbuf, w_r, b_r, r0, lanes):
    taps = [gbuf[r0 + FFN_HALO - (FFN_K - 1) + k:r0 + FFN_HALO - (FFN_K - 1) + k + FFN_RC, lanes]
            for k in range(FFN_K)]
    gp = b_r[:, lanes] + w_r[0:1, lanes] * taps[0]
    for k in range(1, FFN_K):
        gp = gp + w_r[k:k + 1, lanes] * taps[k]
    return gp, taps


def _sum8(v):
    return jnp.sum(v.reshape(v.shape[0] // SUBLANES, SUBLANES, v.shape[1]), axis=0)


def ffn_act_fwd(gu, w, b, *, name):
    S = gu.shape[0]
    F = gu.shape[1] // 2
    T = ROW_TILE // 2

    def body(g_r, u_r, hg_r, w_r, b_r, a_o, gbuf):
        _ffn_fill(g_r, hg_r, gbuf, pl.program_id(0))
        for r0, rows, lanes in _ffn_chunks(T, F):
            gp, _ = _ffn_gate_chunk(gbuf, w_r, b_r, r0, lanes)
            a_o[rows, lanes] = (gp * _sig(gp) * u_r[rows, lanes]).astype(BF16)

    return _pcall(
        body, name=name, grid=(S // T,),
        in_specs=[_rows(T, F, 0), _rows(T, F, 1), _halo_prev(T, FFN_HALO, F, 0), _full((FFN_K, F)), _full((1, F))],
        out_specs=_rows(T, F), out_shape=jax.ShapeDtypeStruct((S, F), BF16),
        scratch_shapes=[pltpu.VMEM((FFN_HALO + T, F), F32)],
        compiler_params=_cparams("parallel"),
    )(gu, gu, gu, w, b)


def ffn_act_bwd1(d_a, gu, w, b, *, name):
    S = gu.shape[0]
    F = gu.shape[1] // 2
    T = ROW_TILE // 2

    def body(da_r, g_r, u_r, hg_r, w_r, b_r, dgp_o, dup_o, dw_o, db_o, gbuf):
        i = pl.program_id(0)
        _ffn_fill(g_r, hg_r, gbuf, i)

        @pl.when(i == 0)
        def _():
            dw_o[...] = jnp.zeros_like(dw_o)
            db_o[...] = jnp.zeros_like(db_o)

        sums = None
        for r0, rows, lanes in _ffn_chunks(T, F):
            if r0 == 0:
                sums = [jnp.zeros((SUBLANES, LANE), F32) for _ in range(FFN_K + 1)]
            gp, taps = _ffn_gate_chunk(gbuf, w_r, b_r, r0, lanes)
            sg = _sig(gp)
            da = da_r[rows, lanes]
            dup_o[rows, lanes] = (da * gp * sg).astype(BF16)
            dgp = da * u_r[rows, lanes] * (sg * (1.0 + gp * (1.0 - sg)))
            dgp_o[rows, lanes] = dgp
            sums[FFN_K] = sums[FFN_K] + _sum8(dgp)
            for k in range(FFN_K):
                sums[k] = sums[k] + _sum8(dgp * taps[k])
            if r0 + FFN_RC == T:
                db_o[:, lanes] += sums[FFN_K]
                for k in range(FFN_K):
                    dw_o[k * SUBLANES:(k + 1) * SUBLANES, lanes] += sums[k]

    return _pcall(
        body, name=name, grid=(S // T,),
        in_specs=[_rows(T, F), _rows(T, F, 0), _rows(T, F, 1), _halo_prev(T, FFN_HALO, F, 0),
                  _full((FFN_K, F)), _full((1, F))],
        out_specs=(_rows(T, F), _rows(T, F), _full((FFN_K * SUBLANES, F)), _full((SUBLANES, F))),
        out_shape=(jax.ShapeDtypeStruct((S, F), F32), jax.ShapeDtypeStruct((S, F), BF16),
                   jax.ShapeDtypeStruct((FFN_K * SUBLANES, F), F32), jax.ShapeDtypeStruct((SUBLANES, F), F32)),
        scratch_shapes=[pltpu.VMEM((FFN_HALO + T, F), F32)],
        compiler_params=_cparams("arbitrary"),
    )(d_a, gu, gu, gu, w, b)


def ffn_act_bwd2(dgp, dup, w, *, name):
    S, F = dgp.shape
    T = ROW_TILE // 2
    nblk = S // T

    def body(d_r, dn_r, up_r, w_r, o_o, dbuf):
        i = pl.program_id(0)
        live = jnp.where(i < nblk - 1, 1.0, 0.0)
        dbuf[0:T, :] = d_r[...]
        dbuf[T:T + FFN_HALO, :] = live * dn_r[...]
        for r0, rows, lanes in _ffn_chunks(T, F):
            dg = w_r[0:1, lanes] * dbuf[r0 + FFN_K - 1:r0 + FFN_K - 1 + FFN_RC, lanes]
            for k in range(1, FFN_K):
                off = r0 + FFN_K - 1 - k
                dg = dg + w_r[k:k + 1, lanes] * dbuf[off:off + FFN_RC, lanes]
            o_o[rows, lanes] = dg.astype(BF16)
        o_o[:, F:2 * F] = up_r[...]

    return _pcall(
        body, name=name, grid=(nblk,),
        in_specs=[_rows(T, F), _halo_next(T, FFN_HALO, F, 0, S), _rows(T, F), _full((FFN_K, F))],
        out_specs=_rows(T, 2 * F), out_shape=jax.ShapeDtypeStruct((S, 2 * F), BF16),
        scratch_shapes=[pltpu.VMEM((T + FFN_HALO, F), F32)],
        compiler_params=_cparams("parallel"),
    )(dgp, dgp, dup, w)


def _row_tile(R, target=512):
    if R <= target:
        return R
    for t in range(target, 7, -8):
        if R % t == 0:
            return t
    return R


def sum_slots(land, *, name):
    _, R, C = land.shape
    T = _row_tile(R)

    def body(l_r, o_o):
        acc = l_r[0].astype(F32)
        for q in range(1, N_DEV):
            acc = acc + l_r[q].astype(F32)
        o_o[...] = acc

    return _pcall(
        body, name=name, grid=(R // T,),
        in_specs=[pl.BlockSpec((N_DEV, T, C), lambda i: (0, i, 0))],
        out_specs=_rows(T, C), out_shape=jax.ShapeDtypeStruct((R, C), F32),
        compiler_params=_cparams("parallel"),
    )(land)


def adamw(w, g, m, v, *, name):
    shape = w.shape
    C = shape[-1]
    R = math.prod(shape[:-1])
    w2, g2, m2, v2 = (t.reshape(R, C) for t in (w, g, m, v))
    T = _row_tile(R)
    c1 = 1.0 - ADAM_B1 ** ADAM_STEP
    c2 = 1.0 - ADAM_B2 ** ADAM_STEP

    def body(w_r, g_r, m_r, v_r, d_o, m_o, v_o):
        gv = g_r[...]
        mn = ADAM_B1 * m_r[...] + (1.0 - ADAM_B1) * gv
        vn = ADAM_B2 * v_r[...] + (1.0 - ADAM_B2) * (gv * gv)
        m_o[...] = mn
        v_o[...] = vn
        d_o[...] = -ADAM_LR * ((mn / c1) / (jnp.sqrt(vn / c2) + ADAM_EPS) + ADAM_WD * w_r[...])

    shp = jax.ShapeDtypeStruct((R, C), F32)
    d, mn, vn = _pcall(
        body, name=name, grid=(R // T,), in_specs=[_rows(T, C)] * 4, out_specs=(_rows(T, C),) * 3,
        out_shape=(shp,) * 3, compiler_params=_cparams("parallel"),
    )(w2, g2, m2, v2)
    return d.reshape(shape), mn.reshape(shape), vn.reshape(shape)


def _my_pos():
    return lax.axis_index("x"), lax.axis_index("y"), lax.axis_index("c")


class _Exchange:
    def __init__(self, inputs, out_shapes, sems, start, finish):
        self.inputs, self.out_shapes, self.sems, self.start, self.finish = inputs, out_shapes, sems, start, finish


def gather_exchange(shards):
    n = len(shards)

    def plan(ins, outs, sems):
        send_sems, recv_sems, local_sems = sems
        x, y, c = _my_pos()
        me, sibling = (x, y, c), (x, y, 1 - c)
        chips = [(1 - x, y), (x, 1 - y), (1 - x, 1 - y)]

        def slot(i, p):
            return outs[i].at[4 * p[0] + 2 * p[1] + p[2]]

        def copy(k, i, block, to, src=None):
            return pltpu.make_async_remote_copy(
                src_ref=slot(i, block) if src is None else src, dst_ref=slot(i, block),
                send_sem=send_sems.at[k, i], recv_sem=recv_sems.at[k, i], device_id=to, device_id_type=MESH)

        mine = [pltpu.make_async_copy(ins[i], slot(i, me), local_sems.at[i]) for i in range(n)]
        first = []
        for i in range(n):
            first.append(copy(0, i, me, sibling, src=ins[i]))
            first += [copy(1 + j, i, me, (*chip, c), src=ins[i]) for j, chip in enumerate(chips)]
        return me, sibling, chips, c, copy, mine, first

    def start(ins, outs, sems):
        _, _, _, _, _, mine, first = plan(ins, outs, sems)
        for cp in mine + first:
            cp.start()

    def finish(ins, outs, sems):
        me, sibling, chips, c, copy, mine, first = plan(ins, outs, sems)
        passed = []
        for j, chip in enumerate(chips):
            for i in range(n):
                copy(1 + j, i, (*chip, c), me).wait_recv()
                cp = copy(4 + j, i, (*chip, c), sibling)
                cp.start()
                passed.append(cp)
        for i in range(n):
            copy(0, i, sibling, me).wait_recv()
            for j, chip in enumerate(chips):
                copy(4 + j, i, (*chip, 1 - c), me).wait_recv()
        for cp in first + passed:
            cp.wait_send()
        for cp in mine:
            cp.wait()

    outs = [jax.ShapeDtypeStruct((N_DEV,) + s.shape, s.dtype) for s in shards]
    sems = [pltpu.SemaphoreType.DMA((7, n)), pltpu.SemaphoreType.DMA((7, n)), pltpu.SemaphoreType.DMA((n,))]
    return _Exchange(list(shards), outs, sems, start, finish)


def scatter_exchange(gs):
    n = len(gs)

    def plan(ins, outs, sems):
        send_sems, recv_sems, local_sems = sems
        x, y, c = _my_pos()
        me_id = 4 * x + 2 * y + c
        mine = [pltpu.make_async_copy(ins[i].at[me_id], outs[i].at[me_id], local_sems.at[i]) for i in range(n)]
        sends, recvs = [], []
        for msk in range(1, N_DEV):
            px = 1 - x if msk & 4 else x
            py = 1 - y if msk & 2 else y
            pc = 1 - c if msk & 1 else c
            pid = 4 * px + 2 * py + pc
            for i in range(n):
                sends.append(pltpu.make_async_remote_copy(
                    src_ref=ins[i].at[pid], dst_ref=outs[i].at[me_id],
                    send_sem=send_sems.at[msk - 1, i], recv_sem=recv_sems.at[msk - 1, i],
                    device_id=(px, py, pc), device_id_type=MESH))
                recvs.append(pltpu.make_async_remote_copy(
                    src_ref=ins[i].at[pid], dst_ref=outs[i].at[pid],
                    send_sem=send_sems.at[msk - 1, i], recv_sem=recv_sems.at[msk - 1, i],
                    device_id=(px, py, pc), device_id_type=MESH))
        return mine, sends, recvs

    def start(ins, outs, sems):
        mine, sends, _ = plan(ins, outs, sems)
        for cp in mine + sends:
            cp.start()

    def finish(ins, outs, sems):
        mine, sends, recvs = plan(ins, outs, sems)
        for rv in recvs:
            rv.wait_recv()
        for cp in sends:
            cp.wait_send()
        for cp in mine:
            cp.wait()

    outs = [jax.ShapeDtypeStruct(g.shape, g.dtype) for g in gs]
    sems = [pltpu.SemaphoreType.DMA((7, n)), pltpu.SemaphoreType.DMA((7, n)), pltpu.SemaphoreType.DMA((n,))]
    return _Exchange(list(gs), outs, sems, start, finish)


def run_exchange(ex, *, name):
    HBM = pl.BlockSpec(memory_space=pl.ANY)
    n_in, n_out = len(ex.inputs), len(ex.out_shapes)

    def body(*refs):
        ins, outs, sems = refs[:n_in], refs[n_in:n_in + n_out], refs[n_in + n_out:]
        ex.start(ins, outs, sems)
        ex.finish(ins, outs, sems)

    return list(_pcall(body, name=name, in_specs=[HBM] * n_in, out_specs=tuple([HBM] * n_out),
                       out_shape=tuple(ex.out_shapes), scratch_shapes=ex.sems)(*ex.inputs))


def all_gather(xs, *, name):
    outs = run_exchange(gather_exchange([xs[l] for l in range(xs.shape[0])]), name=name)
    return jnp.stack(outs)


def _slots(g):
    return g.reshape(N_DEV, g.shape[0] // N_DEV, g.shape[1])


def _rope_tables(positions):
    half = ROT_DIM // 2
    inv_freq = ROPE_THETA ** (-jnp.arange(0, ROT_DIM, 2, dtype=F32) / ROT_DIM)
    ang = positions.astype(F32)[:, None] * inv_freq
    cos, sin = jnp.cos(ang), jnp.sin(ang)
    S = positions.shape[0]
    c_t = jnp.concatenate([cos, cos, jnp.ones((S, LANE - ROT_DIM), F32)], axis=1)
    s1_t = jnp.concatenate([-sin, jnp.zeros((S, LANE - half), F32)], axis=1)
    s2_t = jnp.concatenate([jnp.zeros((S, half), F32), sin, jnp.zeros((S, LANE - ROT_DIM), F32)], axis=1)
    return c_t, s1_t, s2_t


def _row(v):
    return v.reshape(1, -1)


def kernel(x, c, positions, w_ada, b_ada, g_norm1, w_in, g_q, g_k, w_attn_proj, w_conv_dw, b_conv_dw, g_conv_ln, b_conv_ln, w_conv_out, w_o, g_norm2, w_ffn_in, w_ffn_dw, b_ffn_dw, w_ffn_down, loss_target, m_w_ada, m_b_ada, m_g_norm1, m_w_in, m_g_q, m_g_k, m_w_attn_proj, m_w_conv_dw, m_b_conv_dw, m_g_conv_ln, m_b_conv_ln, m_w_conv_out, m_w_o, m_g_norm2, m_w_ffn_in, m_w_ffn_dw, m_b_ffn_dw, m_w_ffn_down, v_w_ada, v_b_ada, v_g_norm1, v_w_in, v_g_q, v_g_k, v_w_attn_proj, v_w_conv_dw, v_b_conv_dw, v_g_conv_ln, v_b_conv_ln, v_w_conv_out, v_w_o, v_g_norm2, v_w_ffn_in, v_w_ffn_dw, v_b_ffn_dw, v_w_ffn_down):
    L = w_in.shape[0]
    S, D = x.shape[1], x.shape[2]
    FF = w_ffn_down.shape[1] * N_DEV
    xi, yi, ci = _my_pos()
    me = 4 * xi + 2 * yi + ci
    x0 = x[0]
    tabs = _rope_tables(positions[0])

    c_act = c * _sig(c)
    c_all = all_gather(jnp.pad(c_act, ((0, 7), (0, 0)))[None], name="ag_c")[0][:, 0, :]
    c_all16 = jnp.pad(c_all, ((0, 8), (0, 0)))
    m_part = jnp.stack([mm(c_all16, w_ada[l], "nn", name="mod_mm") for l in range(L)])
    m_all = all_gather(m_part, name="ag_mod")
    mod = lax.dynamic_index_in_dim(m_all, me, axis=2, keepdims=False).reshape(L, 6 * D) + b_ada
    mod = mod.reshape(L, 6, 1, D)

    sh_in = jnp.transpose(w_in, (0, 2, 1)).astype(BF16)
    sh_fi = jnp.transpose(w_ffn_in, (0, 2, 1)).astype(BF16)
    sh_ap = jnp.transpose(w_attn_proj, (0, 2, 1)).astype(BF16)
    sh_co, sh_oo, sh_dn = w_conv_out.astype(BF16), w_o.astype(BF16), w_ffn_down.astype(BF16)

    def rowcat(g):
        return g.reshape(N_DEV * g.shape[1], g.shape[2])

    wt_in, wt_fi, wt_ap, w_co, w_oo, w_dn = ([None] * L for _ in range(6))
    got = run_exchange(gather_exchange([sh_in[0], sh_fi[0], sh_ap[0], sh_co[0], sh_oo[0], sh_dn[0]]), name="ag_w0")
    wt_in[0], wt_fi[0], wt_ap[0], w_co[0], w_oo[0], w_dn[0] = (rowcat(g) for g in got)
    cdw = all_gather(jnp.pad(w_conv_dw, ((0, 0), (0, 1), (0, 0))), name="ag_cdw")
    cdw = jnp.transpose(cdw, (0, 2, 1, 3)).reshape(L, 32, D)[:, :CONV_K]
    fsh = w_ffn_dw.shape[2]
    fpad = -fsh % LANE
    fdw = all_gather(jnp.pad(w_ffn_dw, ((0, 0), (0, 8 - FFN_K), (0, fpad))), name="ag_fdw")
    fdw = jnp.transpose(fdw[:, :, :FFN_K, :fsh], (0, 2, 1, 3)).reshape(L, FFN_K, FF)

    QKV, CW = 3 * ATTN_W, 2 * D
    seg = ((0, QKV), (QKV, CW), (QKV + CW, 2 * D))

    saved = []
    x_prev, delta, gt_prev = x0, None, None
    for l in range(L):
        sh1, sc1, gt1, sh2, sc2, gt2 = (mod[l, i] for i in range(6))
        x_l, h = norm_mod_fwd(x_prev, delta, gt_prev, _row(g_norm1[l]), sc1, sh1, name="norm_fwd")
        nxt = l + 1 < L
        if nxt:
            zq, got = mm(h, wt_in[l], "nt", b_off=0, b_len=QKV, name="z_mm_ag", comm=gather_exchange([sh_in[l + 1]]))
            wt_in[l + 1] = rowcat(got[0])
        else:
            zq = mm(h, wt_in[l], "nt", b_off=0, b_len=QKV, name="z_mm")
        zc, zg = (mm(h, wt_in[l], "nt", b_off=o, b_len=n, name="z_mm") for o, n in seg[1:])
        gq, gk = _row(g_q[l]), _row(g_k[l])
        qn, kn = qk_prep_fwd(zq, gq, gk, tabs, name="qk_prep")
        o_g, lse_g = [], []
        for gi in range(3):
            o_i, l_i = attn_fwd(qn, kn, zq, gi, name="attn_fwd%d" % gi)
            o_g.append(o_i)
            lse_g.append(l_i)
        attn = combine_fwd(o_g, lse_g, name="combine_fwd")
        y_a = mm(attn, wt_ap[l], "nt", name="ya_mm")
        cw, cb = cdw[l], _row(b_conv_dw[l])
        cg, cbl = _row(g_conv_ln[l]), _row(b_conv_ln[l])
        u2, y_conv = convb_fwd(zc, cw, cb, cg, cbl, name="convb_fwd")
        y_b = mm(u2, w_co[l], "nn", name="yb_mm")
        merged = merge_fwd(y_a, y_b, zg, name="merge_fwd")
        mo = mm(merged, w_oo[l], "nn", name="mo_mm")
        x_mid, h2 = norm_mod_fwd(x_l, mo, gt1, _row(g_norm2[l]), sc2, sh2, name="norm_fwd")
        if nxt:
            gu, got = mm(h2, wt_fi[l], "nt", name="gu_mm_ag", comm=gather_exchange([sh_fi[l + 1]]))
            wt_fi[l + 1] = rowcat(got[0])
        else:
            gu = mm(h2, wt_fi[l], "nt", name="gu_mm")
        fw, fb = fdw[l], _row(b_ffn_dw[l])
        act = ffn_act_fwd(gu, fw, fb, name="ffn_act")
        if nxt:
            ffo, got = mm(act, w_dn[l], "nn", name="ffo_mm_ag",
                          comm=gather_exchange([sh_ap[l + 1], sh_co[l + 1], sh_oo[l + 1], sh_dn[l + 1]]))
            wt_ap[l + 1], w_co[l + 1], w_oo[l + 1], w_dn[l + 1] = (rowcat(g) for g in got)
        else:
            ffo = mm(act, w_dn[l], "nn", name="ffo_mm")
        saved.append(dict(x=x_l, h=h, zq=zq, zc=zc, zg=zg, qn=qn, kn=kn, o=o_g, lse=lse_g, attn=attn, y_a=y_a,
                          u2=u2, y_conv=y_conv, y_b=y_b, merged=merged, mo=mo, x_mid=x_mid, h2=h2, gu=gu, act=act, ffo=ffo))
        x_prev, delta, gt_prev = x_mid, ffo, gt2

    dx, lpart = loss_head(x_prev, delta, gt_prev, loss_target[0], name="loss_head")
    loss = lax.psum(0.5 / D * jnp.sum(lpart[0]), ("x", "y", "c"))

    land = {k: [None] * L for k in ("in", "fi", "ap", "co", "o", "dn")}
    small_rows = []
    g_in_pending = None
    for l in reversed(range(L)):
        sv = saved[l]
        sh1, sc1, gt1, sh2, sc2, gt2 = (mod[l, i] for i in range(6))
        d_ffo, p_gt2 = scale_bwd(dx, sv["ffo"], gt2, name="scale_bwd")
        if g_in_pending is not None:
            d_act, got = mm(d_ffo, w_dn[l], "nt", name="dact_mm_rs", comm=scatter_exchange([_slots(g_in_pending)]))
            land["in"][l + 1] = got[0]
        else:
            d_act = mm(d_ffo, w_dn[l], "nt", name="dact_mm")
        g_dn = mm(sv["act"], d_ffo, "tn", out_dtype=BF16, name="dwdn_mm")
        fw, fb = fdw[l], _row(b_ffn_dw[l])
        dgp, dup, p_fw, p_fb = ffn_act_bwd1(d_act, sv["gu"], fw, fb, name="ffn_bwd1")
        dgu = ffn_act_bwd2(dgp, dup, fw, name="ffn_bwd2")
        dh2, got = mm(dgu, wt_fi[l], "nn", name="dh2_mm_rs", comm=scatter_exchange([_slots(g_dn)]))
        land["dn"][l] = got[0]
        g_fi = mm(dgu, sv["h2"], "tn", out_dtype=BF16, name="dwfi_mm")
        dx, p_g2, p_sc2, p_sh2 = norm_mod_bwd(sv["x_mid"], dh2, _row(g_norm2[l]), sc2, sh2, dx, name="norm_bwd")
        d_mo, p_gt1 = scale_bwd(dx, sv["mo"], gt1, name="scale_bwd")
        d_merged = mm(d_mo, w_oo[l], "nt", name="dmerged_mm")
        g_o = mm(sv["merged"], d_mo, "tn", out_dtype=BF16, name="dwo_mm")
        d_ya, d_yb, dzg = merge_bwd(d_merged, sv["y_a"], sv["y_b"], sv["zg"], name="merge_bwd")
        d_attn = mm(d_ya, wt_ap[l], "nn", name="dattn_mm")
        g_ap = mm(d_ya, sv["attn"], "tn", out_dtype=BF16, name="dwap_mm")
        d_u2 = mm(d_yb, w_co[l], "nt", name="du2_mm")
        g_co = mm(sv["u2"], d_yb, "tn", out_dtype=BF16, name="dwco_mm")
        cw, cb = cdw[l], _row(b_conv_dw[l])
        cg, cbl = _row(g_conv_ln[l]), _row(b_conv_ln[l])
        dy, p_cw, p_cb, p_cg, p_cbl = convb_bwd1(d_u2, sv["y_conv"], sv["zc"], cg, cbl, name="convb_bwd1")
        dzc = convb_bwd2(dy, sv["zc"], cw, name="convb_bwd2")
        dd = combine_bwd(d_attn, sv["o"], sv["lse"], name="combine_bwd")
        do_g, cc_g = dd[:3], dd[3:]
        parts = [attn_bwd(sv["qn"], sv["kn"], sv["zq"], do_g[gi], sv["lse"][gi], cc_g[gi], gi,
                          name="attn_bwd%d" % gi) for gi in range(3)]
        gq, gk = _row(g_q[l]), _row(g_k[l])
        dzq, p_gq, p_gk = attn_bwd_post(sv["zq"], gq, gk, tabs, *[[p[i] for p in parts] for i in range(5)],
                                        name="attn_post")
        dh, got = mm(dzq, wt_in[l], "nn", b_off=0, name="dh_mm_rs", comm=scatter_exchange([_slots(g_fi)]))
        land["fi"][l] = got[0]
        for dz_s, (o, n) in zip((dzc, dzg), seg[1:]):
            dh = mm(dz_s, wt_in[l], "nn", b_off=o, c_in=dh, name="dh_mm")
        g_in_q, got = mm(dzq, sv["h"], "tn", out_dtype=BF16, name="dwin_mm_rs",
                         comm=scatter_exchange([_slots(g_o), _slots(g_ap), _slots(g_co)]))
        land["o"][l], land["ap"][l], land["co"][l] = got
        g_in_pending = jnp.concatenate([g_in_q] + [mm(dz_s, sv["h"], "tn", out_dtype=BF16, name="dwin_mm")
                                                   for dz_s in (dzc, dzg)], axis=0)
        dx, p_g1, p_sc1, p_sh1 = norm_mod_bwd(sv["x"], dh, _row(g_norm1[l]), sc1, sh1, dx, name="norm_bwd")

        def row1k(p):
            v = p[0]
            pad = -v.shape[0] % D
            return jnp.pad(v, (0, pad)).reshape(-1, D)

        rows = [row1k(p) for p in (p_sh1, p_sc1, p_gt1, p_sh2, p_sc2, p_gt2, p_g1, p_g2)]
        rows.append(row1k(jnp.concatenate([p_gq, p_gk], axis=1)))
        rows += [row1k(p) for p in (p_cb, p_cg, p_cbl, jnp.sum(p_fb, axis=0, keepdims=True))]
        rows.append(jnp.sum(p_cw.reshape(CONV_K, SUBLANES, D), axis=1))
        rows += [row1k(jnp.sum(p_fw[k * SUBLANES:(k + 1) * SUBLANES], axis=0, keepdims=True))
                 for k in range(FFN_K)]
        blk = jnp.concatenate(rows, axis=0)
        small_rows.append(jnp.pad(blk, ((0, -blk.shape[0] % 8), (0, 0))))
    land["in"][0] = run_exchange(scatter_exchange([_slots(g_in_pending)]), name="rs_in0")[0]
    small_rows = small_rows[::-1]
    n_small = small_rows[0].shape[0]
    ff_rows = -(-FF // D)

    small = jnp.concatenate(small_rows, axis=0)[None]
    small_all = all_gather(small, name="ag_small")[0]
    small_sum = sum_slots(small_all, name="sum_small").reshape(L, n_small, D)
    small_all = small_all.reshape(N_DEV, L, n_small, D)

    g_b_ada = small_sum[:, 0:6].reshape(L, 6 * D)
    g_g1, g_g2 = small_sum[:, 6], small_sum[:, 7]
    g_gq, g_gk = small_sum[:, 8, 0:LANE], small_sum[:, 8, LANE:2 * LANE]
    g_cb, g_cg, g_cbl = small_sum[:, 9], small_sum[:, 10], small_sum[:, 11]
    r0 = 12
    g_fb = small_sum[:, r0:r0 + ff_rows].reshape(L, -1)[:, :FF]
    r0 += ff_rows
    g_cw_full = small_sum[:, r0:r0 + CONV_K]
    r0 += CONV_K
    g_fw_full = small_sum[:, r0:r0 + FFN_K * ff_rows].reshape(L, FFN_K, -1)[:, :, :FF]
    csh = w_conv_dw.shape[2]
    g_cw = lax.dynamic_slice_in_dim(g_cw_full, me * csh, csh, axis=2)
    g_fw = lax.dynamic_slice_in_dim(g_fw_full, me * fsh, fsh, axis=2)

    ash = w_ada.shape[2]
    dmod_all = small_all[:, :, 0:6].reshape(N_DEV, L, 6 * D)
    dmod_mine = lax.dynamic_slice_in_dim(dmod_all, me * ash, ash, axis=2)
    g_w_ada = jnp.stack([mm(c_all16, jnp.pad(dmod_mine[:, l], ((0, 8), (0, 0))), "tn", name="dwada_mm")
                         for l in range(L)])

    def reduced(key, transposed):
        out = jnp.stack([sum_slots(slots, name="sum_" + key) for slots in land[key]])
        return jnp.transpose(out, (0, 2, 1)) if transposed else out

    g_w_in = reduced("in", True)
    g_w_fi = reduced("fi", True)
    g_w_ap = reduced("ap", True)
    g_w_co = reduced("co", False)
    g_w_o = reduced("o", False)
    g_w_dn = reduced("dn", False)

    grads = [g_w_ada, g_b_ada, g_g1, g_w_in, g_gq, g_gk, g_w_ap, g_cw, g_cb, g_cg, g_cbl, g_w_co, g_w_o, g_g2,
             g_w_fi, g_fw, g_fb, g_w_dn]
    ws = [w_ada, b_ada, g_norm1, w_in, g_q, g_k, w_attn_proj, w_conv_dw, b_conv_dw, g_conv_ln, b_conv_ln,
          w_conv_out, w_o, g_norm2, w_ffn_in, w_ffn_dw, b_ffn_dw, w_ffn_down]
    ms = [m_w_ada, m_b_ada, m_g_norm1, m_w_in, m_g_q, m_g_k, m_w_attn_proj, m_w_conv_dw, m_b_conv_dw, m_g_conv_ln,
          m_b_conv_ln, m_w_conv_out, m_w_o, m_g_norm2, m_w_ffn_in, m_w_ffn_dw, m_b_ffn_dw, m_w_ffn_down]
    vs = [v_w_ada, v_b_ada, v_g_norm1, v_w_in, v_g_q, v_g_k, v_w_attn_proj, v_w_conv_dw, v_b_conv_dw, v_g_conv_ln,
          v_b_conv_ln, v_w_conv_out, v_w_o, v_g_norm2, v_w_ffn_in, v_w_ffn_dw, v_b_ffn_dw, v_w_ffn_down]
    deltas, new_m, new_v = [], [], []
    for w_i, g_i, m_i, v_i in zip(ws, grads, ms, vs):
        d_i, mn_i, vn_i = adamw(w_i, g_i, m_i, v_i, name="adamw")
        deltas.append(d_i)
        new_m.append(mn_i)
        new_v.append(vn_i)
    return (loss, dx[None], *grads, *deltas, *new_m, *new_v)
```

```python
import functools
import math

import jax
import jax.numpy as jnp
from jax import lax
from jax.experimental import pallas as pl
from jax.experimental.pallas import tpu as pltpu

F32 = jnp.float32
BF16 = jnp.bfloat16
MESH = pl.DeviceIdType.MESH
N_DEV = 8

EPS = 1e-6
HEAD_DIM = 128
BLOCK = 128
DILATIONS = (1, 4, 16)
HEADS_PER_GROUP = 4
N_HEADS = 12
ATTN_W = N_HEADS * HEAD_DIM
ROT_DIM = 32
ROPE_THETA = 500000.0
CONV_K = 31
CONV_HALO = 32
FFN_K = 3
FFN_HALO = 8
NEG = -1e30

ADAM_LR, ADAM_B1, ADAM_B2, ADAM_EPS, ADAM_WD, ADAM_STEP = 0.001, 0.9, 0.999, 1e-08, 0.01, 10

LANE = 128
VMEM_LIMIT = 56 * 1024 * 1024
ROW_TILE = 512
POST_TILE = 256


def _pcall(body, **kw):
    return pl.pallas_call(body, **kw)


def _cparams(*sem):
    return pltpu.CompilerParams(dimension_semantics=sem, vmem_limit_bytes=VMEM_LIMIT)


def _sig(v):
    return 1.0 / (1.0 + jnp.exp(-v))


def _divtile(dim, target):
    best = None
    for t in range(LANE, min(dim, target) + 1, LANE):
        if dim % t == 0:
            best = t
    return best or dim


def _rows(t, c, col=0):
    return pl.BlockSpec((t, c), lambda i: (i, col))


def _full(shape):
    nd = len(shape)
    return pl.BlockSpec(shape, lambda i: (0,) * nd)


def _acc_rows(ref, val, i):
    @pl.when(i == 0)
    def _():
        ref[...] = jnp.zeros_like(ref)
    r = val.shape[0]
    ref[0:r, :] += val


MM_TILE = 1536


def mm(a, b, mode, *, name, out_dtype=F32, c_in=None, b_off=0, b_len=None, comm=None):
    if mode == "nn":
        M, K = a.shape
        N = b.shape[1]
    elif mode == "nt":
        M, K = a.shape
        N = b_len if b_len is not None else b.shape[0]
    else:
        K, M = a.shape
        N = b.shape[1]
    g_n = math.gcd(N, b_off) if (mode == "nt" and b_off) else N
    g_k = math.gcd(K, b_off) if (mode == "nn" and b_off) else K
    tn = _divtile(g_n, MM_TILE if c_in is None else 1024)
    tk = _divtile(g_k, MM_TILE if mode != "tn" else 1024)
    tm = _divtile(M, MM_TILE if mode == "tn" else (2048 if (tn <= 512 and c_in is None) else 1024))
    gm, gn, nk = M // tm, N // tn, K // tk
    n_ci = 0 if comm is None else len(comm.inputs)
    n_co = 0 if comm is None else len(comm.out_shapes)
    n_x = 2 + (c_in is not None)
    if mode == "nn":
        dims = (((1,), (0,)), ((), ()))
    elif mode == "nt":
        dims = (((1,), (1,)), ((), ()))
    else:
        dims = (((0,), (0,)), ((), ()))

    def body(*refs):
        a_ref, b_ref = refs[0], refs[1]
        c_ref = refs[2] if c_in is not None else None
        c_ins = refs[n_x:n_x + n_ci]
        o_ref = refs[n_x + n_ci]
        c_outs = refs[n_x + n_ci + 1:n_x + n_ci + 1 + n_co]
        rest = refs[n_x + n_ci + 1 + n_co:]
        acc = rest[0] if nk > 1 else None
        sems = rest[1:] if nk > 1 else rest
        i, j, k = pl.program_id(0), pl.program_id(1), pl.program_id(2)

        if comm is not None:
            @pl.when(jnp.logical_and(jnp.logical_and(i == 0, j == 0), k == 0))
            def _():
                comm.start(c_ins, c_outs, sems)

        prod = lax.dot_general(a_ref[...].astype(BF16), b_ref[...].astype(BF16), dims, preferred_element_type=F32)
        if nk == 1:
            if c_ref is not None:
                prod = prod + c_ref[...].astype(F32)
            o_ref[...] = prod.astype(out_dtype)
        else:
            @pl.when(k == 0)
            def _():
                if c_ref is None:
                    acc[...] = prod
                else:
                    acc[...] = prod + c_ref[...].astype(F32)

            @pl.when(k > 0)
            def _():
                acc[...] += prod

            @pl.when(k == nk - 1)
            def _():
                o_ref[...] = acc[...].astype(out_dtype)

        if comm is not None:
            @pl.when(jnp.logical_and(jnp.logical_and(i == gm - 1, j == gn - 1), k == nk - 1))
            def _():
                comm.finish(c_ins, c_outs, sems)

    if mode == "nn":
        a_spec = pl.BlockSpec((tm, tk), lambda i, j, k: (i, k))
        ob = b_off // tk
        b_spec = pl.BlockSpec((tk, tn), lambda i, j, k: (k + ob, j))
    elif mode == "nt":
        a_spec = pl.BlockSpec((tm, tk), lambda i, j, k: (i, k))
        ob = b_off // tn
        b_spec = pl.BlockSpec((tn, tk), lambda i, j, k: (j + ob, k))
    else:
        a_spec = pl.BlockSpec((tk, tm), lambda i, j, k: (k, i))
        b_spec = pl.BlockSpec((tk, tn), lambda i, j, k: (k, j))
    o_spec = pl.BlockSpec((tm, tn), lambda i, j, k: (i, j))
    HBM = pl.BlockSpec(memory_space=pl.ANY)
    in_specs = [a_spec, b_spec]
    args = [a, b]
    if c_in is not None:
        in_specs.append(o_spec)
        args.append(c_in)
    scratch = [pltpu.VMEM((tm, tn), F32)] if nk > 1 else []
    o_shape = jax.ShapeDtypeStruct((M, N), out_dtype)
    if comm is None:
        return _pcall(
            body, name=name, grid=(gm, gn, nk), in_specs=in_specs, out_specs=o_spec, out_shape=o_shape,
            scratch_shapes=scratch, compiler_params=_cparams("parallel", "parallel", "arbitrary"),
        )(*args)
    res = _pcall(
        body, name=name, grid=(gm, gn, nk), in_specs=in_specs + [HBM] * n_ci,
        out_specs=(o_spec, *[HBM] * n_co), out_shape=(o_shape, *comm.out_shapes),
        scratch_shapes=scratch + comm.sems, compiler_params=_cparams("arbitrary", "arbitrary", "arbitrary"),
    )(*args, *comm.inputs)
    return res[0], list(res[1:])


def norm_mod_fwd(x_prev, delta, gt, g, sc, sh, *, name):
    S, D = x_prev.shape
    T = ROW_TILE
    has_delta = delta is not None

    def body(*refs):
        if has_delta:
            xp, dl, gt_r, g_r, sc_r, sh_r, x_out, h_out = refs
            xv = xp[...] + gt_r[...] * dl[...]
            x_out[...] = xv
        else:
            xp, g_r, sc_r, sh_r, h_out = refs
            xv = xp[...]
        r = lax.rsqrt(jnp.mean(xv * xv, axis=-1, keepdims=True) + EPS)
        h_out[...] = ((xv * r) * g_r[...] * (1.0 + sc_r[...]) + sh_r[...]).astype(BF16)

    vec = _full((1, D))
    if has_delta:
        ins, specs = [x_prev, delta, gt, g, sc, sh], [_rows(T, D), _rows(T, D), vec, vec, vec, vec]
        outs = (jax.ShapeDtypeStruct((S, D), F32), jax.ShapeDtypeStruct((S, D), BF16))
        ospecs = (_rows(T, D), _rows(T, D))
    else:
        ins, specs = [x_prev, g, sc, sh], [_rows(T, D), vec, vec, vec]
        outs = jax.ShapeDtypeStruct((S, D), BF16)
        ospecs = _rows(T, D)
    res = _pcall(body, name=name, grid=(S // T,), in_specs=specs, out_specs=ospecs, out_shape=outs,
                 compiler_params=_cparams("parallel"))(*ins)
    return res if has_delta else (x_prev, res)


def norm_mod_bwd(x, dh, g, sc, sh, dx_res, *, name):
    S, D = x.shape
    T = ROW_TILE

    def body(x_r, dh_r, g_r, sc_r, sh_r, dr_r, dx_o, dg_o, dsc_o, dsh_o):
        i = pl.program_id(0)
        xv = x_r[...]
        dh_v = dh_r[...]
        r = lax.rsqrt(jnp.mean(xv * xv, axis=-1, keepdims=True) + EPS)
        xh = xv * r
        dn = dh_v * (1.0 + sc_r[...])
        dxh = dn * g_r[...]
        dx_o[...] = dr_r[...] + r * (dxh - xh * jnp.mean(dxh * xh, axis=-1, keepdims=True))
        _acc_rows(dg_o, jnp.sum(dn * xh, axis=0, keepdims=True), i)
        _acc_rows(dsc_o, jnp.sum(dh_v * (xh * g_r[...]), axis=0, keepdims=True), i)
        _acc_rows(dsh_o, jnp.sum(dh_v, axis=0, keepdims=True), i)

    vec = _full((1, D))
    part = jax.ShapeDtypeStruct((8, D), F32)
    return _pcall(
        body, name=name, grid=(S // T,),
        in_specs=[_rows(T, D), _rows(T, D), vec, vec, vec, _rows(T, D)],
        out_specs=(_rows(T, D), _full((8, D)), _full((8, D)), _full((8, D))),
        out_shape=(jax.ShapeDtypeStruct((S, D), F32), part, part, part),
        compiler_params=_cparams("arbitrary"),
    )(x, dh, g, sc, sh, dx_res)


def scale_bwd(dx, delta, gt, *, name):
    S, D = dx.shape
    T = ROW_TILE

    def body(dx_r, dl_r, gt_r, dd_o, dgt_o):
        i = pl.program_id(0)
        dv = dx_r[...]
        dd_o[...] = (dv * gt_r[...]).astype(BF16)
        _acc_rows(dgt_o, jnp.sum(dv * dl_r[...], axis=0, keepdims=True), i)

    return _pcall(
        body, name=name, grid=(S // T,),
        in_specs=[_rows(T, D), _rows(T, D), _full((1, D))],
        out_specs=(_rows(T, D), _full((8, D))),
        out_shape=(jax.ShapeDtypeStruct((S, D), BF16), jax.ShapeDtypeStruct((8, D), F32)),
        compiler_params=_cparams("arbitrary"),
    )(dx, delta, gt)


def loss_head(x_mid, ffo, gt, target, *, name):
    S, D = x_mid.shape
    T = ROW_TILE

    def body(x_r, f_r, gt_r, t_r, dy_o, l_o):
        i = pl.program_id(0)
        e = x_r[...] + gt_r[...] * f_r[...] - t_r[...]
        dy_o[...] = e * (1.0 / D)
        _acc_rows(l_o, jnp.sum(e * e, axis=0, keepdims=True), i)

    return _pcall(
        body, name=name, grid=(S // T,),
        in_specs=[_rows(T, D), _rows(T, D), _full((1, D)), _rows(T, D)],
        out_specs=(_rows(T, D), _full((8, D))),
        out_shape=(jax.ShapeDtypeStruct((S, D), F32), jax.ShapeDtypeStruct((8, D), F32)),
        compiler_params=_cparams("arbitrary"),
    )(x_mid, ffo, gt, target)


def _rope(t, c_t, s1_t, s2_t):
    return t * c_t + pltpu.roll(t, LANE - ROT_DIM // 2, 1) * s1_t + pltpu.roll(t, ROT_DIM // 2, 1) * s2_t


def _rope_t(d, c_t, s1_t, s2_t):
    return d * c_t + pltpu.roll(d * s1_t, ROT_DIM // 2, 1) + pltpu.roll(d * s2_t, LANE - ROT_DIM // 2, 1)


def qk_prep_fwd(zq, g_q, g_k, tabs, *, name):
    S = zq.shape[0]
    T = ROW_TILE

    def body(q_r, k_r, gq_r, gk_r, c_r, s1_r, s2_r, qn_o, kn_o):
        c_t, s1_t, s2_t = c_r[...], s1_r[...], s2_r[...]
        for src, g_r, dst in ((q_r, gq_r, qn_o), (k_r, gk_r, kn_o)):
            for h in range(N_HEADS):
                cols = slice(h * HEAD_DIM, (h + 1) * HEAD_DIM)
                t = src[:, cols]
                r = lax.rsqrt(jnp.mean(t * t, axis=-1, keepdims=True) + EPS)
                dst[:, cols] = _rope(t * r * g_r[...], c_t, s1_t, s2_t)

    tab = _rows(T, LANE)
    shp = jax.ShapeDtypeStruct((S, ATTN_W), F32)
    return _pcall(
        body, name=name, grid=(S // T,),
        in_specs=[_rows(T, ATTN_W, 0), _rows(T, ATTN_W, 1), _full((1, LANE)), _full((1, LANE)), tab, tab, tab],
        out_specs=(_rows(T, ATTN_W), _rows(T, ATTN_W)), out_shape=(shp, shp),
        compiler_params=_cparams("parallel"),
    )(zq, zq, g_q, g_k, *tabs)


def attn_bwd_post(zq, g_q, g_k, tabs, dq, dkc, dkp, dvc, dvp, *, name):
    S = zq.shape[0]
    T = min(POST_TILE, S)
    nblk = S // T
    GW = HEADS_PER_GROUP * HEAD_DIM
    n_ref = [7 if BLOCK * d < T else 5 for d in DILATIONS]

    def body(*refs):
        q_r, k_r, gq_r, gk_r, c_r, s1_r, s2_r = refs[:7]
        grp = refs[7:7 + sum(n_ref)]
        dz_o, dgq_o, dgk_o = refs[7 + sum(n_ref):]
        i = pl.program_id(0)
        c_t, s1_t, s2_t = c_r[...], s1_r[...], s2_r[...]
        dgq = jnp.zeros((1, LANE), F32)
        dgk = jnp.zeros((1, LANE), F32)
        at = 0
        for gi, d in enumerate(DILATIONS):
            g_refs = grp[at:at + n_ref[gi]]
            at += n_ref[gi]
            sh = BLOCK * d
            if sh < T:
                dq_r, dkc_r, dkp_r, dkpn_r, dvc_r, dvp_r, dvpn_r = g_refs
                live = jnp.where(i + 1 < nblk, 1.0, 0.0)

                def shifted(cur_r, nxt_r, gc, live=live, sh=sh):
                    return jnp.concatenate([cur_r[sh:T, gc], live * nxt_r[:, gc]], axis=0)
            else:
                dq_r, dkc_r, dkp_r, dvc_r, dvp_r = g_refs
                dkpn_r = dvpn_r = None
                live = jnp.where(i + sh // T < nblk, 1.0, 0.0)

                def shifted(cur_r, nxt_r, gc, live=live):
                    return live * cur_r[:, gc]
            for hh in range(HEADS_PER_GROUP):
                h = gi * HEADS_PER_GROUP + hh
                cols = slice(h * HEAD_DIM, (h + 1) * HEAD_DIM)
                gc = slice(hh * HEAD_DIM, (hh + 1) * HEAD_DIM)
                dk_v = dkc_r[:, gc] + shifted(dkp_r, dkpn_r, gc)
                dv_v = dvc_r[:, gc] + shifted(dvp_r, dvpn_r, gc)
                dz_o[:, 2 * ATTN_W + h * HEAD_DIM:2 * ATTN_W + (h + 1) * HEAD_DIM] = dv_v.astype(BF16)
                for which, (src, g_r, d_out) in enumerate(((q_r, gq_r, dq_r[:, gc]), (k_r, gk_r, dk_v))):
                    t = src[:, cols]
                    r = lax.rsqrt(jnp.mean(t * t, axis=-1, keepdims=True) + EPS)
                    xh = t * r
                    dtn = _rope_t(d_out, c_t, s1_t, s2_t)
                    dxh = dtn * g_r[...]
                    dt = r * (dxh - xh * jnp.mean(dxh * xh, axis=-1, keepdims=True))
                    dz_o[:, which * ATTN_W + h * HEAD_DIM:which * ATTN_W + (h + 1) * HEAD_DIM] = dt.astype(BF16)
                    part = jnp.sum(dtn * xh, axis=0, keepdims=True)
                    if which == 0:
                        dgq = dgq + part
                    else:
                        dgk = dgk + part
        _acc_rows(dgq_o, dgq, i)
        _acc_rows(dgk_o, dgk, i)

    tab = _rows(T, LANE)
    specs = [_rows(T, ATTN_W, 0), _rows(T, ATTN_W, 1), _full((1, LANE)), _full((1, LANE)), tab, tab, tab]
    args = [zq, zq, g_q, g_k, *tabs]
    for gi, d in enumerate(DILATIONS):
        cur = _rows(T, GW)
        sh = BLOCK * d
        if sh < T:
            head = pl.BlockSpec((sh, GW), functools.partial(
                lambda i, k, last: (jnp.minimum((i + 1) * k, last), 0), k=T // sh, last=S // sh - 1))
            specs += [cur, cur, cur, head, cur, cur, head]
            args += [dq[gi], dkc[gi], dkp[gi], dkp[gi], dvc[gi], dvp[gi], dvp[gi]]
        else:
            nxt = pl.BlockSpec((T, GW), functools.partial(lambda i, s: (jnp.minimum(i + s, nblk - 1), 0), s=sh // T))
            specs += [cur, cur, nxt, cur, nxt]
            args += [dq[gi], dkc[gi], dkp[gi], dvc[gi], dvp[gi]]
    part = jax.ShapeDtypeStruct((8, LANE), F32)
    return _pcall(
        body, name=name, grid=(nblk,), in_specs=specs,
        out_specs=(_rows(T, 3 * ATTN_W), _full((8, LANE)), _full((8, LANE))),
        out_shape=(jax.ShapeDtypeStruct((S, 3 * ATTN_W), BF16), part, part),
        compiler_params=_cparams("arbitrary"),
    )(*args)


ATTN_UNITS = 16


def _attn_geometry(d, S):
    R = min(ATTN_UNITS * BLOCK, S)
    return R, R // (BLOCK * d), S // R


def _sub_rows(j, r, d):
    if d == 1:
        return pl.ds(j * BLOCK, BLOCK)
    return pl.ds(j * BLOCK * d + r, BLOCK, stride=d)


def _dot_nt(a, b):
    return lax.dot_general(a, b, (((1,), (1,)), ((), ())), preferred_element_type=F32)


def _dot_tn(a, b):
    return lax.dot_general(a, b, (((0,), (0,)), ((), ())), preferred_element_type=F32)


def _attn_specs(gi, R):
    h0 = gi * HEADS_PER_GROUP
    vcol = 2 * N_HEADS + h0
    cur = lambda off: pl.BlockSpec((R, HEAD_DIM), lambda h, n: (n, off + h))
    prev = lambda off: pl.BlockSpec((R, HEAD_DIM), lambda h, n: (jnp.maximum(n - 1, 0), off + h))
    return [cur(h0), cur(h0), prev(h0), cur(vcol), prev(vcol)]


ATTN_UNROLL = ATTN_UNITS


def _attn_masks(n):
    qi = lax.broadcasted_iota(jnp.int32, (BLOCK, 2 * BLOCK), 0)
    kj = lax.broadcasted_iota(jnp.int32, (BLOCK, 2 * BLOCK), 1)
    band = jnp.where(jnp.logical_and(kj >= qi, kj <= qi + BLOCK), 0.0, NEG)
    no_prev = band + jnp.where(kj < BLOCK, 1.0, 0.0) * jnp.where(n > 0, 0.0, NEG)
    return band, no_prev


def _attn_keys(kc_r, kp_r, vc_r, vp_r, j, r, d, J):
    rq = _sub_rows(j, r, d)
    if j > 0:
        rp = _sub_rows(j - 1, r, d)
        kp, vp = kc_r[rp, :], vc_r[rp, :]
    else:
        rp = _sub_rows(J - 1, r, d)
        kp, vp = kp_r[rp, :], vp_r[rp, :]
    kk = jnp.concatenate([kp, kc_r[rq, :]], axis=0).astype(BF16)
    vv = jnp.concatenate([vp, vc_r[rq, :]], axis=0).astype(BF16)
    return kk, vv


def _attn_units(unit, d, J):
    for j in range(J):
        if d == 1:
            unit(j, 0)
        else:
            def step(r, carry, j=j):
                unit(j, r)
                return carry
            lax.fori_loop(0, d, step, 0, unroll=min(d, ATTN_UNROLL))


def attn_fwd(qn, kn, zq, gi, *, name):
    S = qn.shape[0]
    d = DILATIONS[gi]
    R, J, nblk = _attn_geometry(d, S)
    scale = HEAD_DIM ** -0.5
    GW = HEADS_PER_GROUP * HEAD_DIM

    def body(q_r, kc_r, kp_r, vc_r, vp_r, o_o, l_o):
        band, no_prev = _attn_masks(pl.program_id(1))

        def unit(j, r):
            rq = _sub_rows(j, r, d)
            q = q_r[rq, :].astype(BF16)
            kk, vv = _attn_keys(kc_r, kp_r, vc_r, vp_r, j, r, d, J)
            s = _dot_nt(q, kk) * scale + (band if j > 0 else no_prev)
            m = jnp.max(s, axis=-1, keepdims=True)
            p = jnp.exp(s - m)
            l = jnp.sum(p, axis=-1, keepdims=True)
            o_o[rq, :] = jnp.dot(p.astype(BF16), vv, preferred_element_type=F32) / l
            l_o[rq, :] = jnp.broadcast_to(m + jnp.log(l), (BLOCK, HEAD_DIM))

        _attn_units(unit, d, J)

    ospec = pl.BlockSpec((R, HEAD_DIM), lambda h, n: (n, h))
    shp = jax.ShapeDtypeStruct((S, GW), F32)
    return _pcall(
        body, name=name, grid=(HEADS_PER_GROUP, nblk), in_specs=_attn_specs(gi, R),
        out_specs=(ospec, ospec), out_shape=(shp, shp),
        compiler_params=_cparams("parallel", "arbitrary"),
    )(qn, kn, kn, zq, zq)


def attn_bwd(qn, kn, zq, do, lse, cc, gi, *, name):
    S = qn.shape[0]
    d = DILATIONS[gi]
    R, J, nblk = _attn_geometry(d, S)
    scale = HEAD_DIM ** -0.5
    GW = HEADS_PER_GROUP * HEAD_DIM

    def body(q_r, kc_r, kp_r, vc_r, vp_r, do_r, l_r, c_r, dq_o, dkc_o, dkp_o, dvc_o, dvp_o):
        band, no_prev = _attn_masks(pl.program_id(1))

        def unit(j, r):
            rq = _sub_rows(j, r, d)
            q = q_r[rq, :].astype(BF16)
            kk, vv = _attn_keys(kc_r, kp_r, vc_r, vp_r, j, r, d, J)
            s_mask = band if j > 0 else no_prev
            dob = do_r[rq, :].astype(BF16)
            lv = l_r[rq, :]
            cv = c_r[rq, :]
            lv2 = jnp.concatenate([lv, lv], axis=1)
            cv2 = jnp.concatenate([cv, cv], axis=1)
            p = jnp.exp(_dot_nt(q, kk) * scale + s_mask - lv2)
            ds = (p * (_dot_nt(dob, vv) + cv2)).astype(BF16)
            dq_o[rq, :] = jnp.dot(ds, kk, preferred_element_type=F32) * scale
            dk2 = _dot_tn(ds, q) * scale
            dv2 = _dot_tn(p.astype(BF16), dob)
            dkp_o[rq, :] = dk2[0:BLOCK]
            dkc_o[rq, :] = dk2[BLOCK:2 * BLOCK]
            dvp_o[rq, :] = dv2[0:BLOCK]
            dvc_o[rq, :] = dv2[BLOCK:2 * BLOCK]

        _attn_units(unit, d, J)

    ospec = pl.BlockSpec((R, HEAD_DIM), lambda h, n: (n, h))
    shp = jax.ShapeDtypeStruct((S, GW), F32)
    return _pcall(
        body, name=name, grid=(HEADS_PER_GROUP, nblk),
        in_specs=_attn_specs(gi, R) + [ospec, ospec, ospec],
        out_specs=(ospec,) * 5, out_shape=(shp,) * 5,
        compiler_params=_cparams("parallel", "arbitrary"),
    )(qn, kn, kn, zq, zq, do, lse, cc)


def combine_fwd(o, lse, *, name):
    S, GW = o[0].shape
    T = ROW_TILE

    def body(o0, o1, o2, l0, l1, l2, a_o):
        m = jnp.maximum(jnp.maximum(l0[...], l1[...]), l2[...])
        e0, e1, e2 = jnp.exp(l0[...] - m), jnp.exp(l1[...] - m), jnp.exp(l2[...] - m)
        a_o[...] = ((e0 * o0[...] + e1 * o1[...] + e2 * o2[...]) / (e0 + e1 + e2)).astype(BF16)

    return _pcall(
        body, name=name, grid=(S // T,), in_specs=[_rows(T, GW)] * 6, out_specs=_rows(T, GW),
        out_shape=jax.ShapeDtypeStruct((S, GW), BF16), compiler_params=_cparams("parallel"),
    )(*o, *lse)


def combine_bwd(d_attn, o, lse, *, name):
    S, GW = d_attn.shape
    T = ROW_TILE

    def body(da_r, o0, o1, o2, l0, l1, l2, d0, d1, d2, c0, c1, c2):
        m = jnp.maximum(jnp.maximum(l0[...], l1[...]), l2[...])
        e0, e1, e2 = jnp.exp(l0[...] - m), jnp.exp(l1[...] - m), jnp.exp(l2[...] - m)
        inv = 1.0 / (e0 + e1 + e2)
        w = (e0 * inv, e1 * inv, e2 * inv)
        da = da_r[...]
        attn = w[0] * o0[...] + w[1] * o1[...] + w[2] * o2[...]
        prod = da * attn
        for hh in range(HEADS_PER_GROUP):
            cols = slice(hh * HEAD_DIM, (hh + 1) * HEAD_DIM)
            a_h = jnp.sum(prod[:, cols], axis=-1, keepdims=True)
            for w_g, d_o, c_o in zip(w, (d0, d1, d2), (c0, c1, c2)):
                d_o[:, cols] = w_g[:, cols] * da[:, cols]
                c_o[:, cols] = -w_g[:, cols] * a_h

    shp = jax.ShapeDtypeStruct((S, GW), F32)
    return _pcall(
        body, name=name, grid=(S // T,), in_specs=[_rows(T, GW)] * 7, out_specs=(_rows(T, GW),) * 6,
        out_shape=(shp,) * 6, compiler_params=_cparams("parallel"),
    )(d_attn, *o, *lse)


def _halo_prev(T, H, C, col):
    k = T // H
    return pl.BlockSpec((H, C), lambda i: (jnp.maximum(i * k - 1, 0), col))


def _halo_next(T, H, C, col, n_rows):
    k = T // H
    last = n_rows // H - 1
    return pl.BlockSpec((H, C), lambda i: (jnp.minimum((i + 1) * k, last), col))


CONV_RC = 64
SUBLANES = 8


def _tap_groups(offs):
    groups = {}
    for k, off in offs:
        groups.setdefault(off % SUBLANES, []).append((k, off))
    return [taps for _, taps in sorted(groups.items())]


def _for_taps(src, r0, lanes, offs, fn):
    for taps in _tap_groups(offs):
        lo = min(off for _, off in taps)
        hi = max(off for _, off in taps)
        sb = src[r0 + lo:r0 + hi + CONV_RC, lanes]
        for k, off in taps:
            fn(k, sb[off - lo:off - lo + CONV_RC])


def _dwconv(src, w_r, offs, T, C, bias_r, dst):
    for rc in range(T // CONV_RC):
        for cc in range(C // LANE):
            lanes = slice(cc * LANE, (cc + 1) * LANE)
            r0 = rc * CONV_RC
            acc = None if bias_r is None else jnp.zeros((CONV_RC, LANE), F32) + bias_r[:, lanes]
            for taps in _tap_groups(offs):
                lo = min(off for _, off in taps)
                hi = max(off for _, off in taps)
                sb = src[r0 + lo:r0 + hi + CONV_RC, lanes]
                g_acc = None
                for k, off in taps:
                    term = w_r[k:k + 1, lanes] * sb[off - lo:off - lo + CONV_RC]
                    g_acc = term if g_acc is None else g_acc + term
                acc = g_acc if acc is None else acc + g_acc
            dst[r0:r0 + CONV_RC, lanes] = acc


def _fill_glu(cv, cg, hv, hg, ubuf, i):
    T = cv.shape[0]
    live = jnp.where(i > 0, 1.0, 0.0)
    ubuf[0:CONV_HALO, :] = live * (hv[...] * _sig(hg[...]))
    ubuf[CONV_HALO:CONV_HALO + T, :] = cv[...] * _sig(cg[...])


_CONV_FWD_OFFS = [(k, CONV_HALO - (CONV_K - 1) + k) for k in range(CONV_K)]
_CONV_BWD_OFFS = [(k, CONV_K - 1 - k) for k in range(CONV_K)]


def convb_fwd(zc, w, b, g_ln, b_ln, *, name):
    S = zc.shape[0]
    C = zc.shape[1] // 2
    T = ROW_TILE

    def body(cv, cg, hv, hg, w_r, b_r, g_r, bl_r, u_o, y_o, ubuf):
        _fill_glu(cv, cg, hv, hg, ubuf, pl.program_id(0))
        _dwconv(ubuf, w_r, _CONV_FWD_OFFS, T, C, b_r, y_o)
        y = y_o[...]
        mu = jnp.mean(y, axis=-1, keepdims=True)
        yc = y - mu
        rs = lax.rsqrt(jnp.mean(yc * yc, axis=-1, keepdims=True) + EPS)
        v = yc * rs * g_r[...] + bl_r[...]
        u_o[...] = (v * _sig(v)).astype(BF16)

    vec = _full((1, C))
    return _pcall(
        body, name=name, grid=(S // T,),
        in_specs=[_rows(T, C, 0), _rows(T, C, 1), _halo_prev(T, CONV_HALO, C, 0), _halo_prev(T, CONV_HALO, C, 1),
                  _full((CONV_K, C)), vec, vec, vec],
        out_specs=(_rows(T, C), _rows(T, C)),
        out_shape=(jax.ShapeDtypeStruct((S, C), BF16), jax.ShapeDtypeStruct((S, C), F32)),
        scratch_shapes=[pltpu.VMEM((CONV_HALO + T, C), F32)],
        compiler_params=_cparams("parallel"),
    )(zc, zc, zc, zc, w, b, g_ln, b_ln)


def convb_bwd1(d_u2, y_conv, zc, g_ln, b_ln, *, name):
    S = zc.shape[0]
    C = zc.shape[1] // 2
    T = ROW_TILE

    def body(du_r, y_r, cv, cg, hv, hg, g_r, bl_r, dy_o, dw_o, db_o, dg_o, dbl_o, ubuf):
        i = pl.program_id(0)
        _fill_glu(cv, cg, hv, hg, ubuf, i)
        y = y_r[...]
        mu = jnp.mean(y, axis=-1, keepdims=True)
        yc = y - mu
        rs = lax.rsqrt(jnp.mean(yc * yc, axis=-1, keepdims=True) + EPS)
        yn = yc * rs
        v = yn * g_r[...] + bl_r[...]
        sg = _sig(v)
        dv = du_r[...] * (sg * (1.0 + v * (1.0 - sg)))
        dyn = dv * g_r[...]
        dy = rs * (dyn - jnp.mean(dyn, axis=-1, keepdims=True) - yn * jnp.mean(dyn * yn, axis=-1, keepdims=True))
        dy_o[...] = dy
        _acc_rows(dg_o, jnp.sum(dv * yn, axis=0, keepdims=True), i)
        _acc_rows(dbl_o, jnp.sum(dv, axis=0, keepdims=True), i)
        _acc_rows(db_o, jnp.sum(dy, axis=0, keepdims=True), i)

        @pl.when(i == 0)
        def _():
            dw_o[...] = jnp.zeros_like(dw_o)
        for cc in range(C // LANE):
            lanes = slice(cc * LANE, (cc + 1) * LANE)
            parts = [jnp.zeros((SUBLANES, LANE), F32) for _ in range(CONV_K)]
            for rc in range(T // CONV_RC):
                r0 = rc * CONV_RC
                dyc = dy_o[r0:r0 + CONV_RC, lanes]

                def tap(k, chunk, parts=parts, dyc=dyc):
                    prod = (dyc * chunk).reshape(CONV_RC // SUBLANES, SUBLANES, LANE)
                    parts[k] = parts[k] + jnp.sum(prod, axis=0)

                _for_taps(ubuf, r0, lanes, _CONV_FWD_OFFS, tap)
            for k in range(CONV_K):
                dw_o[k * SUBLANES:(k + 1) * SUBLANES, lanes] += parts[k]

    vec = _full((1, C))
    part = jax.ShapeDtypeStruct((8, C), F32)
    return _pcall(
        body, name=name, grid=(S // T,),
        in_specs=[_rows(T, C), _rows(T, C), _rows(T, C, 0), _rows(T, C, 1), _halo_prev(T, CONV_HALO, C, 0),
                  _halo_prev(T, CONV_HALO, C, 1), vec, vec],
        out_specs=(_rows(T, C), _full((CONV_K * SUBLANES, C)), _full((8, C)), _full((8, C)), _full((8, C))),
        out_shape=(jax.ShapeDtypeStruct((S, C), F32), jax.ShapeDtypeStruct((CONV_K * SUBLANES, C), F32),
                   part, part, part),
        scratch_shapes=[pltpu.VMEM((CONV_HALO + T, C), F32)],
        compiler_params=_cparams("arbitrary"),
    )(d_u2, y_conv, zc, zc, zc, zc, g_ln, b_ln)


def convb_bwd2(dy, zc, w, *, name):
    S = zc.shape[0]
    C = zc.shape[1] // 2
    T = ROW_TILE
    nblk = S // T

    def body(dy_r, dyn_r, cv, cg, w_r, dz_o, dbuf, dubuf):
        i = pl.program_id(0)
        live = jnp.where(i < nblk - 1, 1.0, 0.0)
        dbuf[0:T, :] = dy_r[...]
        dbuf[T:T + CONV_HALO, :] = live * dyn_r[...]
        _dwconv(dbuf, w_r, _CONV_BWD_OFFS, T, C, None, dubuf)
        du = dubuf[...]
        sg = _sig(cg[...])
        dz_o[:, 0:C] = (du * sg).astype(BF16)
        dz_o[:, C:2 * C] = (du * cv[...] * sg * (1.0 - sg)).astype(BF16)

    return _pcall(
        body, name=name, grid=(nblk,),
        in_specs=[_rows(T, C), _halo_next(T, CONV_HALO, C, 0, S), _rows(T, C, 0), _rows(T, C, 1), _full((CONV_K, C))],
        out_specs=_rows(T, 2 * C), out_shape=jax.ShapeDtypeStruct((S, 2 * C), BF16),
        scratch_shapes=[pltpu.VMEM((T + CONV_HALO, C), F32), pltpu.VMEM((T, C), F32)],
        compiler_params=_cparams("parallel"),
    )(dy, dy, zc, zc, w)


def merge_fwd(y_a, y_b, zg, *, name):
    S, D = y_a.shape
    T = ROW_TILE

    def body(a_r, b_r, ga_r, gb_r, m_o):
        m_o[...] = (_sig(ga_r[...]) * a_r[...] + _sig(gb_r[...]) * b_r[...]).astype(BF16)

    return _pcall(
        body, name=name, grid=(S // T,),
        in_specs=[_rows(T, D), _rows(T, D), _rows(T, D, 0), _rows(T, D, 1)],
        out_specs=_rows(T, D), out_shape=jax.ShapeDtypeStruct((S, D), BF16),
        compiler_params=_cparams("parallel"),
    )(y_a, y_b, zg, zg)


def merge_bwd(d_m, y_a, y_b, zg, *, name):
    S, D = y_a.shape
    T = ROW_TILE

    def body(dm_r, a_r, b_r, ga_r, gb_r, da_o, db_o, dz_o):
        dm = dm_r[...]
        sa, sb = _sig(ga_r[...]), _sig(gb_r[...])
        da_o[...] = (dm * sa).astype(BF16)
        db_o[...] = (dm * sb).astype(BF16)
        dz_o[:, 0:D] = (dm * a_r[...] * sa * (1.0 - sa)).astype(BF16)
        dz_o[:, D:2 * D] = (dm * b_r[...] * sb * (1.0 - sb)).astype(BF16)

    shp = jax.ShapeDtypeStruct((S, D), BF16)
    return _pcall(
        body, name=name, grid=(S // T,),
        in_specs=[_rows(T, D), _rows(T, D), _rows(T, D), _rows(T, D, 0), _rows(T, D, 1)],
        out_specs=(_rows(T, D), _rows(T, D), _rows(T, 2 * D)),
        out_shape=(shp, shp, jax.ShapeDtypeStruct((S, 2 * D), BF16)),
        compiler_params=_cparams("parallel"),
    )(d_m, y_a, y_b, zg, zg)


FFN_RC = 64


def _ffn_chunks(T, F):
    for cc in range(F // LANE):
        for rc in range(T // FFN_RC):
            yield rc * FFN_RC, slice(rc * FFN_RC, (rc + 1) * FFN_RC), slice(cc * LANE, (cc + 1) * LANE)


def _ffn_fill(g_r, hg_r, gbuf, i):
    T = g_r.shape[0]
    live = jnp.where(i > 0, 1.0, 0.0)
    gbuf[0:FFN_HALO, :] = live * hg_r[...]
    gbuf[FFN_HALO:FFN_HALO + T, :] = g_r[...]


def _ffn_gate_chunk(gbuf, w_r, b_r, r0, lanes):
    taps = [gbuf[r0 + FFN_HALO - (FFN_K - 1) + k:r0 + FFN_HALO - (FFN_K - 1) + k + FFN_RC, lanes]
            for k in range(FFN_K)]
    gp = b_r[:, lanes] + w_r[0:1, lanes] * taps[0]
    for k in range(1, FFN_K):
        gp = gp + w_r[k:k + 1, lanes] * taps[k]
    return gp, taps


def _sum8(v):
    return jnp.sum(v.reshape(v.shape[0] // SUBLANES, SUBLANES, v.shape[1]), axis=0)


def ffn_act_fwd(gu, w, b, *, name):
    S = gu.shape[0]
    F = gu.shape[1] // 2
    T = ROW_TILE // 2

    def body(g_r, u_r, hg_r, w_r, b_r, a_o, gbuf):
        _ffn_fill(g_r, hg_r, gbuf, pl.program_id(0))
        for r0, rows, lanes in _ffn_chunks(T, F):
            gp, _ = _ffn_gate_chunk(gbuf, w_r, b_r, r0, lanes)
            a_o[rows, lanes] = (gp * _sig(gp) * u_r[rows, lanes]).astype(BF16)

    return _pcall(
        body, name=name, grid=(S // T,),
        in_specs=[_rows(T, F, 0), _rows(T, F, 1), _halo_prev(T, FFN_HALO, F, 0), _full((FFN_K, F)), _full((1, F))],
        out_specs=_rows(T, F), out_shape=jax.ShapeDtypeStruct((S, F), BF16),
        scratch_shapes=[pltpu.VMEM((FFN_HALO + T, F), F32)],
        compiler_params=_cparams("parallel"),
    )(gu, gu, gu, w, b)


def ffn_act_bwd1(d_a, gu, w, b, *, name):
    S = gu.shape[0]
    F = gu.shape[1] // 2
    T = ROW_TILE // 2

    def body(da_r, g_r, u_r, hg_r, w_r, b_r, dgp_o, dup_o, dw_o, db_o, gbuf):
        i = pl.program_id(0)
        _ffn_fill(g_r, hg_r, gbuf, i)

        @pl.when(i == 0)
        def _():
            dw_o[...] = jnp.zeros_like(dw_o)
            db_o[...] = jnp.zeros_like(db_o)

        sums = None
        for r0, rows, lanes in _ffn_chunks(T, F):
            if r0 == 0:
                sums = [jnp.zeros((SUBLANES, LANE), F32) for _ in range(FFN_K + 1)]
            gp, taps = _ffn_gate_chunk(gbuf, w_r, b_r, r0, lanes)
            sg = _sig(gp)
            da = da_r[rows, lanes]
            dup_o[rows, lanes] = (da * gp * sg).astype(BF16)
            dgp = da * u_r[rows, lanes] * (sg * (1.0 + gp * (1.0 - sg)))
            dgp_o[rows, lanes] = dgp
            sums[FFN_K] = sums[FFN_K] + _sum8(dgp)
            for k in range(FFN_K):
                sums[k] = sums[k] + _sum8(dgp * taps[k])
            if r0 + FFN_RC == T:
                db_o[:, lanes] += sums[FFN_K]
                for k in range(FFN_K):
                    dw_o[k * SUBLANES:(k + 1) * SUBLANES, lanes] += sums[k]

    return _pcall(
        body, name=name, grid=(S // T,),
        in_specs=[_rows(T, F), _rows(T, F, 0), _rows(T, F, 1), _halo_prev(T, FFN_HALO, F, 0),
                  _full((FFN_K, F)), _full((1, F))],
        out_specs=(_rows(T, F), _rows(T, F), _full((FFN_K * SUBLANES, F)), _full((SUBLANES, F))),
        out_shape=(jax.ShapeDtypeStruct((S, F), F32), jax.ShapeDtypeStruct((S, F), BF16),
                   jax.ShapeDtypeStruct((FFN_K * SUBLANES, F), F32), jax.ShapeDtypeStruct((SUBLANES, F), F32)),
        scratch_shapes=[pltpu.VMEM((FFN_HALO + T, F), F32)],
        compiler_params=_cparams("arbitrary"),
    )(d_a, gu, gu, gu, w, b)


def ffn_act_bwd2(dgp, dup, w, *, name):
    S, F = dgp.shape
    T = ROW_TILE // 2
    nblk = S // T

    def body(d_r, dn_r, up_r, w_r, o_o, dbuf):
        i = pl.program_id(0)
        live = jnp.where(i < nblk - 1, 1.0, 0.0)
        dbuf[0:T, :] = d_r[...]
        dbuf[T:T + FFN_HALO, :] = live * dn_r[...]
        for r0, rows, lanes in _ffn_chunks(T, F):
            dg = w_r[0:1, lanes] * dbuf[r0 + FFN_K - 1:r0 + FFN_K - 1 + FFN_RC, lanes]
            for k in range(1, FFN_K):
                off = r0 + FFN_K - 1 - k
                dg = dg + w_r[k:k + 1, lanes] * dbuf[off:off + FFN_RC, lanes]
            o_o[rows, lanes] = dg.astype(BF16)
        o_o[:, F:2 * F] = up_r[...]

    return _pcall(
        body, name=name, grid=(nblk,),
        in_specs=[_rows(T, F), _halo_next(T, FFN_HALO, F, 0, S), _rows(T, F), _full((FFN_K, F))],
        out_specs=_rows(T, 2 * F), out_shape=jax.ShapeDtypeStruct((S, 2 * F), BF16),
        scratch_shapes=[pltpu.VMEM((T + FFN_HALO, F), F32)],
        compiler_params=_cparams("parallel"),
    )(dgp, dgp, dup, w)


def _row_tile(R, target=512):
    if R <= target:
        return R
    for t in range(target, 7, -8):
        if R % t == 0:
            return t
    return R


def sum_slots(land, *, name):
    _, R, C = land.shape
    T = _row_tile(R)

    def body(l_r, o_o):
        acc = l_r[0].astype(F32)
        for q in range(1, N_DEV):
            acc = acc + l_r[q].astype(F32)
        o_o[...] = acc

    return _pcall(
        body, name=name, grid=(R // T,),
        in_specs=[pl.BlockSpec((N_DEV, T, C), lambda i: (0, i, 0))],
        out_specs=_rows(T, C), out_shape=jax.ShapeDtypeStruct((R, C), F32),
        compiler_params=_cparams("parallel"),
    )(land)


def adamw(w, g, m, v, *, name):
    shape = w.shape
    C = shape[-1]
    R = math.prod(shape[:-1])
    w2, g2, m2, v2 = (t.reshape(R, C) for t in (w, g, m, v))
    T = _row_tile(R)
    c1 = 1.0 - ADAM_B1 ** ADAM_STEP
    c2 = 1.0 - ADAM_B2 ** ADAM_STEP

    def body(w_r, g_r, m_r, v_r, d_o, m_o, v_o):
        gv = g_r[...]
        mn = ADAM_B1 * m_r[...] + (1.0 - ADAM_B1) * gv
        vn = ADAM_B2 * v_r[...] + (1.0 - ADAM_B2) * (gv * gv)
        m_o[...] = mn
        v_o[...] = vn
        d_o[...] = -ADAM_LR * ((mn / c1) / (jnp.sqrt(vn / c2) + ADAM_EPS) + ADAM_WD * w_r[...])

    shp = jax.ShapeDtypeStruct((R, C), F32)
    d, mn, vn = _pcall(
        body, name=name, grid=(R // T,), in_specs=[_rows(T, C)] * 4, out_specs=(_rows(T, C),) * 3,
        out_shape=(shp,) * 3, compiler_params=_cparams("parallel"),
    )(w2, g2, m2, v2)
    return d.reshape(shape), mn.reshape(shape), vn.reshape(shape)


def _my_pos():
    return lax.axis_index("x"), lax.axis_index("y"), lax.axis_index("c")


class _Exchange:
    def __init__(self, inputs, out_shapes, sems, start, finish):
        self.inputs, self.out_shapes, self.sems, self.start, self.finish = inputs, out_shapes, sems, start, finish


def gather_exchange(shards):
    n = len(shards)

    def plan(ins, outs, sems):
        send_sems, recv_sems, local_sems = sems
        x, y, c = _my_pos()
        me, sibling = (x, y, c), (x, y, 1 - c)
        chips = [(1 - x, y), (x, 1 - y), (1 - x, 1 - y)]

        def slot(i, p):
            return outs[i].at[4 * p[0] + 2 * p[1] + p[2]]

        def copy(k, i, block, to, src=None):
            return pltpu.make_async_remote_copy(
                src_ref=slot(i, block) if src is None else src, dst_ref=slot(i, block),
                send_sem=send_sems.at[k, i], recv_sem=recv_sems.at[k, i], device_id=to, device_id_type=MESH)

        mine = [pltpu.make_async_copy(ins[i], slot(i, me), local_sems.at[i]) for i in range(n)]
        first = []
        for i in range(n):
            first.append(copy(0, i, me, sibling, src=ins[i]))
            first += [copy(1 + j, i, me, (*chip, c), src=ins[i]) for j, chip in enumerate(chips)]
        return me, sibling, chips, c, copy, mine, first

    def start(ins, outs, sems):
        _, _, _, _, _, mine, first = plan(ins, outs, sems)
        for cp in mine + first:
            cp.start()

    def finish(ins, outs, sems):
        me, sibling, chips, c, copy, mine, first = plan(ins, outs, sems)
        passed = []
        for j, chip in enumerate(chips):
            for i in range(n):
                copy(1 + j, i, (*chip, c), me).wait_recv()
                cp = copy(4 + j, i, (*chip, c), sibling)
                cp.start()
                passed.append(cp)
        for i in range(n):
            copy(0, i, sibling, me).wait_recv()
            for j, chip in enumerate(chips):
                copy(4 + j, i, (*chip, 1 - c), me).wait_recv()
        for cp in first + passed:
            cp.wait_send()
        for cp in mine:
            cp.wait()

    outs = [jax.ShapeDtypeStruct((N_DEV,) + s.shape, s.dtype) for s in shards]
    sems = [pltpu.SemaphoreType.DMA((7, n)), pltpu.SemaphoreType.DMA((7, n)), pltpu.SemaphoreType.DMA((n,))]
    return _Exchange(list(shards), outs, sems, start, finish)


def scatter_exchange(gs):
    n = len(gs)

    def plan(ins, outs, sems):
        send_sems, recv_sems, local_sems = sems
        x, y, c = _my_pos()
        me_id = 4 * x + 2 * y + c
        mine = [pltpu.make_async_copy(ins[i].at[me_id], outs[i].at[me_id], local_sems.at[i]) for i in range(n)]
        sends, recvs = [], []
        for msk in range(1, N_DEV):
            px = 1 - x if msk & 4 else x
            py = 1 - y if msk & 2 else y
            pc = 1 - c if msk & 1 else c
            pid = 4 * px + 2 * py + pc
            for i in range(n):
                sends.append(pltpu.make_async_remote_copy(
                    src_ref=ins[i].at[pid], dst_ref=outs[i].at[me_id],
                    send_sem=send_sems.at[msk - 1, i], recv_sem=recv_sems.at[msk - 1, i],
                    device_id=(px, py, pc), device_id_type=MESH))
                recvs.append(pltpu.make_async_remote_copy(
                    src_ref=ins[i].at[pid], dst_ref=outs[i].at[pid],
                    send_sem=send_sems.at[msk - 1, i], recv_sem=recv_sems.at[msk - 1, i],
                    device_id=(px, py, pc), device_id_type=MESH))
        return mine, sends, recvs

    def start(ins, outs, sems):
        mine, sends, _ = plan(ins, outs, sems)
        for cp in mine + sends:
            cp.start()

    def finish(ins, outs, sems):
        mine, sends, recvs = plan(ins, outs, sems)
        for rv in recvs:
            rv.wait_recv()
        for cp in sends:
            cp.wait_send()
        for cp in mine:
            cp.wait()

    outs = [jax.ShapeDtypeStruct(g.shape, g.dtype) for g in gs]
    sems = [pltpu.SemaphoreType.DMA((7, n)), pltpu.SemaphoreType.DMA((7, n)), pltpu.SemaphoreType.DMA((n,))]
    return _Exchange(list(gs), outs, sems, start, finish)


def run_exchange(ex, *, name):
    HBM = pl.BlockSpec(memory_space=pl.ANY)
    n_in, n_out = len(ex.inputs), len(ex.out_shapes)

    def body(*refs):
        ins, outs, sems = refs[:n_in], refs[n_in:n_in + n_out], refs[n_in + n_out:]
        ex.start(ins, outs, sems)
        ex.finish(ins, outs, sems)

    return list(_pcall(body, name=name, in_specs=[HBM] * n_in, out_specs=tuple([HBM] * n_out),
                       out_shape=tuple(ex.out_shapes), scratch_shapes=ex.sems)(*ex.inputs))


def all_gather(xs, *, name):
    outs = run_exchange(gather_exchange([xs[l] for l in range(xs.shape[0])]), name=name)
    return jnp.stack(outs)


def _slots(g):
    return g.reshape(N_DEV, g.shape[0] // N_DEV, g.shape[1])


def _rope_tables(positions):
    half = ROT_DIM // 2
    inv_freq = ROPE_THETA ** (-jnp.arange(0, ROT_DIM, 2, dtype=F32) / ROT_DIM)
    ang = positions.astype(F32)[:, None] * inv_freq
    cos, sin = jnp.cos(ang), jnp.sin(ang)
    S = positions.shape[0]
    c_t = jnp.concatenate([cos, cos, jnp.ones((S, LANE - ROT_DIM), F32)], axis=1)
    s1_t = jnp.concatenate([-sin, jnp.zeros((S, LANE - half), F32)], axis=1)
    s2_t = jnp.concatenate([jnp.zeros((S, half), F32), sin, jnp.zeros((S, LANE - ROT_DIM), F32)], axis=1)
    return c_t, s1_t, s2_t


def _row(v):
    return v.reshape(1, -1)


def kernel(x, c, positions, w_ada, b_ada, g_norm1, w_in, g_q, g_k, w_attn_proj, w_conv_dw, b_conv_dw, g_conv_ln, b_conv_ln, w_conv_out, w_o, g_norm2, w_ffn_in, w_ffn_dw, b_ffn_dw, w_ffn_down, loss_target, m_w_ada, m_b_ada, m_g_norm1, m_w_in, m_g_q, m_g_k, m_w_attn_proj, m_w_conv_dw, m_b_conv_dw, m_g_conv_ln, m_b_conv_ln, m_w_conv_out, m_w_o, m_g_norm2, m_w_ffn_in, m_w_ffn_dw, m_b_ffn_dw, m_w_ffn_down, v_w_ada, v_b_ada, v_g_norm1, v_w_in, v_g_q, v_g_k, v_w_attn_proj, v_w_conv_dw, v_b_conv_dw, v_g_conv_ln, v_b_conv_ln, v_w_conv_out, v_w_o, v_g_norm2, v_w_ffn_in, v_w_ffn_dw, v_b_ffn_dw, v_w_ffn_down):
    L = w_in.shape[0]
    S, D = x.shape[1], x.shape[2]
    FF = w_ffn_down.shape[1] * N_DEV
    xi, yi, ci = _my_pos()
    me = 4 * xi + 2 * yi + ci
    x0 = x[0]
    tabs = _rope_tables(positions[0])

    c_act = c * _sig(c)
    c_all = all_gather(jnp.pad(c_act, ((0, 7), (0, 0)))[None], name="ag_c")[0][:, 0, :]
    c_all16 = jnp.pad(c_all, ((0, 8), (0, 0)))
    m_part = jnp.stack([mm(c_all16, w_ada[l], "nn", name="mod_mm") for l in range(L)])
    m_all = all_gather(m_part, name="ag_mod")
    mod = lax.dynamic_index_in_dim(m_all, me, axis=2, keepdims=False).reshape(L, 6 * D) + b_ada
    mod = mod.reshape(L, 6, 1, D)

    sh_in = jnp.transpose(w_in, (0, 2, 1)).astype(BF16)
    sh_fi = jnp.transpose(w_ffn_in, (0, 2, 1)).astype(BF16)
    sh_ap = jnp.transpose(w_attn_proj, (0, 2, 1)).astype(BF16)
    sh_co, sh_oo, sh_dn = w_conv_out.astype(BF16), w_o.astype(BF16), w_ffn_down.astype(BF16)

    def rowcat(g):
        return g.reshape(N_DEV * g.shape[1], g.shape[2])

    wt_in, wt_fi, wt_ap, w_co, w_oo, w_dn = ([None] * L for _ in range(6))
    wt_in[0] = rowcat(run_exchange(gather_exchange([sh_in[0]]), name="ag_w0")[0])
    cdw = all_gather(jnp.pad(w_conv_dw, ((0, 0), (0, 1), (0, 0))), name="ag_cdw")
    cdw = jnp.transpose(cdw, (0, 2, 1, 3)).reshape(L, 32, D)[:, :CONV_K]
    fsh = w_ffn_dw.shape[2]
    fpad = -fsh % LANE
    fdw = all_gather(jnp.pad(w_ffn_dw, ((0, 0), (0, 8 - FFN_K), (0, fpad))), name="ag_fdw")
    fdw = jnp.transpose(fdw[:, :, :FFN_K, :fsh], (0, 2, 1, 3)).reshape(L, FFN_K, FF)

    QKV, CW = 3 * ATTN_W, 2 * D
    seg = ((0, QKV), (QKV, CW), (QKV + CW, 2 * D))

    saved = []
    x_prev, delta, gt_prev = x0, None, None
    for l in range(L):
        sh1, sc1, gt1, sh2, sc2, gt2 = (mod[l, i] for i in range(6))
        x_l, h = norm_mod_fwd(x_prev, delta, gt_prev, _row(g_norm1[l]), sc1, sh1, name="norm_fwd")
        nxt = l + 1 < L
        if l == 0:
            zq, got = mm(h, wt_in[0], "nt", b_off=0, b_len=QKV, name="z_mm_ag0",
                         comm=gather_exchange([sh_fi[0], sh_ap[0], sh_co[0], sh_oo[0], sh_dn[0]]))
            wt_fi[0], wt_ap[0], w_co[0], w_oo[0], w_dn[0] = (rowcat(g) for g in got)
        elif nxt:
            zq, got = mm(h, wt_in[l], "nt", b_off=0, b_len=QKV, name="z_mm_ag", comm=gather_exchange([sh_in[l + 1]]))
            wt_in[l + 1] = rowcat(got[0])
        else:
            zq = mm(h, wt_in[l], "nt", b_off=0, b_len=QKV, name="z_mm")
        if l == 0 and nxt:
            zc, got = mm(h, wt_in[0], "nt", b_off=seg[1][0], b_len=seg[1][1], name="zc_mm_ag",
                         comm=gather_exchange([sh_in[1]]))
            wt_in[1] = rowcat(got[0])
        else:
            zc = mm(h, wt_in[l], "nt", b_off=seg[1][0], b_len=seg[1][1], name="z_mm")
        zg = mm(h, wt_in[l], "nt", b_off=seg[2][0], b_len=seg[2][1], name="z_mm")
        gq, gk = _row(g_q[l]), _row(g_k[l])
        qn, kn = qk_prep_fwd(zq, gq, gk, tabs, name="qk_prep")
        o_g, lse_g = [], []
        for gi in range(3):
            o_i, l_i = attn_fwd(qn, kn, zq, gi, name="attn_fwd%d" % gi)
            o_g.append(o_i)
            lse_g.append(l_i)
        attn = combine_fwd(o_g, lse_g, name="combine_fwd")
        y_a = mm(attn, wt_ap[l], "nt", name="ya_mm")
        cw, cb = cdw[l], _row(b_conv_dw[l])
        cg, cbl = _row(g_conv_ln[l]), _row(b_conv_ln[l])
        u2, y_conv = convb_fwd(zc, cw, cb, cg, cbl, name="convb_fwd")
        y_b = mm(u2, w_co[l], "nn", name="yb_mm")
        merged = merge_fwd(y_a, y_b, zg, name="merge_fwd")
        mo = mm(merged, w_oo[l], "nn", name="mo_mm")
        x_mid, h2 = norm_mod_fwd(x_l, mo, gt1, _row(g_norm2[l]), sc2, sh2, name="norm_fwd")
        if nxt:
            gu, got = mm(h2, wt_fi[l], "nt", name="gu_mm_ag", comm=gather_exchange([sh_fi[l + 1]]))
            wt_fi[l + 1] = rowcat(got[0])
        else:
            gu = mm(h2, wt_fi[l], "nt", name="gu_mm")
        fw, fb = fdw[l], _row(b_ffn_dw[l])
        act = ffn_act_fwd(gu, fw, fb, name="ffn_act")
        if nxt:
            ffo, got = mm(act, w_dn[l], "nn", name="ffo_mm_ag",
                          comm=gather_exchange([sh_ap[l + 1], sh_co[l + 1], sh_oo[l + 1], sh_dn[l + 1]]))
            wt_ap[l + 1], w_co[l + 1], w_oo[l + 1], w_dn[l + 1] = (rowcat(g) for g in got)
        else:
            ffo = mm(act, w_dn[l], "nn", name="ffo_mm")
        saved.append(dict(x=x_l, h=h, zq=zq, zc=zc, zg=zg, qn=qn, kn=kn, o=o_g, lse=lse_g, attn=attn, y_a=y_a,
                          u2=u2, y_conv=y_conv, y_b=y_b, merged=merged, mo=mo, x_mid=x_mid, h2=h2, gu=gu, act=act, ffo=ffo))
        x_prev, delta, gt_prev = x_mid, ffo, gt2

    dx, lpart = loss_head(x_prev, delta, gt_prev, loss_target[0], name="loss_head")
    loss = lax.psum(0.5 / D * jnp.sum(lpart[0]), ("x", "y", "c"))

    land = {k: [None] * L for k in ("in", "fi", "ap", "co", "o", "dn")}
    small_rows = []
    for l in reversed(range(L)):
        sv = saved[l]
        sh1, sc1, gt1, sh2, sc2, gt2 = (mod[l, i] for i in range(6))
        d_ffo, p_gt2 = scale_bwd(dx, sv["ffo"], gt2, name="scale_bwd")
        d_act = mm(d_ffo, w_dn[l], "nt", name="dact_mm")
        g_dn = mm(sv["act"], d_ffo, "tn", out_dtype=BF16, name="dwdn_mm")
        fw, fb = fdw[l], _row(b_ffn_dw[l])
        dgp, dup, p_fw, p_fb = ffn_act_bwd1(d_act, sv["gu"], fw, fb, name="ffn_bwd1")
        dgu = ffn_act_bwd2(dgp, dup, fw, name="ffn_bwd2")
        dh2, got = mm(dgu, wt_fi[l], "nn", name="dh2_mm_rs", comm=scatter_exchange([_slots(g_dn)]))
        land["dn"][l] = got[0]
        g_fi = mm(dgu, sv["h2"], "tn", out_dtype=BF16, name="dwfi_mm")
        dx, p_g2, p_sc2, p_sh2 = norm_mod_bwd(sv["x_mid"], dh2, _row(g_norm2[l]), sc2, sh2, dx, name="norm_bwd")
        d_mo, p_gt1 = scale_bwd(dx, sv["mo"], gt1, name="scale_bwd")
        d_merged = mm(d_mo, w_oo[l], "nt", name="dmerged_mm")
        g_o = mm(sv["merged"], d_mo, "tn", out_dtype=BF16, name="dwo_mm")
        d_ya, d_yb, dzg = merge_bwd(d_merged, sv["y_a"], sv["y_b"], sv["zg"], name="merge_bwd")
        d_attn = mm(d_ya, wt_ap[l], "nn", name="dattn_mm")
        g_ap = mm(d_ya, sv["attn"], "tn", out_dtype=BF16, name="dwap_mm")
        d_u2 = mm(d_yb, w_co[l], "nt", name="du2_mm")
        g_co = mm(sv["u2"], d_yb, "tn", out_dtype=BF16, name="dwco_mm")
        cw, cb = cdw[l], _row(b_conv_dw[l])
        cg, cbl = _row(g_conv_ln[l]), _row(b_conv_ln[l])
        dy, p_cw, p_cb, p_cg, p_cbl = convb_bwd1(d_u2, sv["y_conv"], sv["zc"], cg, cbl, name="convb_bwd1")
        dzc = convb_bwd2(dy, sv["zc"], cw, name="convb_bwd2")
        dd = combine_bwd(d_attn, sv["o"], sv["lse"], name="combine_bwd")
        do_g, cc_g = dd[:3], dd[3:]
        parts = [attn_bwd(sv["qn"], sv["kn"], sv["zq"], do_g[gi], sv["lse"][gi], cc_g[gi], gi,
                          name="attn_bwd%d" % gi) for gi in range(3)]
        gq, gk = _row(g_q[l]), _row(g_k[l])
        dzq, p_gq, p_gk = attn_bwd_post(sv["zq"], gq, gk, tabs, *[[p[i] for p in parts] for i in range(5)],
                                        name="attn_post")
        g_in_q, got = mm(dzq, sv["h"], "tn", out_dtype=BF16, name="dwin_mm_rs",
                         comm=scatter_exchange([_slots(g_fi), _slots(g_o), _slots(g_ap), _slots(g_co)]))
        land["fi"][l], land["o"][l], land["ap"][l], land["co"][l] = got
        g_in = jnp.concatenate([g_in_q] + [mm(dz_s, sv["h"], "tn", out_dtype=BF16, name="dwin_mm")
                                           for dz_s in (dzc, dzg)], axis=0)
        dh, got = mm(dzq, wt_in[l], "nn", b_off=0, name="dh_mm_rs", comm=scatter_exchange([_slots(g_in)]))
        land["in"][l] = got[0]
        for dz_s, (o, n) in zip((dzc, dzg), seg[1:]):
            dh = mm(dz_s, wt_in[l], "nn", b_off=o, c_in=dh, name="dh_mm")
        dx, p_g1, p_sc1, p_sh1 = norm_mod_bwd(sv["x"], dh, _row(g_norm1[l]), sc1, sh1, dx, name="norm_bwd")

        def row1k(p):
            v = p[0]
            pad = -v.shape[0] % D
            return jnp.pad(v, (0, pad)).reshape(-1, D)

        rows = [row1k(p) for p in (p_sh1, p_sc1, p_gt1, p_sh2, p_sc2, p_gt2, p_g1, p_g2)]
        rows.append(row1k(jnp.concatenate([p_gq, p_gk], axis=1)))
        rows += [row1k(p) for p in (p_cb, p_cg, p_cbl, jnp.sum(p_fb, axis=0, keepdims=True))]
        rows.append(jnp.sum(p_cw.reshape(CONV_K, SUBLANES, D), axis=1))
        rows += [row1k(jnp.sum(p_fw[k * SUBLANES:(k + 1) * SUBLANES], axis=0, keepdims=True))
                 for k in range(FFN_K)]
        blk = jnp.concatenate(rows, axis=0)
        small_rows.append(jnp.pad(blk, ((0, -blk.shape[0] % 8), (0, 0))))
    small_rows = small_rows[::-1]
    n_small = small_rows[0].shape[0]
    ff_rows = -(-FF // D)

    small = jnp.concatenate(small_rows, axis=0)[None]
    small_all = all_gather(small, name="ag_small")[0]
    small_sum = sum_slots(small_all, name="sum_small").reshape(L, n_small, D)
    small_all = small_all.reshape(N_DEV, L, n_small, D)

    g_b_ada = small_sum[:, 0:6].reshape(L, 6 * D)
    g_g1, g_g2 = small_sum[:, 6], small_sum[:, 7]
    g_gq, g_gk = small_sum[:, 8, 0:LANE], small_sum[:, 8, LANE:2 * LANE]
    g_cb, g_cg, g_cbl = small_sum[:, 9], small_sum[:, 10], small_sum[:, 11]
    r0 = 12
    g_fb = small_sum[:, r0:r0 + ff_rows].reshape(L, -1)[:, :FF]
    r0 += ff_rows
    g_cw_full = small_sum[:, r0:r0 + CONV_K]
    r0 += CONV_K
    g_fw_full = small_sum[:, r0:r0 + FFN_K * ff_rows].reshape(L, FFN_K, -1)[:, :, :FF]
    csh = w_conv_dw.shape[2]
    g_cw = lax.dynamic_slice_in_dim(g_cw_full, me * csh, csh, axis=2)
    g_fw = lax.dynamic_slice_in_dim(g_fw_full, me * fsh, fsh, axis=2)

    ash = w_ada.shape[2]
    dmod_all = small_all[:, :, 0:6].reshape(N_DEV, L, 6 * D)
    dmod_mine = lax.dynamic_slice_in_dim(dmod_all, me * ash, ash, axis=2)
    g_w_ada = jnp.stack([mm(c_all16, jnp.pad(dmod_mine[:, l], ((0, 8), (0, 0))), "tn", name="dwada_mm")
                         for l in range(L)])

    def reduced(key, transposed):
        out = jnp.stack([sum_slots(slots, name="sum_" + key) for slots in land[key]])
        return jnp.transpose(out, (0, 2, 1)) if transposed else out

    g_w_in = reduced("in", True)
    g_w_fi = reduced("fi", True)
    g_w_ap = reduced("ap", True)
    g_w_co = reduced("co", False)
    g_w_o = reduced("o", False)
    g_w_dn = reduced("dn", False)

    grads = [g_w_ada, g_b_ada, g_g1, g_w_in, g_gq, g_gk, g_w_ap, g_cw, g_cb, g_cg, g_cbl, g_w_co, g_w_o, g_g2,
             g_w_fi, g_fw, g_fb, g_w_dn]
    ws = [w_ada, b_ada, g_norm1, w_in, g_q, g_k, w_attn_proj, w_conv_dw, b_conv_dw, g_conv_ln, b_conv_ln,
          w_conv_out, w_o, g_norm2, w_ffn_in, w_ffn_dw, b_ffn_dw, w_ffn_down]
    ms = [m_w_ada, m_b_ada, m_g_norm1, m_w_in, m_g_q, m_g_k, m_w_attn_proj, m_w_conv_dw, m_b_conv_dw, m_g_conv_ln,
          m_b_conv_ln, m_w_conv_out, m_w_o, m_g_norm2, m_w_ffn_in, m_w_ffn_dw, m_b_ffn_dw, m_w_ffn_down]
    vs = [v_w_ada, v_b_ada, v_g_norm1, v_w_in, v_g_q, v_g_k, v_w_attn_proj, v_w_conv_dw, v_b_conv_dw, v_g_conv_ln,
          v_b_conv_ln, v_w_conv_out, v_w_o, v_g_norm2, v_w_ffn_in, v_w_ffn_dw, v_b_ffn_dw, v_w_ffn_down]
    deltas, new_m, new_v = [], [], []
    for w_i, g_i, m_i, v_i in zip(ws, grads, ms, vs):
        d_i, mn_i, vn_i = adamw(w_i, g_i, m_i, v_i, name="adamw")
        deltas.append(d_i)
        new_m.append(mn_i)
        new_v.append(vn_i)
    return (loss, dx[None], *grads, *deltas, *new_m, *new_v)
```

```python
import functools
import math

import jax
import jax.numpy as jnp
from jax import lax
from jax.experimental import pallas as pl
from jax.experimental.pallas import tpu as pltpu

F32 = jnp.float32
BF16 = jnp.bfloat16
MESH = pl.DeviceIdType.MESH
N_DEV = 8

EPS = 1e-6
HEAD_DIM = 128
BLOCK = 128
DILATIONS = (1, 4, 16)
HEADS_PER_GROUP = 4
N_HEADS = 12
ATTN_W = N_HEADS * HEAD_DIM
ROT_DIM = 32
ROPE_THETA = 500000.0
CONV_K = 31
CONV_HALO = 32
FFN_K = 3
FFN_HALO = 8
NEG = -1e30

ADAM_LR, ADAM_B1, ADAM_B2, ADAM_EPS, ADAM_WD, ADAM_STEP = 0.001, 0.9, 0.999, 1e-08, 0.01, 10

LANE = 128
VMEM_LIMIT = 56 * 1024 * 1024
ROW_TILE = 512
POST_TILE = 256


def _pcall(body, **kw):
    return pl.pallas_call(body, **kw)


def _cparams(*sem):
    return pltpu.CompilerParams(dimension_semantics=sem, vmem_limit_bytes=VMEM_LIMIT)


def _sig(v):
    return 1.0 / (1.0 + jnp.exp(-v))


def _divtile(dim, target):
    best = None
    for t in range(LANE, min(dim, target) + 1, LANE):
        if dim % t == 0:
            best = t
    return best or dim


def _rows(t, c, col=0):
    return pl.BlockSpec((t, c), lambda i: (i, col))


def _full(shape):
    nd = len(shape)
    return pl.BlockSpec(shape, lambda i: (0,) * nd)


def _acc_rows(ref, val, i):
    @pl.when(i == 0)
    def _():
        ref[...] = jnp.zeros_like(ref)
    r = val.shape[0]
    ref[0:r, :] += val


MM_TILE = 1536


def mm(a, b, mode, *, name, out_dtype=F32, c_in=None, b_off=0, b_len=None, comm=None):
    if mode == "nn":
        M, K = a.shape
        N = b.shape[1]
    elif mode == "nt":
        M, K = a.shape
        N = b_len if b_len is not None else b.shape[0]
    else:
        K, M = a.shape
        N = b.shape[1]
    g_n = math.gcd(N, b_off) if (mode == "nt" and b_off) else N
    g_k = math.gcd(K, b_off) if (mode == "nn" and b_off) else K
    tn = _divtile(g_n, MM_TILE if c_in is None else 1024)
    tk = _divtile(g_k, MM_TILE if mode != "tn" else 1024)
    tm = _divtile(M, MM_TILE if mode == "tn" else (2048 if (tn <= 512 and c_in is None) else 1024))
    gm, gn, nk = M // tm, N // tn, K // tk
    n_ci = 0 if comm is None else len(comm.inputs)
    n_co = 0 if comm is None else len(comm.out_shapes)
    n_x = 2 + (c_in is not None)
    if mode == "nn":
        dims = (((1,), (0,)), ((), ()))
    elif mode == "nt":
        dims = (((1,), (1,)), ((), ()))
    else:
        dims = (((0,), (0,)), ((), ()))

    def body(*refs):
        a_ref, b_ref = refs[0], refs[1]
        c_ref = refs[2] if c_in is not None else None
        c_ins = refs[n_x:n_x + n_ci]
        o_ref = refs[n_x + n_ci]
        c_outs = refs[n_x + n_ci + 1:n_x + n_ci + 1 + n_co]
        rest = refs[n_x + n_ci + 1 + n_co:]
        acc = rest[0] if nk > 1 else None
        sems = rest[1:] if nk > 1 else rest
        i, j, k = pl.program_id(0), pl.program_id(1), pl.program_id(2)

        if comm is not None:
            @pl.when(jnp.logical_and(jnp.logical_and(i == 0, j == 0), k == 0))
            def _():
                comm.start(c_ins, c_outs, sems)

        prod = lax.dot_general(a_ref[...].astype(BF16), b_ref[...].astype(BF16), dims, preferred_element_type=F32)
        if nk == 1:
            if c_ref is not None:
                prod = prod + c_ref[...].astype(F32)
            o_ref[...] = prod.astype(out_dtype)
        else:
            @pl.when(k == 0)
            def _():
                if c_ref is None:
                    acc[...] = prod
                else:
                    acc[...] = prod + c_ref[...].astype(F32)

            @pl.when(k > 0)
            def _():
                acc[...] += prod

            @pl.when(k == nk - 1)
            def _():
                o_ref[...] = acc[...].astype(out_dtype)

        if comm is not None:
            @pl.when(jnp.logical_and(jnp.logical_and(i == gm - 1, j == gn - 1), k == nk - 1))
            def _():
                comm.finish(c_ins, c_outs, sems)

    if mode == "nn":
        a_spec = pl.BlockSpec((tm, tk), lambda i, j, k: (i, k))
        ob = b_off // tk
        b_spec = pl.BlockSpec((tk, tn), lambda i, j, k: (k + ob, j))
    elif mode == "nt":
        a_spec = pl.BlockSpec((tm, tk), lambda i, j, k: (i, k))
        ob = b_off // tn
        b_spec = pl.BlockSpec((tn, tk), lambda i, j, k: (j + ob, k))
    else:
        a_spec = pl.BlockSpec((tk, tm), lambda i, j, k: (k, i))
        b_spec = pl.BlockSpec((tk, tn), lambda i, j, k: (k, j))
    o_spec = pl.BlockSpec((tm, tn), lambda i, j, k: (i, j))
    HBM = pl.BlockSpec(memory_space=pl.ANY)
    in_specs = [a_spec, b_spec]
    args = [a, b]
    if c_in is not None:
        in_specs.append(o_spec)
        args.append(c_in)
    scratch = [pltpu.VMEM((tm, tn), F32)] if nk > 1 else []
    o_shape = jax.ShapeDtypeStruct((M, N), out_dtype)
    if comm is None:
        return _pcall(
            body, name=name, grid=(gm, gn, nk), in_specs=in_specs, out_specs=o_spec, out_shape=o_shape,
            scratch_shapes=scratch, compiler_params=_cparams("parallel", "parallel", "arbitrary"),
        )(*args)
    res = _pcall(
        body, name=name, grid=(gm, gn, nk), in_specs=in_specs + [HBM] * n_ci,
        out_specs=(o_spec, *[HBM] * n_co), out_shape=(o_shape, *comm.out_shapes),
        scratch_shapes=scratch + comm.sems, compiler_params=_cparams("arbitrary", "arbitrary", "arbitrary"),
    )(*args, *comm.inputs)
    return res[0], list(res[1:])


def norm_mod_fwd(x_prev, delta, gt, g, sc, sh, *, name):
    S, D = x_prev.shape
    T = ROW_TILE
    has_delta = delta is not None

    def body(*refs):
        if has_delta:
            xp, dl, gt_r, g_r, sc_r, sh_r, x_out, h_out = refs
            xv = xp[...] + gt_r[...] * dl[...]
            x_out[...] = xv
        else:
            xp, g_r, sc_r, sh_r, h_out = refs
            xv = xp[...]
        r = lax.rsqrt(jnp.mean(xv * xv, axis=-1, keepdims=True) + EPS)
        h_out[...] = ((xv * r) * g_r[...] * (1.0 + sc_r[...]) + sh_r[...]).astype(BF16)

    vec = _full((1, D))
    if has_delta:
        ins, specs = [x_prev, delta, gt, g, sc, sh], [_rows(T, D), _rows(T, D), vec, vec, vec, vec]
        outs = (jax.ShapeDtypeStruct((S, D), F32), jax.ShapeDtypeStruct((S, D), BF16))
        ospecs = (_rows(T, D), _rows(T, D))
    else:
        ins, specs = [x_prev, g, sc, sh], [_rows(T, D), vec, vec, vec]
        outs = jax.ShapeDtypeStruct((S, D), BF16)
        ospecs = _rows(T, D)
    res = _pcall(body, name=name, grid=(S // T,), in_specs=specs, out_specs=ospecs, out_shape=outs,
                 compiler_params=_cparams("parallel"))(*ins)
    return res if has_delta else (x_prev, res)


def norm_mod_bwd(x, dh, g, sc, sh, dx_res, res=None, *, name):
    S, D = x.shape
    T = ROW_TILE
    has_res = res is not None

    def body(*refs):
        x_r, dh_r, g_r, sc_r, sh_r, dr_r = refs[:6]
        dx_o, dg_o, dsc_o, dsh_o = refs[6 + 2 * has_res:10 + 2 * has_res]
        i = pl.program_id(0)
        xv = x_r[...]
        dh_v = dh_r[...]
        r = lax.rsqrt(jnp.mean(xv * xv, axis=-1, keepdims=True) + EPS)
        xh = xv * r
        dn = dh_v * (1.0 + sc_r[...])
        dxh = dn * g_r[...]
        dx = dr_r[...] + r * (dxh - xh * jnp.mean(dxh * xh, axis=-1, keepdims=True))
        dx_o[...] = dx
        _acc_rows(dg_o, jnp.sum(dn * xh, axis=0, keepdims=True), i)
        _acc_rows(dsc_o, jnp.sum(dh_v * (xh * g_r[...]), axis=0, keepdims=True), i)
        _acc_rows(dsh_o, jnp.sum(dh_v, axis=0, keepdims=True), i)
        if has_res:
            dl_r, gt_r = refs[6:8]
            dd_o, dgt_o = refs[12:14]
            dd_o[...] = (dx * gt_r[...]).astype(BF16)
            _acc_rows(dgt_o, jnp.sum(dx * dl_r[...], axis=0, keepdims=True), i)

    vec = _full((1, D))
    part = jax.ShapeDtypeStruct((8, D), F32)
    in_specs = [_rows(T, D), _rows(T, D), vec, vec, vec, _rows(T, D)]
    out_specs = [_rows(T, D), _full((8, D)), _full((8, D)), _full((8, D))]
    out_shape = [jax.ShapeDtypeStruct((S, D), F32), part, part, part]
    args = [x, dh, g, sc, sh, dx_res]
    if has_res:
        in_specs += [_rows(T, D), vec]
        out_specs += [_rows(T, D), _full((8, D))]
        out_shape += [jax.ShapeDtypeStruct((S, D), BF16), part]
        args += list(res)
    return _pcall(
        body, name=name, grid=(S // T,), in_specs=in_specs, out_specs=tuple(out_specs), out_shape=tuple(out_shape),
        compiler_params=_cparams("arbitrary"),
    )(*args)


def loss_head(x_mid, ffo, gt, target, *, name):
    S, D = x_mid.shape
    T = ROW_TILE

    def body(x_r, f_r, gt_r, t_r, dy_o, l_o, dd_o, dgt_o):
        i = pl.program_id(0)
        fv = f_r[...]
        e = x_r[...] + gt_r[...] * fv - t_r[...]
        dy = e * (1.0 / D)
        dy_o[...] = dy
        _acc_rows(l_o, jnp.sum(e * e, axis=0, keepdims=True), i)
        dd_o[...] = (dy * gt_r[...]).astype(BF16)
        _acc_rows(dgt_o, jnp.sum(dy * fv, axis=0, keepdims=True), i)

    part = jax.ShapeDtypeStruct((8, D), F32)
    return _pcall(
        body, name=name, grid=(S // T,),
        in_specs=[_rows(T, D), _rows(T, D), _full((1, D)), _rows(T, D)],
        out_specs=(_rows(T, D), _full((8, D)), _rows(T, D), _full((8, D))),
        out_shape=(jax.ShapeDtypeStruct((S, D), F32), part, jax.ShapeDtypeStruct((S, D), BF16), part),
        compiler_params=_cparams("arbitrary"),
    )(x_mid, ffo, gt, target)


def _rope(t, c_t, s1_t, s2_t):
    return t * c_t + pltpu.roll(t, LANE - ROT_DIM // 2, 1) * s1_t + pltpu.roll(t, ROT_DIM // 2, 1) * s2_t


def _rope_t(d, c_t, s1_t, s2_t):
    return d * c_t + pltpu.roll(d * s1_t, ROT_DIM // 2, 1) + pltpu.roll(d * s2_t, LANE - ROT_DIM // 2, 1)


def qk_prep_fwd(zq, g_q, g_k, tabs, *, name):
    S = zq.shape[0]
    T = ROW_TILE

    def body(q_r, k_r, gq_r, gk_r, c_r, s1_r, s2_r, qn_o, kn_o):
        c_t, s1_t, s2_t = c_r[...], s1_r[...], s2_r[...]
        for src, g_r, dst in ((q_r, gq_r, qn_o), (k_r, gk_r, kn_o)):
            for h in range(N_HEADS):
                cols = slice(h * HEAD_DIM, (h + 1) * HEAD_DIM)
                t = src[:, cols]
                r = lax.rsqrt(jnp.mean(t * t, axis=-1, keepdims=True) + EPS)
                dst[:, cols] = _rope(t * r * g_r[...], c_t, s1_t, s2_t)

    tab = _rows(T, LANE)
    shp = jax.ShapeDtypeStruct((S, ATTN_W), F32)
    return _pcall(
        body, name=name, grid=(S // T,),
        in_specs=[_rows(T, ATTN_W, 0), _rows(T, ATTN_W, 1), _full((1, LANE)), _full((1, LANE)), tab, tab, tab],
        out_specs=(_rows(T, ATTN_W), _rows(T, ATTN_W)), out_shape=(shp, shp),
        compiler_params=_cparams("parallel"),
    )(zq, zq, g_q, g_k, *tabs)


def attn_bwd_post(zq, g_q, g_k, tabs, dq, dkc, dkp, dvc, dvp, *, name):
    S = zq.shape[0]
    T = min(POST_TILE, S)
    nblk = S // T
    GW = HEADS_PER_GROUP * HEAD_DIM
    n_ref = [7 if BLOCK * d < T else 5 for d in DILATIONS]

    def body(*refs):
        q_r, k_r, gq_r, gk_r, c_r, s1_r, s2_r = refs[:7]
        grp = refs[7:7 + sum(n_ref)]
        dz_o, dgq_o, dgk_o = refs[7 + sum(n_ref):]
        i = pl.program_id(0)
        c_t, s1_t, s2_t = c_r[...], s1_r[...], s2_r[...]
        dgq = jnp.zeros((1, LANE), F32)
        dgk = jnp.zeros((1, LANE), F32)
        at = 0
        for gi, d in enumerate(DILATIONS):
            g_refs = grp[at:at + n_ref[gi]]
            at += n_ref[gi]
            sh = BLOCK * d
            if sh < T:
                dq_r, dkc_r, dkp_r, dkpn_r, dvc_r, dvp_r, dvpn_r = g_refs
                live = jnp.where(i + 1 < nblk, 1.0, 0.0)

                def shifted(cur_r, nxt_r, gc, live=live, sh=sh):
                    return jnp.concatenate([cur_r[sh:T, gc], live * nxt_r[:, gc]], axis=0)
            else:
                dq_r, dkc_r, dkp_r, dvc_r, dvp_r = g_refs
                dkpn_r = dvpn_r = None
                live = jnp.where(i + sh // T < nblk, 1.0, 0.0)

                def shifted(cur_r, nxt_r, gc, live=live):
                    return live * cur_r[:, gc]
            for hh in range(HEADS_PER_GROUP):
                h = gi * HEADS_PER_GROUP + hh
                cols = slice(h * HEAD_DIM, (h + 1) * HEAD_DIM)
                gc = slice(hh * HEAD_DIM, (hh + 1) * HEAD_DIM)
                dk_v = dkc_r[:, gc] + shifted(dkp_r, dkpn_r, gc)
                dv_v = dvc_r[:, gc] + shifted(dvp_r, dvpn_r, gc)
                dz_o[:, 2 * ATTN_W + h * HEAD_DIM:2 * ATTN_W + (h + 1) * HEAD_DIM] = dv_v.astype(BF16)
                for which, (src, g_r, d_out) in enumerate(((q_r, gq_r, dq_r[:, gc]), (k_r, gk_r, dk_v))):
                    t = src[:, cols]
                    r = lax.rsqrt(jnp.mean(t * t, axis=-1, keepdims=True) + EPS)
                    xh = t * r
                    dtn = _rope_t(d_out, c_t, s1_t, s2_t)
                    dxh = dtn * g_r[...]
                    dt = r * (dxh - xh * jnp.mean(dxh * xh, axis=-1, keepdims=True))
                    dz_o[:, which * ATTN_W + h * HEAD_DIM:which * ATTN_W + (h + 1) * HEAD_DIM] = dt.astype(BF16)
                    part = jnp.sum(dtn * xh, axis=0, keepdims=True)
                    if which == 0:
                        dgq = dgq + part
                    else:
                        dgk = dgk + part
        _acc_rows(dgq_o, dgq, i)
        _acc_rows(dgk_o, dgk, i)

    tab = _rows(T, LANE)
    specs = [_rows(T, ATTN_W, 0), _rows(T, ATTN_W, 1), _full((1, LANE)), _full((1, LANE)), tab, tab, tab]
    args = [zq, zq, g_q, g_k, *tabs]
    for gi, d in enumerate(DILATIONS):
        cur = _rows(T, GW)
        sh = BLOCK * d
        if sh < T:
            head = pl.BlockSpec((sh, GW), functools.partial(
                lambda i, k, last: (jnp.minimum((i + 1) * k, last), 0), k=T // sh, last=S // sh - 1))
            specs += [cur, cur, cur, head, cur, cur, head]
            args += [dq[gi], dkc[gi], dkp[gi], dkp[gi], dvc[gi], dvp[gi], dvp[gi]]
        else:
            nxt = pl.BlockSpec((T, GW), functools.partial(lambda i, s: (jnp.minimum(i + s, nblk - 1), 0), s=sh // T))
            specs += [cur, cur, nxt, cur, nxt]
            args += [dq[gi], dkc[gi], dkp[gi], dvc[gi], dvp[gi]]
    part = jax.ShapeDtypeStruct((8, LANE), F32)
    return _pcall(
        body, name=name, grid=(nblk,), in_specs=specs,
        out_specs=(_rows(T, 3 * ATTN_W), _full((8, LANE)), _full((8, LANE))),
        out_shape=(jax.ShapeDtypeStruct((S, 3 * ATTN_W), BF16), part, part),
        compiler_params=_cparams("arbitrary"),
    )(*args)


ATTN_UNITS = 16


def _attn_geometry(d, S):
    R = min(ATTN_UNITS * BLOCK, S)
    return R, R // (BLOCK * d), S // R


def _sub_rows(j, r, d):
    if d == 1:
        return pl.ds(j * BLOCK, BLOCK)
    return pl.ds(j * BLOCK * d + r, BLOCK, stride=d)


def _dot_nt(a, b):
    return lax.dot_general(a, b, (((1,), (1,)), ((), ())), preferred_element_type=F32)


def _dot_tn(a, b):
    return lax.dot_general(a, b, (((0,), (0,)), ((), ())), preferred_element_type=F32)


def _attn_specs(gi, R):
    h0 = gi * HEADS_PER_GROUP
    vcol = 2 * N_HEADS + h0
    cur = lambda off: pl.BlockSpec((R, HEAD_DIM), lambda h, n: (n, off + h))
    prev = lambda off: pl.BlockSpec((R, HEAD_DIM), lambda h, n: (jnp.maximum(n - 1, 0), off + h))
    return [cur(h0), cur(h0), prev(h0), cur(vcol), prev(vcol)]


ATTN_UNROLL = ATTN_UNITS


def _attn_masks(n):
    qi = lax.broadcasted_iota(jnp.int32, (BLOCK, 2 * BLOCK), 0)
    kj = lax.broadcasted_iota(jnp.int32, (BLOCK, 2 * BLOCK), 1)
    band = jnp.where(jnp.logical_and(kj >= qi, kj <= qi + BLOCK), 0.0, NEG)
    no_prev = band + jnp.where(kj < BLOCK, 1.0, 0.0) * jnp.where(n > 0, 0.0, NEG)
    return band, no_prev


def _attn_keys(kc_r, kp_r, vc_r, vp_r, j, r, d, J):
    rq = _sub_rows(j, r, d)
    if j > 0:
        rp = _sub_rows(j - 1, r, d)
        kp, vp = kc_r[rp, :], vc_r[rp, :]
    else:
        rp = _sub_rows(J - 1, r, d)
        kp, vp = kp_r[rp, :], vp_r[rp, :]
    kk = jnp.concatenate([kp, kc_r[rq, :]], axis=0).astype(BF16)
    vv = jnp.concatenate([vp, vc_r[rq, :]], axis=0).astype(BF16)
    return kk, vv


def _attn_units(unit, d, J):
    for j in range(J):
        if d == 1:
            unit(j, 0)
        else:
            def step(r, carry, j=j):
                unit(j, r)
                return carry
            lax.fori_loop(0, d, step, 0, unroll=min(d, ATTN_UNROLL))


def attn_fwd(qn, kn, zq, gi, *, name):
    S = qn.shape[0]
    d = DILATIONS[gi]
    R, J, nblk = _attn_geometry(d, S)
    scale = HEAD_DIM ** -0.5
    GW = HEADS_PER_GROUP * HEAD_DIM

    def body(q_r, kc_r, kp_r, vc_r, vp_r, o_o, l_o):
        band, no_prev = _attn_masks(pl.program_id(1))

        def unit(j, r):
            rq = _sub_rows(j, r, d)
            q = q_r[rq, :].astype(BF16)
            kk, vv = _attn_keys(kc_r, kp_r, vc_r, vp_r, j, r, d, J)
            s = _dot_nt(q, kk) * scale + (band if j > 0 else no_prev)
            m = jnp.max(s, axis=-1, keepdims=True)
            p = jnp.exp(s - m)
            l = jnp.sum(p, axis=-1, keepdims=True)
            o_o[rq, :] = jnp.dot(p.astype(BF16), vv, preferred_element_type=F32) / l
            l_o[rq, :] = jnp.broadcast_to(m + jnp.log(l), (BLOCK, HEAD_DIM))

        _attn_units(unit, d, J)

    ospec = pl.BlockSpec((R, HEAD_DIM), lambda h, n: (n, h))
    shp = jax.ShapeDtypeStruct((S, GW), F32)
    return _pcall(
        body, name=name, grid=(HEADS_PER_GROUP, nblk), in_specs=_attn_specs(gi, R),
        out_specs=(ospec, ospec), out_shape=(shp, shp),
        compiler_params=_cparams("parallel", "arbitrary"),
    )(qn, kn, kn, zq, zq)


def attn_bwd(qn, kn, zq, do, lse, cc, gi, *, name):
    S = qn.shape[0]
    d = DILATIONS[gi]
    R, J, nblk = _attn_geometry(d, S)
    scale = HEAD_DIM ** -0.5
    GW = HEADS_PER_GROUP * HEAD_DIM

    def body(q_r, kc_r, kp_r, vc_r, vp_r, do_r, l_r, c_r, dq_o, dkc_o, dkp_o, dvc_o, dvp_o):
        band, no_prev = _attn_masks(pl.program_id(1))

        def unit(j, r):
            rq = _sub_rows(j, r, d)
            q = q_r[rq, :].astype(BF16)
            kk, vv = _attn_keys(kc_r, kp_r, vc_r, vp_r, j, r, d, J)
            s_mask = band if j > 0 else no_prev
            dob = do_r[rq, :].astype(BF16)
            lv = l_r[rq, :]
            cv = c_r[rq, :]
            lv2 = jnp.concatenate([lv, lv], axis=1)
            cv2 = jnp.concatenate([cv, cv], axis=1)
            p = jnp.exp(_dot_nt(q, kk) * scale + s_mask - lv2)
            ds = (p * (_dot_nt(dob, vv) + cv2)).astype(BF16)
            dq_o[rq, :] = jnp.dot(ds, kk, preferred_element_type=F32) * scale
            dk2 = _dot_tn(ds, q) * scale
            dv2 = _dot_tn(p.astype(BF16), dob)
            dkp_o[rq, :] = dk2[0:BLOCK]
            dkc_o[rq, :] = dk2[BLOCK:2 * BLOCK]
            dvp_o[rq, :] = dv2[0:BLOCK]
            dvc_o[rq, :] = dv2[BLOCK:2 * BLOCK]

        _attn_units(unit, d, J)

    ospec = pl.BlockSpec((R, HEAD_DIM), lambda h, n: (n, h))
    shp = jax.ShapeDtypeStruct((S, GW), F32)
    return _pcall(
        body, name=name, grid=(HEADS_PER_GROUP, nblk),
        in_specs=_attn_specs(gi, R) + [ospec, ospec, ospec],
        out_specs=(ospec,) * 5, out_shape=(shp,) * 5,
        compiler_params=_cparams("parallel", "arbitrary"),
    )(qn, kn, kn, zq, zq, do, lse, cc)


def combine_fwd(o, lse, *, name):
    S, GW = o[0].shape
    T = ROW_TILE

    def body(o0, o1, o2, l0, l1, l2, a_o):
        m = jnp.maximum(jnp.maximum(l0[...], l1[...]), l2[...])
        e0, e1, e2 = jnp.exp(l0[...] - m), jnp.exp(l1[...] - m), jnp.exp(l2[...] - m)
        a_o[...] = ((e0 * o0[...] + e1 * o1[...] + e2 * o2[...]) / (e0 + e1 + e2)).astype(BF16)

    return _pcall(
        body, name=name, grid=(S // T,), in_specs=[_rows(T, GW)] * 6, out_specs=_rows(T, GW),
        out_shape=jax.ShapeDtypeStruct((S, GW), BF16), compiler_params=_cparams("parallel"),
    )(*o, *lse)


def combine_bwd(d_attn, o, lse, *, name):
    S, GW = d_attn.shape
    T = ROW_TILE

    def body(da_r, o0, o1, o2, l0, l1, l2, d0, d1, d2, c0, c1, c2):
        m = jnp.maximum(jnp.maximum(l0[...], l1[...]), l2[...])
        e0, e1, e2 = jnp.exp(l0[...] - m), jnp.exp(l1[...] - m), jnp.exp(l2[...] - m)
        inv = 1.0 / (e0 + e1 + e2)
        w = (e0 * inv, e1 * inv, e2 * inv)
        da = da_r[...]
        attn = w[0] * o0[...] + w[1] * o1[...] + w[2] * o2[...]
        prod = da * attn
        for hh in range(HEADS_PER_GROUP):
            cols = slice(hh * HEAD_DIM, (hh + 1) * HEAD_DIM)
            a_h = jnp.sum(prod[:, cols], axis=-1, keepdims=True)
            for w_g, d_o, c_o in zip(w, (d0, d1, d2), (c0, c1, c2)):
                d_o[:, cols] = w_g[:, cols] * da[:, cols]
                c_o[:, cols] = -w_g[:, cols] * a_h

    shp = jax.ShapeDtypeStruct((S, GW), F32)
    return _pcall(
        body, name=name, grid=(S // T,), in_specs=[_rows(T, GW)] * 7, out_specs=(_rows(T, GW),) * 6,
        out_shape=(shp,) * 6, compiler_params=_cparams("parallel"),
    )(d_attn, *o, *lse)


def _halo_prev(T, H, C, col):
    k = T // H
    return pl.BlockSpec((H, C), lambda i: (jnp.maximum(i * k - 1, 0), col))


def _halo_next(T, H, C, col, n_rows):
    k = T // H
    last = n_rows // H - 1
    return pl.BlockSpec((H, C), lambda i: (jnp.minimum((i + 1) * k, last), col))


CONV_RC = 64
SUBLANES = 8


def _tap_groups(offs):
    groups = {}
    for k, off in offs:
        groups.setdefault(off % SUBLANES, []).append((k, off))
    return [taps for _, taps in sorted(groups.items())]


def _for_taps(src, r0, lanes, offs, fn):
    for taps in _tap_groups(offs):
        lo = min(off for _, off in taps)
        hi = max(off for _, off in taps)
        sb = src[r0 + lo:r0 + hi + CONV_RC, lanes]
        for k, off in taps:
            fn(k, sb[off - lo:off - lo + CONV_RC])


def _dwconv(src, w_r, offs, T, C, bias_r, dst):
    for rc in range(T // CONV_RC):
        for cc in range(C // LANE):
            lanes = slice(cc * LANE, (cc + 1) * LANE)
            r0 = rc * CONV_RC
            acc = None if bias_r is None else jnp.zeros((CONV_RC, LANE), F32) + bias_r[:, lanes]
            for taps in _tap_groups(offs):
                lo = min(off for _, off in taps)
                hi = max(off for _, off in taps)
                sb = src[r0 + lo:r0 + hi + CONV_RC, lanes]
                g_acc = None
                for k, off in taps:
                    term = w_r[k:k + 1, lanes] * sb[off - lo:off - lo + CONV_RC]
                    g_acc = term if g_acc is None else g_acc + term
                acc = g_acc if acc is None else acc + g_acc
            dst[r0:r0 + CONV_RC, lanes] = acc


def _fill_glu(cv, cg, hv, hg, ubuf, i):
    T = cv.shape[0]
    live = jnp.where(i > 0, 1.0, 0.0)
    ubuf[0:CONV_HALO, :] = live * (hv[...] * _sig(hg[...]))
    ubuf[CONV_HALO:CONV_HALO + T, :] = cv[...] * _sig(cg[...])


_CONV_FWD_OFFS = [(k, CONV_HALO - (CONV_K - 1) + k) for k in range(CONV_K)]
_CONV_BWD_OFFS = [(k, CONV_K - 1 - k) for k in range(CONV_K)]


def convb_fwd(zc, w, b, g_ln, b_ln, *, name):
    S = zc.shape[0]
    C = zc.shape[1] // 2
    T = ROW_TILE

    def body(cv, cg, hv, hg, w_r, b_r, g_r, bl_r, u_o, y_o, ubuf):
        _fill_glu(cv, cg, hv, hg, ubuf, pl.program_id(0))
        _dwconv(ubuf, w_r, _CONV_FWD_OFFS, T, C, b_r, y_o)
        y = y_o[...]
        mu = jnp.mean(y, axis=-1, keepdims=True)
        yc = y - mu
        rs = lax.rsqrt(jnp.mean(yc * yc, axis=-1, keepdims=True) + EPS)
        v = yc * rs * g_r[...] + bl_r[...]
        u_o[...] = (v * _sig(v)).astype(BF16)

    vec = _full((1, C))
    return _pcall(
        body, name=name, grid=(S // T,),
        in_specs=[_rows(T, C, 0), _rows(T, C, 1), _halo_prev(T, CONV_HALO, C, 0), _halo_prev(T, CONV_HALO, C, 1),
                  _full((CONV_K, C)), vec, vec, vec],
        out_specs=(_rows(T, C), _rows(T, C)),
        out_shape=(jax.ShapeDtypeStruct((S, C), BF16), jax.ShapeDtypeStruct((S, C), F32)),
        scratch_shapes=[pltpu.VMEM((CONV_HALO + T, C), F32)],
        compiler_params=_cparams("parallel"),
    )(zc, zc, zc, zc, w, b, g_ln, b_ln)


def convb_bwd1(d_u2, y_conv, zc, g_ln, b_ln, *, name):
    S = zc.shape[0]
    C = zc.shape[1] // 2
    T = ROW_TILE

    def body(du_r, y_r, cv, cg, hv, hg, g_r, bl_r, dy_o, dw_o, db_o, dg_o, dbl_o, ubuf):
        i = pl.program_id(0)
        _fill_glu(cv, cg, hv, hg, ubuf, i)
        y = y_r[...]
        mu = jnp.mean(y, axis=-1, keepdims=True)
        yc = y - mu
        rs = lax.rsqrt(jnp.mean(yc * yc, axis=-1, keepdims=True) + EPS)
        yn = yc * rs
        v = yn * g_r[...] + bl_r[...]
        sg = _sig(v)
        dv = du_r[...] * (sg * (1.0 + v * (1.0 - sg)))
        dyn = dv * g_r[...]
        dy = rs * (dyn - jnp.mean(dyn, axis=-1, keepdims=True) - yn * jnp.mean(dyn * yn, axis=-1, keepdims=True))
        dy_o[...] = dy
        _acc_rows(dg_o, jnp.sum(dv * yn, axis=0, keepdims=True), i)
        _acc_rows(dbl_o, jnp.sum(dv, axis=0, keepdims=True), i)
        _acc_rows(db_o, jnp.sum(dy, axis=0, keepdims=True), i)

        @pl.when(i == 0)
        def _():
            dw_o[...] = jnp.zeros_like(dw_o)
        for cc in range(C // LANE):
            lanes = slice(cc * LANE, (cc + 1) * LANE)
            parts = [jnp.zeros((SUBLANES, LANE), F32) for _ in range(CONV_K)]
            for rc in range(T // CONV_RC):
                r0 = rc * CONV_RC
                dyc = dy_o[r0:r0 + CONV_RC, lanes]

                def tap(k, chunk, parts=parts, dyc=dyc):
                    prod = (dyc * chunk).reshape(CONV_RC // SUBLANES, SUBLANES, LANE)
                    parts[k] = parts[k] + jnp.sum(prod, axis=0)

                _for_taps(ubuf, r0, lanes, _CONV_FWD_OFFS, tap)
            for k in range(CONV_K):
                dw_o[k * SUBLANES:(k + 1) * SUBLANES, lanes] += parts[k]

    vec = _full((1, C))
    part = jax.ShapeDtypeStruct((8, C), F32)
    return _pcall(
        body, name=name, grid=(S // T,),
        in_specs=[_rows(T, C), _rows(T, C), _rows(T, C, 0), _rows(T, C, 1), _halo_prev(T, CONV_HALO, C, 0),
                  _halo_prev(T, CONV_HALO, C, 1), vec, vec],
        out_specs=(_rows(T, C), _full((CONV_K * SUBLANES, C)), _full((8, C)), _full((8, C)), _full((8, C))),
        out_shape=(jax.ShapeDtypeStruct((S, C), F32), jax.ShapeDtypeStruct((CONV_K * SUBLANES, C), F32),
                   part, part, part),
        scratch_shapes=[pltpu.VMEM((CONV_HALO + T, C), F32)],
        compiler_params=_cparams("arbitrary"),
    )(d_u2, y_conv, zc, zc, zc, zc, g_ln, b_ln)


def convb_bwd2(dy, zc, w, *, name):
    S = zc.shape[0]
    C = zc.shape[1] // 2
    T = ROW_TILE
    nblk = S // T

    def body(dy_r, dyn_r, cv, cg, w_r, dz_o, dbuf, dubuf):
        i = pl.program_id(0)
        live = jnp.where(i < nblk - 1, 1.0, 0.0)
        dbuf[0:T, :] = dy_r[...]
        dbuf[T:T + CONV_HALO, :] = live * dyn_r[...]
        _dwconv(dbuf, w_r, _CONV_BWD_OFFS, T, C, None, dubuf)
        du = dubuf[...]
        sg = _sig(cg[...])
        dz_o[:, 0:C] = (du * sg).astype(BF16)
        dz_o[:, C:2 * C] = (du * cv[...] * sg * (1.0 - sg)).astype(BF16)

    return _pcall(
        body, name=name, grid=(nblk,),
        in_specs=[_rows(T, C), _halo_next(T, CONV_HALO, C, 0, S), _rows(T, C, 0), _rows(T, C, 1), _full((CONV_K, C))],
        out_specs=_rows(T, 2 * C), out_shape=jax.ShapeDtypeStruct((S, 2 * C), BF16),
        scratch_shapes=[pltpu.VMEM((T + CONV_HALO, C), F32), pltpu.VMEM((T, C), F32)],
        compiler_params=_cparams("parallel"),
    )(dy, dy, zc, zc, w)


def merge_fwd(y_a, y_b, zg, *, name):
    S, D = y_a.shape
    T = ROW_TILE

    def body(a_r, b_r, ga_r, gb_r, m_o):
        m_o[...] = (_sig(ga_r[...]) * a_r[...] + _sig(gb_r[...]) * b_r[...]).astype(BF16)

    return _pcall(
        body, name=name, grid=(S // T,),
        in_specs=[_rows(T, D), _rows(T, D), _rows(T, D, 0), _rows(T, D, 1)],
        out_specs=_rows(T, D), out_shape=jax.ShapeDtypeStruct((S, D), BF16),
        compiler_params=_cparams("parallel"),
    )(y_a, y_b, zg, zg)


def merge_bwd(d_m, y_a, y_b, zg, *, name):
    S, D = y_a.shape
    T = ROW_TILE

    def body(dm_r, a_r, b_r, ga_r, gb_r, da_o, db_o, dz_o):
        dm = dm_r[...]
        sa, sb = _sig(ga_r[...]), _sig(gb_r[...])
        da_o[...] = (dm * sa).astype(BF16)
        db_o[...] = (dm * sb).astype(BF16)
        dz_o[:, 0:D] = (dm * a_r[...] * sa * (1.0 - sa)).astype(BF16)
        dz_o[:, D:2 * D] = (dm * b_r[...] * sb * (1.0 - sb)).astype(BF16)

    shp = jax.ShapeDtypeStruct((S, D), BF16)
    return _pcall(
        body, name=name, grid=(S // T,),
        in_specs=[_rows(T, D), _rows(T, D), _rows(T, D), _rows(T, D, 0), _rows(T, D, 1)],
        out_specs=(_rows(T, D), _rows(T, D), _rows(T, 2 * D)),
        out_shape=(shp, shp, jax.ShapeDtypeStruct((S, 2 * D), BF16)),
        compiler_params=_cparams("parallel"),
    )(d_m, y_a, y_b, zg, zg)


FFN_RC = 64


def _ffn_chunks(T, F):
    for cc in range(F // LANE):
        for rc in range(T // FFN_RC):
            yield rc * FFN_RC, slice(rc * FFN_RC, (rc + 1) * FFN_RC), slice(cc * LANE, (cc + 1) * LANE)


def _ffn_fill(g_r, hg_r, gbuf, i):
    T = g_r.shape[0]
    live = jnp.where(i > 0, 1.0, 0.0)
    gbuf[0:FFN_HALO, :] = live * hg_r[...]
    gbuf[FFN_HALO:FFN_HALO + T, :] = g_r[...]


def _ffn_gate_chunk(gbuf, w_r, b_r, r0, lanes, n=FFN_RC):
    taps = [gbuf[r0 + FFN_HALO - (FFN_K - 1) + k:r0 + FFN_HALO - (FFN_K - 1) + k + n, lanes]
            for k in range(FFN_K)]
    gp = b_r[:, lanes] + w_r[0:1, lanes] * taps[0]
    for k in range(1, FFN_K):
        gp = gp + w_r[k:k + 1, lanes] * taps[k]
    return gp, taps


def _sum8(v):
    return jnp.sum(v.reshape(v.shape[0] // SUBLANES, SUBLANES, v.shape[1]), axis=0)


def ffn_act_fwd(gu, w, b, *, name):
    S = gu.shape[0]
    F = gu.shape[1] // 2
    T = ROW_TILE // 2

    def body(g_r, u_r, hg_r, w_r, b_r, a_o, gbuf):
        _ffn_fill(g_r, hg_r, gbuf, pl.program_id(0))
        for r0, rows, lanes in _ffn_chunks(T, F):
            gp, _ = _ffn_gate_chunk(gbuf, w_r, b_r, r0, lanes)
            a_o[rows, lanes] = (gp * _sig(gp) * u_r[rows, lanes]).astype(BF16)

    return _pcall(
        body, name=name, grid=(S // T,),
        in_specs=[_rows(T, F, 0), _rows(T, F, 1), _halo_prev(T, FFN_HALO, F, 0), _full((FFN_K, F)), _full((1, F))],
        out_specs=_rows(T, F), out_shape=jax.ShapeDtypeStruct((S, F), BF16),
        scratch_shapes=[pltpu.VMEM((FFN_HALO + T, F), F32)],
        compiler_params=_cparams("parallel"),
    )(gu, gu, gu, w, b)


def ffn_act_bwd(d_a, gu, w, b, *, name):
    S = gu.shape[0]
    F = gu.shape[1] // 2
    T = ROW_TILE // 2
    H = FFN_HALO
    nblk = S // T

    def dgp_of(gp, da, u):
        sg = _sig(gp)
        return da * u * (sg * (1.0 + gp * (1.0 - sg))), sg

    def body(da_r, dan_r, g_r, gp_r, gn_r, u_r, un_r, w_r, b_r, o_o, dw_o, db_o, gbuf, dbuf):
        i = pl.program_id(0)
        gbuf[0:H, :] = jnp.where(i > 0, 1.0, 0.0) * gp_r[...]
        gbuf[H:H + T, :] = g_r[...]
        gbuf[H + T:H + T + H, :] = gn_r[...]

        @pl.when(i == 0)
        def _():
            dw_o[...] = jnp.zeros_like(dw_o)
            db_o[...] = jnp.zeros_like(db_o)

        live_n = jnp.where(i < nblk - 1, 1.0, 0.0)
        sums = None
        for r0, rows, lanes in _ffn_chunks(T, F):
            if r0 == 0:
                sums = [jnp.zeros((SUBLANES, LANE), F32) for _ in range(FFN_K + 1)]
                gp_h, _ = _ffn_gate_chunk(gbuf, w_r, b_r, T, lanes, n=H)
                dgp_h, _ = dgp_of(gp_h, live_n * dan_r[:, lanes], un_r[:, lanes])
                dbuf[T:T + H, lanes] = dgp_h
            gp, taps = _ffn_gate_chunk(gbuf, w_r, b_r, r0, lanes)
            da = da_r[rows, lanes]
            dgp, sg = dgp_of(gp, da, u_r[rows, lanes])
            o_o[rows, F + lanes.start:F + lanes.stop] = (da * gp * sg).astype(BF16)
            dbuf[rows, lanes] = dgp
            sums[FFN_K] = sums[FFN_K] + _sum8(dgp)
            for k in range(FFN_K):
                sums[k] = sums[k] + _sum8(dgp * taps[k])
            if r0 + FFN_RC == T:
                db_o[:, lanes] += sums[FFN_K]
                for k in range(FFN_K):
                    dw_o[k * SUBLANES:(k + 1) * SUBLANES, lanes] += sums[k]
        for r0, rows, lanes in _ffn_chunks(T, F):
            dg = w_r[0:1, lanes] * dbuf[r0 + FFN_K - 1:r0 + FFN_K - 1 + FFN_RC, lanes]
            for k in range(1, FFN_K):
                off = r0 + FFN_K - 1 - k
                dg = dg + w_r[k:k + 1, lanes] * dbuf[off:off + FFN_RC, lanes]
            o_o[rows, lanes] = dg.astype(BF16)

    return _pcall(
        body, name=name, grid=(nblk,),
        in_specs=[_rows(T, F), _halo_next(T, H, F, 0, S), _rows(T, F, 0), _halo_prev(T, H, F, 0),
                  _halo_next(T, H, F, 0, S), _rows(T, F, 1), _halo_next(T, H, F, 1, S),
                  _full((FFN_K, F)), _full((1, F))],
        out_specs=(_rows(T, 2 * F), _full((FFN_K * SUBLANES, F)), _full((SUBLANES, F))),
        out_shape=(jax.ShapeDtypeStruct((S, 2 * F), BF16),
                   jax.ShapeDtypeStruct((FFN_K * SUBLANES, F), F32), jax.ShapeDtypeStruct((SUBLANES, F), F32)),
        scratch_shapes=[pltpu.VMEM((H + T + H, F), F32), pltpu.VMEM((T + H, F), F32)],
        compiler_params=_cparams("arbitrary"),
    )(d_a, d_a, gu, gu, gu, gu, gu, w, b)


def _row_tile(R, target=512):
    if R <= target:
        return R
    for t in range(target, 7, -8):
        if R % t == 0:
            return t
    return R


def sum_slots(land, *, name):
    _, R, C = land.shape
    T = _row_tile(R)

    def body(l_r, o_o):
        acc = l_r[0].astype(F32)
        for q in range(1, N_DEV):
            acc = acc + l_r[q].astype(F32)
        o_o[...] = acc

    return _pcall(
        body, name=name, grid=(R // T,),
        in_specs=[pl.BlockSpec((N_DEV, T, C), lambda i: (0, i, 0))],
        out_specs=_rows(T, C), out_shape=jax.ShapeDtypeStruct((R, C), F32),
        compiler_params=_cparams("parallel"),
    )(land)


def adamw(w, g, m, v, *, name):
    shape = w.shape
    C = shape[-1]
    R = math.prod(shape[:-1])
    w2, g2, m2, v2 = (t.reshape(R, C) for t in (w, g, m, v))
    T = _row_tile(R)
    c1 = 1.0 - ADAM_B1 ** ADAM_STEP
    c2 = 1.0 - ADAM_B2 ** ADAM_STEP

    def body(w_r, g_r, m_r, v_r, d_o, m_o, v_o):
        gv = g_r[...]
        mn = ADAM_B1 * m_r[...] + (1.0 - ADAM_B1) * gv
        vn = ADAM_B2 * v_r[...] + (1.0 - ADAM_B2) * (gv * gv)
        m_o[...] = mn
        v_o[...] = vn
        d_o[...] = -ADAM_LR * ((mn / c1) / (jnp.sqrt(vn / c2) + ADAM_EPS) + ADAM_WD * w_r[...])

    shp = jax.ShapeDtypeStruct((R, C), F32)
    d, mn, vn = _pcall(
        body, name=name, grid=(R // T,), in_specs=[_rows(T, C)] * 4, out_specs=(_rows(T, C),) * 3,
        out_shape=(shp,) * 3, compiler_params=_cparams("parallel"),
    )(w2, g2, m2, v2)
    return d.reshape(shape), mn.reshape(shape), vn.reshape(shape)


def _my_pos():
    return lax.axis_index("x"), lax.axis_index("y"), lax.axis_index("c")


class _Exchange:
    def __init__(self, inputs, out_shapes, sems, start, finish):
        self.inputs, self.out_shapes, self.sems, self.start, self.finish = inputs, out_shapes, sems, start, finish


def gather_exchange(shards):
    n = len(shards)

    def plan(ins, outs, sems):
        send_sems, recv_sems, local_sems = sems
        x, y, c = _my_pos()
        me, sibling = (x, y, c), (x, y, 1 - c)
        chips = [(1 - x, y), (x, 1 - y), (1 - x, 1 - y)]

        def slot(i, p):
            return outs[i].at[4 * p[0] + 2 * p[1] + p[2]]

        def copy(k, i, block, to, src=None):
            return pltpu.make_async_remote_copy(
                src_ref=slot(i, block) if src is None else src, dst_ref=slot(i, block),
                send_sem=send_sems.at[k, i], recv_sem=recv_sems.at[k, i], device_id=to, device_id_type=MESH)

        mine = [pltpu.make_async_copy(ins[i], slot(i, me), local_sems.at[i]) for i in range(n)]
        first = []
        for i in range(n):
            first.append(copy(0, i, me, sibling, src=ins[i]))
            first += [copy(1 + j, i, me, (*chip, c), src=ins[i]) for j, chip in enumerate(chips)]
        return me, sibling, chips, c, copy, mine, first

    def start(ins, outs, sems):
        _, _, _, _, _, mine, first = plan(ins, outs, sems)
        for cp in mine + first:
            cp.start()

    def finish(ins, outs, sems):
        me, sibling, chips, c, copy, mine, first = plan(ins, outs, sems)
        passed = []
        for j, chip in enumerate(chips):
            for i in range(n):
                copy(1 + j, i, (*chip, c), me).wait_recv()
                cp = copy(4 + j, i, (*chip, c), sibling)
                cp.start()
                passed.append(cp)
        for i in range(n):
            copy(0, i, sibling, me).wait_recv()
            for j, chip in enumerate(chips):
                copy(4 + j, i, (*chip, 1 - c), me).wait_recv()
        for cp in first + passed:
            cp.wait_send()
        for cp in mine:
            cp.wait()

    outs = [jax.ShapeDtypeStruct((N_DEV,) + s.shape, s.dtype) for s in shards]
    sems = [pltpu.SemaphoreType.DMA((7, n)), pltpu.SemaphoreType.DMA((7, n)), pltpu.SemaphoreType.DMA((n,))]
    return _Exchange(list(shards), outs, sems, start, finish)


def scatter_exchange(gs):
    n = len(gs)

    def plan(ins, outs, sems):
        send_sems, recv_sems, local_sems = sems
        x, y, c = _my_pos()
        me_id = 4 * x + 2 * y + c
        mine = [pltpu.make_async_copy(ins[i].at[me_id], outs[i].at[me_id], local_sems.at[i]) for i in range(n)]
        sends, recvs = [], []
        for msk in range(1, N_DEV):
            px = 1 - x if msk & 4 else x
            py = 1 - y if msk & 2 else y
            pc = 1 - c if msk & 1 else c
            pid = 4 * px + 2 * py + pc
            for i in range(n):
                sends.append(pltpu.make_async_remote_copy(
                    src_ref=ins[i].at[pid], dst_ref=outs[i].at[me_id],
                    send_sem=send_sems.at[msk - 1, i], recv_sem=recv_sems.at[msk - 1, i],
                    device_id=(px, py, pc), device_id_type=MESH))
                recvs.append(pltpu.make_async_remote_copy(
                    src_ref=ins[i].at[pid], dst_ref=outs[i].at[pid],
                    send_sem=send_sems.at[msk - 1, i], recv_sem=recv_sems.at[msk - 1, i],
                    device_id=(px, py, pc), device_id_type=MESH))
        return mine, sends, recvs

    def start(ins, outs, sems):
        mine, sends, _ = plan(ins, outs, sems)
        for cp in mine + sends:
            cp.start()

    def finish(ins, outs, sems):
        mine, sends, recvs = plan(ins, outs, sems)
        for rv in recvs:
            rv.wait_recv()
        for cp in sends:
            cp.wait_send()
        for cp in mine:
            cp.wait()

    outs = [jax.ShapeDtypeStruct(g.shape, g.dtype) for g in gs]
    sems = [pltpu.SemaphoreType.DMA((7, n)), pltpu.SemaphoreType.DMA((7, n)), pltpu.SemaphoreType.DMA((n,))]
    return _Exchange(list(gs), outs, sems, start, finish)


def run_exchange(ex, *, name):
    HBM = pl.BlockSpec(memory_space=pl.ANY)
    n_in, n_out = len(ex.inputs), len(ex.out_shapes)

    def body(*refs):
        ins, outs, sems = refs[:n_in], refs[n_in:n_in + n_out], refs[n_in + n_out:]
        ex.start(ins, outs, sems)
        ex.finish(ins, outs, sems)

    return list(_pcall(body, name=name, in_specs=[HBM] * n_in, out_specs=tuple([HBM] * n_out),
                       out_shape=tuple(ex.out_shapes), scratch_shapes=ex.sems)(*ex.inputs))


def all_gather(xs, *, name):
    outs = run_exchange(gather_exchange([xs[l] for l in range(xs.shape[0])]), name=name)
    return jnp.stack(outs)


def _slots(g):
    return g.reshape(N_DEV, g.shape[0] // N_DEV, g.shape[1])


def _rope_tables(positions):
    half = ROT_DIM // 2
    inv_freq = ROPE_THETA ** (-jnp.arange(0, ROT_DIM, 2, dtype=F32) / ROT_DIM)
    ang = positions.astype(F32)[:, None] * inv_freq
    cos, sin = jnp.cos(ang), jnp.sin(ang)
    S = positions.shape[0]
    c_t = jnp.concatenate([cos, cos, jnp.ones((S, LANE - ROT_DIM), F32)], axis=1)
    s1_t = jnp.concatenate([-sin, jnp.zeros((S, LANE - half), F32)], axis=1)
    s2_t = jnp.concatenate([jnp.zeros((S, half), F32), sin, jnp.zeros((S, LANE - ROT_DIM), F32)], axis=1)
    return c_t, s1_t, s2_t


def _row(v):
    return v.reshape(1, -1)


def kernel(x, c, positions, w_ada, b_ada, g_norm1, w_in, g_q, g_k, w_attn_proj, w_conv_dw, b_conv_dw, g_conv_ln, b_conv_ln, w_conv_out, w_o, g_norm2, w_ffn_in, w_ffn_dw, b_ffn_dw, w_ffn_down, loss_target, m_w_ada, m_b_ada, m_g_norm1, m_w_in, m_g_q, m_g_k, m_w_attn_proj, m_w_conv_dw, m_b_conv_dw, m_g_conv_ln, m_b_conv_ln, m_w_conv_out, m_w_o, m_g_norm2, m_w_ffn_in, m_w_ffn_dw, m_b_ffn_dw, m_w_ffn_down, v_w_ada, v_b_ada, v_g_norm1, v_w_in, v_g_q, v_g_k, v_w_attn_proj, v_w_conv_dw, v_b_conv_dw, v_g_conv_ln, v_b_conv_ln, v_w_conv_out, v_w_o, v_g_norm2, v_w_ffn_in, v_w_ffn_dw, v_b_ffn_dw, v_w_ffn_down):
    L = w_in.shape[0]
    S, D = x.shape[1], x.shape[2]
    FF = w_ffn_down.shape[1] * N_DEV
    xi, yi, ci = _my_pos()
    me = 4 * xi + 2 * yi + ci
    x0 = x[0]
    tabs = _rope_tables(positions[0])

    c_act = c * _sig(c)
    c_all = all_gather(jnp.pad(c_act, ((0, 7), (0, 0)))[None], name="ag_c")[0][:, 0, :]
    c_all16 = jnp.pad(c_all, ((0, 8), (0, 0)))
    m_part = jnp.stack([mm(c_all16, w_ada[l], "nn", name="mod_mm") for l in range(L)])
    m_all = all_gather(m_part, name="ag_mod")
    mod = lax.dynamic_index_in_dim(m_all, me, axis=2, keepdims=False).reshape(L, 6 * D) + b_ada
    mod = mod.reshape(L, 6, 1, D)

    sh_in = jnp.transpose(w_in, (0, 2, 1)).astype(BF16)
    sh_fi = jnp.transpose(w_ffn_in, (0, 2, 1)).astype(BF16)
    sh_ap = jnp.transpose(w_attn_proj, (0, 2, 1)).astype(BF16)
    sh_co, sh_oo, sh_dn = w_conv_out.astype(BF16), w_o.astype(BF16), w_ffn_down.astype(BF16)

    def rowcat(g):
        return g.reshape(N_DEV * g.shape[1], g.shape[2])

    wt_in, wt_fi, wt_ap, w_co, w_oo, w_dn = ([None] * L for _ in range(6))
    wt_in[0] = rowcat(run_exchange(gather_exchange([sh_in[0]]), name="ag_w0")[0])
    cdw = all_gather(jnp.pad(w_conv_dw, ((0, 0), (0, 1), (0, 0))), name="ag_cdw")
    cdw = jnp.transpose(cdw, (0, 2, 1, 3)).reshape(L, 32, D)[:, :CONV_K]
    fsh = w_ffn_dw.shape[2]
    fpad = -fsh % LANE
    fdw = all_gather(jnp.pad(w_ffn_dw, ((0, 0), (0, 8 - FFN_K), (0, fpad))), name="ag_fdw")
    fdw = jnp.transpose(fdw[:, :, :FFN_K, :fsh], (0, 2, 1, 3)).reshape(L, FFN_K, FF)

    QKV, CW = 3 * ATTN_W, 2 * D
    seg = ((0, QKV), (QKV, CW), (QKV + CW, 2 * D))

    saved = []
    x_prev, delta, gt_prev = x0, None, None
    for l in range(L):
        sh1, sc1, gt1, sh2, sc2, gt2 = (mod[l, i] for i in range(6))
        x_l, h = norm_mod_fwd(x_prev, delta, gt_prev, _row(g_norm1[l]), sc1, sh1, name="norm_fwd")
        nxt = l + 1 < L
        if l == 0:
            zq, got = mm(h, wt_in[0], "nt", b_off=0, b_len=QKV, name="z_mm_ag0",
                         comm=gather_exchange([sh_fi[0], sh_ap[0], sh_co[0], sh_oo[0], sh_dn[0]]))
            wt_fi[0], wt_ap[0], w_co[0], w_oo[0], w_dn[0] = (rowcat(g) for g in got)
        elif nxt:
            zq, got = mm(h, wt_in[l], "nt", b_off=0, b_len=QKV, name="z_mm_ag", comm=gather_exchange([sh_in[l + 1]]))
            wt_in[l + 1] = rowcat(got[0])
        else:
            zq = mm(h, wt_in[l], "nt", b_off=0, b_len=QKV, name="z_mm")
        if l == 0 and nxt:
            zc, got = mm(h, wt_in[0], "nt", b_off=seg[1][0], b_len=seg[1][1], name="zc_mm_ag",
                         comm=gather_exchange([sh_in[1]]))
            wt_in[1] = rowcat(got[0])
        else:
            zc = mm(h, wt_in[l], "nt", b_off=seg[1][0], b_len=seg[1][1], name="z_mm")
        zg = mm(h, wt_in[l], "nt", b_off=seg[2][0], b_len=seg[2][1], name="z_mm")
        gq, gk = _row(g_q[l]), _row(g_k[l])
        qn, kn = qk_prep_fwd(zq, gq, gk, tabs, name="qk_prep")
        o_g, lse_g = [], []
        for gi in range(3):
            o_i, l_i = attn_fwd(qn, kn, zq, gi, name="attn_fwd%d" % gi)
            o_g.append(o_i)
            lse_g.append(l_i)
        attn = combine_fwd(o_g, lse_g, name="combine_fwd")
        y_a = mm(attn, wt_ap[l], "nt", name="ya_mm")
        cw, cb = cdw[l], _row(b_conv_dw[l])
        cg, cbl = _row(g_conv_ln[l]), _row(b_conv_ln[l])
        u2, y_conv = convb_fwd(zc, cw, cb, cg, cbl, name="convb_fwd")
        y_b = mm(u2, w_co[l], "nn", name="yb_mm")
        merged = merge_fwd(y_a, y_b, zg, name="merge_fwd")
        mo = mm(merged, w_oo[l], "nn", name="mo_mm")
        x_mid, h2 = norm_mod_fwd(x_l, mo, gt1, _row(g_norm2[l]), sc2, sh2, name="norm_fwd")
        if nxt:
            gu, got = mm(h2, wt_fi[l], "nt", name="gu_mm_ag", comm=gather_exchange([sh_fi[l + 1]]))
            wt_fi[l + 1] = rowcat(got[0])
        else:
            gu = mm(h2, wt_fi[l], "nt", name="gu_mm")
        fw, fb = fdw[l], _row(b_ffn_dw[l])
        act = ffn_act_fwd(gu, fw, fb, name="ffn_act")
        if nxt:
            ffo, got = mm(act, w_dn[l], "nn", name="ffo_mm_ag",
                          comm=gather_exchange([sh_ap[l + 1], sh_co[l + 1], sh_oo[l + 1], sh_dn[l + 1]]))
            wt_ap[l + 1], w_co[l + 1], w_oo[l + 1], w_dn[l + 1] = (rowcat(g) for g in got)
        else:
            ffo = mm(act, w_dn[l], "nn", name="ffo_mm")
        saved.append(dict(x=x_l, h=h, zq=zq, zc=zc, zg=zg, qn=qn, kn=kn, o=o_g, lse=lse_g, attn=attn, y_a=y_a,
                          u2=u2, y_conv=y_conv, y_b=y_b, merged=merged, mo=mo, x_mid=x_mid, h2=h2, gu=gu, act=act, ffo=ffo))
        x_prev, delta, gt_prev = x_mid, ffo, gt2

    dx, lpart, d_ffo, p_gt2 = loss_head(x_prev, delta, gt_prev, loss_target[0], name="loss_head")
    loss = lax.psum(0.5 / D * jnp.sum(lpart[0]), ("x", "y", "c"))

    land = {k: [None] * L for k in ("in", "fi", "ap", "co", "o", "dn")}
    small_rows = []
    for l in reversed(range(L)):
        sv = saved[l]
        sh1, sc1, gt1, sh2, sc2, gt2 = (mod[l, i] for i in range(6))
        d_act = mm(d_ffo, w_dn[l], "nt", name="dact_mm")
        g_dn = mm(sv["act"], d_ffo, "tn", out_dtype=BF16, name="dwdn_mm")
        fw, fb = fdw[l], _row(b_ffn_dw[l])
        dgu, p_fw, p_fb = ffn_act_bwd(d_act, sv["gu"], fw, fb, name="ffn_bwd")
        dh2, got = mm(dgu, wt_fi[l], "nn", name="dh2_mm_rs", comm=scatter_exchange([_slots(g_dn)]))
        land["dn"][l] = got[0]
        g_fi = mm(dgu, sv["h2"], "tn", out_dtype=BF16, name="dwfi_mm")
        dx, p_g2, p_sc2, p_sh2, d_mo, p_gt1 = norm_mod_bwd(sv["x_mid"], dh2, _row(g_norm2[l]), sc2, sh2, dx,
                                                           (sv["mo"], gt1), name="norm_bwd")
        d_merged = mm(d_mo, w_oo[l], "nt", name="dmerged_mm")
        g_o = mm(sv["merged"], d_mo, "tn", out_dtype=BF16, name="dwo_mm")
        d_ya, d_yb, dzg = merge_bwd(d_merged, sv["y_a"], sv["y_b"], sv["zg"], name="merge_bwd")
        d_attn = mm(d_ya, wt_ap[l], "nn", name="dattn_mm")
        g_ap = mm(d_ya, sv["attn"], "tn", out_dtype=BF16, name="dwap_mm")
        d_u2 = mm(d_yb, w_co[l], "nt", name="du2_mm")
        g_co = mm(sv["u2"], d_yb, "tn", out_dtype=BF16, name="dwco_mm")
        cw, cb = cdw[l], _row(b_conv_dw[l])
        cg, cbl = _row(g_conv_ln[l]), _row(b_conv_ln[l])
        dy, p_cw, p_cb, p_cg, p_cbl = convb_bwd1(d_u2, sv["y_conv"], sv["zc"], cg, cbl, name="convb_bwd1")
        dzc = convb_bwd2(dy, sv["zc"], cw, name="convb_bwd2")
        dd = combine_bwd(d_attn, sv["o"], sv["lse"], name="combine_bwd")
        do_g, cc_g = dd[:3], dd[3:]
        parts = [attn_bwd(sv["qn"], sv["kn"], sv["zq"], do_g[gi], sv["lse"][gi], cc_g[gi], gi,
                          name="attn_bwd%d" % gi) for gi in range(3)]
        gq, gk = _row(g_q[l]), _row(g_k[l])
        dzq, p_gq, p_gk = attn_bwd_post(sv["zq"], gq, gk, tabs, *[[p[i] for p in parts] for i in range(5)],
                                        name="attn_post")
        g_in_q, got = mm(dzq, sv["h"], "tn", out_dtype=BF16, name="dwin_mm_rs",
                         comm=scatter_exchange([_slots(g_fi), _slots(g_o), _slots(g_ap), _slots(g_co)]))
        land["fi"][l], land["o"][l], land["ap"][l], land["co"][l] = got
        g_in = jnp.concatenate([g_in_q] + [mm(dz_s, sv["h"], "tn", out_dtype=BF16, name="dwin_mm")
                                           for dz_s in (dzc, dzg)], axis=0)
        dh, got = mm(dzq, wt_in[l], "nn", b_off=0, name="dh_mm_rs", comm=scatter_exchange([_slots(g_in)]))
        land["in"][l] = got[0]
        for dz_s, (o, n) in zip((dzc, dzg), seg[1:]):
            dh = mm(dz_s, wt_in[l], "nn", b_off=o, c_in=dh, name="dh_mm")
        if l > 0:
            dx, p_g1, p_sc1, p_sh1, d_ffo_prev, p_gt2_prev = norm_mod_bwd(
                sv["x"], dh, _row(g_norm1[l]), sc1, sh1, dx, (saved[l - 1]["ffo"], mod[l - 1, 5]), name="norm_bwd")
        else:
            dx, p_g1, p_sc1, p_sh1 = norm_mod_bwd(sv["x"], dh, _row(g_norm1[l]), sc1, sh1, dx, name="norm_bwd")

        def row1k(p):
            v = p[0]
            pad = -v.shape[0] % D
            return jnp.pad(v, (0, pad)).reshape(-1, D)

        rows = [row1k(p) for p in (p_sh1, p_sc1, p_gt1, p_sh2, p_sc2, p_gt2, p_g1, p_g2)]
        rows.append(row1k(jnp.concatenate([p_gq, p_gk], axis=1)))
        rows += [row1k(p) for p in (p_cb, p_cg, p_cbl, jnp.sum(p_fb, axis=0, keepdims=True))]
        rows.append(jnp.sum(p_cw.reshape(CONV_K, SUBLANES, D), axis=1))
        rows += [row1k(jnp.sum(p_fw[k * SUBLANES:(k + 1) * SUBLANES], axis=0, keepdims=True))
                 for k in range(FFN_K)]
        blk = jnp.concatenate(rows, axis=0)
        small_rows.append(jnp.pad(blk, ((0, -blk.shape[0] % 8), (0, 0))))
        if l > 0:
            d_ffo, p_gt2 = d_ffo_prev, p_gt2_prev
    small_rows = small_rows[::-1]
    n_small = small_rows[0].shape[0]
    ff_rows = -(-FF // D)

    small = jnp.concatenate(small_rows, axis=0)[None]
    small_all = all_gather(small, name="ag_small")[0]
    small_sum = sum_slots(small_all, name="sum_small").reshape(L, n_small, D)
    small_all = small_all.reshape(N_DEV, L, n_small, D)

    g_b_ada = small_sum[:, 0:6].reshape(L, 6 * D)
    g_g1, g_g2 = small_sum[:, 6], small_sum[:, 7]
    g_gq, g_gk = small_sum[:, 8, 0:LANE], small_sum[:, 8, LANE:2 * LANE]
    g_cb, g_cg, g_cbl = small_sum[:, 9], small_sum[:, 10], small_sum[:, 11]
    r0 = 12
    g_fb = small_sum[:, r0:r0 + ff_rows].reshape(L, -1)[:, :FF]
    r0 += ff_rows
    g_cw_full = small_sum[:, r0:r0 + CONV_K]
    r0 += CONV_K
    g_fw_full = small_sum[:, r0:r0 + FFN_K * ff_rows].reshape(L, FFN_K, -1)[:, :, :FF]
    csh = w_conv_dw.shape[2]
    g_cw = lax.dynamic_slice_in_dim(g_cw_full, me * csh, csh, axis=2)
    g_fw = lax.dynamic_slice_in_dim(g_fw_full, me * fsh, fsh, axis=2)

    ash = w_ada.shape[2]
    dmod_all = small_all[:, :, 0:6].reshape(N_DEV, L, 6 * D)
    dmod_mine = lax.dynamic_slice_in_dim(dmod_all, me * ash, ash, axis=2)
    g_w_ada = jnp.stack([mm(c_all16, jnp.pad(dmod_mine[:, l], ((0, 8), (0, 0))), "tn", name="dwada_mm")
                         for l in range(L)])

    def reduced(key, transposed):
        out = jnp.stack([sum_slots(slots, name="sum_" + key) for slots in land[key]])
        return jnp.transpose(out, (0, 2, 1)) if transposed else out

    g_w_in = reduced("in", True)
    g_w_fi = reduced("fi", True)
    g_w_ap = reduced("ap", True)
    g_w_co = reduced("co", False)
    g_w_o = reduced("o", False)
    g_w_dn = reduced("dn", False)

    grads = [g_w_ada, g_b_ada, g_g1, g_w_in, g_gq, g_gk, g_w_ap, g_cw, g_cb, g_cg, g_cbl, g_w_co, g_w_o, g_g2,
             g_w_fi, g_fw, g_fb, g_w_dn]
    ws = [w_ada, b_ada, g_norm1, w_in, g_q, g_k, w_attn_proj, w_conv_dw, b_conv_dw, g_conv_ln, b_conv_ln,
          w_conv_out, w_o, g_norm2, w_ffn_in, w_ffn_dw, b_ffn_dw, w_ffn_down]
    ms = [m_w_ada, m_b_ada, m_g_norm1, m_w_in, m_g_q, m_g_k, m_w_attn_proj, m_w_conv_dw, m_b_conv_dw, m_g_conv_ln,
          m_b_conv_ln, m_w_conv_out, m_w_o, m_g_norm2, m_w_ffn_in, m_w_ffn_dw, m_b_ffn_dw, m_w_ffn_down]
    vs = [v_w_ada, v_b_ada, v_g_norm1, v_w_in, v_g_q, v_g_k, v_w_attn_proj, v_w_conv_dw, v_b_conv_dw, v_g_conv_ln,
          v_b_conv_ln, v_w_conv_out, v_w_o, v_g_norm2, v_w_ffn_in, v_w_ffn_dw, v_b_ffn_dw, v_w_ffn_down]
    deltas, new_m, new_v = [], [], []
    for w_i, g_i, m_i, v_i in zip(ws, grads, ms, vs):
        d_i, mn_i, vn_i = adamw(w_i, g_i, m_i, v_i, name="adamw")
        deltas.append(d_i)
        new_m.append(mn_i)
        new_v.append(vn_i)
    return (loss, dx[None], *grads, *deltas, *new_m, *new_v)
```

```python
import functools
import math

import jax
import jax.numpy as jnp
from jax import lax
from jax.experimental import pallas as pl
from jax.experimental.pallas import tpu as pltpu

F32 = jnp.float32
BF16 = jnp.bfloat16
MESH = pl.DeviceIdType.MESH
N_DEV = 8

EPS = 1e-6
HEAD_DIM = 128
BLOCK = 128
DILATIONS = (1, 4, 16)
HEADS_PER_GROUP = 4
N_HEADS = 12
ATTN_W = N_HEADS * HEAD_DIM
ROT_DIM = 32
ROPE_THETA = 500000.0
CONV_K = 31
CONV_HALO = 32
FFN_K = 3
FFN_HALO = 16
NEG = -1e30

ADAM_LR, ADAM_B1, ADAM_B2, ADAM_EPS, ADAM_WD, ADAM_STEP = 0.001, 0.9, 0.999, 1e-08, 0.01, 10

LANE = 128
VMEM_LIMIT = 56 * 1024 * 1024
ROW_TILE = 512
POST_TILE = 256


def _pcall(body, **kw):
    return pl.pallas_call(body, **kw)


def _cparams(*sem):
    return pltpu.CompilerParams(dimension_semantics=sem, vmem_limit_bytes=VMEM_LIMIT)


def _sig(v):
    return 1.0 / (1.0 + jnp.exp(-v))


def _divtile(dim, target):
    best = None
    for t in range(LANE, min(dim, target) + 1, LANE):
        if dim % t == 0:
            best = t
    return best or dim


def _rows(t, c, col=0):
    return pl.BlockSpec((t, c), lambda i: (i, col))


def _full(shape):
    nd = len(shape)
    return pl.BlockSpec(shape, lambda i: (0,) * nd)


def _acc_rows(ref, val, i):
    @pl.when(i == 0)
    def _():
        ref[...] = jnp.zeros_like(ref)
    r = val.shape[0]
    ref[0:r, :] += val


MM_TILE = 1536


def mm(a, b, mode, *, name, out_dtype=F32, c_in=None, b_off=0, b_len=None, comm=None):
    if mode == "nn":
        M, K = a.shape
        N = b.shape[1]
    elif mode == "nt":
        M, K = a.shape
        N = b_len if b_len is not None else b.shape[0]
    else:
        K, M = a.shape
        N = b.shape[1]
    g_n = math.gcd(N, b_off) if (mode == "nt" and b_off) else N
    g_k = math.gcd(K, b_off) if (mode == "nn" and b_off) else K
    tn = _divtile(g_n, MM_TILE if c_in is None else 1024)
    tk = _divtile(g_k, MM_TILE if mode != "tn" else 1024)
    tm = _divtile(M, MM_TILE if mode == "tn" else (2048 if (tn <= 512 and c_in is None) else 1024))
    gm, gn, nk = M // tm, N // tn, K // tk
    n_ci = 0 if comm is None else len(comm.inputs)
    n_co = 0 if comm is None else len(comm.out_shapes)
    n_x = 2 + (c_in is not None)
    if mode == "nn":
        dims = (((1,), (0,)), ((), ()))
    elif mode == "nt":
        dims = (((1,), (1,)), ((), ()))
    else:
        dims = (((0,), (0,)), ((), ()))

    def body(*refs):
        a_ref, b_ref = refs[0], refs[1]
        c_ref = refs[2] if c_in is not None else None
        c_ins = refs[n_x:n_x + n_ci]
        o_ref = refs[n_x + n_ci]
        c_outs = refs[n_x + n_ci + 1:n_x + n_ci + 1 + n_co]
        rest = refs[n_x + n_ci + 1 + n_co:]
        acc = rest[0] if nk > 1 else None
        sems = rest[1:] if nk > 1 else rest
        i, j, k = pl.program_id(0), pl.program_id(1), pl.program_id(2)

        if comm is not None:
            @pl.when(jnp.logical_and(jnp.logical_and(i == 0, j == 0), k == 0))
            def _():
                comm.start(c_ins, c_outs, sems)

        prod = lax.dot_general(a_ref[...].astype(BF16), b_ref[...].astype(BF16), dims, preferred_element_type=F32)
        if nk == 1:
            if c_ref is not None:
                prod = prod + c_ref[...].astype(F32)
            o_ref[...] = prod.astype(out_dtype)
        else:
            @pl.when(k == 0)
            def _():
                if c_ref is None:
                    acc[...] = prod
                else:
                    acc[...] = prod + c_ref[...].astype(F32)

            @pl.when(k > 0)
            def _():
                acc[...] += prod

            @pl.when(k == nk - 1)
            def _():
                o_ref[...] = acc[...].astype(out_dtype)

        if comm is not None:
            @pl.when(jnp.logical_and(jnp.logical_and(i == gm - 1, j == gn - 1), k == nk - 1))
            def _():
                comm.finish(c_ins, c_outs, sems)

    if mode == "nn":
        a_spec = pl.BlockSpec((tm, tk), lambda i, j, k: (i, k))
        ob = b_off // tk
        b_spec = pl.BlockSpec((tk, tn), lambda i, j, k: (k + ob, j))
    elif mode == "nt":
        a_spec = pl.BlockSpec((tm, tk), lambda i, j, k: (i, k))
        ob = b_off // tn
        b_spec = pl.BlockSpec((tn, tk), lambda i, j, k: (j + ob, k))
    else:
        a_spec = pl.BlockSpec((tk, tm), lambda i, j, k: (k, i))
        b_spec = pl.BlockSpec((tk, tn), lambda i, j, k: (k, j))
    o_spec = pl.BlockSpec((tm, tn), lambda i, j, k: (i, j))
    HBM = pl.BlockSpec(memory_space=pl.ANY)
    in_specs = [a_spec, b_spec]
    args = [a, b]
    if c_in is not None:
        in_specs.append(o_spec)
        args.append(c_in)
    scratch = [pltpu.VMEM((tm, tn), F32)] if nk > 1 else []
    o_shape = jax.ShapeDtypeStruct((M, N), out_dtype)
    if comm is None:
        return _pcall(
            body, name=name, grid=(gm, gn, nk), in_specs=in_specs, out_specs=o_spec, out_shape=o_shape,
            scratch_shapes=scratch, compiler_params=_cparams("parallel", "parallel", "arbitrary"),
        )(*args)
    res = _pcall(
        body, name=name, grid=(gm, gn, nk), in_specs=in_specs + [HBM] * n_ci,
        out_specs=(o_spec, *[HBM] * n_co), out_shape=(o_shape, *comm.out_shapes),
        scratch_shapes=scratch + comm.sems, compiler_params=_cparams("arbitrary", "arbitrary", "arbitrary"),
    )(*args, *comm.inputs)
    return res[0], list(res[1:])


def norm_mod_fwd(x_prev, delta, gt, g, sc, sh, *, name):
    S, D = x_prev.shape
    T = ROW_TILE
    has_delta = delta is not None

    def body(*refs):
        if has_delta:
            xp, dl, gt_r, g_r, sc_r, sh_r, x_out, h_out = refs
            xv = xp[...] + gt_r[...] * dl[...]
            x_out[...] = xv
        else:
            xp, g_r, sc_r, sh_r, h_out = refs
            xv = xp[...]
        r = lax.rsqrt(jnp.mean(xv * xv, axis=-1, keepdims=True) + EPS)
        h_out[...] = ((xv * r) * g_r[...] * (1.0 + sc_r[...]) + sh_r[...]).astype(BF16)

    vec = _full((1, D))
    if has_delta:
        ins, specs = [x_prev, delta, gt, g, sc, sh], [_rows(T, D), _rows(T, D), vec, vec, vec, vec]
        outs = (jax.ShapeDtypeStruct((S, D), F32), jax.ShapeDtypeStruct((S, D), BF16))
        ospecs = (_rows(T, D), _rows(T, D))
    else:
        ins, specs = [x_prev, g, sc, sh], [_rows(T, D), vec, vec, vec]
        outs = jax.ShapeDtypeStruct((S, D), BF16)
        ospecs = _rows(T, D)
    res = _pcall(body, name=name, grid=(S // T,), in_specs=specs, out_specs=ospecs, out_shape=outs,
                 compiler_params=_cparams("parallel"))(*ins)
    return res if has_delta else (x_prev, res)


def norm_mod_bwd(x, dh, g, sc, sh, dx_res, res=None, *, name):
    S, D = x.shape
    T = ROW_TILE
    has_res = res is not None

    def body(*refs):
        x_r, dh_r, g_r, sc_r, sh_r, dr_r = refs[:6]
        dx_o, dg_o, dsc_o, dsh_o = refs[6 + 2 * has_res:10 + 2 * has_res]
        i = pl.program_id(0)
        xv = x_r[...]
        dh_v = dh_r[...]
        r = lax.rsqrt(jnp.mean(xv * xv, axis=-1, keepdims=True) + EPS)
        xh = xv * r
        dn = dh_v * (1.0 + sc_r[...])
        dxh = dn * g_r[...]
        dx = dr_r[...] + r * (dxh - xh * jnp.mean(dxh * xh, axis=-1, keepdims=True))
        dx_o[...] = dx
        _acc_rows(dg_o, jnp.sum(dn * xh, axis=0, keepdims=True), i)
        _acc_rows(dsc_o, jnp.sum(dh_v * (xh * g_r[...]), axis=0, keepdims=True), i)
        _acc_rows(dsh_o, jnp.sum(dh_v, axis=0, keepdims=True), i)
        if has_res:
            dl_r, gt_r = refs[6:8]
            dd_o, dgt_o = refs[12:14]
            dd_o[...] = (dx * gt_r[...]).astype(BF16)
            _acc_rows(dgt_o, jnp.sum(dx * dl_r[...], axis=0, keepdims=True), i)

    vec = _full((1, D))
    part = jax.ShapeDtypeStruct((8, D), F32)
    in_specs = [_rows(T, D), _rows(T, D), vec, vec, vec, _rows(T, D)]
    out_specs = [_rows(T, D), _full((8, D)), _full((8, D)), _full((8, D))]
    out_shape = [jax.ShapeDtypeStruct((S, D), F32), part, part, part]
    args = [x, dh, g, sc, sh, dx_res]
    if has_res:
        in_specs += [_rows(T, D), vec]
        out_specs += [_rows(T, D), _full((8, D))]
        out_shape += [jax.ShapeDtypeStruct((S, D), BF16), part]
        args += list(res)
    return _pcall(
        body, name=name, grid=(S // T,), in_specs=in_specs, out_specs=tuple(out_specs), out_shape=tuple(out_shape),
        compiler_params=_cparams("arbitrary"),
    )(*args)


def loss_head(x_mid, ffo, gt, target, *, name):
    S, D = x_mid.shape
    T = ROW_TILE

    def body(x_r, f_r, gt_r, t_r, dy_o, l_o, dd_o, dgt_o):
        i = pl.program_id(0)
        fv = f_r[...]
        e = x_r[...] + gt_r[...] * fv - t_r[...]
        dy = e * (1.0 / D)
        dy_o[...] = dy
        _acc_rows(l_o, jnp.sum(e * e, axis=0, keepdims=True), i)
        dd_o[...] = (dy * gt_r[...]).astype(BF16)
        _acc_rows(dgt_o, jnp.sum(dy * fv, axis=0, keepdims=True), i)

    part = jax.ShapeDtypeStruct((8, D), F32)
    return _pcall(
        body, name=name, grid=(S // T,),
        in_specs=[_rows(T, D), _rows(T, D), _full((1, D)), _rows(T, D)],
        out_specs=(_rows(T, D), _full((8, D)), _rows(T, D), _full((8, D))),
        out_shape=(jax.ShapeDtypeStruct((S, D), F32), part, jax.ShapeDtypeStruct((S, D), BF16), part),
        compiler_params=_cparams("arbitrary"),
    )(x_mid, ffo, gt, target)


def _rope(t, c_t, s1_t, s2_t):
    return t * c_t + pltpu.roll(t, LANE - ROT_DIM // 2, 1) * s1_t + pltpu.roll(t, ROT_DIM // 2, 1) * s2_t


def _rope_t(d, c_t, s1_t, s2_t):
    return d * c_t + pltpu.roll(d * s1_t, ROT_DIM // 2, 1) + pltpu.roll(d * s2_t, LANE - ROT_DIM // 2, 1)


def qk_prep_fwd(zq, g_q, g_k, tabs, *, name):
    S = zq.shape[0]
    T = ROW_TILE

    def body(q_r, k_r, gq_r, gk_r, c_r, s1_r, s2_r, qn_o, kn_o):
        c_t, s1_t, s2_t = c_r[...], s1_r[...], s2_r[...]
        for src, g_r, dst in ((q_r, gq_r, qn_o), (k_r, gk_r, kn_o)):
            for h in range(N_HEADS):
                cols = slice(h * HEAD_DIM, (h + 1) * HEAD_DIM)
                t = src[:, cols]
                r = lax.rsqrt(jnp.mean(t * t, axis=-1, keepdims=True) + EPS)
                dst[:, cols] = _rope(t * r * g_r[...], c_t, s1_t, s2_t)

    tab = _rows(T, LANE)
    shp = jax.ShapeDtypeStruct((S, ATTN_W), F32)
    return _pcall(
        body, name=name, grid=(S // T,),
        in_specs=[_rows(T, ATTN_W, 0), _rows(T, ATTN_W, 1), _full((1, LANE)), _full((1, LANE)), tab, tab, tab],
        out_specs=(_rows(T, ATTN_W), _rows(T, ATTN_W)), out_shape=(shp, shp),
        compiler_params=_cparams("parallel"),
    )(zq, zq, g_q, g_k, *tabs)


def attn_bwd_post(zq, g_q, g_k, tabs, dq, dkc, dkp, dvc, dvp, *, name):
    S = zq.shape[0]
    T = min(POST_TILE, S)
    nblk = S // T
    GW = HEADS_PER_GROUP * HEAD_DIM
    n_ref = [7 if BLOCK * d < T else 5 for d in DILATIONS]

    def body(*refs):
        q_r, k_r, gq_r, gk_r, c_r, s1_r, s2_r = refs[:7]
        grp = refs[7:7 + sum(n_ref)]
        dz_o, dgq_o, dgk_o = refs[7 + sum(n_ref):]
        i = pl.program_id(0)
        c_t, s1_t, s2_t = c_r[...], s1_r[...], s2_r[...]
        dgq = jnp.zeros((1, LANE), F32)
        dgk = jnp.zeros((1, LANE), F32)
        at = 0
        for gi, d in enumerate(DILATIONS):
            g_refs = grp[at:at + n_ref[gi]]
            at += n_ref[gi]
            sh = BLOCK * d
            if sh < T:
                dq_r, dkc_r, dkp_r, dkpn_r, dvc_r, dvp_r, dvpn_r = g_refs
                live = jnp.where(i + 1 < nblk, 1.0, 0.0)

                def shifted(cur_r, nxt_r, gc, live=live, sh=sh):
                    return jnp.concatenate([cur_r[sh:T, gc], live * nxt_r[:, gc]], axis=0)
            else:
                dq_r, dkc_r, dkp_r, dvc_r, dvp_r = g_refs
                dkpn_r = dvpn_r = None
                live = jnp.where(i + sh // T < nblk, 1.0, 0.0)

                def shifted(cur_r, nxt_r, gc, live=live):
                    return live * cur_r[:, gc]
            for hh in range(HEADS_PER_GROUP):
                h = gi * HEADS_PER_GROUP + hh
                cols = slice(h * HEAD_DIM, (h + 1) * HEAD_DIM)
                gc = slice(hh * HEAD_DIM, (hh + 1) * HEAD_DIM)
                dk_v = dkc_r[:, gc] + shifted(dkp_r, dkpn_r, gc)
                dv_v = dvc_r[:, gc] + shifted(dvp_r, dvpn_r, gc)
                dz_o[:, 2 * ATTN_W + h * HEAD_DIM:2 * ATTN_W + (h + 1) * HEAD_DIM] = dv_v.astype(BF16)
                for which, (src, g_r, d_out) in enumerate(((q_r, gq_r, dq_r[:, gc]), (k_r, gk_r, dk_v))):
                    t = src[:, cols]
                    r = lax.rsqrt(jnp.mean(t * t, axis=-1, keepdims=True) + EPS)
                    xh = t * r
                    dtn = _rope_t(d_out, c_t, s1_t, s2_t)
                    dxh = dtn * g_r[...]
                    dt = r * (dxh - xh * jnp.mean(dxh * xh, axis=-1, keepdims=True))
                    dz_o[:, which * ATTN_W + h * HEAD_DIM:which * ATTN_W + (h + 1) * HEAD_DIM] = dt.astype(BF16)
                    part = jnp.sum(dtn * xh, axis=0, keepdims=True)
                    if which == 0:
                        dgq = dgq + part
                    else:
                        dgk = dgk + part
        _acc_rows(dgq_o, dgq, i)
        _acc_rows(dgk_o, dgk, i)

    tab = _rows(T, LANE)
    specs = [_rows(T, ATTN_W, 0), _rows(T, ATTN_W, 1), _full((1, LANE)), _full((1, LANE)), tab, tab, tab]
    args = [zq, zq, g_q, g_k, *tabs]
    for gi, d in enumerate(DILATIONS):
        cur = _rows(T, GW)
        sh = BLOCK * d
        if sh < T:
            head = pl.BlockSpec((sh, GW), functools.partial(
                lambda i, k, last: (jnp.minimum((i + 1) * k, last), 0), k=T // sh, last=S // sh - 1))
            specs += [cur, cur, cur, head, cur, cur, head]
            args += [dq[gi], dkc[gi], dkp[gi], dkp[gi], dvc[gi], dvp[gi], dvp[gi]]
        else:
            nxt = pl.BlockSpec((T, GW), functools.partial(lambda i, s: (jnp.minimum(i + s, nblk - 1), 0), s=sh // T))
            specs += [cur, cur, nxt, cur, nxt]
            args += [dq[gi], dkc[gi], dkp[gi], dvc[gi], dvp[gi]]
    part = jax.ShapeDtypeStruct((8, LANE), F32)
    return _pcall(
        body, name=name, grid=(nblk,), in_specs=specs,
        out_specs=(_rows(T, 3 * ATTN_W), _full((8, LANE)), _full((8, LANE))),
        out_shape=(jax.ShapeDtypeStruct((S, 3 * ATTN_W), BF16), part, part),
        compiler_params=_cparams("arbitrary"),
    )(*args)


ATTN_UNITS = 16


def _attn_geometry(d, S):
    R = min(ATTN_UNITS * BLOCK, S)
    return R, R // (BLOCK * d), S // R


def _sub_rows(j, r, d):
    if d == 1:
        return pl.ds(j * BLOCK, BLOCK)
    return pl.ds(j * BLOCK * d + r, BLOCK, stride=d)


def _dot_nt(a, b):
    return lax.dot_general(a, b, (((1,), (1,)), ((), ())), preferred_element_type=F32)


def _dot_tn(a, b):
    return lax.dot_general(a, b, (((0,), (0,)), ((), ())), preferred_element_type=F32)


def _attn_specs(gi, R):
    h0 = gi * HEADS_PER_GROUP
    vcol = 2 * N_HEADS + h0
    cur = lambda off: pl.BlockSpec((R, HEAD_DIM), lambda h, n: (n, off + h))
    prev = lambda off: pl.BlockSpec((R, HEAD_DIM), lambda h, n: (jnp.maximum(n - 1, 0), off + h))
    return [cur(h0), cur(h0), prev(h0), cur(vcol), prev(vcol)]


ATTN_UNROLL = ATTN_UNITS


def _attn_masks(n):
    qi = lax.broadcasted_iota(jnp.int32, (BLOCK, 2 * BLOCK), 0)
    kj = lax.broadcasted_iota(jnp.int32, (BLOCK, 2 * BLOCK), 1)
    band = jnp.where(jnp.logical_and(kj >= qi, kj <= qi + BLOCK), 0.0, NEG)
    no_prev = band + jnp.where(kj < BLOCK, 1.0, 0.0) * jnp.where(n > 0, 0.0, NEG)
    return band, no_prev


def _attn_keys(kc_r, kp_r, vc_r, vp_r, j, r, d, J):
    rq = _sub_rows(j, r, d)
    if j > 0:
        rp = _sub_rows(j - 1, r, d)
        kp, vp = kc_r[rp, :], vc_r[rp, :]
    else:
        rp = _sub_rows(J - 1, r, d)
        kp, vp = kp_r[rp, :], vp_r[rp, :]
    kk = jnp.concatenate([kp, kc_r[rq, :]], axis=0).astype(BF16)
    vv = jnp.concatenate([vp, vc_r[rq, :]], axis=0).astype(BF16)
    return kk, vv


def _attn_units(unit, d, J):
    for j in range(J):
        if d == 1:
            unit(j, 0)
        else:
            def step(r, carry, j=j):
                unit(j, r)
                return carry
            lax.fori_loop(0, d, step, 0, unroll=min(d, ATTN_UNROLL))


def attn_fwd(qn, kn, zq, gi, *, name):
    S = qn.shape[0]
    d = DILATIONS[gi]
    R, J, nblk = _attn_geometry(d, S)
    scale = HEAD_DIM ** -0.5
    GW = HEADS_PER_GROUP * HEAD_DIM

    def body(q_r, kc_r, kp_r, vc_r, vp_r, o_o, l_o):
        band, no_prev = _attn_masks(pl.program_id(1))

        def unit(j, r):
            rq = _sub_rows(j, r, d)
            q = q_r[rq, :].astype(BF16)
            kk, vv = _attn_keys(kc_r, kp_r, vc_r, vp_r, j, r, d, J)
            s = _dot_nt(q, kk) * scale + (band if j > 0 else no_prev)
            m = jnp.max(s, axis=-1, keepdims=True)
            p = jnp.exp(s - m)
            l = jnp.sum(p, axis=-1, keepdims=True)
            o_o[rq, :] = jnp.dot(p.astype(BF16), vv, preferred_element_type=F32) / l
            l_o[rq, :] = jnp.broadcast_to(m + jnp.log(l), (BLOCK, HEAD_DIM))

        _attn_units(unit, d, J)

    ospec = pl.BlockSpec((R, HEAD_DIM), lambda h, n: (n, h))
    shp = jax.ShapeDtypeStruct((S, GW), F32)
    return _pcall(
        body, name=name, grid=(HEADS_PER_GROUP, nblk), in_specs=_attn_specs(gi, R),
        out_specs=(ospec, ospec), out_shape=(shp, shp),
        compiler_params=_cparams("parallel", "arbitrary"),
    )(qn, kn, kn, zq, zq)


def attn_bwd(qn, kn, zq, do, lse, cc, gi, *, name):
    S = qn.shape[0]
    d = DILATIONS[gi]
    R, J, nblk = _attn_geometry(d, S)
    scale = HEAD_DIM ** -0.5
    GW = HEADS_PER_GROUP * HEAD_DIM

    def body(q_r, kc_r, kp_r, vc_r, vp_r, do_r, l_r, c_r, dq_o, dkc_o, dkp_o, dvc_o, dvp_o):
        band, no_prev = _attn_masks(pl.program_id(1))

        def unit(j, r):
            rq = _sub_rows(j, r, d)
            q = q_r[rq, :].astype(BF16)
            kk, vv = _attn_keys(kc_r, kp_r, vc_r, vp_r, j, r, d, J)
            s_mask = band if j > 0 else no_prev
            dob = do_r[rq, :].astype(BF16)
            lv = l_r[rq, :]
            cv = c_r[rq, :]
            lv2 = jnp.concatenate([lv, lv], axis=1)
            cv2 = jnp.concatenate([cv, cv], axis=1)
            p = jnp.exp(_dot_nt(q, kk) * scale + s_mask - lv2)
            ds = (p * (_dot_nt(dob, vv) + cv2)).astype(BF16)
            dq_o[rq, :] = jnp.dot(ds, kk, preferred_element_type=F32) * scale
            dk2 = _dot_tn(ds, q) * scale
            dv2 = _dot_tn(p.astype(BF16), dob)
            dkp_o[rq, :] = dk2[0:BLOCK]
            dkc_o[rq, :] = dk2[BLOCK:2 * BLOCK]
            dvp_o[rq, :] = dv2[0:BLOCK]
            dvc_o[rq, :] = dv2[BLOCK:2 * BLOCK]

        _attn_units(unit, d, J)

    ospec = pl.BlockSpec((R, HEAD_DIM), lambda h, n: (n, h))
    shp = jax.ShapeDtypeStruct((S, GW), F32)
    return _pcall(
        body, name=name, grid=(HEADS_PER_GROUP, nblk),
        in_specs=_attn_specs(gi, R) + [ospec, ospec, ospec],
        out_specs=(ospec,) * 5, out_shape=(shp,) * 5,
        compiler_params=_cparams("parallel", "arbitrary"),
    )(qn, kn, kn, zq, zq, do, lse, cc)


def combine_fwd(o, lse, *, name):
    S, GW = o[0].shape
    T = ROW_TILE

    def body(o0, o1, o2, l0, l1, l2, a_o):
        m = jnp.maximum(jnp.maximum(l0[...], l1[...]), l2[...])
        e0, e1, e2 = jnp.exp(l0[...] - m), jnp.exp(l1[...] - m), jnp.exp(l2[...] - m)
        a_o[...] = ((e0 * o0[...] + e1 * o1[...] + e2 * o2[...]) / (e0 + e1 + e2)).astype(BF16)

    return _pcall(
        body, name=name, grid=(S // T,), in_specs=[_rows(T, GW)] * 6, out_specs=_rows(T, GW),
        out_shape=jax.ShapeDtypeStruct((S, GW), BF16), compiler_params=_cparams("parallel"),
    )(*o, *lse)


def combine_bwd(d_attn, o, lse, *, name):
    S, GW = d_attn.shape
    T = ROW_TILE

    def body(da_r, o0, o1, o2, l0, l1, l2, d0, d1, d2, c0, c1, c2):
        m = jnp.maximum(jnp.maximum(l0[...], l1[...]), l2[...])
        e0, e1, e2 = jnp.exp(l0[...] - m), jnp.exp(l1[...] - m), jnp.exp(l2[...] - m)
        inv = 1.0 / (e0 + e1 + e2)
        w = (e0 * inv, e1 * inv, e2 * inv)
        da = da_r[...]
        attn = w[0] * o0[...] + w[1] * o1[...] + w[2] * o2[...]
        prod = da * attn
        for hh in range(HEADS_PER_GROUP):
            cols = slice(hh * HEAD_DIM, (hh + 1) * HEAD_DIM)
            a_h = jnp.sum(prod[:, cols], axis=-1, keepdims=True)
            for w_g, d_o, c_o in zip(w, (d0, d1, d2), (c0, c1, c2)):
                d_o[:, cols] = w_g[:, cols] * da[:, cols]
                c_o[:, cols] = -w_g[:, cols] * a_h

    shp = jax.ShapeDtypeStruct((S, GW), F32)
    return _pcall(
        body, name=name, grid=(S // T,), in_specs=[_rows(T, GW)] * 7, out_specs=(_rows(T, GW),) * 6,
        out_shape=(shp,) * 6, compiler_params=_cparams("parallel"),
    )(d_attn, *o, *lse)


def _halo_prev(T, H, C, col):
    k = T // H
    return pl.BlockSpec((H, C), lambda i: (jnp.maximum(i * k - 1, 0), col))


def _halo_next(T, H, C, col, n_rows):
    k = T // H
    last = n_rows // H - 1
    return pl.BlockSpec((H, C), lambda i: (jnp.minimum((i + 1) * k, last), col))


CONV_RC = 64
SUBLANES = 8


def _tap_groups(offs):
    groups = {}
    for k, off in offs:
        groups.setdefault(off % SUBLANES, []).append((k, off))
    return [taps for _, taps in sorted(groups.items())]


def _for_taps(src, r0, lanes, offs, fn):
    for taps in _tap_groups(offs):
        lo = min(off for _, off in taps)
        hi = max(off for _, off in taps)
        sb = src[r0 + lo:r0 + hi + CONV_RC, lanes]
        for k, off in taps:
            fn(k, sb[off - lo:off - lo + CONV_RC])


def _dwconv(src, w_r, offs, T, C, bias_r, dst):
    for rc in range(T // CONV_RC):
        for cc in range(C // LANE):
            lanes = slice(cc * LANE, (cc + 1) * LANE)
            r0 = rc * CONV_RC
            acc = None if bias_r is None else jnp.zeros((CONV_RC, LANE), F32) + bias_r[:, lanes]
            for taps in _tap_groups(offs):
                lo = min(off for _, off in taps)
                hi = max(off for _, off in taps)
                sb = src[r0 + lo:r0 + hi + CONV_RC, lanes]
                g_acc = None
                for k, off in taps:
                    term = w_r[k:k + 1, lanes] * sb[off - lo:off - lo + CONV_RC]
                    g_acc = term if g_acc is None else g_acc + term
                acc = g_acc if acc is None else acc + g_acc
            dst[r0:r0 + CONV_RC, lanes] = acc


def _fill_glu(cv, cg, hv, hg, ubuf, i):
    T = cv.shape[0]
    live = jnp.where(i > 0, 1.0, 0.0)
    ubuf[0:CONV_HALO, :] = live * (hv[...] * _sig(hg[...]))
    ubuf[CONV_HALO:CONV_HALO + T, :] = cv[...] * _sig(cg[...])


_CONV_FWD_OFFS = [(k, CONV_HALO - (CONV_K - 1) + k) for k in range(CONV_K)]
_CONV_BWD_OFFS = [(k, CONV_K - 1 - k) for k in range(CONV_K)]


def convb_fwd(zc, w, b, g_ln, b_ln, *, name):
    S = zc.shape[0]
    C = zc.shape[1] // 2
    T = ROW_TILE

    def body(cv, cg, hv, hg, w_r, b_r, g_r, bl_r, u_o, y_o, ubuf):
        _fill_glu(cv, cg, hv, hg, ubuf, pl.program_id(0))
        _dwconv(ubuf, w_r, _CONV_FWD_OFFS, T, C, b_r, y_o)
        y = y_o[...]
        mu = jnp.mean(y, axis=-1, keepdims=True)
        yc = y - mu
        rs = lax.rsqrt(jnp.mean(yc * yc, axis=-1, keepdims=True) + EPS)
        v = yc * rs * g_r[...] + bl_r[...]
        u_o[...] = (v * _sig(v)).astype(BF16)

    vec = _full((1, C))
    return _pcall(
        body, name=name, grid=(S // T,),
        in_specs=[_rows(T, C, 0), _rows(T, C, 1), _halo_prev(T, CONV_HALO, C, 0), _halo_prev(T, CONV_HALO, C, 1),
                  _full((CONV_K, C)), vec, vec, vec],
        out_specs=(_rows(T, C), _rows(T, C)),
        out_shape=(jax.ShapeDtypeStruct((S, C), BF16), jax.ShapeDtypeStruct((S, C), F32)),
        scratch_shapes=[pltpu.VMEM((CONV_HALO + T, C), F32)],
        compiler_params=_cparams("parallel"),
    )(zc, zc, zc, zc, w, b, g_ln, b_ln)


def convb_bwd1(d_u2, y_conv, zc, g_ln, b_ln, *, name):
    S = zc.shape[0]
    C = zc.shape[1] // 2
    T = ROW_TILE

    def body(du_r, y_r, cv, cg, hv, hg, g_r, bl_r, dy_o, dw_o, db_o, dg_o, dbl_o, ubuf):
        i = pl.program_id(0)
        _fill_glu(cv, cg, hv, hg, ubuf, i)
        y = y_r[...]
        mu = jnp.mean(y, axis=-1, keepdims=True)
        yc = y - mu
        rs = lax.rsqrt(jnp.mean(yc * yc, axis=-1, keepdims=True) + EPS)
        yn = yc * rs
        v = yn * g_r[...] + bl_r[...]
        sg = _sig(v)
        dv = du_r[...] * (sg * (1.0 + v * (1.0 - sg)))
        dyn = dv * g_r[...]
        dy = rs * (dyn - jnp.mean(dyn, axis=-1, keepdims=True) - yn * jnp.mean(dyn * yn, axis=-1, keepdims=True))
        dy_o[...] = dy
        _acc_rows(dg_o, jnp.sum(dv * yn, axis=0, keepdims=True), i)
        _acc_rows(dbl_o, jnp.sum(dv, axis=0, keepdims=True), i)
        _acc_rows(db_o, jnp.sum(dy, axis=0, keepdims=True), i)

        @pl.when(i == 0)
        def _():
            dw_o[...] = jnp.zeros_like(dw_o)
        for cc in range(C // LANE):
            lanes = slice(cc * LANE, (cc + 1) * LANE)
            parts = [jnp.zeros((SUBLANES, LANE), F32) for _ in range(CONV_K)]
            for rc in range(T // CONV_RC):
                r0 = rc * CONV_RC
                dyc = dy_o[r0:r0 + CONV_RC, lanes]

                def tap(k, chunk, parts=parts, dyc=dyc):
                    prod = (dyc * chunk).reshape(CONV_RC // SUBLANES, SUBLANES, LANE)
                    parts[k] = parts[k] + jnp.sum(prod, axis=0)

                _for_taps(ubuf, r0, lanes, _CONV_FWD_OFFS, tap)
            for k in range(CONV_K):
                dw_o[k * SUBLANES:(k + 1) * SUBLANES, lanes] += parts[k]

    vec = _full((1, C))
    part = jax.ShapeDtypeStruct((8, C), F32)
    return _pcall(
        body, name=name, grid=(S // T,),
        in_specs=[_rows(T, C), _rows(T, C), _rows(T, C, 0), _rows(T, C, 1), _halo_prev(T, CONV_HALO, C, 0),
                  _halo_prev(T, CONV_HALO, C, 1), vec, vec],
        out_specs=(_rows(T, C), _full((CONV_K * SUBLANES, C)), _full((8, C)), _full((8, C)), _full((8, C))),
        out_shape=(jax.ShapeDtypeStruct((S, C), F32), jax.ShapeDtypeStruct((CONV_K * SUBLANES, C), F32),
                   part, part, part),
        scratch_shapes=[pltpu.VMEM((CONV_HALO + T, C), F32)],
        compiler_params=_cparams("arbitrary"),
    )(d_u2, y_conv, zc, zc, zc, zc, g_ln, b_ln)


def convb_bwd2(dy, zc, w, *, name):
    S = zc.shape[0]
    C = zc.shape[1] // 2
    T = ROW_TILE
    nblk = S // T

    def body(dy_r, dyn_r, cv, cg, w_r, dz_o, dbuf, dubuf):
        i = pl.program_id(0)
        live = jnp.where(i < nblk - 1, 1.0, 0.0)
        dbuf[0:T, :] = dy_r[...]
        dbuf[T:T + CONV_HALO, :] = live * dyn_r[...]
        _dwconv(dbuf, w_r, _CONV_BWD_OFFS, T, C, None, dubuf)
        du = dubuf[...]
        sg = _sig(cg[...])
        dz_o[:, 0:C] = (du * sg).astype(BF16)
        dz_o[:, C:2 * C] = (du * cv[...] * sg * (1.0 - sg)).astype(BF16)

    return _pcall(
        body, name=name, grid=(nblk,),
        in_specs=[_rows(T, C), _halo_next(T, CONV_HALO, C, 0, S), _rows(T, C, 0), _rows(T, C, 1), _full((CONV_K, C))],
        out_specs=_rows(T, 2 * C), out_shape=jax.ShapeDtypeStruct((S, 2 * C), BF16),
        scratch_shapes=[pltpu.VMEM((T + CONV_HALO, C), F32), pltpu.VMEM((T, C), F32)],
        compiler_params=_cparams("parallel"),
    )(dy, dy, zc, zc, w)


def merge_fwd(y_a, y_b, zg, *, name):
    S, D = y_a.shape
    T = ROW_TILE

    def body(a_r, b_r, ga_r, gb_r, m_o):
        ga, gb = ga_r[...].astype(F32), gb_r[...].astype(F32)
        m_o[...] = (_sig(ga) * a_r[...].astype(F32) + _sig(gb) * b_r[...].astype(F32)).astype(BF16)

    return _pcall(
        body, name=name, grid=(S // T,),
        in_specs=[_rows(T, D), _rows(T, D), _rows(T, D, 0), _rows(T, D, 1)],
        out_specs=_rows(T, D), out_shape=jax.ShapeDtypeStruct((S, D), BF16),
        compiler_params=_cparams("parallel"),
    )(y_a, y_b, zg, zg)


def merge_bwd(d_m, y_a, y_b, zg, *, name):
    S, D = y_a.shape
    T = ROW_TILE

    def body(dm_r, a_r, b_r, ga_r, gb_r, da_o, db_o, dz_o):
        dm = dm_r[...]
        sa, sb = _sig(ga_r[...].astype(F32)), _sig(gb_r[...].astype(F32))
        da_o[...] = (dm * sa).astype(BF16)
        db_o[...] = (dm * sb).astype(BF16)
        dz_o[:, 0:D] = (dm * a_r[...].astype(F32) * sa * (1.0 - sa)).astype(BF16)
        dz_o[:, D:2 * D] = (dm * b_r[...].astype(F32) * sb * (1.0 - sb)).astype(BF16)

    shp = jax.ShapeDtypeStruct((S, D), BF16)
    return _pcall(
        body, name=name, grid=(S // T,),
        in_specs=[_rows(T, D), _rows(T, D), _rows(T, D), _rows(T, D, 0), _rows(T, D, 1)],
        out_specs=(_rows(T, D), _rows(T, D), _rows(T, 2 * D)),
        out_shape=(shp, shp, jax.ShapeDtypeStruct((S, 2 * D), BF16)),
        compiler_params=_cparams("parallel"),
    )(d_m, y_a, y_b, zg, zg)


FFN_RC = 64


def _ffn_chunks(T, F):
    for cc in range(F // LANE):
        for rc in range(T // FFN_RC):
            yield rc * FFN_RC, slice(rc * FFN_RC, (rc + 1) * FFN_RC), slice(cc * LANE, (cc + 1) * LANE)


def _ffn_fill(g_r, hg_r, gbuf, i):
    T = g_r.shape[0]
    live = jnp.where(i > 0, 1.0, 0.0)
    gbuf[0:FFN_HALO, :] = live * hg_r[...].astype(F32)
    gbuf[FFN_HALO:FFN_HALO + T, :] = g_r[...].astype(F32)


def _ffn_gate_chunk(gbuf, w_r, b_r, r0, lanes, n=FFN_RC):
    taps = [gbuf[r0 + FFN_HALO - (FFN_K - 1) + k:r0 + FFN_HALO - (FFN_K - 1) + k + n, lanes]
            for k in range(FFN_K)]
    gp = b_r[:, lanes] + w_r[0:1, lanes] * taps[0]
    for k in range(1, FFN_K):
        gp = gp + w_r[k:k + 1, lanes] * taps[k]
    return gp, taps


def _sum8(v):
    return jnp.sum(v.reshape(v.shape[0] // SUBLANES, SUBLANES, v.shape[1]), axis=0)


def ffn_act_fwd(gu, w, b, *, name):
    S = gu.shape[0]
    F = gu.shape[1] // 2
    T = ROW_TILE // 2

    def body(g_r, u_r, hg_r, w_r, b_r, a_o, gbuf):
        _ffn_fill(g_r, hg_r, gbuf, pl.program_id(0))
        for r0, rows, lanes in _ffn_chunks(T, F):
            gp, _ = _ffn_gate_chunk(gbuf, w_r, b_r, r0, lanes)
            a_o[rows, lanes] = (gp * _sig(gp) * u_r[rows, lanes].astype(F32)).astype(BF16)

    return _pcall(
        body, name=name, grid=(S // T,),
        in_specs=[_rows(T, F, 0), _rows(T, F, 1), _halo_prev(T, FFN_HALO, F, 0), _full((FFN_K, F)), _full((1, F))],
        out_specs=_rows(T, F), out_shape=jax.ShapeDtypeStruct((S, F), BF16),
        scratch_shapes=[pltpu.VMEM((FFN_HALO + T, F), F32)],
        compiler_params=_cparams("parallel"),
    )(gu, gu, gu, w, b)


def ffn_act_bwd(d_a, gu, w, b, *, name):
    S = gu.shape[0]
    F = gu.shape[1] // 2
    T = ROW_TILE // 2
    H = FFN_HALO
    nblk = S // T

    def dgp_of(gp, da, u):
        sg = _sig(gp)
        return da * u * (sg * (1.0 + gp * (1.0 - sg))), sg

    def body(da_r, dan_r, g_r, gp_r, gn_r, u_r, un_r, w_r, b_r, o_o, dw_o, db_o, gbuf, dbuf):
        i = pl.program_id(0)
        gbuf[0:H, :] = jnp.where(i > 0, 1.0, 0.0) * gp_r[...].astype(F32)
        gbuf[H:H + T, :] = g_r[...].astype(F32)
        gbuf[H + T:H + T + H, :] = gn_r[...].astype(F32)

        @pl.when(i == 0)
        def _():
            dw_o[...] = jnp.zeros_like(dw_o)
            db_o[...] = jnp.zeros_like(db_o)

        live_n = jnp.where(i < nblk - 1, 1.0, 0.0)
        sums = None
        for r0, rows, lanes in _ffn_chunks(T, F):
            if r0 == 0:
                sums = [jnp.zeros((SUBLANES, LANE), F32) for _ in range(FFN_K + 1)]
                gp_h, _ = _ffn_gate_chunk(gbuf, w_r, b_r, T, lanes, n=H)
                dgp_h, _ = dgp_of(gp_h, live_n * dan_r[:, lanes].astype(F32), un_r[:, lanes].astype(F32))
                dbuf[T:T + H, lanes] = dgp_h
            gp, taps = _ffn_gate_chunk(gbuf, w_r, b_r, r0, lanes)
            da = da_r[rows, lanes].astype(F32)
            dgp, sg = dgp_of(gp, da, u_r[rows, lanes].astype(F32))
            o_o[rows, F + lanes.start:F + lanes.stop] = (da * gp * sg).astype(BF16)
            dbuf[rows, lanes] = dgp
            sums[FFN_K] = sums[FFN_K] + _sum8(dgp)
            for k in range(FFN_K):
                sums[k] = sums[k] + _sum8(dgp * taps[k])
            if r0 + FFN_RC == T:
                db_o[:, lanes] += sums[FFN_K]
                for k in range(FFN_K):
                    dw_o[k * SUBLANES:(k + 1) * SUBLANES, lanes] += sums[k]
        for r0, rows, lanes in _ffn_chunks(T, F):
            dg = w_r[0:1, lanes] * dbuf[r0 + FFN_K - 1:r0 + FFN_K - 1 + FFN_RC, lanes]
            for k in range(1, FFN_K):
                off = r0 + FFN_K - 1 - k
                dg = dg + w_r[k:k + 1, lanes] * dbuf[off:off + FFN_RC, lanes]
            o_o[rows, lanes] = dg.astype(BF16)

    return _pcall(
        body, name=name, grid=(nblk,),
        in_specs=[_rows(T, F), _halo_next(T, H, F, 0, S), _rows(T, F, 0), _halo_prev(T, H, F, 0),
                  _halo_next(T, H, F, 0, S), _rows(T, F, 1), _halo_next(T, H, F, 1, S),
                  _full((FFN_K, F)), _full((1, F))],
        out_specs=(_rows(T, 2 * F), _full((FFN_K * SUBLANES, F)), _full((SUBLANES, F))),
        out_shape=(jax.ShapeDtypeStruct((S, 2 * F), BF16),
                   jax.ShapeDtypeStruct((FFN_K * SUBLANES, F), F32), jax.ShapeDtypeStruct((SUBLANES, F), F32)),
        scratch_shapes=[pltpu.VMEM((H + T + H, F), F32), pltpu.VMEM((T + H, F), F32)],
        compiler_params=_cparams("arbitrary"),
    )(d_a, d_a, gu, gu, gu, gu, gu, w, b)


def _row_tile(R, target=512):
    if R <= target:
        return R
    for t in range(target, 7, -8):
        if R % t == 0:
            return t
    return R


def sum_slots(land, *, name):
    _, R, C = land.shape
    T = _row_tile(R)

    def body(l_r, o_o):
        acc = l_r[0].astype(F32)
        for q in range(1, N_DEV):
            acc = acc + l_r[q].astype(F32)
        o_o[...] = acc

    return _pcall(
        body, name=name, grid=(R // T,),
        in_specs=[pl.BlockSpec((N_DEV, T, C), lambda i: (0, i, 0))],
        out_specs=_rows(T, C), out_shape=jax.ShapeDtypeStruct((R, C), F32),
        compiler_params=_cparams("parallel"),
    )(land)


def adamw(w, g, m, v, *, name):
    shape = w.shape
    C = shape[-1]
    R = math.prod(shape[:-1])
    w2, g2, m2, v2 = (t.reshape(R, C) for t in (w, g, m, v))
    T = _row_tile(R)
    c1 = 1.0 - ADAM_B1 ** ADAM_STEP
    c2 = 1.0 - ADAM_B2 ** ADAM_STEP

    def body(w_r, g_r, m_r, v_r, d_o, m_o, v_o):
        gv = g_r[...]
        mn = ADAM_B1 * m_r[...] + (1.0 - ADAM_B1) * gv
        vn = ADAM_B2 * v_r[...] + (1.0 - ADAM_B2) * (gv * gv)
        m_o[...] = mn
        v_o[...] = vn
        d_o[...] = -ADAM_LR * ((mn / c1) / (jnp.sqrt(vn / c2) + ADAM_EPS) + ADAM_WD * w_r[...])

    shp = jax.ShapeDtypeStruct((R, C), F32)
    d, mn, vn = _pcall(
        body, name=name, grid=(R // T,), in_specs=[_rows(T, C)] * 4, out_specs=(_rows(T, C),) * 3,
        out_shape=(shp,) * 3, compiler_params=_cparams("parallel"),
    )(w2, g2, m2, v2)
    return d.reshape(shape), mn.reshape(shape), vn.reshape(shape)


def _my_pos():
    return lax.axis_index("x"), lax.axis_index("y"), lax.axis_index("c")


class _Exchange:
    def __init__(self, inputs, out_shapes, sems, start, finish):
        self.inputs, self.out_shapes, self.sems, self.start, self.finish = inputs, out_shapes, sems, start, finish


def gather_exchange(shards):
    n = len(shards)

    def plan(ins, outs, sems):
        send_sems, recv_sems, local_sems = sems
        x, y, c = _my_pos()
        me, sibling = (x, y, c), (x, y, 1 - c)
        chips = [(1 - x, y), (x, 1 - y), (1 - x, 1 - y)]

        def slot(i, p):
            return outs[i].at[4 * p[0] + 2 * p[1] + p[2]]

        def copy(k, i, block, to, src=None):
            return pltpu.make_async_remote_copy(
                src_ref=slot(i, block) if src is None else src, dst_ref=slot(i, block),
                send_sem=send_sems.at[k, i], recv_sem=recv_sems.at[k, i], device_id=to, device_id_type=MESH)

        mine = [pltpu.make_async_copy(ins[i], slot(i, me), local_sems.at[i]) for i in range(n)]
        first = []
        for i in range(n):
            first.append(copy(0, i, me, sibling, src=ins[i]))
            first += [copy(1 + j, i, me, (*chip, c), src=ins[i]) for j, chip in enumerate(chips)]
        return me, sibling, chips, c, copy, mine, first

    def start(ins, outs, sems):
        _, _, _, _, _, mine, first = plan(ins, outs, sems)
        for cp in mine + first:
            cp.start()

    def finish(ins, outs, sems):
        me, sibling, chips, c, copy, mine, first = plan(ins, outs, sems)
        passed = []
        for j, chip in enumerate(chips):
            for i in range(n):
                copy(1 + j, i, (*chip, c), me).wait_recv()
                cp = copy(4 + j, i, (*chip, c), sibling)
                cp.start()
                passed.append(cp)
        for i in range(n):
            copy(0, i, sibling, me).wait_recv()
            for j, chip in enumerate(chips):
                copy(4 + j, i, (*chip, 1 - c), me).wait_recv()
        for cp in first + passed:
            cp.wait_send()
        for cp in mine:
            cp.wait()

    outs = [jax.ShapeDtypeStruct((N_DEV,) + s.shape, s.dtype) for s in shards]
    sems = [pltpu.SemaphoreType.DMA((7, n)), pltpu.SemaphoreType.DMA((7, n)), pltpu.SemaphoreType.DMA((n,))]
    return _Exchange(list(shards), outs, sems, start, finish)


def scatter_exchange(gs):
    n = len(gs)

    def plan(ins, outs, sems):
        send_sems, recv_sems, local_sems = sems
        x, y, c = _my_pos()
        me_id = 4 * x + 2 * y + c
        mine = [pltpu.make_async_copy(ins[i].at[me_id], outs[i].at[me_id], local_sems.at[i]) for i in range(n)]
        sends, recvs = [], []
        for msk in range(1, N_DEV):
            px = 1 - x if msk & 4 else x
            py = 1 - y if msk & 2 else y
            pc = 1 - c if msk & 1 else c
            pid = 4 * px + 2 * py + pc
            for i in range(n):
                sends.append(pltpu.make_async_remote_copy(
                    src_ref=ins[i].at[pid], dst_ref=outs[i].at[me_id],
                    send_sem=send_sems.at[msk - 1, i], recv_sem=recv_sems.at[msk - 1, i],
                    device_id=(px, py, pc), device_id_type=MESH))
                recvs.append(pltpu.make_async_remote_copy(
                    src_ref=ins[i].at[pid], dst_ref=outs[i].at[pid],
                    send_sem=send_sems.at[msk - 1, i], recv_sem=recv_sems.at[msk - 1, i],
                    device_id=(px, py, pc), device_id_type=MESH))
        return mine, sends, recvs

    def start(ins, outs, sems):
        mine, sends, _ = plan(ins, outs, sems)
        for cp in mine + sends:
            cp.start()

    def finish(ins, outs, sems):
        mine, sends, recvs = plan(ins, outs, sems)
        for rv in recvs:
            rv.wait_recv()
        for cp in sends:
            cp.wait_send()
        for cp in mine:
            cp.wait()

    outs = [jax.ShapeDtypeStruct(g.shape, g.dtype) for g in gs]
    sems = [pltpu.SemaphoreType.DMA((7, n)), pltpu.SemaphoreType.DMA((7, n)), pltpu.SemaphoreType.DMA((n,))]
    return _Exchange(list(gs), outs, sems, start, finish)


def run_exchange(ex, *, name):
    HBM = pl.BlockSpec(memory_space=pl.ANY)
    n_in, n_out = len(ex.inputs), len(ex.out_shapes)

    def body(*refs):
        ins, outs, sems = refs[:n_in], refs[n_in:n_in + n_out], refs[n_in + n_out:]
        ex.start(ins, outs, sems)
        ex.finish(ins, outs, sems)

    return list(_pcall(body, name=name, in_specs=[HBM] * n_in, out_specs=tuple([HBM] * n_out),
                       out_shape=tuple(ex.out_shapes), scratch_shapes=ex.sems)(*ex.inputs))


def all_gather(xs, *, name):
    outs = run_exchange(gather_exchange([xs[l] for l in range(xs.shape[0])]), name=name)
    return jnp.stack(outs)


def _slots(g):
    return g.reshape(N_DEV, g.shape[0] // N_DEV, g.shape[1])


def _rope_tables(positions):
    half = ROT_DIM // 2
    inv_freq = ROPE_THETA ** (-jnp.arange(0, ROT_DIM, 2, dtype=F32) / ROT_DIM)
    ang = positions.astype(F32)[:, None] * inv_freq
    cos, sin = jnp.cos(ang), jnp.sin(ang)
    S = positions.shape[0]
    c_t = jnp.concatenate([cos, cos, jnp.ones((S, LANE - ROT_DIM), F32)], axis=1)
    s1_t = jnp.concatenate([-sin, jnp.zeros((S, LANE - half), F32)], axis=1)
    s2_t = jnp.concatenate([jnp.zeros((S, half), F32), sin, jnp.zeros((S, LANE - ROT_DIM), F32)], axis=1)
    return c_t, s1_t, s2_t


def _row(v):
    return v.reshape(1, -1)


def kernel(x, c, positions, w_ada, b_ada, g_norm1, w_in, g_q, g_k, w_attn_proj, w_conv_dw, b_conv_dw, g_conv_ln, b_conv_ln, w_conv_out, w_o, g_norm2, w_ffn_in, w_ffn_dw, b_ffn_dw, w_ffn_down, loss_target, m_w_ada, m_b_ada, m_g_norm1, m_w_in, m_g_q, m_g_k, m_w_attn_proj, m_w_conv_dw, m_b_conv_dw, m_g_conv_ln, m_b_conv_ln, m_w_conv_out, m_w_o, m_g_norm2, m_w_ffn_in, m_w_ffn_dw, m_b_ffn_dw, m_w_ffn_down, v_w_ada, v_b_ada, v_g_norm1, v_w_in, v_g_q, v_g_k, v_w_attn_proj, v_w_conv_dw, v_b_conv_dw, v_g_conv_ln, v_b_conv_ln, v_w_conv_out, v_w_o, v_g_norm2, v_w_ffn_in, v_w_ffn_dw, v_b_ffn_dw, v_w_ffn_down):
    L = w_in.shape[0]
    S, D = x.shape[1], x.shape[2]
    FF = w_ffn_down.shape[1] * N_DEV
    xi, yi, ci = _my_pos()
    me = 4 * xi + 2 * yi + ci
    x0 = x[0]
    tabs = _rope_tables(positions[0])

    c_act = c * _sig(c)
    c_all = all_gather(jnp.pad(c_act, ((0, 7), (0, 0)))[None], name="ag_c")[0][:, 0, :]
    c_all16 = jnp.pad(c_all, ((0, 8), (0, 0)))
    m_part = jnp.stack([mm(c_all16, w_ada[l], "nn", name="mod_mm") for l in range(L)])
    m_all = all_gather(m_part, name="ag_mod")
    mod = lax.dynamic_index_in_dim(m_all, me, axis=2, keepdims=False).reshape(L, 6 * D) + b_ada
    mod = mod.reshape(L, 6, 1, D)

    sh_in = jnp.transpose(w_in, (0, 2, 1)).astype(BF16)
    sh_fi = jnp.transpose(w_ffn_in, (0, 2, 1)).astype(BF16)
    sh_ap = jnp.transpose(w_attn_proj, (0, 2, 1)).astype(BF16)
    sh_co, sh_oo, sh_dn = w_conv_out.astype(BF16), w_o.astype(BF16), w_ffn_down.astype(BF16)

    def rowcat(g):
        return g.reshape(N_DEV * g.shape[1], g.shape[2])

    wt_in, wt_fi, wt_ap, w_co, w_oo, w_dn = ([None] * L for _ in range(6))
    wt_in[0] = rowcat(run_exchange(gather_exchange([sh_in[0]]), name="ag_w0")[0])
    cdw = all_gather(jnp.pad(w_conv_dw, ((0, 0), (0, 1), (0, 0))), name="ag_cdw")
    cdw = jnp.transpose(cdw, (0, 2, 1, 3)).reshape(L, 32, D)[:, :CONV_K]
    fsh = w_ffn_dw.shape[2]
    fpad = -fsh % LANE
    fdw = all_gather(jnp.pad(w_ffn_dw, ((0, 0), (0, 8 - FFN_K), (0, fpad))), name="ag_fdw")
    fdw = jnp.transpose(fdw[:, :, :FFN_K, :fsh], (0, 2, 1, 3)).reshape(L, FFN_K, FF)

    QKV, CW = 3 * ATTN_W, 2 * D
    seg = ((0, QKV), (QKV, CW), (QKV + CW, 2 * D))

    saved = []
    x_prev, delta, gt_prev = x0, None, None
    for l in range(L):
        sh1, sc1, gt1, sh2, sc2, gt2 = (mod[l, i] for i in range(6))
        x_l, h = norm_mod_fwd(x_prev, delta, gt_prev, _row(g_norm1[l]), sc1, sh1, name="norm_fwd")
        nxt = l + 1 < L
        if l == 0:
            zq, got = mm(h, wt_in[0], "nt", b_off=0, b_len=QKV, name="z_mm_ag0",
                         comm=gather_exchange([sh_fi[0], sh_ap[0], sh_co[0], sh_oo[0], sh_dn[0]]))
            wt_fi[0], wt_ap[0], w_co[0], w_oo[0], w_dn[0] = (rowcat(g) for g in got)
        elif nxt:
            zq, got = mm(h, wt_in[l], "nt", b_off=0, b_len=QKV, name="z_mm_ag", comm=gather_exchange([sh_in[l + 1]]))
            wt_in[l + 1] = rowcat(got[0])
        else:
            zq = mm(h, wt_in[l], "nt", b_off=0, b_len=QKV, name="z_mm")
        if l == 0 and nxt:
            zc, got = mm(h, wt_in[0], "nt", b_off=seg[1][0], b_len=seg[1][1], name="zc_mm_ag",
                         comm=gather_exchange([sh_in[1]]))
            wt_in[1] = rowcat(got[0])
        else:
            zc = mm(h, wt_in[l], "nt", b_off=seg[1][0], b_len=seg[1][1], name="z_mm")
        zg = mm(h, wt_in[l], "nt", b_off=seg[2][0], b_len=seg[2][1], out_dtype=BF16, name="zg_mm")
        gq, gk = _row(g_q[l]), _row(g_k[l])
        qn, kn = qk_prep_fwd(zq, gq, gk, tabs, name="qk_prep")
        o_g, lse_g = [], []
        for gi in range(3):
            o_i, l_i = attn_fwd(qn, kn, zq, gi, name="attn_fwd%d" % gi)
            o_g.append(o_i)
            lse_g.append(l_i)
        attn = combine_fwd(o_g, lse_g, name="combine_fwd")
        y_a = mm(attn, wt_ap[l], "nt", out_dtype=BF16, name="ya_mm")
        cw, cb = cdw[l], _row(b_conv_dw[l])
        cg, cbl = _row(g_conv_ln[l]), _row(b_conv_ln[l])
        u2, y_conv = convb_fwd(zc, cw, cb, cg, cbl, name="convb_fwd")
        y_b = mm(u2, w_co[l], "nn", out_dtype=BF16, name="yb_mm")
        merged = merge_fwd(y_a, y_b, zg, name="merge_fwd")
        mo = mm(merged, w_oo[l], "nn", name="mo_mm")
        x_mid, h2 = norm_mod_fwd(x_l, mo, gt1, _row(g_norm2[l]), sc2, sh2, name="norm_fwd")
        if nxt:
            gu, got = mm(h2, wt_fi[l], "nt", out_dtype=BF16, name="gu_mm_ag", comm=gather_exchange([sh_fi[l + 1]]))
            wt_fi[l + 1] = rowcat(got[0])
        else:
            gu = mm(h2, wt_fi[l], "nt", out_dtype=BF16, name="gu_mm")
        fw, fb = fdw[l], _row(b_ffn_dw[l])
        act = ffn_act_fwd(gu, fw, fb, name="ffn_act")
        if nxt:
            ffo, got = mm(act, w_dn[l], "nn", name="ffo_mm_ag",
                          comm=gather_exchange([sh_ap[l + 1], sh_co[l + 1], sh_oo[l + 1], sh_dn[l + 1]]))
            wt_ap[l + 1], w_co[l + 1], w_oo[l + 1], w_dn[l + 1] = (rowcat(g) for g in got)
        else:
            ffo = mm(act, w_dn[l], "nn", name="ffo_mm")
        saved.append(dict(x=x_l, h=h, zq=zq, zc=zc, zg=zg, qn=qn, kn=kn, o=o_g, lse=lse_g, attn=attn, y_a=y_a,
                          u2=u2, y_conv=y_conv, y_b=y_b, merged=merged, mo=mo, x_mid=x_mid, h2=h2, gu=gu, act=act, ffo=ffo))
        x_prev, delta, gt_prev = x_mid, ffo, gt2

    dx, lpart, d_ffo, p_gt2 = loss_head(x_prev, delta, gt_prev, loss_target[0], name="loss_head")
    loss = lax.psum(0.5 / D * jnp.sum(lpart[0]), ("x", "y", "c"))

    land = {k: [None] * L for k in ("in", "fi", "ap", "co", "o", "dn")}
    small_rows = []
    for l in reversed(range(L)):
        sv = saved[l]
        sh1, sc1, gt1, sh2, sc2, gt2 = (mod[l, i] for i in range(6))
        d_act = mm(d_ffo, w_dn[l], "nt", out_dtype=BF16, name="dact_mm")
        g_dn = mm(sv["act"], d_ffo, "tn", out_dtype=BF16, name="dwdn_mm")
        fw, fb = fdw[l], _row(b_ffn_dw[l])
        dgu, p_fw, p_fb = ffn_act_bwd(d_act, sv["gu"], fw, fb, name="ffn_bwd")
        dh2, got = mm(dgu, wt_fi[l], "nn", name="dh2_mm_rs", comm=scatter_exchange([_slots(g_dn)]))
        land["dn"][l] = got[0]
        g_fi = mm(dgu, sv["h2"], "tn", out_dtype=BF16, name="dwfi_mm")
        dx, p_g2, p_sc2, p_sh2, d_mo, p_gt1 = norm_mod_bwd(sv["x_mid"], dh2, _row(g_norm2[l]), sc2, sh2, dx,
                                                           (sv["mo"], gt1), name="norm_bwd")
        d_merged = mm(d_mo, w_oo[l], "nt", name="dmerged_mm")
        g_o = mm(sv["merged"], d_mo, "tn", out_dtype=BF16, name="dwo_mm")
        d_ya, d_yb, dzg = merge_bwd(d_merged, sv["y_a"], sv["y_b"], sv["zg"], name="merge_bwd")
        d_attn = mm(d_ya, wt_ap[l], "nn", name="dattn_mm")
        g_ap = mm(d_ya, sv["attn"], "tn", out_dtype=BF16, name="dwap_mm")
        d_u2 = mm(d_yb, w_co[l], "nt", name="du2_mm")
        g_co = mm(sv["u2"], d_yb, "tn", out_dtype=BF16, name="dwco_mm")
        cw, cb = cdw[l], _row(b_conv_dw[l])
        cg, cbl = _row(g_conv_ln[l]), _row(b_conv_ln[l])
        dy, p_cw, p_cb, p_cg, p_cbl = convb_bwd1(d_u2, sv["y_conv"], sv["zc"], cg, cbl, name="convb_bwd1")
        dzc = convb_bwd2(dy, sv["zc"], cw, name="convb_bwd2")
        dd = combine_bwd(d_attn, sv["o"], sv["lse"], name="combine_bwd")
        do_g, cc_g = dd[:3], dd[3:]
        parts = [attn_bwd(sv["qn"], sv["kn"], sv["zq"], do_g[gi], sv["lse"][gi], cc_g[gi], gi,
                          name="attn_bwd%d" % gi) for gi in range(3)]
        gq, gk = _row(g_q[l]), _row(g_k[l])
        dzq, p_gq, p_gk = attn_bwd_post(sv["zq"], gq, gk, tabs, *[[p[i] for p in parts] for i in range(5)],
                                        name="attn_post")
        g_in_q, got = mm(dzq, sv["h"], "tn", out_dtype=BF16, name="dwin_mm_rs",
                         comm=scatter_exchange([_slots(g_fi), _slots(g_o), _slots(g_ap), _slots(g_co)]))
        land["fi"][l], land["o"][l], land["ap"][l], land["co"][l] = got
        g_in = jnp.concatenate([g_in_q] + [mm(dz_s, sv["h"], "tn", out_dtype=BF16, name="dwin_mm")
                                           for dz_s in (dzc, dzg)], axis=0)
        dh, got = mm(dzq, wt_in[l], "nn", b_off=0, name="dh_mm_rs", comm=scatter_exchange([_slots(g_in)]))
        land["in"][l] = got[0]
        for dz_s, (o, n) in zip((dzc, dzg), seg[1:]):
            dh = mm(dz_s, wt_in[l], "nn", b_off=o, c_in=dh, name="dh_mm")
        if l > 0:
            dx, p_g1, p_sc1, p_sh1, d_ffo_prev, p_gt2_prev = norm_mod_bwd(
                sv["x"], dh, _row(g_norm1[l]), sc1, sh1, dx, (saved[l - 1]["ffo"], mod[l - 1, 5]), name="norm_bwd")
        else:
            dx, p_g1, p_sc1, p_sh1 = norm_mod_bwd(sv["x"], dh, _row(g_norm1[l]), sc1, sh1, dx, name="norm_bwd")

        def row1k(p):
            v = p[0]
            pad = -v.shape[0] % D
            return jnp.pad(v, (0, pad)).reshape(-1, D)

        rows = [row1k(p) for p in (p_sh1, p_sc1, p_gt1, p_sh2, p_sc2, p_gt2, p_g1, p_g2)]
        rows.append(row1k(jnp.concatenate([p_gq, p_gk], axis=1)))
        rows += [row1k(p) for p in (p_cb, p_cg, p_cbl, jnp.sum(p_fb, axis=0, keepdims=True))]
        rows.append(jnp.sum(p_cw.reshape(CONV_K, SUBLANES, D), axis=1))
        rows += [row1k(jnp.sum(p_fw[k * SUBLANES:(k + 1) * SUBLANES], axis=0, keepdims=True))
                 for k in range(FFN_K)]
        blk = jnp.concatenate(rows, axis=0)
        small_rows.append(jnp.pad(blk, ((0, -blk.shape[0] % 8), (0, 0))))
        if l > 0:
            d_ffo, p_gt2 = d_ffo_prev, p_gt2_prev
    small_rows = small_rows[::-1]
    n_small = small_rows[0].shape[0]
    ff_rows = -(-FF // D)

    small = jnp.concatenate(small_rows, axis=0)[None]
    small_all = all_gather(small, name="ag_small")[0]
    small_sum = sum_slots(small_all, name="sum_small").reshape(L, n_small, D)
    small_all = small_all.reshape(N_DEV, L, n_small, D)

    g_b_ada = small_sum[:, 0:6].reshape(L, 6 * D)
    g_g1, g_g2 = small_sum[:, 6], small_sum[:, 7]
    g_gq, g_gk = small_sum[:, 8, 0:LANE], small_sum[:, 8, LANE:2 * LANE]
    g_cb, g_cg, g_cbl = small_sum[:, 9], small_sum[:, 10], small_sum[:, 11]
    r0 = 12
    g_fb = small_sum[:, r0:r0 + ff_rows].reshape(L, -1)[:, :FF]
    r0 += ff_rows
    g_cw_full = small_sum[:, r0:r0 + CONV_K]
    r0 += CONV_K
    g_fw_full = small_sum[:, r0:r0 + FFN_K * ff_rows].reshape(L, FFN_K, -1)[:, :, :FF]
    csh = w_conv_dw.shape[2]
    g_cw = lax.dynamic_slice_in_dim(g_cw_full, me * csh, csh, axis=2)
    g_fw = lax.dynamic_slice_in_dim(g_fw_full, me * fsh, fsh, axis=2)

    ash = w_ada.shape[2]
    dmod_all = small_all[:, :, 0:6].reshape(N_DEV, L, 6 * D)
    dmod_mine = lax.dynamic_slice_in_dim(dmod_all, me * ash, ash, axis=2)
    g_w_ada = jnp.stack([mm(c_all16, jnp.pad(dmod_mine[:, l], ((0, 8), (0, 0))), "tn", name="dwada_mm")
                         for l in range(L)])

    def reduced(key, transposed):
        out = jnp.stack([sum_slots(slots, name="sum_" + key) for slots in land[key]])
        return jnp.transpose(out, (0, 2, 1)) if transposed else out

    g_w_in = reduced("in", True)
    g_w_fi = reduced("fi", True)
    g_w_ap = reduced("ap", True)
    g_w_co = reduced("co", False)
    g_w_o = reduced("o", False)
    g_w_dn = reduced("dn", False)

    grads = [g_w_ada, g_b_ada, g_g1, g_w_in, g_gq, g_gk, g_w_ap, g_cw, g_cb, g_cg, g_cbl, g_w_co, g_w_o, g_g2,
             g_w_fi, g_fw, g_fb, g_w_dn]
    ws = [w_ada, b_ada, g_norm1, w_in, g_q, g_k, w_attn_proj, w_conv_dw, b_conv_dw, g_conv_ln, b_conv_ln,
          w_conv_out, w_o, g_norm2, w_ffn_in, w_ffn_dw, b_ffn_dw, w_ffn_down]
    ms = [m_w_ada, m_b_ada, m_g_norm1, m_w_in, m_g_q, m_g_k, m_w_attn_proj, m_w_conv_dw, m_b_conv_dw, m_g_conv_ln,
          m_b_conv_ln, m_w_conv_out, m_w_o, m_g_norm2, m_w_ffn_in, m_w_ffn_dw, m_b_ffn_dw, m_w_ffn_down]
    vs = [v_w_ada, v_b_ada, v_g_norm1, v_w_in, v_g_q, v_g_k, v_w_attn_proj, v_w_conv_dw, v_b_conv_dw, v_g_conv_ln,
          v_b_conv_ln, v_w_conv_out, v_w_o, v_g_norm2, v_w_ffn_in, v_w_ffn_dw, v_b_ffn_dw, v_w_ffn_down]
    deltas, new_m, new_v = [], [], []
    for w_i, g_i, m_i, v_i in zip(ws, grads, ms, vs):
        d_i, mn_i, vn_i = adamw(w_i, g_i, m_i, v_i, name="adamw")
        deltas.append(d_i)
        new_m.append(mn_i)
        new_v.append(vn_i)
    return (loss, dx[None], *grads, *deltas, *new_m, *new_v)
```

```python
import functools
import math

import jax
import jax.numpy as jnp
from jax import lax
from jax.experimental import pallas as pl
from jax.experimental.pallas import tpu as pltpu

F32 = jnp.float32
BF16 = jnp.bfloat16
MESH = pl.DeviceIdType.MESH
N_DEV = 8

EPS = 1e-6
HEAD_DIM = 128
BLOCK = 128
DILATIONS = (1, 4, 16)
HEADS_PER_GROUP = 4
N_HEADS = 12
ATTN_W = N_HEADS * HEAD_DIM
ROT_DIM = 32
ROPE_THETA = 500000.0
CONV_K = 31
CONV_HALO = 32
FFN_K = 3
FFN_HALO = 16
NEG = -1e30

ADAM_LR, ADAM_B1, ADAM_B2, ADAM_EPS, ADAM_WD, ADAM_STEP = 0.001, 0.9, 0.999, 1e-08, 0.01, 10

LANE = 128
VMEM_LIMIT = 56 * 1024 * 1024
ROW_TILE = 512
POST_TILE = 256
GLUE_RC = 64


def _pcall(body, **kw):
    return pl.pallas_call(body, **kw)


def _cparams(*sem):
    return pltpu.CompilerParams(dimension_semantics=sem, vmem_limit_bytes=VMEM_LIMIT)


def _sig(v):
    return 1.0 / (1.0 + jnp.exp(-v))


def _divtile(dim, target):
    best = None
    for t in range(LANE, min(dim, target) + 1, LANE):
        if dim % t == 0:
            best = t
    return best or dim


def _rows(t, c, col=0):
    return pl.BlockSpec((t, c), lambda i: (i, col))


def _full(shape):
    nd = len(shape)
    return pl.BlockSpec(shape, lambda i: (0,) * nd)


def _acc_rows(ref, val, i):
    @pl.when(i == 0)
    def _():
        ref[...] = jnp.zeros_like(ref)
    r = val.shape[0]
    ref[0:r, :] += val


MM_TILE = 1536


def mm(a, b, mode, *, name, out_dtype=F32, c_in=None, b_off=0, b_len=None, comm=None):
    if mode == "nn":
        M, K = a.shape
        N = b.shape[1]
    elif mode == "nt":
        M, K = a.shape
        N = b_len if b_len is not None else b.shape[0]
    else:
        K, M = a.shape
        N = b.shape[1]
    g_n = math.gcd(N, b_off) if (mode == "nt" and b_off) else N
    g_k = math.gcd(K, b_off) if (mode == "nn" and b_off) else K
    tn = _divtile(g_n, MM_TILE if c_in is None else 1024)
    tk = _divtile(g_k, MM_TILE if mode != "tn" else 2048)
    tm = _divtile(M, MM_TILE if mode == "tn" else (2048 if c_in is None else 1024))
    gm, gn, nk = M // tm, N // tn, K // tk
    n_ci = 0 if comm is None else len(comm.inputs)
    n_co = 0 if comm is None else len(comm.out_shapes)
    n_x = 2 + (c_in is not None)
    if mode == "nn":
        dims = (((1,), (0,)), ((), ()))
    elif mode == "nt":
        dims = (((1,), (1,)), ((), ()))
    else:
        dims = (((0,), (0,)), ((), ()))

    def body(*refs):
        a_ref, b_ref = refs[0], refs[1]
        c_ref = refs[2] if c_in is not None else None
        c_ins = refs[n_x:n_x + n_ci]
        o_ref = refs[n_x + n_ci]
        c_outs = refs[n_x + n_ci + 1:n_x + n_ci + 1 + n_co]
        rest = refs[n_x + n_ci + 1 + n_co:]
        acc = rest[0] if nk > 1 else None
        sems = rest[1:] if nk > 1 else rest
        i, j, k = pl.program_id(0), pl.program_id(1), pl.program_id(2)

        if comm is not None:
            @pl.when(jnp.logical_and(jnp.logical_and(i == 0, j == 0), k == 0))
            def _():
                comm.start(c_ins, c_outs, sems)

        prod = lax.dot_general(a_ref[...].astype(BF16), b_ref[...].astype(BF16), dims, preferred_element_type=F32)
        if nk == 1:
            if c_ref is not None:
                prod = prod + c_ref[...].astype(F32)
            o_ref[...] = prod.astype(out_dtype)
        else:
            @pl.when(k == 0)
            def _():
                if c_ref is None:
                    acc[...] = prod
                else:
                    acc[...] = prod + c_ref[...].astype(F32)

            @pl.when(k > 0)
            def _():
                acc[...] += prod

            @pl.when(k == nk - 1)
            def _():
                o_ref[...] = acc[...].astype(out_dtype)

        if comm is not None:
            @pl.when(jnp.logical_and(jnp.logical_and(i == gm - 1, j == gn - 1), k == nk - 1))
            def _():
                comm.finish(c_ins, c_outs, sems)

    if mode == "nn":
        a_spec = pl.BlockSpec((tm, tk), lambda i, j, k: (i, k))
        ob = b_off // tk
        b_spec = pl.BlockSpec((tk, tn), lambda i, j, k: (k + ob, j))
    elif mode == "nt":
        a_spec = pl.BlockSpec((tm, tk), lambda i, j, k: (i, k))
        ob = b_off // tn
        b_spec = pl.BlockSpec((tn, tk), lambda i, j, k: (j + ob, k))
    else:
        a_spec = pl.BlockSpec((tk, tm), lambda i, j, k: (k, i))
        b_spec = pl.BlockSpec((tk, tn), lambda i, j, k: (k, j))
    o_spec = pl.BlockSpec((tm, tn), lambda i, j, k: (i, j))
    HBM = pl.BlockSpec(memory_space=pl.ANY)
    in_specs = [a_spec, b_spec]
    args = [a, b]
    if c_in is not None:
        in_specs.append(o_spec)
        args.append(c_in)
    scratch = [pltpu.VMEM((tm, tn), F32)] if nk > 1 else []
    o_shape = jax.ShapeDtypeStruct((M, N), out_dtype)
    if comm is None:
        return _pcall(
            body, name=name, grid=(gm, gn, nk), in_specs=in_specs, out_specs=o_spec, out_shape=o_shape,
            scratch_shapes=scratch, compiler_params=_cparams("parallel", "parallel", "arbitrary"),
        )(*args)
    res = _pcall(
        body, name=name, grid=(gm, gn, nk), in_specs=in_specs + [HBM] * n_ci,
        out_specs=(o_spec, *[HBM] * n_co), out_shape=(o_shape, *comm.out_shapes),
        scratch_shapes=scratch + comm.sems, compiler_params=_cparams("arbitrary", "arbitrary", "arbitrary"),
    )(*args, *comm.inputs)
    return res[0], list(res[1:])


def norm_mod_fwd(x_prev, delta, gt, g, sc, sh, *, name):
    S, D = x_prev.shape
    T = ROW_TILE
    has_delta = delta is not None

    def body(*refs):
        if has_delta:
            xp, dl, gt_r, g_r, sc_r, sh_r, x_out, h_out = refs
            xv = xp[...] + gt_r[...] * dl[...]
            x_out[...] = xv
        else:
            xp, g_r, sc_r, sh_r, h_out = refs
            xv = xp[...]
        r = lax.rsqrt(jnp.mean(xv * xv, axis=-1, keepdims=True) + EPS)
        h_out[...] = ((xv * r) * g_r[...] * (1.0 + sc_r[...]) + sh_r[...]).astype(BF16)

    vec = _full((1, D))
    if has_delta:
        ins, specs = [x_prev, delta, gt, g, sc, sh], [_rows(T, D), _rows(T, D), vec, vec, vec, vec]
        outs = (jax.ShapeDtypeStruct((S, D), F32), jax.ShapeDtypeStruct((S, D), BF16))
        ospecs = (_rows(T, D), _rows(T, D))
    else:
        ins, specs = [x_prev, g, sc, sh], [_rows(T, D), vec, vec, vec]
        outs = jax.ShapeDtypeStruct((S, D), BF16)
        ospecs = _rows(T, D)
    res = _pcall(body, name=name, grid=(S // T,), in_specs=specs, out_specs=ospecs, out_shape=outs,
                 compiler_params=_cparams("parallel"))(*ins)
    return res if has_delta else (x_prev, res)


def norm_mod_bwd(x, dh, g, sc, sh, dx_res, res=None, *, name):
    S, D = x.shape
    T = ROW_TILE
    has_res = res is not None

    def body(*refs):
        x_r, dh_r, g_r, sc_r, sh_r, dr_r = refs[:6]
        dx_o, dg_o, dsc_o, dsh_o = refs[6 + 2 * has_res:10 + 2 * has_res]
        i = pl.program_id(0)
        xv = x_r[...]
        dh_v = dh_r[...]
        r = lax.rsqrt(jnp.mean(xv * xv, axis=-1, keepdims=True) + EPS)
        xh = xv * r
        dn = dh_v * (1.0 + sc_r[...])
        dxh = dn * g_r[...]
        dx = dr_r[...] + r * (dxh - xh * jnp.mean(dxh * xh, axis=-1, keepdims=True))
        dx_o[...] = dx
        _acc_rows(dg_o, jnp.sum(dn * xh, axis=0, keepdims=True), i)
        _acc_rows(dsc_o, jnp.sum(dh_v * (xh * g_r[...]), axis=0, keepdims=True), i)
        _acc_rows(dsh_o, jnp.sum(dh_v, axis=0, keepdims=True), i)
        if has_res:
            dl_r, gt_r = refs[6:8]
            dd_o, dgt_o = refs[12:14]
            dd_o[...] = (dx * gt_r[...]).astype(BF16)
            _acc_rows(dgt_o, jnp.sum(dx * dl_r[...], axis=0, keepdims=True), i)

    vec = _full((1, D))
    part = jax.ShapeDtypeStruct((8, D), F32)
    in_specs = [_rows(T, D), _rows(T, D), vec, vec, vec, _rows(T, D)]
    out_specs = [_rows(T, D), _full((8, D)), _full((8, D)), _full((8, D))]
    out_shape = [jax.ShapeDtypeStruct((S, D), F32), part, part, part]
    args = [x, dh, g, sc, sh, dx_res]
    if has_res:
        in_specs += [_rows(T, D), vec]
        out_specs += [_rows(T, D), _full((8, D))]
        out_shape += [jax.ShapeDtypeStruct((S, D), BF16), part]
        args += list(res)
    return _pcall(
        body, name=name, grid=(S // T,), in_specs=in_specs, out_specs=tuple(out_specs), out_shape=tuple(out_shape),
        compiler_params=_cparams("arbitrary"),
    )(*args)


def loss_head(x_mid, ffo, gt, target, *, name):
    S, D = x_mid.shape
    T = ROW_TILE

    def body(x_r, f_r, gt_r, t_r, dy_o, l_o, dd_o, dgt_o):
        i = pl.program_id(0)
        fv = f_r[...]
        e = x_r[...] + gt_r[...] * fv - t_r[...]
        dy = e * (1.0 / D)
        dy_o[...] = dy
        _acc_rows(l_o, jnp.sum(e * e, axis=0, keepdims=True), i)
        dd_o[...] = (dy * gt_r[...]).astype(BF16)
        _acc_rows(dgt_o, jnp.sum(dy * fv, axis=0, keepdims=True), i)

    part = jax.ShapeDtypeStruct((8, D), F32)
    return _pcall(
        body, name=name, grid=(S // T,),
        in_specs=[_rows(T, D), _rows(T, D), _full((1, D)), _rows(T, D)],
        out_specs=(_rows(T, D), _full((8, D)), _rows(T, D), _full((8, D))),
        out_shape=(jax.ShapeDtypeStruct((S, D), F32), part, jax.ShapeDtypeStruct((S, D), BF16), part),
        compiler_params=_cparams("arbitrary"),
    )(x_mid, ffo, gt, target)


def _rope(t, c_t, s1_t, s2_t):
    return t * c_t + pltpu.roll(t, LANE - ROT_DIM // 2, 1) * s1_t + pltpu.roll(t, ROT_DIM // 2, 1) * s2_t


def _rope_t(d, c_t, s1_t, s2_t):
    return d * c_t + pltpu.roll(d * s1_t, ROT_DIM // 2, 1) + pltpu.roll(d * s2_t, LANE - ROT_DIM // 2, 1)


def qk_prep_fwd(zq, g_q, g_k, tabs, *, name):
    S = zq.shape[0]
    T = ROW_TILE

    def body(q_r, k_r, gq_r, gk_r, c_r, s1_r, s2_r, qn_o, kn_o):
        for r0 in range(0, T, GLUE_RC):
            rows = slice(r0, r0 + GLUE_RC)
            c_t, s1_t, s2_t = c_r[rows, :], s1_r[rows, :], s2_r[rows, :]
            for src, g_r, dst in ((q_r, gq_r, qn_o), (k_r, gk_r, kn_o)):
                for h in range(N_HEADS):
                    cols = slice(h * HEAD_DIM, (h + 1) * HEAD_DIM)
                    t = src[rows, cols]
                    r = lax.rsqrt(jnp.mean(t * t, axis=-1, keepdims=True) + EPS)
                    dst[rows, cols] = _rope(t * r * g_r[...], c_t, s1_t, s2_t)

    tab = _rows(T, LANE)
    shp = jax.ShapeDtypeStruct((S, ATTN_W), F32)
    return _pcall(
        body, name=name, grid=(S // T,),
        in_specs=[_rows(T, ATTN_W, 0), _rows(T, ATTN_W, 1), _full((1, LANE)), _full((1, LANE)), tab, tab, tab],
        out_specs=(_rows(T, ATTN_W), _rows(T, ATTN_W)), out_shape=(shp, shp),
        compiler_params=_cparams("parallel"),
    )(zq, zq, g_q, g_k, *tabs)


def attn_bwd_post(zq, g_q, g_k, tabs, dq, dkc, dkp, dvc, dvp, *, name):
    S = zq.shape[0]
    T = min(POST_TILE, S)
    nblk = S // T
    GW = HEADS_PER_GROUP * HEAD_DIM
    n_ref = [7 if BLOCK * d < T else 5 for d in DILATIONS]

    def body(*refs):
        q_r, k_r, gq_r, gk_r, c_r, s1_r, s2_r = refs[:7]
        grp = refs[7:7 + sum(n_ref)]
        dz_o, dgq_o, dgk_o = refs[7 + sum(n_ref):]
        i = pl.program_id(0)
        dgq = jnp.zeros((1, LANE), F32)
        dgk = jnp.zeros((1, LANE), F32)
        RC = GLUE_RC
        for r0 in range(0, T, RC):
            rows = slice(r0, r0 + RC)
            c_t, s1_t, s2_t = c_r[rows, :], s1_r[rows, :], s2_r[rows, :]
            at = 0
            for gi, d in enumerate(DILATIONS):
                g_refs = grp[at:at + n_ref[gi]]
                at += n_ref[gi]
                sh = BLOCK * d
                if sh < T:
                    dq_r, dkc_r, dkp_r, dkpn_r, dvc_r, dvp_r, dvpn_r = g_refs
                    live = jnp.where(i + 1 < nblk, 1.0, 0.0)

                    def shifted(cur_r, nxt_r, gc, live=live, sh=sh, r0=r0):
                        if r0 + sh + RC <= T:
                            return cur_r[r0 + sh:r0 + sh + RC, gc]
                        return live * nxt_r[r0 + sh - T:r0 + sh - T + RC, gc]
                else:
                    dq_r, dkc_r, dkp_r, dvc_r, dvp_r = g_refs
                    dkpn_r = dvpn_r = None
                    live = jnp.where(i + sh // T < nblk, 1.0, 0.0)

                    def shifted(cur_r, nxt_r, gc, live=live, rows=rows):
                        return live * cur_r[rows, gc]
                for hh in range(HEADS_PER_GROUP):
                    h = gi * HEADS_PER_GROUP + hh
                    cols = slice(h * HEAD_DIM, (h + 1) * HEAD_DIM)
                    gc = slice(hh * HEAD_DIM, (hh + 1) * HEAD_DIM)
                    dk_v = dkc_r[rows, gc] + shifted(dkp_r, dkpn_r, gc)
                    dv_v = dvc_r[rows, gc] + shifted(dvp_r, dvpn_r, gc)
                    dz_o[rows, 2 * ATTN_W + h * HEAD_DIM:2 * ATTN_W + (h + 1) * HEAD_DIM] = dv_v.astype(BF16)
                    for which, (src, g_r, d_out) in enumerate(((q_r, gq_r, dq_r[rows, gc]), (k_r, gk_r, dk_v))):
                        t = src[rows, cols]
                        r = lax.rsqrt(jnp.mean(t * t, axis=-1, keepdims=True) + EPS)
                        xh = t * r
                        dtn = _rope_t(d_out, c_t, s1_t, s2_t)
                        dxh = dtn * g_r[...]
                        dt = r * (dxh - xh * jnp.mean(dxh * xh, axis=-1, keepdims=True))
                        dz_o[rows, which * ATTN_W + h * HEAD_DIM:which * ATTN_W + (h + 1) * HEAD_DIM] = dt.astype(BF16)
                        part = jnp.sum(dtn * xh, axis=0, keepdims=True)
                        if which == 0:
                            dgq = dgq + part
                        else:
                            dgk = dgk + part
        _acc_rows(dgq_o, dgq, i)
        _acc_rows(dgk_o, dgk, i)

    tab = _rows(T, LANE)
    specs = [_rows(T, ATTN_W, 0), _rows(T, ATTN_W, 1), _full((1, LANE)), _full((1, LANE)), tab, tab, tab]
    args = [zq, zq, g_q, g_k, *tabs]
    for gi, d in enumerate(DILATIONS):
        cur = _rows(T, GW)
        sh = BLOCK * d
        if sh < T:
            head = pl.BlockSpec((sh, GW), functools.partial(
                lambda i, k, last: (jnp.minimum((i + 1) * k, last), 0), k=T // sh, last=S // sh - 1))
            specs += [cur, cur, cur, head, cur, cur, head]
            args += [dq[gi], dkc[gi], dkp[gi], dkp[gi], dvc[gi], dvp[gi], dvp[gi]]
        else:
            nxt = pl.BlockSpec((T, GW), functools.partial(lambda i, s: (jnp.minimum(i + s, nblk - 1), 0), s=sh // T))
            specs += [cur, cur, nxt, cur, nxt]
            args += [dq[gi], dkc[gi], dkp[gi], dvc[gi], dvp[gi]]
    part = jax.ShapeDtypeStruct((8, LANE), F32)
    return _pcall(
        body, name=name, grid=(nblk,), in_specs=specs,
        out_specs=(_rows(T, 3 * ATTN_W), _full((8, LANE)), _full((8, LANE))),
        out_shape=(jax.ShapeDtypeStruct((S, 3 * ATTN_W), BF16), part, part),
        compiler_params=_cparams("arbitrary"),
    )(*args)


ATTN_UNITS = 16


def _attn_geometry(d, S):
    R = min(ATTN_UNITS * BLOCK, S)
    return R, R // (BLOCK * d), S // R


def _sub_rows(j, r, d):
    if d == 1:
        return pl.ds(j * BLOCK, BLOCK)
    return pl.ds(j * BLOCK * d + r, BLOCK, stride=d)


def _dot_nt(a, b):
    return lax.dot_general(a, b, (((1,), (1,)), ((), ())), preferred_element_type=F32)


def _dot_tn(a, b):
    return lax.dot_general(a, b, (((0,), (0,)), ((), ())), preferred_element_type=F32)


def _attn_specs(gi, R):
    h0 = gi * HEADS_PER_GROUP
    vcol = 2 * N_HEADS + h0
    cur = lambda off: pl.BlockSpec((R, HEAD_DIM), lambda h, n: (n, off + h))
    prev = lambda off: pl.BlockSpec((R, HEAD_DIM), lambda h, n: (jnp.maximum(n - 1, 0), off + h))
    return [cur(h0), cur(h0), prev(h0), cur(vcol), prev(vcol)]


ATTN_UNROLL = ATTN_UNITS


def _attn_masks(n):
    qi = lax.broadcasted_iota(jnp.int32, (BLOCK, 2 * BLOCK), 0)
    kj = lax.broadcasted_iota(jnp.int32, (BLOCK, 2 * BLOCK), 1)
    band = jnp.where(jnp.logical_and(kj >= qi, kj <= qi + BLOCK), 0.0, NEG)
    no_prev = band + jnp.where(kj < BLOCK, 1.0, 0.0) * jnp.where(n > 0, 0.0, NEG)
    return band, no_prev


def _attn_keys(kc_r, kp_r, vc_r, vp_r, j, r, d, J):
    rq = _sub_rows(j, r, d)
    if j > 0:
        rp = _sub_rows(j - 1, r, d)
        kp, vp = kc_r[rp, :], vc_r[rp, :]
    else:
        rp = _sub_rows(J - 1, r, d)
        kp, vp = kp_r[rp, :], vp_r[rp, :]
    kk = jnp.concatenate([kp, kc_r[rq, :]], axis=0).astype(BF16)
    vv = jnp.concatenate([vp, vc_r[rq, :]], axis=0).astype(BF16)
    return kk, vv


def _attn_units(unit, d, J):
    for j in range(J):
        if d == 1:
            unit(j, 0)
        else:
            def step(r, carry, j=j):
                unit(j, r)
                return carry
            lax.fori_loop(0, d, step, 0, unroll=min(d, ATTN_UNROLL))


def attn_fwd(qn, kn, zq, gi, *, name):
    S = qn.shape[0]
    d = DILATIONS[gi]
    R, J, nblk = _attn_geometry(d, S)
    scale = HEAD_DIM ** -0.5
    GW = HEADS_PER_GROUP * HEAD_DIM

    def body(q_r, kc_r, kp_r, vc_r, vp_r, o_o, l_o):
        band, no_prev = _attn_masks(pl.program_id(1))

        def unit(j, r):
            rq = _sub_rows(j, r, d)
            q = q_r[rq, :].astype(BF16)
            kk, vv = _attn_keys(kc_r, kp_r, vc_r, vp_r, j, r, d, J)
            s = _dot_nt(q, kk) * scale + (band if j > 0 else no_prev)
            m = jnp.max(s, axis=-1, keepdims=True)
            p = jnp.exp(s - m)
            l = jnp.sum(p, axis=-1, keepdims=True)
            o_o[rq, :] = jnp.dot(p.astype(BF16), vv, preferred_element_type=F32) / l
            l_o[rq, :] = jnp.broadcast_to(m + jnp.log(l), (BLOCK, HEAD_DIM))

        _attn_units(unit, d, J)

    ospec = pl.BlockSpec((R, HEAD_DIM), lambda h, n: (n, h))
    shp = jax.ShapeDtypeStruct((S, GW), F32)
    return _pcall(
        body, name=name, grid=(HEADS_PER_GROUP, nblk), in_specs=_attn_specs(gi, R),
        out_specs=(ospec, ospec), out_shape=(shp, shp),
        compiler_params=_cparams("parallel", "arbitrary"),
    )(qn, kn, kn, zq, zq)


def attn_bwd(qn, kn, zq, do, lse, cc, gi, *, name):
    S = qn.shape[0]
    d = DILATIONS[gi]
    R, J, nblk = _attn_geometry(d, S)
    scale = HEAD_DIM ** -0.5
    GW = HEADS_PER_GROUP * HEAD_DIM

    def body(q_r, kc_r, kp_r, vc_r, vp_r, do_r, l_r, c_r, dq_o, dkc_o, dkp_o, dvc_o, dvp_o):
        band, no_prev = _attn_masks(pl.program_id(1))

        def unit(j, r):
            rq = _sub_rows(j, r, d)
            q = q_r[rq, :].astype(BF16)
            kk, vv = _attn_keys(kc_r, kp_r, vc_r, vp_r, j, r, d, J)
            s_mask = band if j > 0 else no_prev
            dob = do_r[rq, :].astype(BF16)
            lv = l_r[rq, :]
            cv = c_r[rq, :]
            lv2 = jnp.concatenate([lv, lv], axis=1)
            cv2 = jnp.concatenate([cv, cv], axis=1)
            p = jnp.exp(_dot_nt(q, kk) * scale + s_mask - lv2)
            ds = (p * (_dot_nt(dob, vv) + cv2)).astype(BF16)
            dq_o[rq, :] = jnp.dot(ds, kk, preferred_element_type=F32) * scale
            dk2 = _dot_tn(ds, q) * scale
            dv2 = _dot_tn(p.astype(BF16), dob)
            dkp_o[rq, :] = dk2[0:BLOCK]
            dkc_o[rq, :] = dk2[BLOCK:2 * BLOCK]
            dvp_o[rq, :] = dv2[0:BLOCK]
            dvc_o[rq, :] = dv2[BLOCK:2 * BLOCK]

        _attn_units(unit, d, J)

    ospec = pl.BlockSpec((R, HEAD_DIM), lambda h, n: (n, h))
    shp = jax.ShapeDtypeStruct((S, GW), F32)
    return _pcall(
        body, name=name, grid=(HEADS_PER_GROUP, nblk),
        in_specs=_attn_specs(gi, R) + [ospec, ospec, ospec],
        out_specs=(ospec,) * 5, out_shape=(shp,) * 5,
        compiler_params=_cparams("parallel", "arbitrary"),
    )(qn, kn, kn, zq, zq, do, lse, cc)


def combine_fwd(o, lse, *, name):
    S, GW = o[0].shape
    T = ROW_TILE

    def body(o0, o1, o2, l0, l1, l2, a_o):
        m = jnp.maximum(jnp.maximum(l0[...], l1[...]), l2[...])
        e0, e1, e2 = jnp.exp(l0[...] - m), jnp.exp(l1[...] - m), jnp.exp(l2[...] - m)
        a_o[...] = ((e0 * o0[...] + e1 * o1[...] + e2 * o2[...]) / (e0 + e1 + e2)).astype(BF16)

    return _pcall(
        body, name=name, grid=(S // T,), in_specs=[_rows(T, GW)] * 6, out_specs=_rows(T, GW),
        out_shape=jax.ShapeDtypeStruct((S, GW), BF16), compiler_params=_cparams("parallel"),
    )(*o, *lse)


def combine_bwd(d_attn, o, lse, *, name):
    S, GW = d_attn.shape
    T = ROW_TILE

    def body(da_r, o0, o1, o2, l0, l1, l2, d0, d1, d2, c0, c1, c2):
        for r0 in range(0, T, GLUE_RC):
            rows = slice(r0, r0 + GLUE_RC)
            for hh in range(HEADS_PER_GROUP):
                cols = slice(hh * HEAD_DIM, (hh + 1) * HEAD_DIM)
                lv = [l_r[rows, cols] for l_r in (l0, l1, l2)]
                m = jnp.maximum(jnp.maximum(lv[0], lv[1]), lv[2])
                e = [jnp.exp(v - m) for v in lv]
                inv = 1.0 / (e[0] + e[1] + e[2])
                w = [e_g * inv for e_g in e]
                da = da_r[rows, cols]
                attn = w[0] * o0[rows, cols] + w[1] * o1[rows, cols] + w[2] * o2[rows, cols]
                a_h = jnp.sum(da * attn, axis=-1, keepdims=True)
                for w_g, d_o, c_o in zip(w, (d0, d1, d2), (c0, c1, c2)):
                    d_o[rows, cols] = w_g * da
                    c_o[rows, cols] = -w_g * a_h

    shp = jax.ShapeDtypeStruct((S, GW), F32)
    return _pcall(
        body, name=name, grid=(S // T,), in_specs=[_rows(T, GW)] * 7, out_specs=(_rows(T, GW),) * 6,
        out_shape=(shp,) * 6, compiler_params=_cparams("parallel"),
    )(d_attn, *o, *lse)


def _halo_prev(T, H, C, col):
    k = T // H
    return pl.BlockSpec((H, C), lambda i: (jnp.maximum(i * k - 1, 0), col))


def _halo_next(T, H, C, col, n_rows):
    k = T // H
    last = n_rows // H - 1
    return pl.BlockSpec((H, C), lambda i: (jnp.minimum((i + 1) * k, last), col))


CONV_RC = 64
SUBLANES = 8


def _tap_groups(offs):
    groups = {}
    for k, off in offs:
        groups.setdefault(off % SUBLANES, []).append((k, off))
    return [taps for _, taps in sorted(groups.items())]


def _for_taps(src, r0, lanes, offs, fn):
    for taps in _tap_groups(offs):
        lo = min(off for _, off in taps)
        hi = max(off for _, off in taps)
        sb = src[r0 + lo:r0 + hi + CONV_RC, lanes]
        for k, off in taps:
            fn(k, sb[off - lo:off - lo + CONV_RC])


def _dwconv(src, w_r, offs, T, C, bias_r, dst):
    for rc in range(T // CONV_RC):
        for cc in range(C // LANE):
            lanes = slice(cc * LANE, (cc + 1) * LANE)
            r0 = rc * CONV_RC
            acc = None if bias_r is None else jnp.zeros((CONV_RC, LANE), F32) + bias_r[:, lanes]
            for taps in _tap_groups(offs):
                lo = min(off for _, off in taps)
                hi = max(off for _, off in taps)
                sb = src[r0 + lo:r0 + hi + CONV_RC, lanes]
                g_acc = None
                for k, off in taps:
                    term = w_r[k:k + 1, lanes] * sb[off - lo:off - lo + CONV_RC]
                    g_acc = term if g_acc is None else g_acc + term
                acc = g_acc if acc is None else acc + g_acc
            dst[r0:r0 + CONV_RC, lanes] = acc


def _fill_glu(cv, cg, hv, hg, ubuf, i):
    T = cv.shape[0]
    live = jnp.where(i > 0, 1.0, 0.0)
    ubuf[0:CONV_HALO, :] = live * (hv[...] * _sig(hg[...]))
    ubuf[CONV_HALO:CONV_HALO + T, :] = cv[...] * _sig(cg[...])


_CONV_FWD_OFFS = [(k, CONV_HALO - (CONV_K - 1) + k) for k in range(CONV_K)]
_CONV_BWD_OFFS = [(k, CONV_K - 1 - k) for k in range(CONV_K)]


def convb_fwd(zc, w, b, g_ln, b_ln, *, name):
    S = zc.shape[0]
    C = zc.shape[1] // 2
    T = ROW_TILE

    def body(cv, cg, hv, hg, w_r, b_r, g_r, bl_r, u_o, y_o, ubuf):
        _fill_glu(cv, cg, hv, hg, ubuf, pl.program_id(0))
        _dwconv(ubuf, w_r, _CONV_FWD_OFFS, T, C, b_r, y_o)
        y = y_o[...]
        mu = jnp.mean(y, axis=-1, keepdims=True)
        yc = y - mu
        rs = lax.rsqrt(jnp.mean(yc * yc, axis=-1, keepdims=True) + EPS)
        v = yc * rs * g_r[...] + bl_r[...]
        u_o[...] = (v * _sig(v)).astype(BF16)

    vec = _full((1, C))
    return _pcall(
        body, name=name, grid=(S // T,),
        in_specs=[_rows(T, C, 0), _rows(T, C, 1), _halo_prev(T, CONV_HALO, C, 0), _halo_prev(T, CONV_HALO, C, 1),
                  _full((CONV_K, C)), vec, vec, vec],
        out_specs=(_rows(T, C), _rows(T, C)),
        out_shape=(jax.ShapeDtypeStruct((S, C), BF16), jax.ShapeDtypeStruct((S, C), F32)),
        scratch_shapes=[pltpu.VMEM((CONV_HALO + T, C), F32)],
        compiler_params=_cparams("parallel"),
    )(zc, zc, zc, zc, w, b, g_ln, b_ln)


def convb_bwd1(d_u2, y_conv, zc, g_ln, b_ln, *, name):
    S = zc.shape[0]
    C = zc.shape[1] // 2
    T = ROW_TILE

    def body(du_r, y_r, cv, cg, hv, hg, g_r, bl_r, dy_o, dw_o, db_o, dg_o, dbl_o, ubuf):
        i = pl.program_id(0)
        _fill_glu(cv, cg, hv, hg, ubuf, i)
        y = y_r[...]
        mu = jnp.mean(y, axis=-1, keepdims=True)
        yc = y - mu
        rs = lax.rsqrt(jnp.mean(yc * yc, axis=-1, keepdims=True) + EPS)
        yn = yc * rs
        v = yn * g_r[...] + bl_r[...]
        sg = _sig(v)
        dv = du_r[...] * (sg * (1.0 + v * (1.0 - sg)))
        dyn = dv * g_r[...]
        dy = rs * (dyn - jnp.mean(dyn, axis=-1, keepdims=True) - yn * jnp.mean(dyn * yn, axis=-1, keepdims=True))
        dy_o[...] = dy
        _acc_rows(dg_o, jnp.sum(dv * yn, axis=0, keepdims=True), i)
        _acc_rows(dbl_o, jnp.sum(dv, axis=0, keepdims=True), i)
        _acc_rows(db_o, jnp.sum(dy, axis=0, keepdims=True), i)

        @pl.when(i == 0)
        def _():
            dw_o[...] = jnp.zeros_like(dw_o)
        for cc in range(C // LANE):
            lanes = slice(cc * LANE, (cc + 1) * LANE)
            parts = [jnp.zeros((SUBLANES, LANE), F32) for _ in range(CONV_K)]
            for rc in range(T // CONV_RC):
                r0 = rc * CONV_RC
                dyc = dy_o[r0:r0 + CONV_RC, lanes]

                def tap(k, chunk, parts=parts, dyc=dyc):
                    prod = (dyc * chunk).reshape(CONV_RC // SUBLANES, SUBLANES, LANE)
                    parts[k] = parts[k] + jnp.sum(prod, axis=0)

                _for_taps(ubuf, r0, lanes, _CONV_FWD_OFFS, tap)
            for k in range(CONV_K):
                dw_o[k * SUBLANES:(k + 1) * SUBLANES, lanes] += parts[k]

    vec = _full((1, C))
    part = jax.ShapeDtypeStruct((8, C), F32)
    return _pcall(
        body, name=name, grid=(S // T,),
        in_specs=[_rows(T, C), _rows(T, C), _rows(T, C, 0), _rows(T, C, 1), _halo_prev(T, CONV_HALO, C, 0),
                  _halo_prev(T, CONV_HALO, C, 1), vec, vec],
        out_specs=(_rows(T, C), _full((CONV_K * SUBLANES, C)), _full((8, C)), _full((8, C)), _full((8, C))),
        out_shape=(jax.ShapeDtypeStruct((S, C), F32), jax.ShapeDtypeStruct((CONV_K * SUBLANES, C), F32),
                   part, part, part),
        scratch_shapes=[pltpu.VMEM((CONV_HALO + T, C), F32)],
        compiler_params=_cparams("arbitrary"),
    )(d_u2, y_conv, zc, zc, zc, zc, g_ln, b_ln)


def convb_bwd2(dy, zc, w, *, name):
    S = zc.shape[0]
    C = zc.shape[1] // 2
    T = ROW_TILE
    nblk = S // T

    def body(dy_r, dyn_r, cv, cg, w_r, dz_o, dbuf, dubuf):
        i = pl.program_id(0)
        live = jnp.where(i < nblk - 1, 1.0, 0.0)
        dbuf[0:T, :] = dy_r[...]
        dbuf[T:T + CONV_HALO, :] = live * dyn_r[...]
        _dwconv(dbuf, w_r, _CONV_BWD_OFFS, T, C, None, dubuf)
        du = dubuf[...]
        sg = _sig(cg[...])
        dz_o[:, 0:C] = (du * sg).astype(BF16)
        dz_o[:, C:2 * C] = (du * cv[...] * sg * (1.0 - sg)).astype(BF16)

    return _pcall(
        body, name=name, grid=(nblk,),
        in_specs=[_rows(T, C), _halo_next(T, CONV_HALO, C, 0, S), _rows(T, C, 0), _rows(T, C, 1), _full((CONV_K, C))],
        out_specs=_rows(T, 2 * C), out_shape=jax.ShapeDtypeStruct((S, 2 * C), BF16),
        scratch_shapes=[pltpu.VMEM((T + CONV_HALO, C), F32), pltpu.VMEM((T, C), F32)],
        compiler_params=_cparams("parallel"),
    )(dy, dy, zc, zc, w)


def merge_fwd(y_a, y_b, zg, *, name):
    S, D = y_a.shape
    T = ROW_TILE

    def body(a_r, b_r, ga_r, gb_r, m_o):
        ga, gb = ga_r[...].astype(F32), gb_r[...].astype(F32)
        m_o[...] = (_sig(ga) * a_r[...].astype(F32) + _sig(gb) * b_r[...].astype(F32)).astype(BF16)

    return _pcall(
        body, name=name, grid=(S // T,),
        in_specs=[_rows(T, D), _rows(T, D), _rows(T, D, 0), _rows(T, D, 1)],
        out_specs=_rows(T, D), out_shape=jax.ShapeDtypeStruct((S, D), BF16),
        compiler_params=_cparams("parallel"),
    )(y_a, y_b, zg, zg)


def merge_bwd(d_m, y_a, y_b, zg, *, name):
    S, D = y_a.shape
    T = ROW_TILE

    def body(dm_r, a_r, b_r, ga_r, gb_r, da_o, db_o, dz_o):
        dm = dm_r[...]
        sa, sb = _sig(ga_r[...].astype(F32)), _sig(gb_r[...].astype(F32))
        da_o[...] = (dm * sa).astype(BF16)
        db_o[...] = (dm * sb).astype(BF16)
        dz_o[:, 0:D] = (dm * a_r[...].astype(F32) * sa * (1.0 - sa)).astype(BF16)
        dz_o[:, D:2 * D] = (dm * b_r[...].astype(F32) * sb * (1.0 - sb)).astype(BF16)

    shp = jax.ShapeDtypeStruct((S, D), BF16)
    return _pcall(
        body, name=name, grid=(S // T,),
        in_specs=[_rows(T, D), _rows(T, D), _rows(T, D), _rows(T, D, 0), _rows(T, D, 1)],
        out_specs=(_rows(T, D), _rows(T, D), _rows(T, 2 * D)),
        out_shape=(shp, shp, jax.ShapeDtypeStruct((S, 2 * D), BF16)),
        compiler_params=_cparams("parallel"),
    )(d_m, y_a, y_b, zg, zg)


FFN_RC = 64


def _ffn_chunks(T, F):
    for cc in range(F // LANE):
        for rc in range(T // FFN_RC):
            yield rc * FFN_RC, slice(rc * FFN_RC, (rc + 1) * FFN_RC), slice(cc * LANE, (cc + 1) * LANE)


def _ffn_fill(g_r, hg_r, gbuf, i):
    T = g_r.shape[0]
    live = jnp.where(i > 0, 1.0, 0.0)
    gbuf[0:FFN_HALO, :] = live * hg_r[...].astype(F32)
    gbuf[FFN_HALO:FFN_HALO + T, :] = g_r[...].astype(F32)


def _ffn_gate_chunk(gbuf, w_r, b_r, r0, lanes, n=FFN_RC):
    taps = [gbuf[r0 + FFN_HALO - (FFN_K - 1) + k:r0 + FFN_HALO - (FFN_K - 1) + k + n, lanes]
            for k in range(FFN_K)]
    gp = b_r[:, lanes] + w_r[0:1, lanes] * taps[0]
    for k in range(1, FFN_K):
        gp = gp + w_r[k:k + 1, lanes] * taps[k]
    return gp, taps


def _sum8(v):
    return jnp.sum(v.reshape(v.shape[0] // SUBLANES, SUBLANES, v.shape[1]), axis=0)


def ffn_act_fwd(gu, w, b, *, name):
    S = gu.shape[0]
    F = gu.shape[1] // 2
    T = ROW_TILE // 2

    def body(g_r, u_r, hg_r, w_r, b_r, a_o, gbuf):
        _ffn_fill(g_r, hg_r, gbuf, pl.program_id(0))
        for r0, rows, lanes in _ffn_chunks(T, F):
            gp, _ = _ffn_gate_chunk(gbuf, w_r, b_r, r0, lanes)
            a_o[rows, lanes] = (gp * _sig(gp) * u_r[rows, lanes].astype(F32)).astype(BF16)

    return _pcall(
        body, name=name, grid=(S // T,),
        in_specs=[_rows(T, F, 0), _rows(T, F, 1), _halo_prev(T, FFN_HALO, F, 0), _full((FFN_K, F)), _full((1, F))],
        out_specs=_rows(T, F), out_shape=jax.ShapeDtypeStruct((S, F), BF16),
        scratch_shapes=[pltpu.VMEM((FFN_HALO + T, F), F32)],
        compiler_params=_cparams("parallel"),
    )(gu, gu, gu, w, b)


def ffn_act_bwd(d_a, gu, w, b, *, name):
    S = gu.shape[0]
    F = gu.shape[1] // 2
    T = ROW_TILE // 2
    H = FFN_HALO
    nblk = S // T

    def dgp_of(gp, da, u):
        sg = _sig(gp)
        return da * u * (sg * (1.0 + gp * (1.0 - sg))), sg

    def body(da_r, dan_r, g_r, gp_r, gn_r, u_r, un_r, w_r, b_r, o_o, dw_o, db_o, gbuf, dbuf):
        i = pl.program_id(0)
        gbuf[0:H, :] = jnp.where(i > 0, 1.0, 0.0) * gp_r[...].astype(F32)
        gbuf[H:H + T, :] = g_r[...].astype(F32)
        gbuf[H + T:H + T + H, :] = gn_r[...].astype(F32)

        @pl.when(i == 0)
        def _():
            dw_o[...] = jnp.zeros_like(dw_o)
            db_o[...] = jnp.zeros_like(db_o)

        live_n = jnp.where(i < nblk - 1, 1.0, 0.0)
        sums = None
        for r0, rows, lanes in _ffn_chunks(T, F):
            if r0 == 0:
                sums = [jnp.zeros((SUBLANES, LANE), F32) for _ in range(FFN_K + 1)]
                gp_h, _ = _ffn_gate_chunk(gbuf, w_r, b_r, T, lanes, n=H)
                dgp_h, _ = dgp_of(gp_h, live_n * dan_r[:, lanes].astype(F32), un_r[:, lanes].astype(F32))
                dbuf[T:T + H, lanes] = dgp_h
            gp, taps = _ffn_gate_chunk(gbuf, w_r, b_r, r0, lanes)
            da = da_r[rows, lanes].astype(F32)
            dgp, sg = dgp_of(gp, da, u_r[rows, lanes].astype(F32))
            o_o[rows, F + lanes.start:F + lanes.stop] = (da * gp * sg).astype(BF16)
            dbuf[rows, lanes] = dgp
            sums[FFN_K] = sums[FFN_K] + _sum8(dgp)
            for k in range(FFN_K):
                sums[k] = sums[k] + _sum8(dgp * taps[k])
            if r0 + FFN_RC == T:
                db_o[:, lanes] += sums[FFN_K]
                for k in range(FFN_K):
                    dw_o[k * SUBLANES:(k + 1) * SUBLANES, lanes] += sums[k]
        for r0, rows, lanes in _ffn_chunks(T, F):
            dg = w_r[0:1, lanes] * dbuf[r0 + FFN_K - 1:r0 + FFN_K - 1 + FFN_RC, lanes]
            for k in range(1, FFN_K):
                off = r0 + FFN_K - 1 - k
                dg = dg + w_r[k:k + 1, lanes] * dbuf[off:off + FFN_RC, lanes]
            o_o[rows, lanes] = dg.astype(BF16)

    return _pcall(
        body, name=name, grid=(nblk,),
        in_specs=[_rows(T, F), _halo_next(T, H, F, 0, S), _rows(T, F, 0), _halo_prev(T, H, F, 0),
                  _halo_next(T, H, F, 0, S), _rows(T, F, 1), _halo_next(T, H, F, 1, S),
                  _full((FFN_K, F)), _full((1, F))],
        out_specs=(_rows(T, 2 * F), _full((FFN_K * SUBLANES, F)), _full((SUBLANES, F))),
        out_shape=(jax.ShapeDtypeStruct((S, 2 * F), BF16),
                   jax.ShapeDtypeStruct((FFN_K * SUBLANES, F), F32), jax.ShapeDtypeStruct((SUBLANES, F), F32)),
        scratch_shapes=[pltpu.VMEM((H + T + H, F), F32), pltpu.VMEM((T + H, F), F32)],
        compiler_params=_cparams("arbitrary"),
    )(d_a, d_a, gu, gu, gu, gu, gu, w, b)


def _row_tile(R, target=512):
    if R <= target:
        return R
    for t in range(target, 7, -8):
        if R % t == 0:
            return t
    return R


def sum_slots(land, *, name):
    _, R, C = land.shape
    T = _row_tile(R)

    def body(l_r, o_o):
        acc = l_r[0].astype(F32)
        for q in range(1, N_DEV):
            acc = acc + l_r[q].astype(F32)
        o_o[...] = acc

    return _pcall(
        body, name=name, grid=(R // T,),
        in_specs=[pl.BlockSpec((N_DEV, T, C), lambda i: (0, i, 0))],
        out_specs=_rows(T, C), out_shape=jax.ShapeDtypeStruct((R, C), F32),
        compiler_params=_cparams("parallel"),
    )(land)


def adamw(w, g, m, v, *, name):
    shape = w.shape
    C = shape[-1]
    R = math.prod(shape[:-1])
    w2, g2, m2, v2 = (t.reshape(R, C) for t in (w, g, m, v))
    T = _row_tile(R)
    c1 = 1.0 - ADAM_B1 ** ADAM_STEP
    c2 = 1.0 - ADAM_B2 ** ADAM_STEP

    def body(w_r, g_r, m_r, v_r, d_o, m_o, v_o):
        gv = g_r[...]
        mn = ADAM_B1 * m_r[...] + (1.0 - ADAM_B1) * gv
        vn = ADAM_B2 * v_r[...] + (1.0 - ADAM_B2) * (gv * gv)
        m_o[...] = mn
        v_o[...] = vn
        d_o[...] = -ADAM_LR * ((mn / c1) / (jnp.sqrt(vn / c2) + ADAM_EPS) + ADAM_WD * w_r[...])

    shp = jax.ShapeDtypeStruct((R, C), F32)
    d, mn, vn = _pcall(
        body, name=name, grid=(R // T,), in_specs=[_rows(T, C)] * 4, out_specs=(_rows(T, C),) * 3,
        out_shape=(shp,) * 3, compiler_params=_cparams("parallel"),
    )(w2, g2, m2, v2)
    return d.reshape(shape), mn.reshape(shape), vn.reshape(shape)


def _my_pos():
    return lax.axis_index("x"), lax.axis_index("y"), lax.axis_index("c")


class _Exchange:
    def __init__(self, inputs, out_shapes, sems, start, finish):
        self.inputs, self.out_shapes, self.sems, self.start, self.finish = inputs, out_shapes, sems, start, finish


def gather_exchange(shards):
    n = len(shards)

    def plan(ins, outs, sems):
        send_sems, recv_sems, local_sems = sems
        x, y, c = _my_pos()
        me, sibling = (x, y, c), (x, y, 1 - c)
        chips = [(1 - x, y), (x, 1 - y), (1 - x, 1 - y)]

        def slot(i, p):
            return outs[i].at[4 * p[0] + 2 * p[1] + p[2]]

        def copy(k, i, block, to, src=None):
            return pltpu.make_async_remote_copy(
                src_ref=slot(i, block) if src is None else src, dst_ref=slot(i, block),
                send_sem=send_sems.at[k, i], recv_sem=recv_sems.at[k, i], device_id=to, device_id_type=MESH)

        mine = [pltpu.make_async_copy(ins[i], slot(i, me), local_sems.at[i]) for i in range(n)]
        first = []
        for i in range(n):
            first.append(copy(0, i, me, sibling, src=ins[i]))
            first += [copy(1 + j, i, me, (*chip, c), src=ins[i]) for j, chip in enumerate(chips)]
        return me, sibling, chips, c, copy, mine, first

    def start(ins, outs, sems):
        _, _, _, _, _, mine, first = plan(ins, outs, sems)
        for cp in mine + first:
            cp.start()

    def finish(ins, outs, sems):
        me, sibling, chips, c, copy, mine, first = plan(ins, outs, sems)
        passed = []
        for j, chip in enumerate(chips):
            for i in range(n):
                copy(1 + j, i, (*chip, c), me).wait_recv()
                cp = copy(4 + j, i, (*chip, c), sibling)
                cp.start()
                passed.append(cp)
        for i in range(n):
            copy(0, i, sibling, me).wait_recv()
            for j, chip in enumerate(chips):
                copy(4 + j, i, (*chip, 1 - c), me).wait_recv()
        for cp in first + passed:
            cp.wait_send()
        for cp in mine:
            cp.wait()

    outs = [jax.ShapeDtypeStruct((N_DEV,) + s.shape, s.dtype) for s in shards]
    sems = [pltpu.SemaphoreType.DMA((7, n)), pltpu.SemaphoreType.DMA((7, n)), pltpu.SemaphoreType.DMA((n,))]
    return _Exchange(list(shards), outs, sems, start, finish)


def scatter_exchange(gs):
    n = len(gs)

    def plan(ins, outs, sems):
        send_sems, recv_sems, local_sems = sems
        x, y, c = _my_pos()
        me_id = 4 * x + 2 * y + c
        mine = [pltpu.make_async_copy(ins[i].at[me_id], outs[i].at[me_id], local_sems.at[i]) for i in range(n)]
        sends, recvs = [], []
        for msk in range(1, N_DEV):
            px = 1 - x if msk & 4 else x
            py = 1 - y if msk & 2 else y
            pc = 1 - c if msk & 1 else c
            pid = 4 * px + 2 * py + pc
            for i in range(n):
                sends.append(pltpu.make_async_remote_copy(
                    src_ref=ins[i].at[pid], dst_ref=outs[i].at[me_id],
                    send_sem=send_sems.at[msk - 1, i], recv_sem=recv_sems.at[msk - 1, i],
                    device_id=(px, py, pc), device_id_type=MESH))
                recvs.append(pltpu.make_async_remote_copy(
                    src_ref=ins[i].at[pid], dst_ref=outs[i].at[pid],
                    send_sem=send_sems.at[msk - 1, i], recv_sem=recv_sems.at[msk - 1, i],
                    device_id=(px, py, pc), device_id_type=MESH))
        return mine, sends, recvs

    def start(ins, outs, sems):
        mine, sends, _ = plan(ins, outs, sems)
        for cp in mine + sends:
            cp.start()

    def finish(ins, outs, sems):
        mine, sends, recvs = plan(ins, outs, sems)
        for rv in recvs:
            rv.wait_recv()
        for cp in sends:
            cp.wait_send()
        for cp in mine:
            cp.wait()

    outs = [jax.ShapeDtypeStruct(g.shape, g.dtype) for g in gs]
    sems = [pltpu.SemaphoreType.DMA((7, n)), pltpu.SemaphoreType.DMA((7, n)), pltpu.SemaphoreType.DMA((n,))]
    return _Exchange(list(gs), outs, sems, start, finish)


def run_exchange(ex, *, name):
    HBM = pl.BlockSpec(memory_space=pl.ANY)
    n_in, n_out = len(ex.inputs), len(ex.out_shapes)

    def body(*refs):
        ins, outs, sems = refs[:n_in], refs[n_in:n_in + n_out], refs[n_in + n_out:]
        ex.start(ins, outs, sems)
        ex.finish(ins, outs, sems)

    return list(_pcall(body, name=name, in_specs=[HBM] * n_in, out_specs=tuple([HBM] * n_out),
                       out_shape=tuple(ex.out_shapes), scratch_shapes=ex.sems)(*ex.inputs))


def all_gather(xs, *, name):
    outs = run_exchange(gather_exchange([xs[l] for l in range(xs.shape[0])]), name=name)
    return jnp.stack(outs)


def _slots(g):
    return g.reshape(N_DEV, g.shape[0] // N_DEV, g.shape[1])


def _rope_tables(positions):
    half = ROT_DIM // 2
    inv_freq = ROPE_THETA ** (-jnp.arange(0, ROT_DIM, 2, dtype=F32) / ROT_DIM)
    ang = positions.astype(F32)[:, None] * inv_freq
    cos, sin = jnp.cos(ang), jnp.sin(ang)
    S = positions.shape[0]
    c_t = jnp.concatenate([cos, cos, jnp.ones((S, LANE - ROT_DIM), F32)], axis=1)
    s1_t = jnp.concatenate([-sin, jnp.zeros((S, LANE - half), F32)], axis=1)
    s2_t = jnp.concatenate([jnp.zeros((S, half), F32), sin, jnp.zeros((S, LANE - ROT_DIM), F32)], axis=1)
    return c_t, s1_t, s2_t


def _row(v):
    return v.reshape(1, -1)


def kernel(x, c, positions, w_ada, b_ada, g_norm1, w_in, g_q, g_k, w_attn_proj, w_conv_dw, b_conv_dw, g_conv_ln, b_conv_ln, w_conv_out, w_o, g_norm2, w_ffn_in, w_ffn_dw, b_ffn_dw, w_ffn_down, loss_target, m_w_ada, m_b_ada, m_g_norm1, m_w_in, m_g_q, m_g_k, m_w_attn_proj, m_w_conv_dw, m_b_conv_dw, m_g_conv_ln, m_b_conv_ln, m_w_conv_out, m_w_o, m_g_norm2, m_w_ffn_in, m_w_ffn_dw, m_b_ffn_dw, m_w_ffn_down, v_w_ada, v_b_ada, v_g_norm1, v_w_in, v_g_q, v_g_k, v_w_attn_proj, v_w_conv_dw, v_b_conv_dw, v_g_conv_ln, v_b_conv_ln, v_w_conv_out, v_w_o, v_g_norm2, v_w_ffn_in, v_w_ffn_dw, v_b_ffn_dw, v_w_ffn_down):
    L = w_in.shape[0]
    S, D = x.shape[1], x.shape[2]
    FF = w_ffn_down.shape[1] * N_DEV
    xi, yi, ci = _my_pos()
    me = 4 * xi + 2 * yi + ci
    x0 = x[0]
    tabs = _rope_tables(positions[0])

    c_act = c * _sig(c)
    c_all = all_gather(jnp.pad(c_act, ((0, 7), (0, 0)))[None], name="ag_c")[0][:, 0, :]
    c_all16 = jnp.pad(c_all, ((0, 8), (0, 0)))
    m_part = jnp.stack([mm(c_all16, w_ada[l], "nn", name="mod_mm") for l in range(L)])
    m_all = all_gather(m_part, name="ag_mod")
    mod = lax.dynamic_index_in_dim(m_all, me, axis=2, keepdims=False).reshape(L, 6 * D) + b_ada
    mod = mod.reshape(L, 6, 1, D)

    sh_in = jnp.transpose(w_in, (0, 2, 1)).astype(BF16)
    sh_fi = jnp.transpose(w_ffn_in, (0, 2, 1)).astype(BF16)
    sh_ap = jnp.transpose(w_attn_proj, (0, 2, 1)).astype(BF16)
    sh_co, sh_oo, sh_dn = w_conv_out.astype(BF16), w_o.astype(BF16), w_ffn_down.astype(BF16)

    def rowcat(g):
        return g.reshape(N_DEV * g.shape[1], g.shape[2])

    wt_in, wt_fi, wt_ap, w_co, w_oo, w_dn = ([None] * L for _ in range(6))
    wt_in[0] = rowcat(run_exchange(gather_exchange([sh_in[0]]), name="ag_w0")[0])
    cdw = all_gather(jnp.pad(w_conv_dw, ((0, 0), (0, 1), (0, 0))), name="ag_cdw")
    cdw = jnp.transpose(cdw, (0, 2, 1, 3)).reshape(L, 32, D)[:, :CONV_K]
    fsh = w_ffn_dw.shape[2]
    fpad = -fsh % LANE
    fdw = all_gather(jnp.pad(w_ffn_dw, ((0, 0), (0, 8 - FFN_K), (0, fpad))), name="ag_fdw")
    fdw = jnp.transpose(fdw[:, :, :FFN_K, :fsh], (0, 2, 1, 3)).reshape(L, FFN_K, FF)

    QKV, CW = 3 * ATTN_W, 2 * D
    seg = ((0, QKV), (QKV, CW), (QKV + CW, 2 * D))

    saved = []
    x_prev, delta, gt_prev = x0, None, None
    for l in range(L):
        sh1, sc1, gt1, sh2, sc2, gt2 = (mod[l, i] for i in range(6))
        x_l, h = norm_mod_fwd(x_prev, delta, gt_prev, _row(g_norm1[l]), sc1, sh1, name="norm_fwd")
        nxt = l + 1 < L
        if l == 0:
            zq, got = mm(h, wt_in[0], "nt", b_off=0, b_len=QKV, name="z_mm_ag0",
                         comm=gather_exchange([sh_fi[0], sh_ap[0], sh_co[0], sh_oo[0], sh_dn[0]]))
            wt_fi[0], wt_ap[0], w_co[0], w_oo[0], w_dn[0] = (rowcat(g) for g in got)
        elif nxt:
            zq, got = mm(h, wt_in[l], "nt", b_off=0, b_len=QKV, name="z_mm_ag", comm=gather_exchange([sh_in[l + 1]]))
            wt_in[l + 1] = rowcat(got[0])
        else:
            zq = mm(h, wt_in[l], "nt", b_off=0, b_len=QKV, name="z_mm")
        if l == 0 and nxt:
            zc, got = mm(h, wt_in[0], "nt", b_off=seg[1][0], b_len=seg[1][1], name="zc_mm_ag",
                         comm=gather_exchange([sh_in[1]]))
            wt_in[1] = rowcat(got[0])
        else:
            zc = mm(h, wt_in[l], "nt", b_off=seg[1][0], b_len=seg[1][1], name="z_mm")
        zg = mm(h, wt_in[l], "nt", b_off=seg[2][0], b_len=seg[2][1], out_dtype=BF16, name="zg_mm")
        gq, gk = _row(g_q[l]), _row(g_k[l])
        qn, kn = qk_prep_fwd(zq, gq, gk, tabs, name="qk_prep")
        o_g, lse_g = [], []
        for gi in range(3):
            o_i, l_i = attn_fwd(qn, kn, zq, gi, name="attn_fwd%d" % gi)
            o_g.append(o_i)
            lse_g.append(l_i)
        attn = combine_fwd(o_g, lse_g, name="combine_fwd")
        y_a = mm(attn, wt_ap[l], "nt", out_dtype=BF16, name="ya_mm")
        cw, cb = cdw[l], _row(b_conv_dw[l])
        cg, cbl = _row(g_conv_ln[l]), _row(b_conv_ln[l])
        u2, y_conv = convb_fwd(zc, cw, cb, cg, cbl, name="convb_fwd")
        y_b = mm(u2, w_co[l], "nn", out_dtype=BF16, name="yb_mm")
        merged = merge_fwd(y_a, y_b, zg, name="merge_fwd")
        mo = mm(merged, w_oo[l], "nn", name="mo_mm")
        x_mid, h2 = norm_mod_fwd(x_l, mo, gt1, _row(g_norm2[l]), sc2, sh2, name="norm_fwd")
        if nxt:
            gu, got = mm(h2, wt_fi[l], "nt", out_dtype=BF16, name="gu_mm_ag", comm=gather_exchange([sh_fi[l + 1]]))
            wt_fi[l + 1] = rowcat(got[0])
        else:
            gu = mm(h2, wt_fi[l], "nt", out_dtype=BF16, name="gu_mm")
        fw, fb = fdw[l], _row(b_ffn_dw[l])
        act = ffn_act_fwd(gu, fw, fb, name="ffn_act")
        if nxt:
            ffo, got = mm(act, w_dn[l], "nn", name="ffo_mm_ag",
                          comm=gather_exchange([sh_ap[l + 1], sh_co[l + 1], sh_oo[l + 1], sh_dn[l + 1]]))
            wt_ap[l + 1], w_co[l + 1], w_oo[l + 1], w_dn[l + 1] = (rowcat(g) for g in got)
        else:
            ffo = mm(act, w_dn[l], "nn", name="ffo_mm")
        saved.append(dict(x=x_l, h=h, zq=zq, zc=zc, zg=zg, qn=qn, kn=kn, o=o_g, lse=lse_g, attn=attn, y_a=y_a,
                          u2=u2, y_conv=y_conv, y_b=y_b, merged=merged, mo=mo, x_mid=x_mid, h2=h2, gu=gu, act=act, ffo=ffo))
        x_prev, delta, gt_prev = x_mid, ffo, gt2

    dx, lpart, d_ffo, p_gt2 = loss_head(x_prev, delta, gt_prev, loss_target[0], name="loss_head")
    loss = lax.psum(0.5 / D * jnp.sum(lpart[0]), ("x", "y", "c"))

    land = {k: [None] * L for k in ("in", "fi", "ap", "co", "o", "dn")}
    small_rows = []
    for l in reversed(range(L)):
        sv = saved[l]
        sh1, sc1, gt1, sh2, sc2, gt2 = (mod[l, i] for i in range(6))
        d_act = mm(d_ffo, w_dn[l], "nt", out_dtype=BF16, name="dact_mm")
        g_dn = mm(sv["act"], d_ffo, "tn", out_dtype=BF16, name="dwdn_mm")
        fw, fb = fdw[l], _row(b_ffn_dw[l])
        dgu, p_fw, p_fb = ffn_act_bwd(d_act, sv["gu"], fw, fb, name="ffn_bwd")
        dh2, got = mm(dgu, wt_fi[l], "nn", name="dh2_mm_rs", comm=scatter_exchange([_slots(g_dn)]))
        land["dn"][l] = got[0]
        g_fi = mm(dgu, sv["h2"], "tn", out_dtype=BF16, name="dwfi_mm")
        dx, p_g2, p_sc2, p_sh2, d_mo, p_gt1 = norm_mod_bwd(sv["x_mid"], dh2, _row(g_norm2[l]), sc2, sh2, dx,
                                                           (sv["mo"], gt1), name="norm_bwd")
        d_merged = mm(d_mo, w_oo[l], "nt", name="dmerged_mm")
        g_o = mm(sv["merged"], d_mo, "tn", out_dtype=BF16, name="dwo_mm")
        d_ya, d_yb, dzg = merge_bwd(d_merged, sv["y_a"], sv["y_b"], sv["zg"], name="merge_bwd")
        d_attn = mm(d_ya, wt_ap[l], "nn", name="dattn_mm")
        g_ap = mm(d_ya, sv["attn"], "tn", out_dtype=BF16, name="dwap_mm")
        d_u2 = mm(d_yb, w_co[l], "nt", name="du2_mm")
        g_co = mm(sv["u2"], d_yb, "tn", out_dtype=BF16, name="dwco_mm")
        cw, cb = cdw[l], _row(b_conv_dw[l])
        cg, cbl = _row(g_conv_ln[l]), _row(b_conv_ln[l])
        dy, p_cw, p_cb, p_cg, p_cbl = convb_bwd1(d_u2, sv["y_conv"], sv["zc"], cg, cbl, name="convb_bwd1")
        dzc = convb_bwd2(dy, sv["zc"], cw, name="convb_bwd2")
        dd = combine_bwd(d_attn, sv["o"], sv["lse"], name="combine_bwd")
        do_g, cc_g = dd[:3], dd[3:]
        parts = [attn_bwd(sv["qn"], sv["kn"], sv["zq"], do_g[gi], sv["lse"][gi], cc_g[gi], gi,
                          name="attn_bwd%d" % gi) for gi in range(3)]
        gq, gk = _row(g_q[l]), _row(g_k[l])
        dzq, p_gq, p_gk = attn_bwd_post(sv["zq"], gq, gk, tabs, *[[p[i] for p in parts] for i in range(5)],
                                        name="attn_post")
        g_in_q, got = mm(dzq, sv["h"], "tn", out_dtype=BF16, name="dwin_mm_rs",
                         comm=scatter_exchange([_slots(g_fi), _slots(g_o), _slots(g_ap), _slots(g_co)]))
        land["fi"][l], land["o"][l], land["ap"][l], land["co"][l] = got
        g_in = jnp.concatenate([g_in_q] + [mm(dz_s, sv["h"], "tn", out_dtype=BF16, name="dwin_mm")
                                           for dz_s in (dzc, dzg)], axis=0)
        dh, got = mm(dzq, wt_in[l], "nn", b_off=0, name="dh_mm_rs", comm=scatter_exchange([_slots(g_in)]))
        land["in"][l] = got[0]
        for dz_s, (o, n) in zip((dzc, dzg), seg[1:]):
            dh = mm(dz_s, wt_in[l], "nn", b_off=o, c_in=dh, name="dh_mm")
        if l > 0:
            dx, p_g1, p_sc1, p_sh1, d_ffo_prev, p_gt2_prev = norm_mod_bwd(
                sv["x"], dh, _row(g_norm1[l]), sc1, sh1, dx, (saved[l - 1]["ffo"], mod[l - 1, 5]), name="norm_bwd")
        else:
            dx, p_g1, p_sc1, p_sh1 = norm_mod_bwd(sv["x"], dh, _row(g_norm1[l]), sc1, sh1, dx, name="norm_bwd")

        def row1k(p):
            v = p[0]
            pad = -v.shape[0] % D
            return jnp.pad(v, (0, pad)).reshape(-1, D)

        rows = [row1k(p) for p in (p_sh1, p_sc1, p_gt1, p_sh2, p_sc2, p_gt2, p_g1, p_g2)]
        rows.append(row1k(jnp.concatenate([p_gq, p_gk], axis=1)))
        rows += [row1k(p) for p in (p_cb, p_cg, p_cbl, jnp.sum(p_fb, axis=0, keepdims=True))]
        rows.append(jnp.sum(p_cw.reshape(CONV_K, SUBLANES, D), axis=1))
        rows += [row1k(jnp.sum(p_fw[k * SUBLANES:(k + 1) * SUBLANES], axis=0, keepdims=True))
                 for k in range(FFN_K)]
        blk = jnp.concatenate(rows, axis=0)
        small_rows.append(jnp.pad(blk, ((0, -blk.shape[0] % 8), (0, 0))))
        if l > 0:
            d_ffo, p_gt2 = d_ffo_prev, p_gt2_prev
    small_rows = small_rows[::-1]
    n_small = small_rows[0].shape[0]
    ff_rows = -(-FF // D)

    small = jnp.concatenate(small_rows, axis=0)[None]
    small_all = all_gather(small, name="ag_small")[0]
    small_sum = sum_slots(small_all, name="sum_small").reshape(L, n_small, D)
    small_all = small_all.reshape(N_DEV, L, n_small, D)

    g_b_ada = small_sum[:, 0:6].reshape(L, 6 * D)
    g_g1, g_g2 = small_sum[:, 6], small_sum[:, 7]
    g_gq, g_gk = small_sum[:, 8, 0:LANE], small_sum[:, 8, LANE:2 * LANE]
    g_cb, g_cg, g_cbl = small_sum[:, 9], small_sum[:, 10], small_sum[:, 11]
    r0 = 12
    g_fb = small_sum[:, r0:r0 + ff_rows].reshape(L, -1)[:, :FF]
    r0 += ff_rows
    g_cw_full = small_sum[:, r0:r0 + CONV_K]
    r0 += CONV_K
    g_fw_full = small_sum[:, r0:r0 + FFN_K * ff_rows].reshape(L, FFN_K, -1)[:, :, :FF]
    csh = w_conv_dw.shape[2]
    g_cw = lax.dynamic_slice_in_dim(g_cw_full, me * csh, csh, axis=2)
    g_fw = lax.dynamic_slice_in_dim(g_fw_full, me * fsh, fsh, axis=2)

    ash = w_ada.shape[2]
    dmod_all = small_all[:, :, 0:6].reshape(N_DEV, L, 6 * D)
    dmod_mine = lax.dynamic_slice_in_dim(dmod_all, me * ash, ash, axis=2)
    g_w_ada = jnp.stack([mm(c_all16, jnp.pad(dmod_mine[:, l], ((0, 8), (0, 0))), "tn", name="dwada_mm")
                         for l in range(L)])

    def reduced(key, transposed):
        out = jnp.stack([sum_slots(slots, name="sum_" + key) for slots in land[key]])
        return jnp.transpose(out, (0, 2, 1)) if transposed else out

    g_w_in = reduced("in", True)
    g_w_fi = reduced("fi", True)
    g_w_ap = reduced("ap", True)
    g_w_co = reduced("co", False)
    g_w_o = reduced("o", False)
    g_w_dn = reduced("dn", False)

    grads = [g_w_ada, g_b_ada, g_g1, g_w_in, g_gq, g_gk, g_w_ap, g_cw, g_cb, g_cg, g_cbl, g_w_co, g_w_o, g_g2,
             g_w_fi, g_fw, g_fb, g_w_dn]
    ws = [w_ada, b_ada, g_norm1, w_in, g_q, g_k, w_attn_proj, w_conv_dw, b_conv_dw, g_conv_ln, b_conv_ln,
          w_conv_out, w_o, g_norm2, w_ffn_in, w_ffn_dw, b_ffn_dw, w_ffn_down]
    ms = [m_w_ada, m_b_ada, m_g_norm1, m_w_in, m_g_q, m_g_k, m_w_attn_proj, m_w_conv_dw, m_b_conv_dw, m_g_conv_ln,
          m_b_conv_ln, m_w_conv_out, m_w_o, m_g_norm2, m_w_ffn_in, m_w_ffn_dw, m_b_ffn_dw, m_w_ffn_down]
    vs = [v_w_ada, v_b_ada, v_g_norm1, v_w_in, v_g_q, v_g_k, v_w_attn_proj, v_w_conv_dw, v_b_conv_dw, v_g_conv_ln,
          v_b_conv_ln, v_w_conv_out, v_w_o, v_g_norm2, v_w_ffn_in, v_w_ffn_dw, v_b_ffn_dw, v_w_ffn_down]
    deltas, new_m, new_v = [], [], []
    for w_i, g_i, m_i, v_i in zip(ws, grads, ms, vs):
        d_i, mn_i, vn_i = adamw(w_i, g_i, m_i, v_i, name="adamw")
        deltas.append(d_i)
        new_m.append(mn_i)
        new_v.append(vn_i)
    return (loss, dx[None], *grads, *deltas, *new_m, *new_v)
```

```python
import functools
import math

import jax
import jax.numpy as jnp
from jax import lax
from jax.experimental import pallas as pl
from jax.experimental.pallas import tpu as pltpu

F32 = jnp.float32
BF16 = jnp.bfloat16
MESH = pl.DeviceIdType.MESH
N_DEV = 8

EPS = 1e-6
HEAD_DIM = 128
BLOCK = 128
DILATIONS = (1, 4, 16)
HEADS_PER_GROUP = 4
N_HEADS = 12
ATTN_W = N_HEADS * HEAD_DIM
ROT_DIM = 32
ROPE_THETA = 500000.0
CONV_K = 31
CONV_HALO = 32
FFN_K = 3
FFN_HALO = 16
NEG = -1e30

ADAM_LR, ADAM_B1, ADAM_B2, ADAM_EPS, ADAM_WD, ADAM_STEP = 0.001, 0.9, 0.999, 1e-08, 0.01, 10

LANE = 128
VMEM_LIMIT = 56 * 1024 * 1024
ROW_TILE = 512
POST_TILE = 256
GLUE_RC = 64


def _pcall(body, **kw):
    return pl.pallas_call(body, **kw)


def _cparams(*sem):
    return pltpu.CompilerParams(dimension_semantics=sem, vmem_limit_bytes=VMEM_LIMIT)


def _sig(v):
    return 1.0 / (1.0 + jnp.exp(-v))


def _divtile(dim, target):
    best = None
    for t in range(LANE, min(dim, target) + 1, LANE):
        if dim % t == 0:
            best = t
    return best or dim


def _rows(t, c, col=0):
    return pl.BlockSpec((t, c), lambda i: (i, col))


def _full(shape):
    nd = len(shape)
    return pl.BlockSpec(shape, lambda i: (0,) * nd)


def _acc_rows(ref, val, i):
    @pl.when(i == 0)
    def _():
        ref[...] = jnp.zeros_like(ref)
    r = val.shape[0]
    ref[0:r, :] += val


MM_TILE = 1536


def mm(a, b, mode, *, name, out_dtype=F32, c_in=None, b_off=0, b_len=None, comm=None):
    if mode == "nn":
        M, K = a.shape
        N = b.shape[1]
    elif mode == "nt":
        M, K = a.shape
        N = b_len if b_len is not None else b.shape[0]
    else:
        K, M = a.shape
        N = b.shape[1]
    g_n = math.gcd(N, b_off) if (mode == "nt" and b_off) else N
    g_k = math.gcd(K, b_off) if (mode == "nn" and b_off) else K
    tn = _divtile(g_n, MM_TILE if c_in is None else 1024)
    tk = _divtile(g_k, MM_TILE if mode != "tn" else 2048)
    tm = _divtile(M, MM_TILE if mode == "tn" else (2048 if c_in is None else 1024))
    gm, gn, nk = M // tm, N // tn, K // tk
    n_ci = 0 if comm is None else len(comm.inputs)
    n_co = 0 if comm is None else len(comm.out_shapes)
    n_x = 2 + (c_in is not None)
    if mode == "nn":
        dims = (((1,), (0,)), ((), ()))
    elif mode == "nt":
        dims = (((1,), (1,)), ((), ()))
    else:
        dims = (((0,), (0,)), ((), ()))

    def body(*refs):
        a_ref, b_ref = refs[0], refs[1]
        c_ref = refs[2] if c_in is not None else None
        c_ins = refs[n_x:n_x + n_ci]
        o_ref = refs[n_x + n_ci]
        c_outs = refs[n_x + n_ci + 1:n_x + n_ci + 1 + n_co]
        rest = refs[n_x + n_ci + 1 + n_co:]
        acc = rest[0] if nk > 1 else None
        sems = rest[1:] if nk > 1 else rest
        i, j, k = pl.program_id(0), pl.program_id(1), pl.program_id(2)

        if comm is not None:
            @pl.when(jnp.logical_and(jnp.logical_and(i == 0, j == 0), k == 0))
            def _():
                comm.start(c_ins, c_outs, sems)

        prod = lax.dot_general(a_ref[...].astype(BF16), b_ref[...].astype(BF16), dims, preferred_element_type=F32)
        if nk == 1:
            if c_ref is not None:
                prod = prod + c_ref[...].astype(F32)
            o_ref[...] = prod.astype(out_dtype)
        else:
            @pl.when(k == 0)
            def _():
                if c_ref is None:
                    acc[...] = prod
                else:
                    acc[...] = prod + c_ref[...].astype(F32)

            @pl.when(k > 0)
            def _():
                acc[...] += prod

            @pl.when(k == nk - 1)
            def _():
                o_ref[...] = acc[...].astype(out_dtype)

        if comm is not None:
            @pl.when(jnp.logical_and(jnp.logical_and(i == gm - 1, j == gn - 1), k == nk - 1))
            def _():
                comm.finish(c_ins, c_outs, sems)

    if mode == "nn":
        a_spec = pl.BlockSpec((tm, tk), lambda i, j, k: (i, k))
        ob = b_off // tk
        b_spec = pl.BlockSpec((tk, tn), lambda i, j, k: (k + ob, j))
    elif mode == "nt":
        a_spec = pl.BlockSpec((tm, tk), lambda i, j, k: (i, k))
        ob = b_off // tn
        b_spec = pl.BlockSpec((tn, tk), lambda i, j, k: (j + ob, k))
    else:
        a_spec = pl.BlockSpec((tk, tm), lambda i, j, k: (k, i))
        b_spec = pl.BlockSpec((tk, tn), lambda i, j, k: (k, j))
    o_spec = pl.BlockSpec((tm, tn), lambda i, j, k: (i, j))
    HBM = pl.BlockSpec(memory_space=pl.ANY)
    in_specs = [a_spec, b_spec]
    args = [a, b]
    if c_in is not None:
        in_specs.append(o_spec)
        args.append(c_in)
    scratch = [pltpu.VMEM((tm, tn), F32)] if nk > 1 else []
    o_shape = jax.ShapeDtypeStruct((M, N), out_dtype)
    if comm is None:
        return _pcall(
            body, name=name, grid=(gm, gn, nk), in_specs=in_specs, out_specs=o_spec, out_shape=o_shape,
            scratch_shapes=scratch, compiler_params=_cparams("parallel", "parallel", "arbitrary"),
        )(*args)
    res = _pcall(
        body, name=name, grid=(gm, gn, nk), in_specs=in_specs + [HBM] * n_ci,
        out_specs=(o_spec, *[HBM] * n_co), out_shape=(o_shape, *comm.out_shapes),
        scratch_shapes=scratch + comm.sems, compiler_params=_cparams("arbitrary", "arbitrary", "arbitrary"),
    )(*args, *comm.inputs)
    return res[0], list(res[1:])


def norm_mod_fwd(x_prev, delta, gt, g, sc, sh, *, name):
    S, D = x_prev.shape
    T = ROW_TILE
    has_delta = delta is not None

    def body(*refs):
        if has_delta:
            xp, dl, gt_r, g_r, sc_r, sh_r, x_out, h_out = refs
            xv = xp[...] + gt_r[...] * dl[...]
            x_out[...] = xv
        else:
            xp, g_r, sc_r, sh_r, h_out = refs
            xv = xp[...]
        r = lax.rsqrt(jnp.mean(xv * xv, axis=-1, keepdims=True) + EPS)
        h_out[...] = ((xv * r) * g_r[...] * (1.0 + sc_r[...]) + sh_r[...]).astype(BF16)

    vec = _full((1, D))
    if has_delta:
        ins, specs = [x_prev, delta, gt, g, sc, sh], [_rows(T, D), _rows(T, D), vec, vec, vec, vec]
        outs = (jax.ShapeDtypeStruct((S, D), F32), jax.ShapeDtypeStruct((S, D), BF16))
        ospecs = (_rows(T, D), _rows(T, D))
    else:
        ins, specs = [x_prev, g, sc, sh], [_rows(T, D), vec, vec, vec]
        outs = jax.ShapeDtypeStruct((S, D), BF16)
        ospecs = _rows(T, D)
    res = _pcall(body, name=name, grid=(S // T,), in_specs=specs, out_specs=ospecs, out_shape=outs,
                 compiler_params=_cparams("parallel"))(*ins)
    return res if has_delta else (x_prev, res)


def norm_mod_bwd(x, dh, g, sc, sh, dx_res, res=None, *, name):
    S, D = x.shape
    T = ROW_TILE
    has_res = res is not None

    def body(*refs):
        x_r, dh_r, g_r, sc_r, sh_r, dr_r = refs[:6]
        dx_o, dg_o, dsc_o, dsh_o = refs[6 + 2 * has_res:10 + 2 * has_res]
        i = pl.program_id(0)
        xv = x_r[...]
        dh_v = dh_r[...]
        r = lax.rsqrt(jnp.mean(xv * xv, axis=-1, keepdims=True) + EPS)
        xh = xv * r
        dn = dh_v * (1.0 + sc_r[...])
        dxh = dn * g_r[...]
        dx = dr_r[...] + r * (dxh - xh * jnp.mean(dxh * xh, axis=-1, keepdims=True))
        dx_o[...] = dx
        _acc_rows(dg_o, jnp.sum(dn * xh, axis=0, keepdims=True), i)
        _acc_rows(dsc_o, jnp.sum(dh_v * (xh * g_r[...]), axis=0, keepdims=True), i)
        _acc_rows(dsh_o, jnp.sum(dh_v, axis=0, keepdims=True), i)
        if has_res:
            dl_r, gt_r = refs[6:8]
            dd_o, dgt_o = refs[12:14]
            dd_o[...] = (dx * gt_r[...]).astype(BF16)
            _acc_rows(dgt_o, jnp.sum(dx * dl_r[...], axis=0, keepdims=True), i)

    vec = _full((1, D))
    part = jax.ShapeDtypeStruct((8, D), F32)
    in_specs = [_rows(T, D), _rows(T, D), vec, vec, vec, _rows(T, D)]
    out_specs = [_rows(T, D), _full((8, D)), _full((8, D)), _full((8, D))]
    out_shape = [jax.ShapeDtypeStruct((S, D), F32), part, part, part]
    args = [x, dh, g, sc, sh, dx_res]
    if has_res:
        in_specs += [_rows(T, D), vec]
        out_specs += [_rows(T, D), _full((8, D))]
        out_shape += [jax.ShapeDtypeStruct((S, D), BF16), part]
        args += list(res)
    return _pcall(
        body, name=name, grid=(S // T,), in_specs=in_specs, out_specs=tuple(out_specs), out_shape=tuple(out_shape),
        compiler_params=_cparams("arbitrary"),
    )(*args)


def loss_head(x_mid, ffo, gt, target, *, name):
    S, D = x_mid.shape
    T = ROW_TILE

    def body(x_r, f_r, gt_r, t_r, dy_o, l_o, dd_o, dgt_o):
        i = pl.program_id(0)
        fv = f_r[...]
        e = x_r[...] + gt_r[...] * fv - t_r[...]
        dy = e * (1.0 / D)
        dy_o[...] = dy
        _acc_rows(l_o, jnp.sum(e * e, axis=0, keepdims=True), i)
        dd_o[...] = (dy * gt_r[...]).astype(BF16)
        _acc_rows(dgt_o, jnp.sum(dy * fv, axis=0, keepdims=True), i)

    part = jax.ShapeDtypeStruct((8, D), F32)
    return _pcall(
        body, name=name, grid=(S // T,),
        in_specs=[_rows(T, D), _rows(T, D), _full((1, D)), _rows(T, D)],
        out_specs=(_rows(T, D), _full((8, D)), _rows(T, D), _full((8, D))),
        out_shape=(jax.ShapeDtypeStruct((S, D), F32), part, jax.ShapeDtypeStruct((S, D), BF16), part),
        compiler_params=_cparams("arbitrary"),
    )(x_mid, ffo, gt, target)


def _rope(t, c_t, s1_t, s2_t):
    return t * c_t + pltpu.roll(t, LANE - ROT_DIM // 2, 1) * s1_t + pltpu.roll(t, ROT_DIM // 2, 1) * s2_t


def _rope_t(d, c_t, s1_t, s2_t):
    return d * c_t + pltpu.roll(d * s1_t, ROT_DIM // 2, 1) + pltpu.roll(d * s2_t, LANE - ROT_DIM // 2, 1)


def qk_prep_fwd(zq, g_q, g_k, tabs, *, name):
    S = zq.shape[0]
    T = ROW_TILE

    def body(q_r, k_r, gq_r, gk_r, c_r, s1_r, s2_r, qn_o, kn_o):
        c_t, s1_t, s2_t = c_r[...], s1_r[...], s2_r[...]
        for src, g_r, dst in ((q_r, gq_r, qn_o), (k_r, gk_r, kn_o)):
            for h in range(N_HEADS):
                cols = slice(h * HEAD_DIM, (h + 1) * HEAD_DIM)
                t = src[:, cols]
                r = lax.rsqrt(jnp.mean(t * t, axis=-1, keepdims=True) + EPS)
                dst[:, cols] = _rope(t * r * g_r[...], c_t, s1_t, s2_t)

    tab = _rows(T, LANE)
    shp = jax.ShapeDtypeStruct((S, ATTN_W), F32)
    return _pcall(
        body, name=name, grid=(S // T,),
        in_specs=[_rows(T, ATTN_W, 0), _rows(T, ATTN_W, 1), _full((1, LANE)), _full((1, LANE)), tab, tab, tab],
        out_specs=(_rows(T, ATTN_W), _rows(T, ATTN_W)), out_shape=(shp, shp),
        compiler_params=_cparams("parallel"),
    )(zq, zq, g_q, g_k, *tabs)


def attn_bwd_post(zq, g_q, g_k, tabs, dq, dkc, dkp, dvc, dvp, *, name):
    S = zq.shape[0]
    T = min(POST_TILE, S)
    nblk = S // T
    GW = HEADS_PER_GROUP * HEAD_DIM
    n_ref = [7 if BLOCK * d < T else 5 for d in DILATIONS]

    def body(*refs):
        q_r, k_r, gq_r, gk_r, c_r, s1_r, s2_r = refs[:7]
        grp = refs[7:7 + sum(n_ref)]
        dz_o, dgq_o, dgk_o = refs[7 + sum(n_ref):]
        i = pl.program_id(0)
        dgq = jnp.zeros((1, LANE), F32)
        dgk = jnp.zeros((1, LANE), F32)
        RC = T
        for r0 in range(0, T, RC):
            rows = slice(r0, r0 + RC)
            c_t, s1_t, s2_t = c_r[rows, :], s1_r[rows, :], s2_r[rows, :]
            at = 0
            for gi, d in enumerate(DILATIONS):
                g_refs = grp[at:at + n_ref[gi]]
                at += n_ref[gi]
                sh = BLOCK * d
                if sh < T:
                    dq_r, dkc_r, dkp_r, dkpn_r, dvc_r, dvp_r, dvpn_r = g_refs
                    live = jnp.where(i + 1 < nblk, 1.0, 0.0)

                    def shifted(cur_r, nxt_r, gc, live=live, lo=r0 + sh):
                        if lo + RC <= T:
                            return cur_r[lo:lo + RC, gc]
                        if lo >= T:
                            return live * nxt_r[lo - T:lo - T + RC, gc]
                        return jnp.concatenate([cur_r[lo:T, gc], live * nxt_r[0:lo + RC - T, gc]], axis=0)
                else:
                    dq_r, dkc_r, dkp_r, dvc_r, dvp_r = g_refs
                    dkpn_r = dvpn_r = None
                    live = jnp.where(i + sh // T < nblk, 1.0, 0.0)

                    def shifted(cur_r, nxt_r, gc, live=live, rows=rows):
                        return live * cur_r[rows, gc]
                for hh in range(HEADS_PER_GROUP):
                    h = gi * HEADS_PER_GROUP + hh
                    cols = slice(h * HEAD_DIM, (h + 1) * HEAD_DIM)
                    gc = slice(hh * HEAD_DIM, (hh + 1) * HEAD_DIM)
                    dk_v = dkc_r[rows, gc] + shifted(dkp_r, dkpn_r, gc)
                    dv_v = dvc_r[rows, gc] + shifted(dvp_r, dvpn_r, gc)
                    dz_o[rows, 2 * ATTN_W + h * HEAD_DIM:2 * ATTN_W + (h + 1) * HEAD_DIM] = dv_v.astype(BF16)
                    for which, (src, g_r, d_out) in enumerate(((q_r, gq_r, dq_r[rows, gc]), (k_r, gk_r, dk_v))):
                        t = src[rows, cols]
                        r = lax.rsqrt(jnp.mean(t * t, axis=-1, keepdims=True) + EPS)
                        xh = t * r
                        dtn = _rope_t(d_out, c_t, s1_t, s2_t)
                        dxh = dtn * g_r[...]
                        dt = r * (dxh - xh * jnp.mean(dxh * xh, axis=-1, keepdims=True))
                        dz_o[rows, which * ATTN_W + h * HEAD_DIM:which * ATTN_W + (h + 1) * HEAD_DIM] = dt.astype(BF16)
                        part = jnp.sum(dtn * xh, axis=0, keepdims=True)
                        if which == 0:
                            dgq = dgq + part
                        else:
                            dgk = dgk + part
        _acc_rows(dgq_o, dgq, i)
        _acc_rows(dgk_o, dgk, i)

    tab = _rows(T, LANE)
    specs = [_rows(T, ATTN_W, 0), _rows(T, ATTN_W, 1), _full((1, LANE)), _full((1, LANE)), tab, tab, tab]
    args = [zq, zq, g_q, g_k, *tabs]
    for gi, d in enumerate(DILATIONS):
        cur = _rows(T, GW)
        sh = BLOCK * d
        if sh < T:
            head = pl.BlockSpec((sh, GW), functools.partial(
                lambda i, k, last: (jnp.minimum((i + 1) * k, last), 0), k=T // sh, last=S // sh - 1))
            specs += [cur, cur, cur, head, cur, cur, head]
            args += [dq[gi], dkc[gi], dkp[gi], dkp[gi], dvc[gi], dvp[gi], dvp[gi]]
        else:
            nxt = pl.BlockSpec((T, GW), functools.partial(lambda i, s: (jnp.minimum(i + s, nblk - 1), 0), s=sh // T))
            specs += [cur, cur, nxt, cur, nxt]
            args += [dq[gi], dkc[gi], dkp[gi], dvc[gi], dvp[gi]]
    part = jax.ShapeDtypeStruct((8, LANE), F32)
    return _pcall(
        body, name=name, grid=(nblk,), in_specs=specs,
        out_specs=(_rows(T, 3 * ATTN_W), _full((8, LANE)), _full((8, LANE))),
        out_shape=(jax.ShapeDtypeStruct((S, 3 * ATTN_W), BF16), part, part),
        compiler_params=_cparams("arbitrary"),
    )(*args)


ATTN_UNITS = 16


def _attn_geometry(d, S):
    R = min(ATTN_UNITS * BLOCK, S)
    return R, R // (BLOCK * d), S // R


def _sub_rows(j, r, d):
    if d == 1:
        return pl.ds(j * BLOCK, BLOCK)
    return pl.ds(j * BLOCK * d + r, BLOCK, stride=d)


def _dot_nt(a, b):
    return lax.dot_general(a, b, (((1,), (1,)), ((), ())), preferred_element_type=F32)


def _dot_tn(a, b):
    return lax.dot_general(a, b, (((0,), (0,)), ((), ())), preferred_element_type=F32)


def _attn_specs(gi, R):
    h0 = gi * HEADS_PER_GROUP
    vcol = 2 * N_HEADS + h0
    cur = lambda off: pl.BlockSpec((R, HEAD_DIM), lambda h, n: (n, off + h))
    prev = lambda off: pl.BlockSpec((R, HEAD_DIM), lambda h, n: (jnp.maximum(n - 1, 0), off + h))
    return [cur(h0), cur(h0), prev(h0), cur(vcol), prev(vcol)]


ATTN_UNROLL = ATTN_UNITS


def _attn_masks(n):
    qi = lax.broadcasted_iota(jnp.int32, (BLOCK, 2 * BLOCK), 0)
    kj = lax.broadcasted_iota(jnp.int32, (BLOCK, 2 * BLOCK), 1)
    band = jnp.where(jnp.logical_and(kj >= qi, kj <= qi + BLOCK), 0.0, NEG)
    no_prev = band + jnp.where(kj < BLOCK, 1.0, 0.0) * jnp.where(n > 0, 0.0, NEG)
    return band, no_prev


def _attn_keys(kc_r, kp_r, vc_r, vp_r, j, r, d, J):
    rq = _sub_rows(j, r, d)
    if j > 0:
        rp = _sub_rows(j - 1, r, d)
        kp, vp = kc_r[rp, :], vc_r[rp, :]
    else:
        rp = _sub_rows(J - 1, r, d)
        kp, vp = kp_r[rp, :], vp_r[rp, :]
    kk = jnp.concatenate([kp, kc_r[rq, :]], axis=0).astype(BF16)
    vv = jnp.concatenate([vp, vc_r[rq, :]], axis=0).astype(BF16)
    return kk, vv


def _attn_units(unit, d, J):
    for j in range(J):
        if d == 1:
            unit(j, 0)
        else:
            def step(r, carry, j=j):
                unit(j, r)
                return carry
            lax.fori_loop(0, d, step, 0, unroll=min(d, ATTN_UNROLL))


def attn_fwd(qn, kn, zq, gi, *, name):
    S = qn.shape[0]
    d = DILATIONS[gi]
    R, J, nblk = _attn_geometry(d, S)
    scale = HEAD_DIM ** -0.5
    GW = HEADS_PER_GROUP * HEAD_DIM

    def body(q_r, kc_r, kp_r, vc_r, vp_r, o_o, l_o):
        band, no_prev = _attn_masks(pl.program_id(1))

        def unit(j, r):
            rq = _sub_rows(j, r, d)
            q = q_r[rq, :].astype(BF16)
            kk, vv = _attn_keys(kc_r, kp_r, vc_r, vp_r, j, r, d, J)
            s = _dot_nt(q, kk) * scale + (band if j > 0 else no_prev)
            m = jnp.max(s, axis=-1, keepdims=True)
            p = jnp.exp(s - m)
            l = jnp.sum(p, axis=-1, keepdims=True)
            o_o[rq, :] = jnp.dot(p.astype(BF16), vv, preferred_element_type=F32) / l
            l_o[rq, :] = jnp.broadcast_to(m + jnp.log(l), (BLOCK, HEAD_DIM))

        _attn_units(unit, d, J)

    ospec = pl.BlockSpec((R, HEAD_DIM), lambda h, n: (n, h))
    shp = jax.ShapeDtypeStruct((S, GW), F32)
    return _pcall(
        body, name=name, grid=(HEADS_PER_GROUP, nblk), in_specs=_attn_specs(gi, R),
        out_specs=(ospec, ospec), out_shape=(shp, shp),
        compiler_params=_cparams("parallel", "arbitrary"),
    )(qn, kn, kn, zq, zq)


def attn_bwd(qn, kn, zq, do, lse, cc, gi, *, name):
    S = qn.shape[0]
    d = DILATIONS[gi]
    R, J, nblk = _attn_geometry(d, S)
    scale = HEAD_DIM ** -0.5
    GW = HEADS_PER_GROUP * HEAD_DIM

    def body(q_r, kc_r, kp_r, vc_r, vp_r, do_r, l_r, c_r, dq_o, dkc_o, dkp_o, dvc_o, dvp_o):
        band, no_prev = _attn_masks(pl.program_id(1))

        def unit(j, r):
            rq = _sub_rows(j, r, d)
            q = q_r[rq, :].astype(BF16)
            kk, vv = _attn_keys(kc_r, kp_r, vc_r, vp_r, j, r, d, J)
            s_mask = band if j > 0 else no_prev
            dob = do_r[rq, :].astype(BF16)
            lv = l_r[rq, :]
            cv = c_r[rq, :]
            lv2 = jnp.concatenate([lv, lv], axis=1)
            cv2 = jnp.concatenate([cv, cv], axis=1)
            p = jnp.exp(_dot_nt(q, kk) * scale + s_mask - lv2)
            ds = (p * (_dot_nt(dob, vv) + cv2)).astype(BF16)
            dq_o[rq, :] = jnp.dot(ds, kk, preferred_element_type=F32) * scale
            dk2 = _dot_tn(ds, q) * scale
            dv2 = _dot_tn(p.astype(BF16), dob)
            dkp_o[rq, :] = dk2[0:BLOCK]
            dkc_o[rq, :] = dk2[BLOCK:2 * BLOCK]
            dvp_o[rq, :] = dv2[0:BLOCK]
            dvc_o[rq, :] = dv2[BLOCK:2 * BLOCK]

        _attn_units(unit, d, J)

    ospec = pl.BlockSpec((R, HEAD_DIM), lambda h, n: (n, h))
    shp = jax.ShapeDtypeStruct((S, GW), F32)
    return _pcall(
        body, name=name, grid=(HEADS_PER_GROUP, nblk),
        in_specs=_attn_specs(gi, R) + [ospec, ospec, ospec],
        out_specs=(ospec,) * 5, out_shape=(shp,) * 5,
        compiler_params=_cparams("parallel", "arbitrary"),
    )(qn, kn, kn, zq, zq, do, lse, cc)


def combine_fwd(o, lse, *, name):
    S, GW = o[0].shape
    T = ROW_TILE

    def body(o0, o1, o2, l0, l1, l2, a_o):
        m = jnp.maximum(jnp.maximum(l0[...], l1[...]), l2[...])
        e0, e1, e2 = jnp.exp(l0[...] - m), jnp.exp(l1[...] - m), jnp.exp(l2[...] - m)
        a_o[...] = ((e0 * o0[...] + e1 * o1[...] + e2 * o2[...]) / (e0 + e1 + e2)).astype(BF16)

    return _pcall(
        body, name=name, grid=(S // T,), in_specs=[_rows(T, GW)] * 6, out_specs=_rows(T, GW),
        out_shape=jax.ShapeDtypeStruct((S, GW), BF16), compiler_params=_cparams("parallel"),
    )(*o, *lse)


def combine_bwd(d_attn, o, lse, *, name):
    S, GW = d_attn.shape
    T = ROW_TILE

    def body(da_r, o0, o1, o2, l0, l1, l2, d0, d1, d2, c0, c1, c2):
        for r0 in range(0, T, GLUE_RC):
            rows = slice(r0, r0 + GLUE_RC)
            for hh in range(HEADS_PER_GROUP):
                cols = slice(hh * HEAD_DIM, (hh + 1) * HEAD_DIM)
                lv = [l_r[rows, cols] for l_r in (l0, l1, l2)]
                m = jnp.maximum(jnp.maximum(lv[0], lv[1]), lv[2])
                e = [jnp.exp(v - m) for v in lv]
                inv = 1.0 / (e[0] + e[1] + e[2])
                w = [e_g * inv for e_g in e]
                da = da_r[rows, cols]
                attn = w[0] * o0[rows, cols] + w[1] * o1[rows, cols] + w[2] * o2[rows, cols]
                a_h = jnp.sum(da * attn, axis=-1, keepdims=True)
                for w_g, d_o, c_o in zip(w, (d0, d1, d2), (c0, c1, c2)):
                    d_o[rows, cols] = w_g * da
                    c_o[rows, cols] = -w_g * a_h

    shp = jax.ShapeDtypeStruct((S, GW), F32)
    return _pcall(
        body, name=name, grid=(S // T,), in_specs=[_rows(T, GW)] * 7, out_specs=(_rows(T, GW),) * 6,
        out_shape=(shp,) * 6, compiler_params=_cparams("parallel"),
    )(d_attn, *o, *lse)


def _halo_prev(T, H, C, col):
    k = T // H
    return pl.BlockSpec((H, C), lambda i: (jnp.maximum(i * k - 1, 0), col))


def _halo_next(T, H, C, col, n_rows):
    k = T // H
    last = n_rows // H - 1
    return pl.BlockSpec((H, C), lambda i: (jnp.minimum((i + 1) * k, last), col))


CONV_RC = 64
SUBLANES = 8


def _tap_groups(offs):
    groups = {}
    for k, off in offs:
        groups.setdefault(off % SUBLANES, []).append((k, off))
    return [taps for _, taps in sorted(groups.items())]


def _for_taps(src, r0, lanes, offs, fn):
    for taps in _tap_groups(offs):
        lo = min(off for _, off in taps)
        hi = max(off for _, off in taps)
        sb = src[r0 + lo:r0 + hi + CONV_RC, lanes]
        for k, off in taps:
            fn(k, sb[off - lo:off - lo + CONV_RC])


def _dwconv(src, w_r, offs, T, C, bias_r, dst):
    for rc in range(T // CONV_RC):
        for cc in range(C // LANE):
            lanes = slice(cc * LANE, (cc + 1) * LANE)
            r0 = rc * CONV_RC
            acc = None if bias_r is None else jnp.zeros((CONV_RC, LANE), F32) + bias_r[:, lanes]
            for taps in _tap_groups(offs):
                lo = min(off for _, off in taps)
                hi = max(off for _, off in taps)
                sb = src[r0 + lo:r0 + hi + CONV_RC, lanes]
                g_acc = None
                for k, off in taps:
                    term = w_r[k:k + 1, lanes] * sb[off - lo:off - lo + CONV_RC]
                    g_acc = term if g_acc is None else g_acc + term
                acc = g_acc if acc is None else acc + g_acc
            dst[r0:r0 + CONV_RC, lanes] = acc


def _fill_glu(cv, cg, hv, hg, ubuf, i):
    T = cv.shape[0]
    live = jnp.where(i > 0, 1.0, 0.0)
    ubuf[0:CONV_HALO, :] = live * (hv[...] * _sig(hg[...]))
    ubuf[CONV_HALO:CONV_HALO + T, :] = cv[...] * _sig(cg[...])


_CONV_FWD_OFFS = [(k, CONV_HALO - (CONV_K - 1) + k) for k in range(CONV_K)]
_CONV_BWD_OFFS = [(k, CONV_K - 1 - k) for k in range(CONV_K)]


def convb_fwd(zc, w, b, g_ln, b_ln, *, name):
    S = zc.shape[0]
    C = zc.shape[1] // 2
    T = ROW_TILE

    def body(cv, cg, hv, hg, w_r, b_r, g_r, bl_r, u_o, y_o, ubuf):
        _fill_glu(cv, cg, hv, hg, ubuf, pl.program_id(0))
        _dwconv(ubuf, w_r, _CONV_FWD_OFFS, T, C, b_r, y_o)
        y = y_o[...]
        mu = jnp.mean(y, axis=-1, keepdims=True)
        yc = y - mu
        rs = lax.rsqrt(jnp.mean(yc * yc, axis=-1, keepdims=True) + EPS)
        v = yc * rs * g_r[...] + bl_r[...]
        u_o[...] = (v * _sig(v)).astype(BF16)

    vec = _full((1, C))
    return _pcall(
        body, name=name, grid=(S // T,),
        in_specs=[_rows(T, C, 0), _rows(T, C, 1), _halo_prev(T, CONV_HALO, C, 0), _halo_prev(T, CONV_HALO, C, 1),
                  _full((CONV_K, C)), vec, vec, vec],
        out_specs=(_rows(T, C), _rows(T, C)),
        out_shape=(jax.ShapeDtypeStruct((S, C), BF16), jax.ShapeDtypeStruct((S, C), F32)),
        scratch_shapes=[pltpu.VMEM((CONV_HALO + T, C), F32)],
        compiler_params=_cparams("parallel"),
    )(zc, zc, zc, zc, w, b, g_ln, b_ln)


def convb_bwd1(d_u2, y_conv, zc, g_ln, b_ln, *, name):
    S = zc.shape[0]
    C = zc.shape[1] // 2
    T = ROW_TILE

    def body(du_r, y_r, cv, cg, hv, hg, g_r, bl_r, dy_o, dw_o, db_o, dg_o, dbl_o, ubuf):
        i = pl.program_id(0)
        _fill_glu(cv, cg, hv, hg, ubuf, i)
        y = y_r[...]
        mu = jnp.mean(y, axis=-1, keepdims=True)
        yc = y - mu
        rs = lax.rsqrt(jnp.mean(yc * yc, axis=-1, keepdims=True) + EPS)
        yn = yc * rs
        v = yn * g_r[...] + bl_r[...]
        sg = _sig(v)
        dv = du_r[...] * (sg * (1.0 + v * (1.0 - sg)))
        dyn = dv * g_r[...]
        dy = rs * (dyn - jnp.mean(dyn, axis=-1, keepdims=True) - yn * jnp.mean(dyn * yn, axis=-1, keepdims=True))
        dy_o[...] = dy
        _acc_rows(dg_o, jnp.sum(dv * yn, axis=0, keepdims=True), i)
        _acc_rows(dbl_o, jnp.sum(dv, axis=0, keepdims=True), i)
        _acc_rows(db_o, jnp.sum(dy, axis=0, keepdims=True), i)

        @pl.when(i == 0)
        def _():
            dw_o[...] = jnp.zeros_like(dw_o)
        for cc in range(C // LANE):
            lanes = slice(cc * LANE, (cc + 1) * LANE)
            parts = [jnp.zeros((SUBLANES, LANE), F32) for _ in range(CONV_K)]
            for rc in range(T // CONV_RC):
                r0 = rc * CONV_RC
                dyc = dy_o[r0:r0 + CONV_RC, lanes]

                def tap(k, chunk, parts=parts, dyc=dyc):
                    prod = (dyc * chunk).reshape(CONV_RC // SUBLANES, SUBLANES, LANE)
                    parts[k] = parts[k] + jnp.sum(prod, axis=0)

                _for_taps(ubuf, r0, lanes, _CONV_FWD_OFFS, tap)
            for k in range(CONV_K):
                dw_o[k * SUBLANES:(k + 1) * SUBLANES, lanes] += parts[k]

    vec = _full((1, C))
    part = jax.ShapeDtypeStruct((8, C), F32)
    return _pcall(
        body, name=name, grid=(S // T,),
        in_specs=[_rows(T, C), _rows(T, C), _rows(T, C, 0), _rows(T, C, 1), _halo_prev(T, CONV_HALO, C, 0),
                  _halo_prev(T, CONV_HALO, C, 1), vec, vec],
        out_specs=(_rows(T, C), _full((CONV_K * SUBLANES, C)), _full((8, C)), _full((8, C)), _full((8, C))),
        out_shape=(jax.ShapeDtypeStruct((S, C), F32), jax.ShapeDtypeStruct((CONV_K * SUBLANES, C), F32),
                   part, part, part),
        scratch_shapes=[pltpu.VMEM((CONV_HALO + T, C), F32)],
        compiler_params=_cparams("arbitrary"),
    )(d_u2, y_conv, zc, zc, zc, zc, g_ln, b_ln)


def convb_bwd2(dy, zc, w, *, name):
    S = zc.shape[0]
    C = zc.shape[1] // 2
    T = ROW_TILE
    nblk = S // T

    def body(dy_r, dyn_r, cv, cg, w_r, dz_o, dbuf, dubuf):
        i = pl.program_id(0)
        live = jnp.where(i < nblk - 1, 1.0, 0.0)
        dbuf[0:T, :] = dy_r[...]
        dbuf[T:T + CONV_HALO, :] = live * dyn_r[...]
        _dwconv(dbuf, w_r, _CONV_BWD_OFFS, T, C, None, dubuf)
        du = dubuf[...]
        sg = _sig(cg[...])
        dz_o[:, 0:C] = (du * sg).astype(BF16)
        dz_o[:, C:2 * C] = (du * cv[...] * sg * (1.0 - sg)).astype(BF16)

    return _pcall(
        body, name=name, grid=(nblk,),
        in_specs=[_rows(T, C), _halo_next(T, CONV_HALO, C, 0, S), _rows(T, C, 0), _rows(T, C, 1), _full((CONV_K, C))],
        out_specs=_rows(T, 2 * C), out_shape=jax.ShapeDtypeStruct((S, 2 * C), BF16),
        scratch_shapes=[pltpu.VMEM((T + CONV_HALO, C), F32), pltpu.VMEM((T, C), F32)],
        compiler_params=_cparams("parallel"),
    )(dy, dy, zc, zc, w)


def merge_fwd(y_a, y_b, zg, *, name):
    S, D = y_a.shape
    T = ROW_TILE

    def body(a_r, b_r, ga_r, gb_r, m_o):
        ga, gb = ga_r[...].astype(F32), gb_r[...].astype(F32)
        m_o[...] = (_sig(ga) * a_r[...].astype(F32) + _sig(gb) * b_r[...].astype(F32)).astype(BF16)

    return _pcall(
        body, name=name, grid=(S // T,),
        in_specs=[_rows(T, D), _rows(T, D), _rows(T, D, 0), _rows(T, D, 1)],
        out_specs=_rows(T, D), out_shape=jax.ShapeDtypeStruct((S, D), BF16),
        compiler_params=_cparams("parallel"),
    )(y_a, y_b, zg, zg)


def merge_bwd(d_m, y_a, y_b, zg, *, name):
    S, D = y_a.shape
    T = ROW_TILE

    def body(dm_r, a_r, b_r, ga_r, gb_r, da_o, db_o, dz_o):
        dm = dm_r[...]
        sa, sb = _sig(ga_r[...].astype(F32)), _sig(gb_r[...].astype(F32))
        da_o[...] = (dm * sa).astype(BF16)
        db_o[...] = (dm * sb).astype(BF16)
        dz_o[:, 0:D] = (dm * a_r[...].astype(F32) * sa * (1.0 - sa)).astype(BF16)
        dz_o[:, D:2 * D] = (dm * b_r[...].astype(F32) * sb * (1.0 - sb)).astype(BF16)

    shp = jax.ShapeDtypeStruct((S, D), BF16)
    return _pcall(
        body, name=name, grid=(S // T,),
        in_specs=[_rows(T, D), _rows(T, D), _rows(T, D), _rows(T, D, 0), _rows(T, D, 1)],
        out_specs=(_rows(T, D), _rows(T, D), _rows(T, 2 * D)),
        out_shape=(shp, shp, jax.ShapeDtypeStruct((S, 2 * D), BF16)),
        compiler_params=_cparams("parallel"),
    )(d_m, y_a, y_b, zg, zg)


FFN_RC = 64


def _ffn_chunks(T, F):
    for cc in range(F // LANE):
        for rc in range(T // FFN_RC):
            yield rc * FFN_RC, slice(rc * FFN_RC, (rc + 1) * FFN_RC), slice(cc * LANE, (cc + 1) * LANE)


def _ffn_fill(g_r, hg_r, gbuf, i):
    T = g_r.shape[0]
    live = jnp.where(i > 0, 1.0, 0.0)
    gbuf[0:FFN_HALO, :] = live * hg_r[...].astype(F32)
    gbuf[FFN_HALO:FFN_HALO + T, :] = g_r[...].astype(F32)


def _ffn_gate_chunk(gbuf, w_r, b_r, r0, lanes, n=FFN_RC):
    taps = [gbuf[r0 + FFN_HALO - (FFN_K - 1) + k:r0 + FFN_HALO - (FFN_K - 1) + k + n, lanes]
            for k in range(FFN_K)]
    gp = b_r[:, lanes] + w_r[0:1, lanes] * taps[0]
    for k in range(1, FFN_K):
        gp = gp + w_r[k:k + 1, lanes] * taps[k]
    return gp, taps


def _sum8(v):
    return jnp.sum(v.reshape(v.shape[0] // SUBLANES, SUBLANES, v.shape[1]), axis=0)


def ffn_act_fwd(gu, w, b, *, name):
    S = gu.shape[0]
    F = gu.shape[1] // 2
    T = ROW_TILE // 2

    def body(g_r, u_r, hg_r, w_r, b_r, a_o, gbuf):
        _ffn_fill(g_r, hg_r, gbuf, pl.program_id(0))
        for r0, rows, lanes in _ffn_chunks(T, F):
            gp, _ = _ffn_gate_chunk(gbuf, w_r, b_r, r0, lanes)
            a_o[rows, lanes] = (gp * _sig(gp) * u_r[rows, lanes].astype(F32)).astype(BF16)

    return _pcall(
        body, name=name, grid=(S // T,),
        in_specs=[_rows(T, F, 0), _rows(T, F, 1), _halo_prev(T, FFN_HALO, F, 0), _full((FFN_K, F)), _full((1, F))],
        out_specs=_rows(T, F), out_shape=jax.ShapeDtypeStruct((S, F), BF16),
        scratch_shapes=[pltpu.VMEM((FFN_HALO + T, F), F32)],
        compiler_params=_cparams("parallel"),
    )(gu, gu, gu, w, b)


def ffn_act_bwd(d_a, gu, w, b, *, name):
    S = gu.shape[0]
    F = gu.shape[1] // 2
    T = ROW_TILE // 2
    H = FFN_HALO
    nblk = S // T

    def dgp_of(gp, da, u):
        sg = _sig(gp)
        return da * u * (sg * (1.0 + gp * (1.0 - sg))), sg

    def body(da_r, dan_r, g_r, gp_r, gn_r, u_r, un_r, w_r, b_r, o_o, dw_o, db_o, gbuf, dbuf):
        i = pl.program_id(0)
        gbuf[0:H, :] = jnp.where(i > 0, 1.0, 0.0) * gp_r[...].astype(F32)
        gbuf[H:H + T, :] = g_r[...].astype(F32)
        gbuf[H + T:H + T + H, :] = gn_r[...].astype(F32)

        @pl.when(i == 0)
        def _():
            dw_o[...] = jnp.zeros_like(dw_o)
            db_o[...] = jnp.zeros_like(db_o)

        live_n = jnp.where(i < nblk - 1, 1.0, 0.0)
        sums = None
        for r0, rows, lanes in _ffn_chunks(T, F):
            if r0 == 0:
                sums = [jnp.zeros((SUBLANES, LANE), F32) for _ in range(FFN_K + 1)]
                gp_h, _ = _ffn_gate_chunk(gbuf, w_r, b_r, T, lanes, n=H)
                dgp_h, _ = dgp_of(gp_h, live_n * dan_r[:, lanes].astype(F32), un_r[:, lanes].astype(F32))
                dbuf[T:T + H, lanes] = dgp_h
            gp, taps = _ffn_gate_chunk(gbuf, w_r, b_r, r0, lanes)
            da = da_r[rows, lanes].astype(F32)
            dgp, sg = dgp_of(gp, da, u_r[rows, lanes].astype(F32))
            o_o[rows, F + lanes.start:F + lanes.stop] = (da * gp * sg).astype(BF16)
            dbuf[rows, lanes] = dgp
            sums[FFN_K] = sums[FFN_K] + _sum8(dgp)
            for k in range(FFN_K):
                sums[k] = sums[k] + _sum8(dgp * taps[k])
            if r0 + FFN_RC == T:
                db_o[:, lanes] += sums[FFN_K]
                for k in range(FFN_K):
                    dw_o[k * SUBLANES:(k + 1) * SUBLANES, lanes] += sums[k]
        for r0, rows, lanes in _ffn_chunks(T, F):
            dg = w_r[0:1, lanes] * dbuf[r0 + FFN_K - 1:r0 + FFN_K - 1 + FFN_RC, lanes]
            for k in range(1, FFN_K):
                off = r0 + FFN_K - 1 - k
                dg = dg + w_r[k:k + 1, lanes] * dbuf[off:off + FFN_RC, lanes]
            o_o[rows, lanes] = dg.astype(BF16)

    return _pcall(
        body, name=name, grid=(nblk,),
        in_specs=[_rows(T, F), _halo_next(T, H, F, 0, S), _rows(T, F, 0), _halo_prev(T, H, F, 0),
                  _halo_next(T, H, F, 0, S), _rows(T, F, 1), _halo_next(T, H, F, 1, S),
                  _full((FFN_K, F)), _full((1, F))],
        out_specs=(_rows(T, 2 * F), _full((FFN_K * SUBLANES, F)), _full((SUBLANES, F))),
        out_shape=(jax.ShapeDtypeStruct((S, 2 * F), BF16),
                   jax.ShapeDtypeStruct((FFN_K * SUBLANES, F), F32), jax.ShapeDtypeStruct((SUBLANES, F), F32)),
        scratch_shapes=[pltpu.VMEM((H + T + H, F), F32), pltpu.VMEM((T + H, F), F32)],
        compiler_params=_cparams("arbitrary"),
    )(d_a, d_a, gu, gu, gu, gu, gu, w, b)


def _row_tile(R, target=512):
    if R <= target:
        return R
    for t in range(target, 7, -8):
        if R % t == 0:
            return t
    return R


def sum_slots(land, *, name):
    _, R, C = land.shape
    T = _row_tile(R)

    def body(l_r, o_o):
        acc = l_r[0].astype(F32)
        for q in range(1, N_DEV):
            acc = acc + l_r[q].astype(F32)
        o_o[...] = acc

    return _pcall(
        body, name=name, grid=(R // T,),
        in_specs=[pl.BlockSpec((N_DEV, T, C), lambda i: (0, i, 0))],
        out_specs=_rows(T, C), out_shape=jax.ShapeDtypeStruct((R, C), F32),
        compiler_params=_cparams("parallel"),
    )(land)


def adamw(w, g, m, v, *, name):
    shape = w.shape
    C = shape[-1]
    R = math.prod(shape[:-1])
    w2, g2, m2, v2 = (t.reshape(R, C) for t in (w, g, m, v))
    T = _row_tile(R)
    c1 = 1.0 - ADAM_B1 ** ADAM_STEP
    c2 = 1.0 - ADAM_B2 ** ADAM_STEP

    def body(w_r, g_r, m_r, v_r, d_o, m_o, v_o):
        gv = g_r[...]
        mn = ADAM_B1 * m_r[...] + (1.0 - ADAM_B1) * gv
        vn = ADAM_B2 * v_r[...] + (1.0 - ADAM_B2) * (gv * gv)
        m_o[...] = mn
        v_o[...] = vn
        d_o[...] = -ADAM_LR * ((mn / c1) / (jnp.sqrt(vn / c2) + ADAM_EPS) + ADAM_WD * w_r[...])

    shp = jax.ShapeDtypeStruct((R, C), F32)
    d, mn, vn = _pcall(
        body, name=name, grid=(R // T,), in_specs=[_rows(T, C)] * 4, out_specs=(_rows(T, C),) * 3,
        out_shape=(shp,) * 3, compiler_params=_cparams("parallel"),
    )(w2, g2, m2, v2)
    return d.reshape(shape), mn.reshape(shape), vn.reshape(shape)


def _my_pos():
    return lax.axis_index("x"), lax.axis_index("y"), lax.axis_index("c")


class _Exchange:
    def __init__(self, inputs, out_shapes, sems, start, finish):
        self.inputs, self.out_shapes, self.sems, self.start, self.finish = inputs, out_shapes, sems, start, finish


def gather_exchange(shards):
    n = len(shards)

    def plan(ins, outs, sems):
        send_sems, recv_sems, local_sems = sems
        x, y, c = _my_pos()
        me, sibling = (x, y, c), (x, y, 1 - c)
        chips = [(1 - x, y), (x, 1 - y), (1 - x, 1 - y)]

        def slot(i, p):
            return outs[i].at[4 * p[0] + 2 * p[1] + p[2]]

        def copy(k, i, block, to, src=None):
            return pltpu.make_async_remote_copy(
                src_ref=slot(i, block) if src is None else src, dst_ref=slot(i, block),
                send_sem=send_sems.at[k, i], recv_sem=recv_sems.at[k, i], device_id=to, device_id_type=MESH)

        mine = [pltpu.make_async_copy(ins[i], slot(i, me), local_sems.at[i]) for i in range(n)]
        first = []
        for i in range(n):
            first.append(copy(0, i, me, sibling, src=ins[i]))
            first += [copy(1 + j, i, me, (*chip, c), src=ins[i]) for j, chip in enumerate(chips)]
        return me, sibling, chips, c, copy, mine, first

    def start(ins, outs, sems):
        _, _, _, _, _, mine, first = plan(ins, outs, sems)
        for cp in mine + first:
            cp.start()

    def finish(ins, outs, sems):
        me, sibling, chips, c, copy, mine, first = plan(ins, outs, sems)
        passed = []
        for j, chip in enumerate(chips):
            for i in range(n):
                copy(1 + j, i, (*chip, c), me).wait_recv()
                cp = copy(4 + j, i, (*chip, c), sibling)
                cp.start()
                passed.append(cp)
        for i in range(n):
            copy(0, i, sibling, me).wait_recv()
            for j, chip in enumerate(chips):
                copy(4 + j, i, (*chip, 1 - c), me).wait_recv()
        for cp in first + passed:
            cp.wait_send()
        for cp in mine:
            cp.wait()

    outs = [jax.ShapeDtypeStruct((N_DEV,) + s.shape, s.dtype) for s in shards]
    sems = [pltpu.SemaphoreType.DMA((7, n)), pltpu.SemaphoreType.DMA((7, n)), pltpu.SemaphoreType.DMA((n,))]
    return _Exchange(list(shards), outs, sems, start, finish)


def scatter_exchange(gs):
    n = len(gs)

    def plan(ins, outs, sems):
        send_sems, recv_sems, local_sems = sems
        x, y, c = _my_pos()
        me_id = 4 * x + 2 * y + c
        mine = [pltpu.make_async_copy(ins[i].at[me_id], outs[i].at[me_id], local_sems.at[i]) for i in range(n)]
        sends, recvs = [], []
        for msk in range(1, N_DEV):
            px = 1 - x if msk & 4 else x
            py = 1 - y if msk & 2 else y
            pc = 1 - c if msk & 1 else c
            pid = 4 * px + 2 * py + pc
            for i in range(n):
                sends.append(pltpu.make_async_remote_copy(
                    src_ref=ins[i].at[pid], dst_ref=outs[i].at[me_id],
                    send_sem=send_sems.at[msk - 1, i], recv_sem=recv_sems.at[msk - 1, i],
                    device_id=(px, py, pc), device_id_type=MESH))
                recvs.append(pltpu.make_async_remote_copy(
                    src_ref=ins[i].at[pid], dst_ref=outs[i].at[pid],
                    send_sem=send_sems.at[msk - 1, i], recv_sem=recv_sems.at[msk - 1, i],
                    device_id=(px, py, pc), device_id_type=MESH))
        return mine, sends, recvs

    def start(ins, outs, sems):
        mine, sends, _ = plan(ins, outs, sems)
        for cp in mine + sends:
            cp.start()

    def finish(ins, outs, sems):
        mine, sends, recvs = plan(ins, outs, sems)
        for rv in recvs:
            rv.wait_recv()
        for cp in sends:
            cp.wait_send()
        for cp in mine:
            cp.wait()

    outs = [jax.ShapeDtypeStruct(g.shape, g.dtype) for g in gs]
    sems = [pltpu.SemaphoreType.DMA((7, n)), pltpu.SemaphoreType.DMA((7, n)), pltpu.SemaphoreType.DMA((n,))]
    return _Exchange(list(gs), outs, sems, start, finish)


def run_exchange(ex, *, name):
    HBM = pl.BlockSpec(memory_space=pl.ANY)
    n_in, n_out = len(ex.inputs), len(ex.out_shapes)

    def body(*refs):
        ins, outs, sems = refs[:n_in], refs[n_in:n_in + n_out], refs[n_in + n_out:]
        ex.start(ins, outs, sems)
        ex.finish(ins, outs, sems)

    return list(_pcall(body, name=name, in_specs=[HBM] * n_in, out_specs=tuple([HBM] * n_out),
                       out_shape=tuple(ex.out_shapes), scratch_shapes=ex.sems)(*ex.inputs))


def all_gather(xs, *, name):
    outs = run_exchange(gather_exchange([xs[l] for l in range(xs.shape[0])]), name=name)
    return jnp.stack(outs)


def _slots(g):
    return g.reshape(N_DEV, g.shape[0] // N_DEV, g.shape[1])


def _rope_tables(positions):
    half = ROT_DIM // 2
    inv_freq = ROPE_THETA ** (-jnp.arange(0, ROT_DIM, 2, dtype=F32) / ROT_DIM)
    ang = positions.astype(F32)[:, None] * inv_freq
    cos, sin = jnp.cos(ang), jnp.sin(ang)
    S = positions.shape[0]
    c_t = jnp.concatenate([cos, cos, jnp.ones((S, LANE - ROT_DIM), F32)], axis=1)
    s1_t = jnp.concatenate([-sin, jnp.zeros((S, LANE - half), F32)], axis=1)
    s2_t = jnp.concatenate([jnp.zeros((S, half), F32), sin, jnp.zeros((S, LANE - ROT_DIM), F32)], axis=1)
    return c_t, s1_t, s2_t


def _row(v):
    return v.reshape(1, -1)


def kernel(x, c, positions, w_ada, b_ada, g_norm1, w_in, g_q, g_k, w_attn_proj, w_conv_dw, b_conv_dw, g_conv_ln, b_conv_ln, w_conv_out, w_o, g_norm2, w_ffn_in, w_ffn_dw, b_ffn_dw, w_ffn_down, loss_target, m_w_ada, m_b_ada, m_g_norm1, m_w_in, m_g_q, m_g_k, m_w_attn_proj, m_w_conv_dw, m_b_conv_dw, m_g_conv_ln, m_b_conv_ln, m_w_conv_out, m_w_o, m_g_norm2, m_w_ffn_in, m_w_ffn_dw, m_b_ffn_dw, m_w_ffn_down, v_w_ada, v_b_ada, v_g_norm1, v_w_in, v_g_q, v_g_k, v_w_attn_proj, v_w_conv_dw, v_b_conv_dw, v_g_conv_ln, v_b_conv_ln, v_w_conv_out, v_w_o, v_g_norm2, v_w_ffn_in, v_w_ffn_dw, v_b_ffn_dw, v_w_ffn_down):
    L = w_in.shape[0]
    S, D = x.shape[1], x.shape[2]
    FF = w_ffn_down.shape[1] * N_DEV
    xi, yi, ci = _my_pos()
    me = 4 * xi + 2 * yi + ci
    x0 = x[0]
    tabs = _rope_tables(positions[0])

    c_act = c * _sig(c)
    c_all = all_gather(jnp.pad(c_act, ((0, 7), (0, 0)))[None], name="ag_c")[0][:, 0, :]
    c_all16 = jnp.pad(c_all, ((0, 8), (0, 0)))
    m_part = jnp.stack([mm(c_all16, w_ada[l], "nn", name="mod_mm") for l in range(L)])
    m_all = all_gather(m_part, name="ag_mod")
    mod = lax.dynamic_index_in_dim(m_all, me, axis=2, keepdims=False).reshape(L, 6 * D) + b_ada
    mod = mod.reshape(L, 6, 1, D)

    sh_in = jnp.transpose(w_in, (0, 2, 1)).astype(BF16)
    sh_fi = jnp.transpose(w_ffn_in, (0, 2, 1)).astype(BF16)
    sh_ap = jnp.transpose(w_attn_proj, (0, 2, 1)).astype(BF16)
    sh_co, sh_oo, sh_dn = w_conv_out.astype(BF16), w_o.astype(BF16), w_ffn_down.astype(BF16)

    def rowcat(g):
        return g.reshape(N_DEV * g.shape[1], g.shape[2])

    wt_in, wt_fi, wt_ap, w_co, w_oo, w_dn = ([None] * L for _ in range(6))
    wt_in[0] = rowcat(run_exchange(gather_exchange([sh_in[0]]), name="ag_w0")[0])
    cdw = all_gather(jnp.pad(w_conv_dw, ((0, 0), (0, 1), (0, 0))), name="ag_cdw")
    cdw = jnp.transpose(cdw, (0, 2, 1, 3)).reshape(L, 32, D)[:, :CONV_K]
    fsh = w_ffn_dw.shape[2]
    fpad = -fsh % LANE
    fdw = all_gather(jnp.pad(w_ffn_dw, ((0, 0), (0, 8 - FFN_K), (0, fpad))), name="ag_fdw")
    fdw = jnp.transpose(fdw[:, :, :FFN_K, :fsh], (0, 2, 1, 3)).reshape(L, FFN_K, FF)

    QKV, CW = 3 * ATTN_W, 2 * D
    seg = ((0, QKV), (QKV, CW), (QKV + CW, 2 * D))

    saved = []
    x_prev, delta, gt_prev = x0, None, None
    for l in range(L):
        sh1, sc1, gt1, sh2, sc2, gt2 = (mod[l, i] for i in range(6))
        x_l, h = norm_mod_fwd(x_prev, delta, gt_prev, _row(g_norm1[l]), sc1, sh1, name="norm_fwd")
        nxt = l + 1 < L
        if l == 0:
            zq, got = mm(h, wt_in[0], "nt", b_off=0, b_len=QKV, name="z_mm_ag0",
                         comm=gather_exchange([sh_fi[0], sh_ap[0], sh_co[0], sh_oo[0], sh_dn[0]]))
            wt_fi[0], wt_ap[0], w_co[0], w_oo[0], w_dn[0] = (rowcat(g) for g in got)
        elif nxt:
            zq, got = mm(h, wt_in[l], "nt", b_off=0, b_len=QKV, name="z_mm_ag", comm=gather_exchange([sh_in[l + 1]]))
            wt_in[l + 1] = rowcat(got[0])
        else:
            zq = mm(h, wt_in[l], "nt", b_off=0, b_len=QKV, name="z_mm")
        if l == 0 and nxt:
            zc, got = mm(h, wt_in[0], "nt", b_off=seg[1][0], b_len=seg[1][1], name="zc_mm_ag",
                         comm=gather_exchange([sh_in[1]]))
            wt_in[1] = rowcat(got[0])
        else:
            zc = mm(h, wt_in[l], "nt", b_off=seg[1][0], b_len=seg[1][1], name="z_mm")
        zg = mm(h, wt_in[l], "nt", b_off=seg[2][0], b_len=seg[2][1], out_dtype=BF16, name="zg_mm")
        gq, gk = _row(g_q[l]), _row(g_k[l])
        qn, kn = qk_prep_fwd(zq, gq, gk, tabs, name="qk_prep")
        o_g, lse_g = [], []
        for gi in range(3):
            o_i, l_i = attn_fwd(qn, kn, zq, gi, name="attn_fwd%d" % gi)
            o_g.append(o_i)
            lse_g.append(l_i)
        attn = combine_fwd(o_g, lse_g, name="combine_fwd")
        y_a = mm(attn, wt_ap[l], "nt", out_dtype=BF16, name="ya_mm")
        cw, cb = cdw[l], _row(b_conv_dw[l])
        cg, cbl = _row(g_conv_ln[l]), _row(b_conv_ln[l])
        u2, y_conv = convb_fwd(zc, cw, cb, cg, cbl, name="convb_fwd")
        y_b = mm(u2, w_co[l], "nn", out_dtype=BF16, name="yb_mm")
        merged = merge_fwd(y_a, y_b, zg, name="merge_fwd")
        mo = mm(merged, w_oo[l], "nn", name="mo_mm")
        x_mid, h2 = norm_mod_fwd(x_l, mo, gt1, _row(g_norm2[l]), sc2, sh2, name="norm_fwd")
        if nxt:
            gu, got = mm(h2, wt_fi[l], "nt", out_dtype=BF16, name="gu_mm_ag", comm=gather_exchange([sh_fi[l + 1]]))
            wt_fi[l + 1] = rowcat(got[0])
        else:
            gu = mm(h2, wt_fi[l], "nt", out_dtype=BF16, name="gu_mm")
        fw, fb = fdw[l], _row(b_ffn_dw[l])
        act = ffn_act_fwd(gu, fw, fb, name="ffn_act")
        if nxt:
            ffo, got = mm(act, w_dn[l], "nn", name="ffo_mm_ag",
                          comm=gather_exchange([sh_ap[l + 1], sh_co[l + 1], sh_oo[l + 1], sh_dn[l + 1]]))
            wt_ap[l + 1], w_co[l + 1], w_oo[l + 1], w_dn[l + 1] = (rowcat(g) for g in got)
        else:
            ffo = mm(act, w_dn[l], "nn", name="ffo_mm")
        saved.append(dict(x=x_l, h=h, zq=zq, zc=zc, zg=zg, qn=qn, kn=kn, o=o_g, lse=lse_g, attn=attn, y_a=y_a,
                          u2=u2, y_conv=y_conv, y_b=y_b, merged=merged, mo=mo, x_mid=x_mid, h2=h2, gu=gu, act=act, ffo=ffo))
        x_prev, delta, gt_prev = x_mid, ffo, gt2

    dx, lpart, d_ffo, p_gt2 = loss_head(x_prev, delta, gt_prev, loss_target[0], name="loss_head")
    loss = lax.psum(0.5 / D * jnp.sum(lpart[0]), ("x", "y", "c"))

    land = {k: [None] * L for k in ("in", "fi", "ap", "co", "o", "dn")}
    small_rows = []
    for l in reversed(range(L)):
        sv = saved[l]
        sh1, sc1, gt1, sh2, sc2, gt2 = (mod[l, i] for i in range(6))
        d_act = mm(d_ffo, w_dn[l], "nt", out_dtype=BF16, name="dact_mm")
        g_dn = mm(sv["act"], d_ffo, "tn", out_dtype=BF16, name="dwdn_mm")
        fw, fb = fdw[l], _row(b_ffn_dw[l])
        dgu, p_fw, p_fb = ffn_act_bwd(d_act, sv["gu"], fw, fb, name="ffn_bwd")
        dh2, got = mm(dgu, wt_fi[l], "nn", name="dh2_mm_rs", comm=scatter_exchange([_slots(g_dn)]))
        land["dn"][l] = got[0]
        g_fi = mm(dgu, sv["h2"], "tn", out_dtype=BF16, name="dwfi_mm")
        dx, p_g2, p_sc2, p_sh2, d_mo, p_gt1 = norm_mod_bwd(sv["x_mid"], dh2, _row(g_norm2[l]), sc2, sh2, dx,
                                                           (sv["mo"], gt1), name="norm_bwd")
        d_merged = mm(d_mo, w_oo[l], "nt", name="dmerged_mm")
        g_o = mm(sv["merged"], d_mo, "tn", out_dtype=BF16, name="dwo_mm")
        d_ya, d_yb, dzg = merge_bwd(d_merged, sv["y_a"], sv["y_b"], sv["zg"], name="merge_bwd")
        d_attn = mm(d_ya, wt_ap[l], "nn", name="dattn_mm")
        g_ap = mm(d_ya, sv["attn"], "tn", out_dtype=BF16, name="dwap_mm")
        d_u2 = mm(d_yb, w_co[l], "nt", name="du2_mm")
        g_co = mm(sv["u2"], d_yb, "tn", out_dtype=BF16, name="dwco_mm")
        cw, cb = cdw[l], _row(b_conv_dw[l])
        cg, cbl = _row(g_conv_ln[l]), _row(b_conv_ln[l])
        dy, p_cw, p_cb, p_cg, p_cbl = convb_bwd1(d_u2, sv["y_conv"], sv["zc"], cg, cbl, name="convb_bwd1")
        dzc = convb_bwd2(dy, sv["zc"], cw, name="convb_bwd2")
        dd = combine_bwd(d_attn, sv["o"], sv["lse"], name="combine_bwd")
        do_g, cc_g = dd[:3], dd[3:]
        parts = [attn_bwd(sv["qn"], sv["kn"], sv["zq"], do_g[gi], sv["lse"][gi], cc_g[gi], gi,
                          name="attn_bwd%d" % gi) for gi in range(3)]
        gq, gk = _row(g_q[l]), _row(g_k[l])
        dzq, p_gq, p_gk = attn_bwd_post(sv["zq"], gq, gk, tabs, *[[p[i] for p in parts] for i in range(5)],
                                        name="attn_post")
        g_in_q, got = mm(dzq, sv["h"], "tn", out_dtype=BF16, name="dwin_mm_rs",
                         comm=scatter_exchange([_slots(g_fi), _slots(g_o), _slots(g_ap), _slots(g_co)]))
        land["fi"][l], land["o"][l], land["ap"][l], land["co"][l] = got
        g_in = jnp.concatenate([g_in_q] + [mm(dz_s, sv["h"], "tn", out_dtype=BF16, name="dwin_mm")
                                           for dz_s in (dzc, dzg)], axis=0)
        dh, got = mm(dzq, wt_in[l], "nn", b_off=0, name="dh_mm_rs", comm=scatter_exchange([_slots(g_in)]))
        land["in"][l] = got[0]
        for dz_s, (o, n) in zip((dzc, dzg), seg[1:]):
            dh = mm(dz_s, wt_in[l], "nn", b_off=o, c_in=dh, name="dh_mm")
        if l > 0:
            dx, p_g1, p_sc1, p_sh1, d_ffo_prev, p_gt2_prev = norm_mod_bwd(
                sv["x"], dh, _row(g_norm1[l]), sc1, sh1, dx, (saved[l - 1]["ffo"], mod[l - 1, 5]), name="norm_bwd")
        else:
            dx, p_g1, p_sc1, p_sh1 = norm_mod_bwd(sv["x"], dh, _row(g_norm1[l]), sc1, sh1, dx, name="norm_bwd")

        def row1k(p):
            v = p[0]
            pad = -v.shape[0] % D
            return jnp.pad(v, (0, pad)).reshape(-1, D)

        rows = [row1k(p) for p in (p_sh1, p_sc1, p_gt1, p_sh2, p_sc2, p_gt2, p_g1, p_g2)]
        rows.append(row1k(jnp.concatenate([p_gq, p_gk], axis=1)))
        rows += [row1k(p) for p in (p_cb, p_cg, p_cbl, jnp.sum(p_fb, axis=0, keepdims=True))]
        rows.append(jnp.sum(p_cw.reshape(CONV_K, SUBLANES, D), axis=1))
        rows += [row1k(jnp.sum(p_fw[k * SUBLANES:(k + 1) * SUBLANES], axis=0, keepdims=True))
                 for k in range(FFN_K)]
        blk = jnp.concatenate(rows, axis=0)
        small_rows.append(jnp.pad(blk, ((0, -blk.shape[0] % 8), (0, 0))))
        if l > 0:
            d_ffo, p_gt2 = d_ffo_prev, p_gt2_prev
    small_rows = small_rows[::-1]
    n_small = small_rows[0].shape[0]
    ff_rows = -(-FF // D)

    small = jnp.concatenate(small_rows, axis=0)[None]
    small_all = all_gather(small, name="ag_small")[0]
    small_sum = sum_slots(small_all, name="sum_small").reshape(L, n_small, D)
    small_all = small_all.reshape(N_DEV, L, n_small, D)

    g_b_ada = small_sum[:, 0:6].reshape(L, 6 * D)
    g_g1, g_g2 = small_sum[:, 6], small_sum[:, 7]
    g_gq, g_gk = small_sum[:, 8, 0:LANE], small_sum[:, 8, LANE:2 * LANE]
    g_cb, g_cg, g_cbl = small_sum[:, 9], small_sum[:, 10], small_sum[:, 11]
    r0 = 12
    g_fb = small_sum[:, r0:r0 + ff_rows].reshape(L, -1)[:, :FF]
    r0 += ff_rows
    g_cw_full = small_sum[:, r0:r0 + CONV_K]
    r0 += CONV_K
    g_fw_full = small_sum[:, r0:r0 + FFN_K * ff_rows].reshape(L, FFN_K, -1)[:, :, :FF]
    csh = w_conv_dw.shape[2]
    g_cw = lax.dynamic_slice_in_dim(g_cw_full, me * csh, csh, axis=2)
    g_fw = lax.dynamic_slice_in_dim(g_fw_full, me * fsh, fsh, axis=2)

    ash = w_ada.shape[2]
    dmod_all = small_all[:, :, 0:6].reshape(N_DEV, L, 6 * D)
    dmod_mine = lax.dynamic_slice_in_dim(dmod_all, me * ash, ash, axis=2)
    g_w_ada = jnp.stack([mm(c_all16, jnp.pad(dmod_mine[:, l], ((0, 8), (0, 0))), "tn", name="dwada_mm")
                         for l in range(L)])

    def reduced(key, transposed):
        out = jnp.stack([sum_slots(slots, name="sum_" + key) for slots in land[key]])
        return jnp.transpose(out, (0, 2, 1)) if transposed else out

    g_w_in = reduced("in", True)
    g_w_fi = reduced("fi", True)
    g_w_ap = reduced("ap", True)
    g_w_co = reduced("co", False)
    g_w_o = reduced("o", False)
    g_w_dn = reduced("dn", False)

    grads = [g_w_ada, g_b_ada, g_g1, g_w_in, g_gq, g_gk, g_w_ap, g_cw, g_cb, g_cg, g_cbl, g_w_co, g_w_o, g_g2,
             g_w_fi, g_fw, g_fb, g_w_dn]
    ws = [w_ada, b_ada, g_norm1, w_in, g_q, g_k, w_attn_proj, w_conv_dw, b_conv_dw, g_conv_ln, b_conv_ln,
          w_conv_out, w_o, g_norm2, w_ffn_in, w_ffn_dw, b_ffn_dw, w_ffn_down]
    ms = [m_w_ada, m_b_ada, m_g_norm1, m_w_in, m_g_q, m_g_k, m_w_attn_proj, m_w_conv_dw, m_b_conv_dw, m_g_conv_ln,
          m_b_conv_ln, m_w_conv_out, m_w_o, m_g_norm2, m_w_ffn_in, m_w_ffn_dw, m_b_ffn_dw, m_w_ffn_down]
    vs = [v_w_ada, v_b_ada, v_g_norm1, v_w_in, v_g_q, v_g_k, v_w_attn_proj, v_w_conv_dw, v_b_conv_dw, v_g_conv_ln,
          v_b_conv_ln, v_w_conv_out, v_w_o, v_g_norm2, v_w_ffn_in, v_w_ffn_dw, v_b_ffn_dw, v_w_ffn_down]
    deltas, new_m, new_v = [], [], []
    for w_i, g_i, m_i, v_i in zip(ws, grads, ms, vs):
        d_i, mn_i, vn_i = adamw(w_i, g_i, m_i, v_i, name="adamw")
        deltas.append(d_i)
        new_m.append(mn_i)
        new_v.append(vn_i)
    return (loss, dx[None], *grads, *deltas, *new_m, *new_v)
```

```python
import functools
import math

import jax
import jax.numpy as jnp
from jax import lax
from jax.experimental import pallas as pl
from jax.experimental.pallas import tpu as pltpu

F32 = jnp.float32
BF16 = jnp.bfloat16
MESH = pl.DeviceIdType.MESH
N_DEV = 8

EPS = 1e-6
HEAD_DIM = 128
BLOCK = 128
DILATIONS = (1, 4, 16)
HEADS_PER_GROUP = 4
N_HEADS = 12
ATTN_W = N_HEADS * HEAD_DIM
ROT_DIM = 32
ROPE_THETA = 500000.0
CONV_K = 31
CONV_HALO = 32
FFN_K = 3
FFN_HALO = 16
NEG = -1e30

ADAM_LR, ADAM_B1, ADAM_B2, ADAM_EPS, ADAM_WD, ADAM_STEP = 0.001, 0.9, 0.999, 1e-08, 0.01, 10

LANE = 128
VMEM_LIMIT = 56 * 1024 * 1024
ROW_TILE = 512
POST_TILE = 256
GLUE_RC = 64


def _pcall(body, **kw):
    return pl.pallas_call(body, **kw)


def _cparams(*sem):
    return pltpu.CompilerParams(dimension_semantics=sem, vmem_limit_bytes=VMEM_LIMIT)


def _sig(v):
    return 1.0 / (1.0 + jnp.exp(-v))


def _divtile(dim, target):
    best = None
    for t in range(LANE, min(dim, target) + 1, LANE):
        if dim % t == 0:
            best = t
    return best or dim


def _rows(t, c, col=0):
    return pl.BlockSpec((t, c), lambda i: (i, col))


def _full(shape):
    nd = len(shape)
    return pl.BlockSpec(shape, lambda i: (0,) * nd)


def _acc_rows(ref, val, i):
    @pl.when(i == 0)
    def _():
        ref[...] = jnp.zeros_like(ref)
    r = val.shape[0]
    ref[0:r, :] += val


MM_TILE = 1536


def mm(a, b, mode, *, name, out_dtype=F32, c_in=None, b_off=0, b_len=None, comm=None):
    if mode == "nn":
        M, K = a.shape
        N = b.shape[1]
    elif mode == "nt":
        M, K = a.shape
        N = b_len if b_len is not None else b.shape[0]
    else:
        K, M = a.shape
        N = b.shape[1]
    g_n = math.gcd(N, b_off) if (mode == "nt" and b_off) else N
    g_k = math.gcd(K, b_off) if (mode == "nn" and b_off) else K
    tn = _divtile(g_n, MM_TILE if c_in is None else 1024)
    tk = _divtile(g_k, MM_TILE if mode != "tn" else 2048)
    tm = _divtile(M, MM_TILE if mode == "tn" else (2048 if c_in is None else 1024))
    gm, gn, nk = M // tm, N // tn, K // tk
    n_ci = 0 if comm is None else len(comm.inputs)
    n_co = 0 if comm is None else len(comm.out_shapes)
    n_x = 2 + (c_in is not None)
    if mode == "nn":
        dims = (((1,), (0,)), ((), ()))
    elif mode == "nt":
        dims = (((1,), (1,)), ((), ()))
    else:
        dims = (((0,), (0,)), ((), ()))

    def body(*refs):
        a_ref, b_ref = refs[0], refs[1]
        c_ref = refs[2] if c_in is not None else None
        c_ins = refs[n_x:n_x + n_ci]
        o_ref = refs[n_x + n_ci]
        c_outs = refs[n_x + n_ci + 1:n_x + n_ci + 1 + n_co]
        rest = refs[n_x + n_ci + 1 + n_co:]
        acc = rest[0] if nk > 1 else None
        sems = rest[1:] if nk > 1 else rest
        i, j, k = pl.program_id(0), pl.program_id(1), pl.program_id(2)

        if comm is not None:
            @pl.when(jnp.logical_and(jnp.logical_and(i == 0, j == 0), k == 0))
            def _():
                comm.start(c_ins, c_outs, sems)

        prod = lax.dot_general(a_ref[...].astype(BF16), b_ref[...].astype(BF16), dims, preferred_element_type=F32)
        if nk == 1:
            if c_ref is not None:
                prod = prod + c_ref[...].astype(F32)
            o_ref[...] = prod.astype(out_dtype)
        else:
            @pl.when(k == 0)
            def _():
                if c_ref is None:
                    acc[...] = prod
                else:
                    acc[...] = prod + c_ref[...].astype(F32)

            @pl.when(k > 0)
            def _():
                acc[...] += prod

            @pl.when(k == nk - 1)
            def _():
                o_ref[...] = acc[...].astype(out_dtype)

        if comm is not None:
            @pl.when(jnp.logical_and(jnp.logical_and(i == gm - 1, j == gn - 1), k == nk - 1))
            def _():
                comm.finish(c_ins, c_outs, sems)

    if mode == "nn":
        a_spec = pl.BlockSpec((tm, tk), lambda i, j, k: (i, k))
        ob = b_off // tk
        b_spec = pl.BlockSpec((tk, tn), lambda i, j, k: (k + ob, j))
    elif mode == "nt":
        a_spec = pl.BlockSpec((tm, tk), lambda i, j, k: (i, k))
        ob = b_off // tn
        b_spec = pl.BlockSpec((tn, tk), lambda i, j, k: (j + ob, k))
    else:
        a_spec = pl.BlockSpec((tk, tm), lambda i, j, k: (k, i))
        b_spec = pl.BlockSpec((tk, tn), lambda i, j, k: (k, j))
    o_spec = pl.BlockSpec((tm, tn), lambda i, j, k: (i, j))
    HBM = pl.BlockSpec(memory_space=pl.ANY)
    in_specs = [a_spec, b_spec]
    args = [a, b]
    if c_in is not None:
        in_specs.append(o_spec)
        args.append(c_in)
    scratch = [pltpu.VMEM((tm, tn), F32)] if nk > 1 else []
    o_shape = jax.ShapeDtypeStruct((M, N), out_dtype)
    if comm is None:
        return _pcall(
            body, name=name, grid=(gm, gn, nk), in_specs=in_specs, out_specs=o_spec, out_shape=o_shape,
            scratch_shapes=scratch, compiler_params=_cparams("parallel", "parallel", "arbitrary"),
        )(*args)
    res = _pcall(
        body, name=name, grid=(gm, gn, nk), in_specs=in_specs + [HBM] * n_ci,
        out_specs=(o_spec, *[HBM] * n_co), out_shape=(o_shape, *comm.out_shapes),
        scratch_shapes=scratch + comm.sems, compiler_params=_cparams("arbitrary", "arbitrary", "arbitrary"),
    )(*args, *comm.inputs)
    return res[0], list(res[1:])


def mm_kcat(a_list, b, *, name, comm=None):
    M = a_list[0].shape[0]
    ks = [a.shape[1] for a in a_list]
    N = b.shape[1]
    tk = _divtile(math.gcd(*ks), 1024)
    tm, tn = _divtile(M, 2048), _divtile(N, 1024)
    nks = [k // tk for k in ks]
    starts = [sum(nks[:t]) for t in range(len(ks))]
    nk = sum(nks)
    gm, gn = M // tm, N // tn
    n_a = len(a_list)
    n_ci = 0 if comm is None else len(comm.inputs)
    n_co = 0 if comm is None else len(comm.out_shapes)

    def body(*refs):
        a_refs, b_ref = refs[:n_a], refs[n_a]
        c_ins = refs[n_a + 1:n_a + 1 + n_ci]
        o_ref = refs[n_a + 1 + n_ci]
        c_outs = refs[n_a + 2 + n_ci:n_a + 2 + n_ci + n_co]
        acc = refs[n_a + 2 + n_ci + n_co]
        sems = refs[n_a + 3 + n_ci + n_co:]
        i, j, k = pl.program_id(0), pl.program_id(1), pl.program_id(2)

        if comm is not None:
            @pl.when(jnp.logical_and(jnp.logical_and(i == 0, j == 0), k == 0))
            def _():
                comm.start(c_ins, c_outs, sems)

        for t in range(n_a):
            @pl.when(jnp.logical_and(k >= starts[t], k < starts[t] + nks[t]))
            def _(t=t):
                prod = jnp.dot(a_refs[t][...].astype(BF16), b_ref[...].astype(BF16), preferred_element_type=F32)
                if t == 0:
                    @pl.when(k == 0)
                    def _():
                        acc[...] = prod

                    @pl.when(k > 0)
                    def _():
                        acc[...] += prod
                else:
                    acc[...] += prod

        @pl.when(k == nk - 1)
        def _():
            o_ref[...] = acc[...]

        if comm is not None:
            @pl.when(jnp.logical_and(jnp.logical_and(i == gm - 1, j == gn - 1), k == nk - 1))
            def _():
                comm.finish(c_ins, c_outs, sems)

    a_specs = [pl.BlockSpec((tm, tk), functools.partial(
        lambda i, j, k, s, n: (i, jnp.clip(k - s, 0, n - 1)), s=starts[t], n=nks[t])) for t in range(n_a)]
    b_spec = pl.BlockSpec((tk, tn), lambda i, j, k: (k, j))
    o_spec = pl.BlockSpec((tm, tn), lambda i, j, k: (i, j))
    HBM = pl.BlockSpec(memory_space=pl.ANY)
    o_shape = jax.ShapeDtypeStruct((M, N), F32)
    scratch = [pltpu.VMEM((tm, tn), F32)]
    if comm is None:
        return _pcall(
            body, name=name, grid=(gm, gn, nk), in_specs=a_specs + [b_spec], out_specs=o_spec, out_shape=o_shape,
            scratch_shapes=scratch, compiler_params=_cparams("parallel", "parallel", "arbitrary"),
        )(*a_list, b)
    res = _pcall(
        body, name=name, grid=(gm, gn, nk), in_specs=a_specs + [b_spec] + [HBM] * n_ci,
        out_specs=(o_spec, *[HBM] * n_co), out_shape=(o_shape, *comm.out_shapes),
        scratch_shapes=scratch + comm.sems, compiler_params=_cparams("arbitrary", "arbitrary", "arbitrary"),
    )(*a_list, b, *comm.inputs)
    return res[0], list(res[1:])


def norm_mod_fwd(x_prev, delta, gt, g, sc, sh, *, name):
    S, D = x_prev.shape
    T = ROW_TILE
    has_delta = delta is not None

    def body(*refs):
        if has_delta:
            xp, dl, gt_r, g_r, sc_r, sh_r, x_out, h_out = refs
            xv = xp[...] + gt_r[...] * dl[...]
            x_out[...] = xv
        else:
            xp, g_r, sc_r, sh_r, h_out = refs
            xv = xp[...]
        r = lax.rsqrt(jnp.mean(xv * xv, axis=-1, keepdims=True) + EPS)
        h_out[...] = ((xv * r) * g_r[...] * (1.0 + sc_r[...]) + sh_r[...]).astype(BF16)

    vec = _full((1, D))
    if has_delta:
        ins, specs = [x_prev, delta, gt, g, sc, sh], [_rows(T, D), _rows(T, D), vec, vec, vec, vec]
        outs = (jax.ShapeDtypeStruct((S, D), F32), jax.ShapeDtypeStruct((S, D), BF16))
        ospecs = (_rows(T, D), _rows(T, D))
    else:
        ins, specs = [x_prev, g, sc, sh], [_rows(T, D), vec, vec, vec]
        outs = jax.ShapeDtypeStruct((S, D), BF16)
        ospecs = _rows(T, D)
    res = _pcall(body, name=name, grid=(S // T,), in_specs=specs, out_specs=ospecs, out_shape=outs,
                 compiler_params=_cparams("parallel"))(*ins)
    return res if has_delta else (x_prev, res)


def norm_mod_bwd(x, dh, g, sc, sh, dx_res, res=None, *, name):
    S, D = x.shape
    T = ROW_TILE
    has_res = res is not None

    def body(*refs):
        x_r, dh_r, g_r, sc_r, sh_r, dr_r = refs[:6]
        dx_o, dg_o, dsc_o, dsh_o = refs[6 + 2 * has_res:10 + 2 * has_res]
        i = pl.program_id(0)
        xv = x_r[...]
        dh_v = dh_r[...]
        r = lax.rsqrt(jnp.mean(xv * xv, axis=-1, keepdims=True) + EPS)
        xh = xv * r
        dn = dh_v * (1.0 + sc_r[...])
        dxh = dn * g_r[...]
        dx = dr_r[...] + r * (dxh - xh * jnp.mean(dxh * xh, axis=-1, keepdims=True))
        dx_o[...] = dx
        _acc_rows(dg_o, jnp.sum(dn * xh, axis=0, keepdims=True), i)
        _acc_rows(dsc_o, jnp.sum(dh_v * (xh * g_r[...]), axis=0, keepdims=True), i)
        _acc_rows(dsh_o, jnp.sum(dh_v, axis=0, keepdims=True), i)
        if has_res:
            dl_r, gt_r = refs[6:8]
            dd_o, dgt_o = refs[12:14]
            dd_o[...] = (dx * gt_r[...]).astype(BF16)
            _acc_rows(dgt_o, jnp.sum(dx * dl_r[...], axis=0, keepdims=True), i)

    vec = _full((1, D))
    part = jax.ShapeDtypeStruct((8, D), F32)
    in_specs = [_rows(T, D), _rows(T, D), vec, vec, vec, _rows(T, D)]
    out_specs = [_rows(T, D), _full((8, D)), _full((8, D)), _full((8, D))]
    out_shape = [jax.ShapeDtypeStruct((S, D), F32), part, part, part]
    args = [x, dh, g, sc, sh, dx_res]
    if has_res:
        in_specs += [_rows(T, D), vec]
        out_specs += [_rows(T, D), _full((8, D))]
        out_shape += [jax.ShapeDtypeStruct((S, D), BF16), part]
        args += list(res)
    return _pcall(
        body, name=name, grid=(S // T,), in_specs=in_specs, out_specs=tuple(out_specs), out_shape=tuple(out_shape),
        compiler_params=_cparams("arbitrary"),
    )(*args)


def loss_head(x_mid, ffo, gt, target, *, name):
    S, D = x_mid.shape
    T = ROW_TILE

    def body(x_r, f_r, gt_r, t_r, dy_o, l_o, dd_o, dgt_o):
        i = pl.program_id(0)
        fv = f_r[...]
        e = x_r[...] + gt_r[...] * fv - t_r[...]
        dy = e * (1.0 / D)
        dy_o[...] = dy
        _acc_rows(l_o, jnp.sum(e * e, axis=0, keepdims=True), i)
        dd_o[...] = (dy * gt_r[...]).astype(BF16)
        _acc_rows(dgt_o, jnp.sum(dy * fv, axis=0, keepdims=True), i)

    part = jax.ShapeDtypeStruct((8, D), F32)
    return _pcall(
        body, name=name, grid=(S // T,),
        in_specs=[_rows(T, D), _rows(T, D), _full((1, D)), _rows(T, D)],
        out_specs=(_rows(T, D), _full((8, D)), _rows(T, D), _full((8, D))),
        out_shape=(jax.ShapeDtypeStruct((S, D), F32), part, jax.ShapeDtypeStruct((S, D), BF16), part),
        compiler_params=_cparams("arbitrary"),
    )(x_mid, ffo, gt, target)


def _rope(t, c_t, s1_t, s2_t):
    return t * c_t + pltpu.roll(t, LANE - ROT_DIM // 2, 1) * s1_t + pltpu.roll(t, ROT_DIM // 2, 1) * s2_t


def _rope_t(d, c_t, s1_t, s2_t):
    return d * c_t + pltpu.roll(d * s1_t, ROT_DIM // 2, 1) + pltpu.roll(d * s2_t, LANE - ROT_DIM // 2, 1)


def qk_prep_fwd(zq, g_q, g_k, tabs, *, name):
    S = zq.shape[0]
    T = ROW_TILE

    def body(q_r, k_r, gq_r, gk_r, c_r, s1_r, s2_r, qn_o, kn_o):
        c_t, s1_t, s2_t = c_r[...], s1_r[...], s2_r[...]
        for src, g_r, dst in ((q_r, gq_r, qn_o), (k_r, gk_r, kn_o)):
            for h in range(N_HEADS):
                cols = slice(h * HEAD_DIM, (h + 1) * HEAD_DIM)
                t = src[:, cols]
                r = lax.rsqrt(jnp.mean(t * t, axis=-1, keepdims=True) + EPS)
                dst[:, cols] = _rope(t * r * g_r[...], c_t, s1_t, s2_t)

    tab = _rows(T, LANE)
    shp = jax.ShapeDtypeStruct((S, ATTN_W), F32)
    return _pcall(
        body, name=name, grid=(S // T,),
        in_specs=[_rows(T, ATTN_W, 0), _rows(T, ATTN_W, 1), _full((1, LANE)), _full((1, LANE)), tab, tab, tab],
        out_specs=(_rows(T, ATTN_W), _rows(T, ATTN_W)), out_shape=(shp, shp),
        compiler_params=_cparams("parallel"),
    )(zq, zq, g_q, g_k, *tabs)


def attn_bwd_post(zq, g_q, g_k, tabs, dq, dkc, dkp, dvc, dvp, *, name):
    S = zq.shape[0]
    T = min(POST_TILE, S)
    nblk = S // T
    GW = HEADS_PER_GROUP * HEAD_DIM
    n_ref = [7 if BLOCK * d < T else 5 for d in DILATIONS]

    def body(*refs):
        q_r, k_r, gq_r, gk_r, c_r, s1_r, s2_r = refs[:7]
        grp = refs[7:7 + sum(n_ref)]
        dz_o, dgq_o, dgk_o = refs[7 + sum(n_ref):]
        i = pl.program_id(0)
        dgq = jnp.zeros((1, LANE), F32)
        dgk = jnp.zeros((1, LANE), F32)
        RC = T
        for r0 in range(0, T, RC):
            rows = slice(r0, r0 + RC)
            c_t, s1_t, s2_t = c_r[rows, :], s1_r[rows, :], s2_r[rows, :]
            at = 0
            for gi, d in enumerate(DILATIONS):
                g_refs = grp[at:at + n_ref[gi]]
                at += n_ref[gi]
                sh = BLOCK * d
                if sh < T:
                    dq_r, dkc_r, dkp_r, dkpn_r, dvc_r, dvp_r, dvpn_r = g_refs
                    live = jnp.where(i + 1 < nblk, 1.0, 0.0)

                    def shifted(cur_r, nxt_r, gc, live=live, lo=r0 + sh):
                        if lo + RC <= T:
                            return cur_r[lo:lo + RC, gc]
                        if lo >= T:
                            return live * nxt_r[lo - T:lo - T + RC, gc]
                        return jnp.concatenate([cur_r[lo:T, gc], live * nxt_r[0:lo + RC - T, gc]], axis=0)
                else:
                    dq_r, dkc_r, dkp_r, dvc_r, dvp_r = g_refs
                    dkpn_r = dvpn_r = None
                    live = jnp.where(i + sh // T < nblk, 1.0, 0.0)

                    def shifted(cur_r, nxt_r, gc, live=live, rows=rows):
                        return live * cur_r[rows, gc]
                for hh in range(HEADS_PER_GROUP):
                    h = gi * HEADS_PER_GROUP + hh
                    cols = slice(h * HEAD_DIM, (h + 1) * HEAD_DIM)
                    gc = slice(hh * HEAD_DIM, (hh + 1) * HEAD_DIM)
                    dk_v = dkc_r[rows, gc] + shifted(dkp_r, dkpn_r, gc)
                    dv_v = dvc_r[rows, gc] + shifted(dvp_r, dvpn_r, gc)
                    dz_o[rows, 2 * ATTN_W + h * HEAD_DIM:2 * ATTN_W + (h + 1) * HEAD_DIM] = dv_v.astype(BF16)
                    for which, (src, g_r, d_out) in enumerate(((q_r, gq_r, dq_r[rows, gc]), (k_r, gk_r, dk_v))):
                        t = src[rows, cols]
                        r = lax.rsqrt(jnp.mean(t * t, axis=-1, keepdims=True) + EPS)
                        xh = t * r
                        dtn = _rope_t(d_out, c_t, s1_t, s2_t)
                        dxh = dtn * g_r[...]
                        dt = r * (dxh - xh * jnp.mean(dxh * xh, axis=-1, keepdims=True))
                        dz_o[rows, which * ATTN_W + h * HEAD_DIM:which * ATTN_W + (h + 1) * HEAD_DIM] = dt.astype(BF16)
                        part = jnp.sum(dtn * xh, axis=0, keepdims=True)
                        if which == 0:
                            dgq = dgq + part
                        else:
                            dgk = dgk + part
        _acc_rows(dgq_o, dgq, i)
        _acc_rows(dgk_o, dgk, i)

    tab = _rows(T, LANE)
    specs = [_rows(T, ATTN_W, 0), _rows(T, ATTN_W, 1), _full((1, LANE)), _full((1, LANE)), tab, tab, tab]
    args = [zq, zq, g_q, g_k, *tabs]
    for gi, d in enumerate(DILATIONS):
        cur = _rows(T, GW)
        sh = BLOCK * d
        if sh < T:
            head = pl.BlockSpec((sh, GW), functools.partial(
                lambda i, k, last: (jnp.minimum((i + 1) * k, last), 0), k=T // sh, last=S // sh - 1))
            specs += [cur, cur, cur, head, cur, cur, head]
            args += [dq[gi], dkc[gi], dkp[gi], dkp[gi], dvc[gi], dvp[gi], dvp[gi]]
        else:
            nxt = pl.BlockSpec((T, GW), functools.partial(lambda i, s: (jnp.minimum(i + s, nblk - 1), 0), s=sh // T))
            specs += [cur, cur, nxt, cur, nxt]
            args += [dq[gi], dkc[gi], dkp[gi], dvc[gi], dvp[gi]]
    part = jax.ShapeDtypeStruct((8, LANE), F32)
    return _pcall(
        body, name=name, grid=(nblk,), in_specs=specs,
        out_specs=(_rows(T, 3 * ATTN_W), _full((8, LANE)), _full((8, LANE))),
        out_shape=(jax.ShapeDtypeStruct((S, 3 * ATTN_W), BF16), part, part),
        compiler_params=_cparams("arbitrary"),
    )(*args)


ATTN_UNITS = 16


def _attn_geometry(d, S):
    R = min(ATTN_UNITS * BLOCK, S)
    return R, R // (BLOCK * d), S // R


def _sub_rows(j, r, d):
    if d == 1:
        return pl.ds(j * BLOCK, BLOCK)
    return pl.ds(j * BLOCK * d + r, BLOCK, stride=d)


def _dot_nt(a, b):
    return lax.dot_general(a, b, (((1,), (1,)), ((), ())), preferred_element_type=F32)


def _dot_tn(a, b):
    return lax.dot_general(a, b, (((0,), (0,)), ((), ())), preferred_element_type=F32)


def _attn_specs(gi, R):
    h0 = gi * HEADS_PER_GROUP
    vcol = 2 * N_HEADS + h0
    cur = lambda off: pl.BlockSpec((R, HEAD_DIM), lambda h, n: (n, off + h))
    prev = lambda off: pl.BlockSpec((R, HEAD_DIM), lambda h, n: (jnp.maximum(n - 1, 0), off + h))
    return [cur(h0), cur(h0), prev(h0), cur(vcol), prev(vcol)]


ATTN_UNROLL = ATTN_UNITS


def _attn_masks(n):
    qi = lax.broadcasted_iota(jnp.int32, (BLOCK, 2 * BLOCK), 0)
    kj = lax.broadcasted_iota(jnp.int32, (BLOCK, 2 * BLOCK), 1)
    band = jnp.where(jnp.logical_and(kj >= qi, kj <= qi + BLOCK), 0.0, NEG)
    no_prev = band + jnp.where(kj < BLOCK, 1.0, 0.0) * jnp.where(n > 0, 0.0, NEG)
    return band, no_prev


def _attn_keys(kc_r, kp_r, vc_r, vp_r, j, r, d, J):
    rq = _sub_rows(j, r, d)
    if j > 0:
        rp = _sub_rows(j - 1, r, d)
        kp, vp = kc_r[rp, :], vc_r[rp, :]
    else:
        rp = _sub_rows(J - 1, r, d)
        kp, vp = kp_r[rp, :], vp_r[rp, :]
    kk = jnp.concatenate([kp, kc_r[rq, :]], axis=0).astype(BF16)
    vv = jnp.concatenate([vp, vc_r[rq, :]], axis=0).astype(BF16)
    return kk, vv


def _attn_units(unit, d, J):
    for j in range(J):
        if d == 1:
            unit(j, 0)
        else:
            def step(r, carry, j=j):
                unit(j, r)
                return carry
            lax.fori_loop(0, d, step, 0, unroll=min(d, ATTN_UNROLL))


def attn_fwd(qn, kn, zq, gi, *, name):
    S = qn.shape[0]
    d = DILATIONS[gi]
    R, J, nblk = _attn_geometry(d, S)
    scale = HEAD_DIM ** -0.5
    GW = HEADS_PER_GROUP * HEAD_DIM

    def body(q_r, kc_r, kp_r, vc_r, vp_r, o_o, l_o):
        band, no_prev = _attn_masks(pl.program_id(1))

        def unit(j, r):
            rq = _sub_rows(j, r, d)
            q = q_r[rq, :].astype(BF16)
            kk, vv = _attn_keys(kc_r, kp_r, vc_r, vp_r, j, r, d, J)
            s = _dot_nt(q, kk) * scale + (band if j > 0 else no_prev)
            m = jnp.max(s, axis=-1, keepdims=True)
            p = jnp.exp(s - m)
            l = jnp.sum(p, axis=-1, keepdims=True)
            o_o[rq, :] = jnp.dot(p.astype(BF16), vv, preferred_element_type=F32) / l
            l_o[rq, :] = jnp.broadcast_to(m + jnp.log(l), (BLOCK, HEAD_DIM))

        _attn_units(unit, d, J)

    ospec = pl.BlockSpec((R, HEAD_DIM), lambda h, n: (n, h))
    shp = jax.ShapeDtypeStruct((S, GW), F32)
    return _pcall(
        body, name=name, grid=(HEADS_PER_GROUP, nblk), in_specs=_attn_specs(gi, R),
        out_specs=(ospec, ospec), out_shape=(shp, shp),
        compiler_params=_cparams("parallel", "arbitrary"),
    )(qn, kn, kn, zq, zq)


def attn_bwd(qn, kn, zq, do, lse, cc, gi, *, name):
    S = qn.shape[0]
    d = DILATIONS[gi]
    R, J, nblk = _attn_geometry(d, S)
    scale = HEAD_DIM ** -0.5
    GW = HEADS_PER_GROUP * HEAD_DIM

    def body(q_r, kc_r, kp_r, vc_r, vp_r, do_r, l_r, c_r, dq_o, dkc_o, dkp_o, dvc_o, dvp_o):
        band, no_prev = _attn_masks(pl.program_id(1))

        def unit(j, r):
            rq = _sub_rows(j, r, d)
            q = q_r[rq, :].astype(BF16)
            kk, vv = _attn_keys(kc_r, kp_r, vc_r, vp_r, j, r, d, J)
            s_mask = band if j > 0 else no_prev
            dob = do_r[rq, :].astype(BF16)
            lv = l_r[rq, :]
            cv = c_r[rq, :]
            lv2 = jnp.concatenate([lv, lv], axis=1)
            cv2 = jnp.concatenate([cv, cv], axis=1)
            p = jnp.exp(_dot_nt(q, kk) * scale + s_mask - lv2)
            ds = (p * (_dot_nt(dob, vv) + cv2)).astype(BF16)
            dq_o[rq, :] = jnp.dot(ds, kk, preferred_element_type=F32) * scale
            dk2 = _dot_tn(ds, q) * scale
            dv2 = _dot_tn(p.astype(BF16), dob)
            dkp_o[rq, :] = dk2[0:BLOCK]
            dkc_o[rq, :] = dk2[BLOCK:2 * BLOCK]
            dvp_o[rq, :] = dv2[0:BLOCK]
            dvc_o[rq, :] = dv2[BLOCK:2 * BLOCK]

        _attn_units(unit, d, J)

    ospec = pl.BlockSpec((R, HEAD_DIM), lambda h, n: (n, h))
    shp = jax.ShapeDtypeStruct((S, GW), F32)
    return _pcall(
        body, name=name, grid=(HEADS_PER_GROUP, nblk),
        in_specs=_attn_specs(gi, R) + [ospec, ospec, ospec],
        out_specs=(ospec,) * 5, out_shape=(shp,) * 5,
        compiler_params=_cparams("parallel", "arbitrary"),
    )(qn, kn, kn, zq, zq, do, lse, cc)


def combine_fwd(o, lse, *, name):
    S, GW = o[0].shape
    T = ROW_TILE

    def body(o0, o1, o2, l0, l1, l2, a_o):
        m = jnp.maximum(jnp.maximum(l0[...], l1[...]), l2[...])
        e0, e1, e2 = jnp.exp(l0[...] - m), jnp.exp(l1[...] - m), jnp.exp(l2[...] - m)
        a_o[...] = ((e0 * o0[...] + e1 * o1[...] + e2 * o2[...]) / (e0 + e1 + e2)).astype(BF16)

    return _pcall(
        body, name=name, grid=(S // T,), in_specs=[_rows(T, GW)] * 6, out_specs=_rows(T, GW),
        out_shape=jax.ShapeDtypeStruct((S, GW), BF16), compiler_params=_cparams("parallel"),
    )(*o, *lse)


def combine_bwd(d_attn, o, lse, *, name):
    S, GW = d_attn.shape
    T = ROW_TILE

    def body(da_r, o0, o1, o2, l0, l1, l2, d0, d1, d2, c0, c1, c2):
        for r0 in range(0, T, GLUE_RC):
            rows = slice(r0, r0 + GLUE_RC)
            for hh in range(HEADS_PER_GROUP):
                cols = slice(hh * HEAD_DIM, (hh + 1) * HEAD_DIM)
                lv = [l_r[rows, cols] for l_r in (l0, l1, l2)]
                m = jnp.maximum(jnp.maximum(lv[0], lv[1]), lv[2])
                e = [jnp.exp(v - m) for v in lv]
                inv = 1.0 / (e[0] + e[1] + e[2])
                w = [e_g * inv for e_g in e]
                da = da_r[rows, cols]
                attn = w[0] * o0[rows, cols] + w[1] * o1[rows, cols] + w[2] * o2[rows, cols]
                a_h = jnp.sum(da * attn, axis=-1, keepdims=True)
                for w_g, d_o, c_o in zip(w, (d0, d1, d2), (c0, c1, c2)):
                    d_o[rows, cols] = w_g * da
                    c_o[rows, cols] = -w_g * a_h

    shp = jax.ShapeDtypeStruct((S, GW), F32)
    return _pcall(
        body, name=name, grid=(S // T,), in_specs=[_rows(T, GW)] * 7, out_specs=(_rows(T, GW),) * 6,
        out_shape=(shp,) * 6, compiler_params=_cparams("parallel"),
    )(d_attn, *o, *lse)


def _halo_prev(T, H, C, col):
    k = T // H
    return pl.BlockSpec((H, C), lambda i: (jnp.maximum(i * k - 1, 0), col))


def _halo_next(T, H, C, col, n_rows):
    k = T // H
    last = n_rows // H - 1
    return pl.BlockSpec((H, C), lambda i: (jnp.minimum((i + 1) * k, last), col))


CONV_RC = 64
SUBLANES = 8


def _tap_groups(offs):
    groups = {}
    for k, off in offs:
        groups.setdefault(off % SUBLANES, []).append((k, off))
    return [taps for _, taps in sorted(groups.items())]


def _for_taps(src, r0, lanes, offs, fn):
    for taps in _tap_groups(offs):
        lo = min(off for _, off in taps)
        hi = max(off for _, off in taps)
        sb = src[r0 + lo:r0 + hi + CONV_RC, lanes]
        for k, off in taps:
            fn(k, sb[off - lo:off - lo + CONV_RC])


def _dwconv(src, w_r, offs, T, C, bias_r, dst):
    for rc in range(T // CONV_RC):
        for cc in range(C // LANE):
            lanes = slice(cc * LANE, (cc + 1) * LANE)
            r0 = rc * CONV_RC
            acc = None if bias_r is None else jnp.zeros((CONV_RC, LANE), F32) + bias_r[:, lanes]
            for taps in _tap_groups(offs):
                lo = min(off for _, off in taps)
                hi = max(off for _, off in taps)
                sb = src[r0 + lo:r0 + hi + CONV_RC, lanes]
                g_acc = None
                for k, off in taps:
                    term = w_r[k:k + 1, lanes] * sb[off - lo:off - lo + CONV_RC]
                    g_acc = term if g_acc is None else g_acc + term
                acc = g_acc if acc is None else acc + g_acc
            dst[r0:r0 + CONV_RC, lanes] = acc


def _fill_glu(cv, cg, hv, hg, ubuf, i):
    T = cv.shape[0]
    live = jnp.where(i > 0, 1.0, 0.0)
    ubuf[0:CONV_HALO, :] = live * (hv[...] * _sig(hg[...]))
    ubuf[CONV_HALO:CONV_HALO + T, :] = cv[...] * _sig(cg[...])


_CONV_FWD_OFFS = [(k, CONV_HALO - (CONV_K - 1) + k) for k in range(CONV_K)]
_CONV_BWD_OFFS = [(k, CONV_K - 1 - k) for k in range(CONV_K)]


def convb_fwd(zc, w, b, g_ln, b_ln, *, name):
    S = zc.shape[0]
    C = zc.shape[1] // 2
    T = ROW_TILE

    def body(cv, cg, hv, hg, w_r, b_r, g_r, bl_r, u_o, y_o, ubuf):
        _fill_glu(cv, cg, hv, hg, ubuf, pl.program_id(0))
        _dwconv(ubuf, w_r, _CONV_FWD_OFFS, T, C, b_r, y_o)
        y = y_o[...]
        mu = jnp.mean(y, axis=-1, keepdims=True)
        yc = y - mu
        rs = lax.rsqrt(jnp.mean(yc * yc, axis=-1, keepdims=True) + EPS)
        v = yc * rs * g_r[...] + bl_r[...]
        u_o[...] = (v * _sig(v)).astype(BF16)

    vec = _full((1, C))
    return _pcall(
        body, name=name, grid=(S // T,),
        in_specs=[_rows(T, C, 0), _rows(T, C, 1), _halo_prev(T, CONV_HALO, C, 0), _halo_prev(T, CONV_HALO, C, 1),
                  _full((CONV_K, C)), vec, vec, vec],
        out_specs=(_rows(T, C), _rows(T, C)),
        out_shape=(jax.ShapeDtypeStruct((S, C), BF16), jax.ShapeDtypeStruct((S, C), F32)),
        scratch_shapes=[pltpu.VMEM((CONV_HALO + T, C), F32)],
        compiler_params=_cparams("parallel"),
    )(zc, zc, zc, zc, w, b, g_ln, b_ln)


def convb_bwd1(d_u2, y_conv, zc, g_ln, b_ln, *, name):
    S = zc.shape[0]
    C = zc.shape[1] // 2
    T = ROW_TILE

    def body(du_r, y_r, cv, cg, hv, hg, g_r, bl_r, dy_o, dw_o, db_o, dg_o, dbl_o, ubuf):
        i = pl.program_id(0)
        _fill_glu(cv, cg, hv, hg, ubuf, i)
        y = y_r[...]
        mu = jnp.mean(y, axis=-1, keepdims=True)
        yc = y - mu
        rs = lax.rsqrt(jnp.mean(yc * yc, axis=-1, keepdims=True) + EPS)
        yn = yc * rs
        v = yn * g_r[...] + bl_r[...]
        sg = _sig(v)
        dv = du_r[...] * (sg * (1.0 + v * (1.0 - sg)))
        dyn = dv * g_r[...]
        dy = rs * (dyn - jnp.mean(dyn, axis=-1, keepdims=True) - yn * jnp.mean(dyn * yn, axis=-1, keepdims=True))
        dy_o[...] = dy
        _acc_rows(dg_o, jnp.sum(dv * yn, axis=0, keepdims=True), i)
        _acc_rows(dbl_o, jnp.sum(dv, axis=0, keepdims=True), i)
        _acc_rows(db_o, jnp.sum(dy, axis=0, keepdims=True), i)

        @pl.when(i == 0)
        def _():
            dw_o[...] = jnp.zeros_like(dw_o)
        for cc in range(C // LANE):
            lanes = slice(cc * LANE, (cc + 1) * LANE)
            parts = [jnp.zeros((SUBLANES, LANE), F32) for _ in range(CONV_K)]
            for rc in range(T // CONV_RC):
                r0 = rc * CONV_RC
                dyc = dy_o[r0:r0 + CONV_RC, lanes]

                def tap(k, chunk, parts=parts, dyc=dyc):
                    prod = (dyc * chunk).reshape(CONV_RC // SUBLANES, SUBLANES, LANE)
                    parts[k] = parts[k] + jnp.sum(prod, axis=0)

                _for_taps(ubuf, r0, lanes, _CONV_FWD_OFFS, tap)
            for k in range(CONV_K):
                dw_o[k * SUBLANES:(k + 1) * SUBLANES, lanes] += parts[k]

    vec = _full((1, C))
    part = jax.ShapeDtypeStruct((8, C), F32)
    return _pcall(
        body, name=name, grid=(S // T,),
        in_specs=[_rows(T, C), _rows(T, C), _rows(T, C, 0), _rows(T, C, 1), _halo_prev(T, CONV_HALO, C, 0),
                  _halo_prev(T, CONV_HALO, C, 1), vec, vec],
        out_specs=(_rows(T, C), _full((CONV_K * SUBLANES, C)), _full((8, C)), _full((8, C)), _full((8, C))),
        out_shape=(jax.ShapeDtypeStruct((S, C), F32), jax.ShapeDtypeStruct((CONV_K * SUBLANES, C), F32),
                   part, part, part),
        scratch_shapes=[pltpu.VMEM((CONV_HALO + T, C), F32)],
        compiler_params=_cparams("arbitrary"),
    )(d_u2, y_conv, zc, zc, zc, zc, g_ln, b_ln)


def convb_bwd2(dy, zc, w, *, name):
    S = zc.shape[0]
    C = zc.shape[1] // 2
    T = ROW_TILE
    nblk = S // T

    def body(dy_r, dyn_r, cv, cg, w_r, dz_o, dbuf, dubuf):
        i = pl.program_id(0)
        live = jnp.where(i < nblk - 1, 1.0, 0.0)
        dbuf[0:T, :] = dy_r[...]
        dbuf[T:T + CONV_HALO, :] = live * dyn_r[...]
        _dwconv(dbuf, w_r, _CONV_BWD_OFFS, T, C, None, dubuf)
        du = dubuf[...]
        sg = _sig(cg[...])
        dz_o[:, 0:C] = (du * sg).astype(BF16)
        dz_o[:, C:2 * C] = (du * cv[...] * sg * (1.0 - sg)).astype(BF16)

    return _pcall(
        body, name=name, grid=(nblk,),
        in_specs=[_rows(T, C), _halo_next(T, CONV_HALO, C, 0, S), _rows(T, C, 0), _rows(T, C, 1), _full((CONV_K, C))],
        out_specs=_rows(T, 2 * C), out_shape=jax.ShapeDtypeStruct((S, 2 * C), BF16),
        scratch_shapes=[pltpu.VMEM((T + CONV_HALO, C), F32), pltpu.VMEM((T, C), F32)],
        compiler_params=_cparams("parallel"),
    )(dy, dy, zc, zc, w)


def merge_fwd(y_a, y_b, zg, *, name):
    S, D = y_a.shape
    T = ROW_TILE

    def body(a_r, b_r, ga_r, gb_r, m_o):
        ga, gb = ga_r[...].astype(F32), gb_r[...].astype(F32)
        m_o[...] = (_sig(ga) * a_r[...].astype(F32) + _sig(gb) * b_r[...].astype(F32)).astype(BF16)

    return _pcall(
        body, name=name, grid=(S // T,),
        in_specs=[_rows(T, D), _rows(T, D), _rows(T, D, 0), _rows(T, D, 1)],
        out_specs=_rows(T, D), out_shape=jax.ShapeDtypeStruct((S, D), BF16),
        compiler_params=_cparams("parallel"),
    )(y_a, y_b, zg, zg)


def merge_bwd(d_m, y_a, y_b, zg, *, name):
    S, D = y_a.shape
    T = ROW_TILE

    def body(dm_r, a_r, b_r, ga_r, gb_r, da_o, db_o, dz_o):
        dm = dm_r[...]
        sa, sb = _sig(ga_r[...].astype(F32)), _sig(gb_r[...].astype(F32))
        da_o[...] = (dm * sa).astype(BF16)
        db_o[...] = (dm * sb).astype(BF16)
        dz_o[:, 0:D] = (dm * a_r[...].astype(F32) * sa * (1.0 - sa)).astype(BF16)
        dz_o[:, D:2 * D] = (dm * b_r[...].astype(F32) * sb * (1.0 - sb)).astype(BF16)

    shp = jax.ShapeDtypeStruct((S, D), BF16)
    return _pcall(
        body, name=name, grid=(S // T,),
        in_specs=[_rows(T, D), _rows(T, D), _rows(T, D), _rows(T, D, 0), _rows(T, D, 1)],
        out_specs=(_rows(T, D), _rows(T, D), _rows(T, 2 * D)),
        out_shape=(shp, shp, jax.ShapeDtypeStruct((S, 2 * D), BF16)),
        compiler_params=_cparams("parallel"),
    )(d_m, y_a, y_b, zg, zg)


FFN_RC = 64


def _ffn_chunks(T, F):
    for cc in range(F // LANE):
        for rc in range(T // FFN_RC):
            yield rc * FFN_RC, slice(rc * FFN_RC, (rc + 1) * FFN_RC), slice(cc * LANE, (cc + 1) * LANE)


def _ffn_fill(g_r, hg_r, gbuf, i):
    T = g_r.shape[0]
    live = jnp.where(i > 0, 1.0, 0.0)
    gbuf[0:FFN_HALO, :] = live * hg_r[...].astype(F32)
    gbuf[FFN_HALO:FFN_HALO + T, :] = g_r[...].astype(F32)


def _ffn_gate_chunk(gbuf, w_r, b_r, r0, lanes, n=FFN_RC):
    taps = [gbuf[r0 + FFN_HALO - (FFN_K - 1) + k:r0 + FFN_HALO - (FFN_K - 1) + k + n, lanes]
            for k in range(FFN_K)]
    gp = b_r[:, lanes] + w_r[0:1, lanes] * taps[0]
    for k in range(1, FFN_K):
        gp = gp + w_r[k:k + 1, lanes] * taps[k]
    return gp, taps


def _sum8(v):
    return jnp.sum(v.reshape(v.shape[0] // SUBLANES, SUBLANES, v.shape[1]), axis=0)


def ffn_act_fwd(gu, w, b, *, name):
    S = gu.shape[0]
    F = gu.shape[1] // 2
    T = ROW_TILE // 2

    def body(g_r, u_r, hg_r, w_r, b_r, a_o, gbuf):
        _ffn_fill(g_r, hg_r, gbuf, pl.program_id(0))
        for r0, rows, lanes in _ffn_chunks(T, F):
            gp, _ = _ffn_gate_chunk(gbuf, w_r, b_r, r0, lanes)
            a_o[rows, lanes] = (gp * _sig(gp) * u_r[rows, lanes].astype(F32)).astype(BF16)

    return _pcall(
        body, name=name, grid=(S // T,),
        in_specs=[_rows(T, F, 0), _rows(T, F, 1), _halo_prev(T, FFN_HALO, F, 0), _full((FFN_K, F)), _full((1, F))],
        out_specs=_rows(T, F), out_shape=jax.ShapeDtypeStruct((S, F), BF16),
        scratch_shapes=[pltpu.VMEM((FFN_HALO + T, F), F32)],
        compiler_params=_cparams("parallel"),
    )(gu, gu, gu, w, b)


def ffn_act_bwd(d_a, gu, w, b, *, name):
    S = gu.shape[0]
    F = gu.shape[1] // 2
    T = ROW_TILE // 2
    H = FFN_HALO
    nblk = S // T

    def dgp_of(gp, da, u):
        sg = _sig(gp)
        return da * u * (sg * (1.0 + gp * (1.0 - sg))), sg

    def body(da_r, dan_r, g_r, gp_r, gn_r, u_r, un_r, w_r, b_r, o_o, dw_o, db_o, gbuf, dbuf):
        i = pl.program_id(0)
        gbuf[0:H, :] = jnp.where(i > 0, 1.0, 0.0) * gp_r[...].astype(F32)
        gbuf[H:H + T, :] = g_r[...].astype(F32)
        gbuf[H + T:H + T + H, :] = gn_r[...].astype(F32)

        @pl.when(i == 0)
        def _():
            dw_o[...] = jnp.zeros_like(dw_o)
            db_o[...] = jnp.zeros_like(db_o)

        live_n = jnp.where(i < nblk - 1, 1.0, 0.0)
        sums = None
        for r0, rows, lanes in _ffn_chunks(T, F):
            if r0 == 0:
                sums = [jnp.zeros((SUBLANES, LANE), F32) for _ in range(FFN_K + 1)]
                gp_h, _ = _ffn_gate_chunk(gbuf, w_r, b_r, T, lanes, n=H)
                dgp_h, _ = dgp_of(gp_h, live_n * dan_r[:, lanes].astype(F32), un_r[:, lanes].astype(F32))
                dbuf[T:T + H, lanes] = dgp_h
            gp, taps = _ffn_gate_chunk(gbuf, w_r, b_r, r0, lanes)
            da = da_r[rows, lanes].astype(F32)
            dgp, sg = dgp_of(gp, da, u_r[rows, lanes].astype(F32))
            o_o[rows, F + lanes.start:F + lanes.stop] = (da * gp * sg).astype(BF16)
            dbuf[rows, lanes] = dgp
            sums[FFN_K] = sums[FFN_K] + _sum8(dgp)
            for k in range(FFN_K):
                sums[k] = sums[k] + _sum8(dgp * taps[k])
            if r0 + FFN_RC == T:
                db_o[:, lanes] += sums[FFN_K]
                for k in range(FFN_K):
                    dw_o[k * SUBLANES:(k + 1) * SUBLANES, lanes] += sums[k]
        for r0, rows, lanes in _ffn_chunks(T, F):
            dg = w_r[0:1, lanes] * dbuf[r0 + FFN_K - 1:r0 + FFN_K - 1 + FFN_RC, lanes]
            for k in range(1, FFN_K):
                off = r0 + FFN_K - 1 - k
                dg = dg + w_r[k:k + 1, lanes] * dbuf[off:off + FFN_RC, lanes]
            o_o[rows, lanes] = dg.astype(BF16)

    return _pcall(
        body, name=name, grid=(nblk,),
        in_specs=[_rows(T, F), _halo_next(T, H, F, 0, S), _rows(T, F, 0), _halo_prev(T, H, F, 0),
                  _halo_next(T, H, F, 0, S), _rows(T, F, 1), _halo_next(T, H, F, 1, S),
                  _full((FFN_K, F)), _full((1, F))],
        out_specs=(_rows(T, 2 * F), _full((FFN_K * SUBLANES, F)), _full((SUBLANES, F))),
        out_shape=(jax.ShapeDtypeStruct((S, 2 * F), BF16),
                   jax.ShapeDtypeStruct((FFN_K * SUBLANES, F), F32), jax.ShapeDtypeStruct((SUBLANES, F), F32)),
        scratch_shapes=[pltpu.VMEM((H + T + H, F), F32), pltpu.VMEM((T + H, F), F32)],
        compiler_params=_cparams("arbitrary"),
    )(d_a, d_a, gu, gu, gu, gu, gu, w, b)


def _row_tile(R, target=512):
    if R <= target:
        return R
    for t in range(target, 7, -8):
        if R % t == 0:
            return t
    return R


def sum_slots(land, *, name):
    _, R, C = land.shape
    T = _row_tile(R)

    def body(l_r, o_o):
        acc = l_r[0].astype(F32)
        for q in range(1, N_DEV):
            acc = acc + l_r[q].astype(F32)
        o_o[...] = acc

    return _pcall(
        body, name=name, grid=(R // T,),
        in_specs=[pl.BlockSpec((N_DEV, T, C), lambda i: (0, i, 0))],
        out_specs=_rows(T, C), out_shape=jax.ShapeDtypeStruct((R, C), F32),
        compiler_params=_cparams("parallel"),
    )(land)


def adamw(w, g, m, v, *, name):
    shape = w.shape
    C = shape[-1]
    R = math.prod(shape[:-1])
    w2, g2, m2, v2 = (t.reshape(R, C) for t in (w, g, m, v))
    T = _row_tile(R)
    c1 = 1.0 - ADAM_B1 ** ADAM_STEP
    c2 = 1.0 - ADAM_B2 ** ADAM_STEP

    def body(w_r, g_r, m_r, v_r, d_o, m_o, v_o):
        gv = g_r[...]
        mn = ADAM_B1 * m_r[...] + (1.0 - ADAM_B1) * gv
        vn = ADAM_B2 * v_r[...] + (1.0 - ADAM_B2) * (gv * gv)
        m_o[...] = mn
        v_o[...] = vn
        d_o[...] = -ADAM_LR * ((mn / c1) / (jnp.sqrt(vn / c2) + ADAM_EPS) + ADAM_WD * w_r[...])

    shp = jax.ShapeDtypeStruct((R, C), F32)
    d, mn, vn = _pcall(
        body, name=name, grid=(R // T,), in_specs=[_rows(T, C)] * 4, out_specs=(_rows(T, C),) * 3,
        out_shape=(shp,) * 3, compiler_params=_cparams("parallel"),
    )(w2, g2, m2, v2)
    return d.reshape(shape), mn.reshape(shape), vn.reshape(shape)


def _my_pos():
    return lax.axis_index("x"), lax.axis_index("y"), lax.axis_index("c")


class _Exchange:
    def __init__(self, inputs, out_shapes, sems, start, finish):
        self.inputs, self.out_shapes, self.sems, self.start, self.finish = inputs, out_shapes, sems, start, finish


def gather_exchange(shards):
    n = len(shards)

    def plan(ins, outs, sems):
        send_sems, recv_sems, local_sems = sems
        x, y, c = _my_pos()
        me, sibling = (x, y, c), (x, y, 1 - c)
        chips = [(1 - x, y), (x, 1 - y), (1 - x, 1 - y)]

        def slot(i, p):
            return outs[i].at[4 * p[0] + 2 * p[1] + p[2]]

        def copy(k, i, block, to, src=None):
            return pltpu.make_async_remote_copy(
                src_ref=slot(i, block) if src is None else src, dst_ref=slot(i, block),
                send_sem=send_sems.at[k, i], recv_sem=recv_sems.at[k, i], device_id=to, device_id_type=MESH)

        mine = [pltpu.make_async_copy(ins[i], slot(i, me), local_sems.at[i]) for i in range(n)]
        first = []
        for i in range(n):
            first.append(copy(0, i, me, sibling, src=ins[i]))
            first += [copy(1 + j, i, me, (*chip, c), src=ins[i]) for j, chip in enumerate(chips)]
        return me, sibling, chips, c, copy, mine, first

    def start(ins, outs, sems):
        _, _, _, _, _, mine, first = plan(ins, outs, sems)
        for cp in mine + first:
            cp.start()

    def finish(ins, outs, sems):
        me, sibling, chips, c, copy, mine, first = plan(ins, outs, sems)
        passed = []
        for j, chip in enumerate(chips):
            for i in range(n):
                copy(1 + j, i, (*chip, c), me).wait_recv()
                cp = copy(4 + j, i, (*chip, c), sibling)
                cp.start()
                passed.append(cp)
        for i in range(n):
            copy(0, i, sibling, me).wait_recv()
            for j, chip in enumerate(chips):
                copy(4 + j, i, (*chip, 1 - c), me).wait_recv()
        for cp in first + passed:
            cp.wait_send()
        for cp in mine:
            cp.wait()

    outs = [jax.ShapeDtypeStruct((N_DEV,) + s.shape, s.dtype) for s in shards]
    sems = [pltpu.SemaphoreType.DMA((7, n)), pltpu.SemaphoreType.DMA((7, n)), pltpu.SemaphoreType.DMA((n,))]
    return _Exchange(list(shards), outs, sems, start, finish)


def scatter_exchange(gs):
    n = len(gs)

    def plan(ins, outs, sems):
        send_sems, recv_sems, local_sems = sems
        x, y, c = _my_pos()
        me_id = 4 * x + 2 * y + c
        mine = [pltpu.make_async_copy(ins[i].at[me_id], outs[i].at[me_id], local_sems.at[i]) for i in range(n)]
        sends, recvs = [], []
        for msk in range(1, N_DEV):
            px = 1 - x if msk & 4 else x
            py = 1 - y if msk & 2 else y
            pc = 1 - c if msk & 1 else c
            pid = 4 * px + 2 * py + pc
            for i in range(n):
                sends.append(pltpu.make_async_remote_copy(
                    src_ref=ins[i].at[pid], dst_ref=outs[i].at[me_id],
                    send_sem=send_sems.at[msk - 1, i], recv_sem=recv_sems.at[msk - 1, i],
                    device_id=(px, py, pc), device_id_type=MESH))
                recvs.append(pltpu.make_async_remote_copy(
                    src_ref=ins[i].at[pid], dst_ref=outs[i].at[pid],
                    send_sem=send_sems.at[msk - 1, i], recv_sem=recv_sems.at[msk - 1, i],
                    device_id=(px, py, pc), device_id_type=MESH))
        return mine, sends, recvs

    def start(ins, outs, sems):
        mine, sends, _ = plan(ins, outs, sems)
        for cp in mine + sends:
            cp.start()

    def finish(ins, outs, sems):
        mine, sends, recvs = plan(ins, outs, sems)
        for rv in recvs:
            rv.wait_recv()
        for cp in sends:
            cp.wait_send()
        for cp in mine:
            cp.wait()

    outs = [jax.ShapeDtypeStruct(g.shape, g.dtype) for g in gs]
    sems = [pltpu.SemaphoreType.DMA((7, n)), pltpu.SemaphoreType.DMA((7, n)), pltpu.SemaphoreType.DMA((n,))]
    return _Exchange(list(gs), outs, sems, start, finish)


def run_exchange(ex, *, name):
    HBM = pl.BlockSpec(memory_space=pl.ANY)
    n_in, n_out = len(ex.inputs), len(ex.out_shapes)

    def body(*refs):
        ins, outs, sems = refs[:n_in], refs[n_in:n_in + n_out], refs[n_in + n_out:]
        ex.start(ins, outs, sems)
        ex.finish(ins, outs, sems)

    return list(_pcall(body, name=name, in_specs=[HBM] * n_in, out_specs=tuple([HBM] * n_out),
                       out_shape=tuple(ex.out_shapes), scratch_shapes=ex.sems)(*ex.inputs))


def all_gather(xs, *, name):
    outs = run_exchange(gather_exchange([xs[l] for l in range(xs.shape[0])]), name=name)
    return jnp.stack(outs)


def _slots(g):
    return g.reshape(N_DEV, g.shape[0] // N_DEV, g.shape[1])


def _rope_tables(positions):
    half = ROT_DIM // 2
    inv_freq = ROPE_THETA ** (-jnp.arange(0, ROT_DIM, 2, dtype=F32) / ROT_DIM)
    ang = positions.astype(F32)[:, None] * inv_freq
    cos, sin = jnp.cos(ang), jnp.sin(ang)
    S = positions.shape[0]
    c_t = jnp.concatenate([cos, cos, jnp.ones((S, LANE - ROT_DIM), F32)], axis=1)
    s1_t = jnp.concatenate([-sin, jnp.zeros((S, LANE - half), F32)], axis=1)
    s2_t = jnp.concatenate([jnp.zeros((S, half), F32), sin, jnp.zeros((S, LANE - ROT_DIM), F32)], axis=1)
    return c_t, s1_t, s2_t


def _row(v):
    return v.reshape(1, -1)


def kernel(x, c, positions, w_ada, b_ada, g_norm1, w_in, g_q, g_k, w_attn_proj, w_conv_dw, b_conv_dw, g_conv_ln, b_conv_ln, w_conv_out, w_o, g_norm2, w_ffn_in, w_ffn_dw, b_ffn_dw, w_ffn_down, loss_target, m_w_ada, m_b_ada, m_g_norm1, m_w_in, m_g_q, m_g_k, m_w_attn_proj, m_w_conv_dw, m_b_conv_dw, m_g_conv_ln, m_b_conv_ln, m_w_conv_out, m_w_o, m_g_norm2, m_w_ffn_in, m_w_ffn_dw, m_b_ffn_dw, m_w_ffn_down, v_w_ada, v_b_ada, v_g_norm1, v_w_in, v_g_q, v_g_k, v_w_attn_proj, v_w_conv_dw, v_b_conv_dw, v_g_conv_ln, v_b_conv_ln, v_w_conv_out, v_w_o, v_g_norm2, v_w_ffn_in, v_w_ffn_dw, v_b_ffn_dw, v_w_ffn_down):
    L = w_in.shape[0]
    S, D = x.shape[1], x.shape[2]
    FF = w_ffn_down.shape[1] * N_DEV
    xi, yi, ci = _my_pos()
    me = 4 * xi + 2 * yi + ci
    x0 = x[0]
    tabs = _rope_tables(positions[0])

    c_act = c * _sig(c)
    c_all = all_gather(jnp.pad(c_act, ((0, 7), (0, 0)))[None], name="ag_c")[0][:, 0, :]
    c_all16 = jnp.pad(c_all, ((0, 8), (0, 0)))
    m_part = jnp.stack([mm(c_all16, w_ada[l], "nn", name="mod_mm") for l in range(L)])
    m_all = all_gather(m_part, name="ag_mod")
    mod = lax.dynamic_index_in_dim(m_all, me, axis=2, keepdims=False).reshape(L, 6 * D) + b_ada
    mod = mod.reshape(L, 6, 1, D)

    sh_in = jnp.transpose(w_in, (0, 2, 1)).astype(BF16)
    sh_fi = jnp.transpose(w_ffn_in, (0, 2, 1)).astype(BF16)
    sh_ap = jnp.transpose(w_attn_proj, (0, 2, 1)).astype(BF16)
    sh_co, sh_oo, sh_dn = w_conv_out.astype(BF16), w_o.astype(BF16), w_ffn_down.astype(BF16)

    def rowcat(g):
        return g.reshape(N_DEV * g.shape[1], g.shape[2])

    wt_in, wt_fi, wt_ap, w_co, w_oo, w_dn = ([None] * L for _ in range(6))
    wt_in[0] = rowcat(run_exchange(gather_exchange([sh_in[0]]), name="ag_w0")[0])
    cdw = all_gather(jnp.pad(w_conv_dw, ((0, 0), (0, 1), (0, 0))), name="ag_cdw")
    cdw = jnp.transpose(cdw, (0, 2, 1, 3)).reshape(L, 32, D)[:, :CONV_K]
    fsh = w_ffn_dw.shape[2]
    fpad = -fsh % LANE
    fdw = all_gather(jnp.pad(w_ffn_dw, ((0, 0), (0, 8 - FFN_K), (0, fpad))), name="ag_fdw")
    fdw = jnp.transpose(fdw[:, :, :FFN_K, :fsh], (0, 2, 1, 3)).reshape(L, FFN_K, FF)

    QKV, CW = 3 * ATTN_W, 2 * D
    seg = ((0, QKV), (QKV, CW), (QKV + CW, 2 * D))

    saved = []
    x_prev, delta, gt_prev = x0, None, None
    for l in range(L):
        sh1, sc1, gt1, sh2, sc2, gt2 = (mod[l, i] for i in range(6))
        x_l, h = norm_mod_fwd(x_prev, delta, gt_prev, _row(g_norm1[l]), sc1, sh1, name="norm_fwd")
        nxt = l + 1 < L
        if l == 0:
            zq, got = mm(h, wt_in[0], "nt", b_off=0, b_len=QKV, name="z_mm_ag0",
                         comm=gather_exchange([sh_fi[0], sh_ap[0], sh_co[0], sh_oo[0], sh_dn[0]]))
            wt_fi[0], wt_ap[0], w_co[0], w_oo[0], w_dn[0] = (rowcat(g) for g in got)
        elif nxt:
            zq, got = mm(h, wt_in[l], "nt", b_off=0, b_len=QKV, name="z_mm_ag", comm=gather_exchange([sh_in[l + 1]]))
            wt_in[l + 1] = rowcat(got[0])
        else:
            zq = mm(h, wt_in[l], "nt", b_off=0, b_len=QKV, name="z_mm")
        if l == 0 and nxt:
            zc, got = mm(h, wt_in[0], "nt", b_off=seg[1][0], b_len=seg[1][1], name="zc_mm_ag",
                         comm=gather_exchange([sh_in[1]]))
            wt_in[1] = rowcat(got[0])
        else:
            zc = mm(h, wt_in[l], "nt", b_off=seg[1][0], b_len=seg[1][1], name="z_mm")
        zg = mm(h, wt_in[l], "nt", b_off=seg[2][0], b_len=seg[2][1], out_dtype=BF16, name="zg_mm")
        gq, gk = _row(g_q[l]), _row(g_k[l])
        qn, kn = qk_prep_fwd(zq, gq, gk, tabs, name="qk_prep")
        o_g, lse_g = [], []
        for gi in range(3):
            o_i, l_i = attn_fwd(qn, kn, zq, gi, name="attn_fwd%d" % gi)
            o_g.append(o_i)
            lse_g.append(l_i)
        attn = combine_fwd(o_g, lse_g, name="combine_fwd")
        y_a = mm(attn, wt_ap[l], "nt", out_dtype=BF16, name="ya_mm")
        cw, cb = cdw[l], _row(b_conv_dw[l])
        cg, cbl = _row(g_conv_ln[l]), _row(b_conv_ln[l])
        u2, y_conv = convb_fwd(zc, cw, cb, cg, cbl, name="convb_fwd")
        y_b = mm(u2, w_co[l], "nn", out_dtype=BF16, name="yb_mm")
        merged = merge_fwd(y_a, y_b, zg, name="merge_fwd")
        mo = mm(merged, w_oo[l], "nn", name="mo_mm")
        x_mid, h2 = norm_mod_fwd(x_l, mo, gt1, _row(g_norm2[l]), sc2, sh2, name="norm_fwd")
        if nxt:
            gu, got = mm(h2, wt_fi[l], "nt", out_dtype=BF16, name="gu_mm_ag", comm=gather_exchange([sh_fi[l + 1]]))
            wt_fi[l + 1] = rowcat(got[0])
        else:
            gu = mm(h2, wt_fi[l], "nt", out_dtype=BF16, name="gu_mm")
        fw, fb = fdw[l], _row(b_ffn_dw[l])
        act = ffn_act_fwd(gu, fw, fb, name="ffn_act")
        if nxt:
            ffo, got = mm(act, w_dn[l], "nn", name="ffo_mm_ag",
                          comm=gather_exchange([sh_ap[l + 1], sh_co[l + 1], sh_oo[l + 1], sh_dn[l + 1]]))
            wt_ap[l + 1], w_co[l + 1], w_oo[l + 1], w_dn[l + 1] = (rowcat(g) for g in got)
        else:
            ffo = mm(act, w_dn[l], "nn", name="ffo_mm")
        saved.append(dict(x=x_l, h=h, zq=zq, zc=zc, zg=zg, qn=qn, kn=kn, o=o_g, lse=lse_g, attn=attn, y_a=y_a,
                          u2=u2, y_conv=y_conv, y_b=y_b, merged=merged, mo=mo, x_mid=x_mid, h2=h2, gu=gu, act=act, ffo=ffo))
        x_prev, delta, gt_prev = x_mid, ffo, gt2

    dx, lpart, d_ffo, p_gt2 = loss_head(x_prev, delta, gt_prev, loss_target[0], name="loss_head")
    loss = lax.psum(0.5 / D * jnp.sum(lpart[0]), ("x", "y", "c"))

    land = {k: [None] * L for k in ("in", "fi", "ap", "co", "o", "dn")}
    small_rows = []
    for l in reversed(range(L)):
        sv = saved[l]
        sh1, sc1, gt1, sh2, sc2, gt2 = (mod[l, i] for i in range(6))
        d_act = mm(d_ffo, w_dn[l], "nt", out_dtype=BF16, name="dact_mm")
        g_dn = mm(sv["act"], d_ffo, "tn", out_dtype=BF16, name="dwdn_mm")
        fw, fb = fdw[l], _row(b_ffn_dw[l])
        dgu, p_fw, p_fb = ffn_act_bwd(d_act, sv["gu"], fw, fb, name="ffn_bwd")
        dh2, got = mm(dgu, wt_fi[l], "nn", name="dh2_mm_rs", comm=scatter_exchange([_slots(g_dn)]))
        land["dn"][l] = got[0]
        g_fi = mm(dgu, sv["h2"], "tn", out_dtype=BF16, name="dwfi_mm")
        dx, p_g2, p_sc2, p_sh2, d_mo, p_gt1 = norm_mod_bwd(sv["x_mid"], dh2, _row(g_norm2[l]), sc2, sh2, dx,
                                                           (sv["mo"], gt1), name="norm_bwd")
        d_merged = mm(d_mo, w_oo[l], "nt", name="dmerged_mm")
        g_o = mm(sv["merged"], d_mo, "tn", out_dtype=BF16, name="dwo_mm")
        d_ya, d_yb, dzg = merge_bwd(d_merged, sv["y_a"], sv["y_b"], sv["zg"], name="merge_bwd")
        d_attn = mm(d_ya, wt_ap[l], "nn", name="dattn_mm")
        g_ap = mm(d_ya, sv["attn"], "tn", out_dtype=BF16, name="dwap_mm")
        d_u2 = mm(d_yb, w_co[l], "nt", name="du2_mm")
        g_co = mm(sv["u2"], d_yb, "tn", out_dtype=BF16, name="dwco_mm")
        cw, cb = cdw[l], _row(b_conv_dw[l])
        cg, cbl = _row(g_conv_ln[l]), _row(b_conv_ln[l])
        dy, p_cw, p_cb, p_cg, p_cbl = convb_bwd1(d_u2, sv["y_conv"], sv["zc"], cg, cbl, name="convb_bwd1")
        dzc = convb_bwd2(dy, sv["zc"], cw, name="convb_bwd2")
        dd = combine_bwd(d_attn, sv["o"], sv["lse"], name="combine_bwd")
        do_g, cc_g = dd[:3], dd[3:]
        parts = [attn_bwd(sv["qn"], sv["kn"], sv["zq"], do_g[gi], sv["lse"][gi], cc_g[gi], gi,
                          name="attn_bwd%d" % gi) for gi in range(3)]
        gq, gk = _row(g_q[l]), _row(g_k[l])
        dzq, p_gq, p_gk = attn_bwd_post(sv["zq"], gq, gk, tabs, *[[p[i] for p in parts] for i in range(5)],
                                        name="attn_post")
        g_in_q, got = mm(dzq, sv["h"], "tn", out_dtype=BF16, name="dwin_mm_rs",
                         comm=scatter_exchange([_slots(g_fi), _slots(g_o), _slots(g_ap), _slots(g_co)]))
        land["fi"][l], land["o"][l], land["ap"][l], land["co"][l] = got
        g_in = jnp.concatenate([g_in_q] + [mm(dz_s, sv["h"], "tn", out_dtype=BF16, name="dwin_mm")
                                           for dz_s in (dzc, dzg)], axis=0)
        dh, got = mm_kcat([dzq, dzc, dzg], wt_in[l], name="dh_mm_rs", comm=scatter_exchange([_slots(g_in)]))
        land["in"][l] = got[0]
        if l > 0:
            dx, p_g1, p_sc1, p_sh1, d_ffo_prev, p_gt2_prev = norm_mod_bwd(
                sv["x"], dh, _row(g_norm1[l]), sc1, sh1, dx, (saved[l - 1]["ffo"], mod[l - 1, 5]), name="norm_bwd")
        else:
            dx, p_g1, p_sc1, p_sh1 = norm_mod_bwd(sv["x"], dh, _row(g_norm1[l]), sc1, sh1, dx, name="norm_bwd")

        def row1k(p):
            v = p[0]
            pad = -v.shape[0] % D
            return jnp.pad(v, (0, pad)).reshape(-1, D)

        rows = [row1k(p) for p in (p_sh1, p_sc1, p_gt1, p_sh2, p_sc2, p_gt2, p_g1, p_g2)]
        rows.append(row1k(jnp.concatenate([p_gq, p_gk], axis=1)))
        rows += [row1k(p) for p in (p_cb, p_cg, p_cbl, jnp.sum(p_fb, axis=0, keepdims=True))]
        rows.append(jnp.sum(p_cw.reshape(CONV_K, SUBLANES, D), axis=1))
        rows += [row1k(jnp.sum(p_fw[k * SUBLANES:(k + 1) * SUBLANES], axis=0, keepdims=True))
                 for k in range(FFN_K)]
        blk = jnp.concatenate(rows, axis=0)
        small_rows.append(jnp.pad(blk, ((0, -blk.shape[0] % 8), (0, 0))))
        if l > 0:
            d_ffo, p_gt2 = d_ffo_prev, p_gt2_prev
    small_rows = small_rows[::-1]
    n_small = small_rows[0].shape[0]
    ff_rows = -(-FF // D)

    small = jnp.concatenate(small_rows, axis=0)[None]
    small_all = all_gather(small, name="ag_small")[0]
    small_sum = sum_slots(small_all, name="sum_small").reshape(L, n_small, D)
    small_all = small_all.reshape(N_DEV, L, n_small, D)

    g_b_ada = small_sum[:, 0:6].reshape(L, 6 * D)
    g_g1, g_g2 = small_sum[:, 6], small_sum[:, 7]
    g_gq, g_gk = small_sum[:, 8, 0:LANE], small_sum[:, 8, LANE:2 * LANE]
    g_cb, g_cg, g_cbl = small_sum[:, 9], small_sum[:, 10], small_sum[:, 11]
    r0 = 12
    g_fb = small_sum[:, r0:r0 + ff_rows].reshape(L, -1)[:, :FF]
    r0 += ff_rows
    g_cw_full = small_sum[:, r0:r0 + CONV_K]
    r0 += CONV_K
    g_fw_full = small_sum[:, r0:r0 + FFN_K * ff_rows].reshape(L, FFN_K, -1)[:, :, :FF]
    csh = w_conv_dw.shape[2]
    g_cw = lax.dynamic_slice_in_dim(g_cw_full, me * csh, csh, axis=2)
    g_fw = lax.dynamic_slice_in_dim(g_fw_full, me * fsh, fsh, axis=2)

    ash = w_ada.shape[2]
    dmod_all = small_all[:, :, 0:6].reshape(N_DEV, L, 6 * D)
    dmod_mine = lax.dynamic_slice_in_dim(dmod_all, me * ash, ash, axis=2)
    g_w_ada = jnp.stack([mm(c_all16, jnp.pad(dmod_mine[:, l], ((0, 8), (0, 0))), "tn", name="dwada_mm")
                         for l in range(L)])

    def reduced(key, transposed):
        out = jnp.stack([sum_slots(slots, name="sum_" + key) for slots in land[key]])
        return jnp.transpose(out, (0, 2, 1)) if transposed else out

    g_w_in = reduced("in", True)
    g_w_fi = reduced("fi", True)
    g_w_ap = reduced("ap", True)
    g_w_co = reduced("co", False)
    g_w_o = reduced("o", False)
    g_w_dn = reduced("dn", False)

    grads = [g_w_ada, g_b_ada, g_g1, g_w_in, g_gq, g_gk, g_w_ap, g_cw, g_cb, g_cg, g_cbl, g_w_co, g_w_o, g_g2,
             g_w_fi, g_fw, g_fb, g_w_dn]
    ws = [w_ada, b_ada, g_norm1, w_in, g_q, g_k, w_attn_proj, w_conv_dw, b_conv_dw, g_conv_ln, b_conv_ln,
          w_conv_out, w_o, g_norm2, w_ffn_in, w_ffn_dw, b_ffn_dw, w_ffn_down]
    ms = [m_w_ada, m_b_ada, m_g_norm1, m_w_in, m_g_q, m_g_k, m_w_attn_proj, m_w_conv_dw, m_b_conv_dw, m_g_conv_ln,
          m_b_conv_ln, m_w_conv_out, m_w_o, m_g_norm2, m_w_ffn_in, m_w_ffn_dw, m_b_ffn_dw, m_w_ffn_down]
    vs = [v_w_ada, v_b_ada, v_g_norm1, v_w_in, v_g_q, v_g_k, v_w_attn_proj, v_w_conv_dw, v_b_conv_dw, v_g_conv_ln,
          v_b_conv_ln, v_w_conv_out, v_w_o, v_g_norm2, v_w_ffn_in, v_w_ffn_dw, v_b_ffn_dw, v_w_ffn_down]
    deltas, new_m, new_v = [], [], []
    for w_i, g_i, m_i, v_i in zip(ws, grads, ms, vs):
        d_i, mn_i, vn_i = adamw(w_i, g_i, m_i, v_i, name="adamw")
        deltas.append(d_i)
        new_m.append(mn_i)
        new_v.append(vn_i)
    return (loss, dx[None], *grads, *deltas, *new_m, *new_v)
```

```python
import functools
import math

import jax
import jax.numpy as jnp
from jax import lax
from jax.experimental import pallas as pl
from jax.experimental.pallas import tpu as pltpu

F32 = jnp.float32
BF16 = jnp.bfloat16
MESH = pl.DeviceIdType.MESH
N_DEV = 8

EPS = 1e-6
HEAD_DIM = 128
BLOCK = 128
DILATIONS = (1, 4, 16)
HEADS_PER_GROUP = 4
N_HEADS = 12
ATTN_W = N_HEADS * HEAD_DIM
ROT_DIM = 32
ROPE_THETA = 500000.0
CONV_K = 31
CONV_HALO = 32
FFN_K = 3
FFN_HALO = 16
NEG = -1e30

ADAM_LR, ADAM_B1, ADAM_B2, ADAM_EPS, ADAM_WD, ADAM_STEP = 0.001, 0.9, 0.999, 1e-08, 0.01, 10

LANE = 128
VMEM_LIMIT = 56 * 1024 * 1024
ROW_TILE = 512
STREAM_TILE = 1024
POST_TILE = 256
GLUE_RC = 64


def _pcall(body, **kw):
    return pl.pallas_call(body, **kw)


def _cparams(*sem):
    return pltpu.CompilerParams(dimension_semantics=sem, vmem_limit_bytes=VMEM_LIMIT)


def _sig(v):
    return 1.0 / (1.0 + jnp.exp(-v))


def _divtile(dim, target):
    best = None
    for t in range(LANE, min(dim, target) + 1, LANE):
        if dim % t == 0:
            best = t
    return best or dim


def _rows(t, c, col=0):
    return pl.BlockSpec((t, c), lambda i: (i, col))


def _full(shape):
    nd = len(shape)
    return pl.BlockSpec(shape, lambda i: (0,) * nd)


def _acc_rows(ref, val, i):
    @pl.when(i == 0)
    def _():
        ref[...] = jnp.zeros_like(ref)
    r = val.shape[0]
    ref[0:r, :] += val


MM_TILE = 1536


def mm(a, b, mode, *, name, out_dtype=F32, c_in=None, b_off=0, b_len=None, comm=None):
    if mode == "nn":
        M, K = a.shape
        N = b.shape[1]
    elif mode == "nt":
        M, K = a.shape
        N = b_len if b_len is not None else b.shape[0]
    else:
        K, M = a.shape
        N = b.shape[1]
    g_n = math.gcd(N, b_off) if (mode == "nt" and b_off) else N
    g_k = math.gcd(K, b_off) if (mode == "nn" and b_off) else K
    tn = _divtile(g_n, MM_TILE if c_in is None else 1024)
    tk = _divtile(g_k, MM_TILE if mode != "tn" else 2048)
    tm = _divtile(M, MM_TILE if mode == "tn" else (2048 if c_in is None else 1024))
    gm, gn, nk = M // tm, N // tn, K // tk
    n_ci = 0 if comm is None else len(comm.inputs)
    n_co = 0 if comm is None else len(comm.out_shapes)
    n_x = 2 + (c_in is not None)
    if mode == "nn":
        dims = (((1,), (0,)), ((), ()))
    elif mode == "nt":
        dims = (((1,), (1,)), ((), ()))
    else:
        dims = (((0,), (0,)), ((), ()))

    def body(*refs):
        a_ref, b_ref = refs[0], refs[1]
        c_ref = refs[2] if c_in is not None else None
        c_ins = refs[n_x:n_x + n_ci]
        o_ref = refs[n_x + n_ci]
        c_outs = refs[n_x + n_ci + 1:n_x + n_ci + 1 + n_co]
        rest = refs[n_x + n_ci + 1 + n_co:]
        acc = rest[0] if nk > 1 else None
        sems = rest[1:] if nk > 1 else rest
        i, j, k = pl.program_id(0), pl.program_id(1), pl.program_id(2)

        if comm is not None:
            @pl.when(jnp.logical_and(jnp.logical_and(i == 0, j == 0), k == 0))
            def _():
                comm.start(c_ins, c_outs, sems)

        prod = lax.dot_general(a_ref[...].astype(BF16), b_ref[...].astype(BF16), dims, preferred_element_type=F32)
        if nk == 1:
            if c_ref is not None:
                prod = prod + c_ref[...].astype(F32)
            o_ref[...] = prod.astype(out_dtype)
        else:
            @pl.when(k == 0)
            def _():
                if c_ref is None:
                    acc[...] = prod
                else:
                    acc[...] = prod + c_ref[...].astype(F32)

            @pl.when(k > 0)
            def _():
                acc[...] += prod

            @pl.when(k == nk - 1)
            def _():
                o_ref[...] = acc[...].astype(out_dtype)

        if comm is not None:
            @pl.when(jnp.logical_and(jnp.logical_and(i == gm - 1, j == gn - 1), k == nk - 1))
            def _():
                comm.finish(c_ins, c_outs, sems)

    if mode == "nn":
        a_spec = pl.BlockSpec((tm, tk), lambda i, j, k: (i, k))
        ob = b_off // tk
        b_spec = pl.BlockSpec((tk, tn), lambda i, j, k: (k + ob, j))
    elif mode == "nt":
        a_spec = pl.BlockSpec((tm, tk), lambda i, j, k: (i, k))
        ob = b_off // tn
        b_spec = pl.BlockSpec((tn, tk), lambda i, j, k: (j + ob, k))
    else:
        a_spec = pl.BlockSpec((tk, tm), lambda i, j, k: (k, i))
        b_spec = pl.BlockSpec((tk, tn), lambda i, j, k: (k, j))
    o_spec = pl.BlockSpec((tm, tn), lambda i, j, k: (i, j))
    HBM = pl.BlockSpec(memory_space=pl.ANY)
    in_specs = [a_spec, b_spec]
    args = [a, b]
    if c_in is not None:
        in_specs.append(o_spec)
        args.append(c_in)
    scratch = [pltpu.VMEM((tm, tn), F32)] if nk > 1 else []
    o_shape = jax.ShapeDtypeStruct((M, N), out_dtype)
    if comm is None:
        return _pcall(
            body, name=name, grid=(gm, gn, nk), in_specs=in_specs, out_specs=o_spec, out_shape=o_shape,
            scratch_shapes=scratch, compiler_params=_cparams("parallel", "parallel", "arbitrary"),
        )(*args)
    res = _pcall(
        body, name=name, grid=(gm, gn, nk), in_specs=in_specs + [HBM] * n_ci,
        out_specs=(o_spec, *[HBM] * n_co), out_shape=(o_shape, *comm.out_shapes),
        scratch_shapes=scratch + comm.sems, compiler_params=_cparams("arbitrary", "arbitrary", "arbitrary"),
    )(*args, *comm.inputs)
    return res[0], list(res[1:])


def mm_kcat(a_list, b, *, name, comm=None):
    M = a_list[0].shape[0]
    ks = [a.shape[1] for a in a_list]
    N = b.shape[1]
    tk = _divtile(math.gcd(*ks), 1024)
    tm, tn = _divtile(M, 2048), _divtile(N, 1024)
    nks = [k // tk for k in ks]
    starts = [sum(nks[:t]) for t in range(len(ks))]
    nk = sum(nks)
    gm, gn = M // tm, N // tn
    n_a = len(a_list)
    n_ci = 0 if comm is None else len(comm.inputs)
    n_co = 0 if comm is None else len(comm.out_shapes)

    def body(*refs):
        a_refs, b_ref = refs[:n_a], refs[n_a]
        c_ins = refs[n_a + 1:n_a + 1 + n_ci]
        o_ref = refs[n_a + 1 + n_ci]
        c_outs = refs[n_a + 2 + n_ci:n_a + 2 + n_ci + n_co]
        acc = refs[n_a + 2 + n_ci + n_co]
        sems = refs[n_a + 3 + n_ci + n_co:]
        i, j, k = pl.program_id(0), pl.program_id(1), pl.program_id(2)

        if comm is not None:
            @pl.when(jnp.logical_and(jnp.logical_and(i == 0, j == 0), k == 0))
            def _():
                comm.start(c_ins, c_outs, sems)

        for t in range(n_a):
            @pl.when(jnp.logical_and(k >= starts[t], k < starts[t] + nks[t]))
            def _(t=t):
                prod = jnp.dot(a_refs[t][...].astype(BF16), b_ref[...].astype(BF16), preferred_element_type=F32)
                if t == 0:
                    @pl.when(k == 0)
                    def _():
                        acc[...] = prod

                    @pl.when(k > 0)
                    def _():
                        acc[...] += prod
                else:
                    acc[...] += prod

        @pl.when(k == nk - 1)
        def _():
            o_ref[...] = acc[...]

        if comm is not None:
            @pl.when(jnp.logical_and(jnp.logical_and(i == gm - 1, j == gn - 1), k == nk - 1))
            def _():
                comm.finish(c_ins, c_outs, sems)

    a_specs = [pl.BlockSpec((tm, tk), functools.partial(
        lambda i, j, k, s, n: (i, jnp.clip(k - s, 0, n - 1)), s=starts[t], n=nks[t])) for t in range(n_a)]
    b_spec = pl.BlockSpec((tk, tn), lambda i, j, k: (k, j))
    o_spec = pl.BlockSpec((tm, tn), lambda i, j, k: (i, j))
    HBM = pl.BlockSpec(memory_space=pl.ANY)
    o_shape = jax.ShapeDtypeStruct((M, N), F32)
    scratch = [pltpu.VMEM((tm, tn), F32)]
    if comm is None:
        return _pcall(
            body, name=name, grid=(gm, gn, nk), in_specs=a_specs + [b_spec], out_specs=o_spec, out_shape=o_shape,
            scratch_shapes=scratch, compiler_params=_cparams("parallel", "parallel", "arbitrary"),
        )(*a_list, b)
    res = _pcall(
        body, name=name, grid=(gm, gn, nk), in_specs=a_specs + [b_spec] + [HBM] * n_ci,
        out_specs=(o_spec, *[HBM] * n_co), out_shape=(o_shape, *comm.out_shapes),
        scratch_shapes=scratch + comm.sems, compiler_params=_cparams("arbitrary", "arbitrary", "arbitrary"),
    )(*a_list, b, *comm.inputs)
    return res[0], list(res[1:])


def norm_mod_fwd(x_prev, delta, gt, g, sc, sh, *, name):
    S, D = x_prev.shape
    T = min(STREAM_TILE, S)
    has_delta = delta is not None

    def body(*refs):
        if has_delta:
            xp, dl, gt_r, g_r, sc_r, sh_r, x_out, h_out = refs
            xv = xp[...] + gt_r[...] * dl[...]
            x_out[...] = xv
        else:
            xp, g_r, sc_r, sh_r, h_out = refs
            xv = xp[...]
        r = lax.rsqrt(jnp.mean(xv * xv, axis=-1, keepdims=True) + EPS)
        h_out[...] = ((xv * r) * g_r[...] * (1.0 + sc_r[...]) + sh_r[...]).astype(BF16)

    vec = _full((1, D))
    if has_delta:
        ins, specs = [x_prev, delta, gt, g, sc, sh], [_rows(T, D), _rows(T, D), vec, vec, vec, vec]
        outs = (jax.ShapeDtypeStruct((S, D), F32), jax.ShapeDtypeStruct((S, D), BF16))
        ospecs = (_rows(T, D), _rows(T, D))
    else:
        ins, specs = [x_prev, g, sc, sh], [_rows(T, D), vec, vec, vec]
        outs = jax.ShapeDtypeStruct((S, D), BF16)
        ospecs = _rows(T, D)
    res = _pcall(body, name=name, grid=(S // T,), in_specs=specs, out_specs=ospecs, out_shape=outs,
                 compiler_params=_cparams("parallel"))(*ins)
    return res if has_delta else (x_prev, res)


def norm_mod_bwd(x, dh, g, sc, sh, dx_res, res=None, *, name):
    S, D = x.shape
    T = ROW_TILE
    has_res = res is not None

    def body(*refs):
        x_r, dh_r, g_r, sc_r, sh_r, dr_r = refs[:6]
        dx_o, dg_o, dsc_o, dsh_o = refs[6 + 2 * has_res:10 + 2 * has_res]
        i = pl.program_id(0)
        xv = x_r[...]
        dh_v = dh_r[...]
        r = lax.rsqrt(jnp.mean(xv * xv, axis=-1, keepdims=True) + EPS)
        xh = xv * r
        dn = dh_v * (1.0 + sc_r[...])
        dxh = dn * g_r[...]
        dx = dr_r[...] + r * (dxh - xh * jnp.mean(dxh * xh, axis=-1, keepdims=True))
        dx_o[...] = dx
        _acc_rows(dg_o, jnp.sum(dn * xh, axis=0, keepdims=True), i)
        _acc_rows(dsc_o, jnp.sum(dh_v * (xh * g_r[...]), axis=0, keepdims=True), i)
        _acc_rows(dsh_o, jnp.sum(dh_v, axis=0, keepdims=True), i)
        if has_res:
            dl_r, gt_r = refs[6:8]
            dd_o, dgt_o = refs[12:14]
            dd_o[...] = (dx * gt_r[...]).astype(BF16)
            _acc_rows(dgt_o, jnp.sum(dx * dl_r[...], axis=0, keepdims=True), i)

    vec = _full((1, D))
    part = jax.ShapeDtypeStruct((8, D), F32)
    in_specs = [_rows(T, D), _rows(T, D), vec, vec, vec, _rows(T, D)]
    out_specs = [_rows(T, D), _full((8, D)), _full((8, D)), _full((8, D))]
    out_shape = [jax.ShapeDtypeStruct((S, D), F32), part, part, part]
    args = [x, dh, g, sc, sh, dx_res]
    if has_res:
        in_specs += [_rows(T, D), vec]
        out_specs += [_rows(T, D), _full((8, D))]
        out_shape += [jax.ShapeDtypeStruct((S, D), BF16), part]
        args += list(res)
    return _pcall(
        body, name=name, grid=(S // T,), in_specs=in_specs, out_specs=tuple(out_specs), out_shape=tuple(out_shape),
        compiler_params=_cparams("arbitrary"),
    )(*args)


def loss_head(x_mid, ffo, gt, target, *, name):
    S, D = x_mid.shape
    T = ROW_TILE

    def body(x_r, f_r, gt_r, t_r, dy_o, l_o, dd_o, dgt_o):
        i = pl.program_id(0)
        fv = f_r[...]
        e = x_r[...] + gt_r[...] * fv - t_r[...]
        dy = e * (1.0 / D)
        dy_o[...] = dy
        _acc_rows(l_o, jnp.sum(e * e, axis=0, keepdims=True), i)
        dd_o[...] = (dy * gt_r[...]).astype(BF16)
        _acc_rows(dgt_o, jnp.sum(dy * fv, axis=0, keepdims=True), i)

    part = jax.ShapeDtypeStruct((8, D), F32)
    return _pcall(
        body, name=name, grid=(S // T,),
        in_specs=[_rows(T, D), _rows(T, D), _full((1, D)), _rows(T, D)],
        out_specs=(_rows(T, D), _full((8, D)), _rows(T, D), _full((8, D))),
        out_shape=(jax.ShapeDtypeStruct((S, D), F32), part, jax.ShapeDtypeStruct((S, D), BF16), part),
        compiler_params=_cparams("arbitrary"),
    )(x_mid, ffo, gt, target)


def _rope(t, c_t, s1_t, s2_t):
    return t * c_t + pltpu.roll(t, LANE - ROT_DIM // 2, 1) * s1_t + pltpu.roll(t, ROT_DIM // 2, 1) * s2_t


def _rope_t(d, c_t, s1_t, s2_t):
    return d * c_t + pltpu.roll(d * s1_t, ROT_DIM // 2, 1) + pltpu.roll(d * s2_t, LANE - ROT_DIM // 2, 1)


def qk_prep_fwd(zq, g_q, g_k, tabs, *, name):
    S = zq.shape[0]
    T = ROW_TILE

    def body(q_r, k_r, gq_r, gk_r, c_r, s1_r, s2_r, qn_o, kn_o):
        c_t, s1_t, s2_t = c_r[...], s1_r[...], s2_r[...]
        for src, g_r, dst in ((q_r, gq_r, qn_o), (k_r, gk_r, kn_o)):
            for h in range(N_HEADS):
                cols = slice(h * HEAD_DIM, (h + 1) * HEAD_DIM)
                t = src[:, cols]
                r = lax.rsqrt(jnp.mean(t * t, axis=-1, keepdims=True) + EPS)
                dst[:, cols] = _rope(t * r * g_r[...], c_t, s1_t, s2_t)

    tab = _rows(T, LANE)
    shp = jax.ShapeDtypeStruct((S, ATTN_W), F32)
    return _pcall(
        body, name=name, grid=(S // T,),
        in_specs=[_rows(T, ATTN_W, 0), _rows(T, ATTN_W, 1), _full((1, LANE)), _full((1, LANE)), tab, tab, tab],
        out_specs=(_rows(T, ATTN_W), _rows(T, ATTN_W)), out_shape=(shp, shp),
        compiler_params=_cparams("parallel"),
    )(zq, zq, g_q, g_k, *tabs)


def attn_bwd_post(zq, g_q, g_k, tabs, dq, dkc, dkp, dvc, dvp, *, name):
    S = zq.shape[0]
    T = min(POST_TILE, S)
    nblk = S // T
    GW = HEADS_PER_GROUP * HEAD_DIM
    n_ref = [7 if BLOCK * d < T else 5 for d in DILATIONS]

    def body(*refs):
        q_r, k_r, gq_r, gk_r, c_r, s1_r, s2_r = refs[:7]
        grp = refs[7:7 + sum(n_ref)]
        dz_o, dgq_o, dgk_o = refs[7 + sum(n_ref):]
        i = pl.program_id(0)
        dgq = jnp.zeros((1, LANE), F32)
        dgk = jnp.zeros((1, LANE), F32)
        RC = T
        for r0 in range(0, T, RC):
            rows = slice(r0, r0 + RC)
            c_t, s1_t, s2_t = c_r[rows, :], s1_r[rows, :], s2_r[rows, :]
            at = 0
            for gi, d in enumerate(DILATIONS):
                g_refs = grp[at:at + n_ref[gi]]
                at += n_ref[gi]
                sh = BLOCK * d
                if sh < T:
                    dq_r, dkc_r, dkp_r, dkpn_r, dvc_r, dvp_r, dvpn_r = g_refs
                    live = jnp.where(i + 1 < nblk, 1.0, 0.0)

                    def shifted(cur_r, nxt_r, gc, live=live, lo=r0 + sh):
                        if lo + RC <= T:
                            return cur_r[lo:lo + RC, gc]
                        if lo >= T:
                            return live * nxt_r[lo - T:lo - T + RC, gc]
                        return jnp.concatenate([cur_r[lo:T, gc], live * nxt_r[0:lo + RC - T, gc]], axis=0)
                else:
                    dq_r, dkc_r, dkp_r, dvc_r, dvp_r = g_refs
                    dkpn_r = dvpn_r = None
                    live = jnp.where(i + sh // T < nblk, 1.0, 0.0)

                    def shifted(cur_r, nxt_r, gc, live=live, rows=rows):
                        return live * cur_r[rows, gc]
                for hh in range(HEADS_PER_GROUP):
                    h = gi * HEADS_PER_GROUP + hh
                    cols = slice(h * HEAD_DIM, (h + 1) * HEAD_DIM)
                    gc = slice(hh * HEAD_DIM, (hh + 1) * HEAD_DIM)
                    dk_v = dkc_r[rows, gc] + shifted(dkp_r, dkpn_r, gc)
                    dv_v = dvc_r[rows, gc] + shifted(dvp_r, dvpn_r, gc)
                    dz_o[rows, 2 * ATTN_W + h * HEAD_DIM:2 * ATTN_W + (h + 1) * HEAD_DIM] = dv_v.astype(BF16)
                    for which, (src, g_r, d_out) in enumerate(((q_r, gq_r, dq_r[rows, gc]), (k_r, gk_r, dk_v))):
                        t = src[rows, cols]
                        r = lax.rsqrt(jnp.mean(t * t, axis=-1, keepdims=True) + EPS)
                        xh = t * r
                        dtn = _rope_t(d_out, c_t, s1_t, s2_t)
                        dxh = dtn * g_r[...]
                        dt = r * (dxh - xh * jnp.mean(dxh * xh, axis=-1, keepdims=True))
                        dz_o[rows, which * ATTN_W + h * HEAD_DIM:which * ATTN_W + (h + 1) * HEAD_DIM] = dt.astype(BF16)
                        part = jnp.sum(dtn * xh, axis=0, keepdims=True)
                        if which == 0:
                            dgq = dgq + part
                        else:
                            dgk = dgk + part
        _acc_rows(dgq_o, dgq, i)
        _acc_rows(dgk_o, dgk, i)

    tab = _rows(T, LANE)
    specs = [_rows(T, ATTN_W, 0), _rows(T, ATTN_W, 1), _full((1, LANE)), _full((1, LANE)), tab, tab, tab]
    args = [zq, zq, g_q, g_k, *tabs]
    for gi, d in enumerate(DILATIONS):
        cur = _rows(T, GW)
        sh = BLOCK * d
        if sh < T:
            head = pl.BlockSpec((sh, GW), functools.partial(
                lambda i, k, last: (jnp.minimum((i + 1) * k, last), 0), k=T // sh, last=S // sh - 1))
            specs += [cur, cur, cur, head, cur, cur, head]
            args += [dq[gi], dkc[gi], dkp[gi], dkp[gi], dvc[gi], dvp[gi], dvp[gi]]
        else:
            nxt = pl.BlockSpec((T, GW), functools.partial(lambda i, s: (jnp.minimum(i + s, nblk - 1), 0), s=sh // T))
            specs += [cur, cur, nxt, cur, nxt]
            args += [dq[gi], dkc[gi], dkp[gi], dvc[gi], dvp[gi]]
    part = jax.ShapeDtypeStruct((8, LANE), F32)
    return _pcall(
        body, name=name, grid=(nblk,), in_specs=specs,
        out_specs=(_rows(T, 3 * ATTN_W), _full((8, LANE)), _full((8, LANE))),
        out_shape=(jax.ShapeDtypeStruct((S, 3 * ATTN_W), BF16), part, part),
        compiler_params=_cparams("arbitrary"),
    )(*args)


ATTN_UNITS = 16


def _attn_geometry(d, S):
    R = min(ATTN_UNITS * BLOCK, S)
    return R, R // (BLOCK * d), S // R


def _sub_rows(j, r, d):
    if d == 1:
        return pl.ds(j * BLOCK, BLOCK)
    return pl.ds(j * BLOCK * d + r, BLOCK, stride=d)


def _dot_nt(a, b):
    return lax.dot_general(a, b, (((1,), (1,)), ((), ())), preferred_element_type=F32)


def _dot_tn(a, b):
    return lax.dot_general(a, b, (((0,), (0,)), ((), ())), preferred_element_type=F32)


def _attn_specs(gi, R):
    h0 = gi * HEADS_PER_GROUP
    vcol = 2 * N_HEADS + h0
    cur = lambda off: pl.BlockSpec((R, HEAD_DIM), lambda h, n: (n, off + h))
    prev = lambda off: pl.BlockSpec((R, HEAD_DIM), lambda h, n: (jnp.maximum(n - 1, 0), off + h))
    return [cur(h0), cur(h0), prev(h0), cur(vcol), prev(vcol)]


ATTN_UNROLL = ATTN_UNITS


def _attn_masks(n):
    qi = lax.broadcasted_iota(jnp.int32, (BLOCK, 2 * BLOCK), 0)
    kj = lax.broadcasted_iota(jnp.int32, (BLOCK, 2 * BLOCK), 1)
    band = jnp.where(jnp.logical_and(kj >= qi, kj <= qi + BLOCK), 0.0, NEG)
    no_prev = band + jnp.where(kj < BLOCK, 1.0, 0.0) * jnp.where(n > 0, 0.0, NEG)
    return band, no_prev


def _attn_keys(kc_r, kp_r, vc_r, vp_r, j, r, d, J):
    rq = _sub_rows(j, r, d)
    if j > 0:
        rp = _sub_rows(j - 1, r, d)
        kp, vp = kc_r[rp, :], vc_r[rp, :]
    else:
        rp = _sub_rows(J - 1, r, d)
        kp, vp = kp_r[rp, :], vp_r[rp, :]
    kk = jnp.concatenate([kp, kc_r[rq, :]], axis=0).astype(BF16)
    vv = jnp.concatenate([vp, vc_r[rq, :]], axis=0).astype(BF16)
    return kk, vv


def _attn_units(unit, d, J):
    for j in range(J):
        if d == 1:
            unit(j, 0)
        else:
            def step(r, carry, j=j):
                unit(j, r)
                return carry
            lax.fori_loop(0, d, step, 0, unroll=min(d, ATTN_UNROLL))


def attn_fwd(qn, kn, zq, gi, *, name):
    S = qn.shape[0]
    d = DILATIONS[gi]
    R, J, nblk = _attn_geometry(d, S)
    scale = HEAD_DIM ** -0.5
    GW = HEADS_PER_GROUP * HEAD_DIM

    def body(q_r, kc_r, kp_r, vc_r, vp_r, o_o, l_o):
        band, no_prev = _attn_masks(pl.program_id(1))

        def unit(j, r):
            rq = _sub_rows(j, r, d)
            q = q_r[rq, :].astype(BF16)
            kk, vv = _attn_keys(kc_r, kp_r, vc_r, vp_r, j, r, d, J)
            s = _dot_nt(q, kk) * scale + (band if j > 0 else no_prev)
            m = jnp.max(s, axis=-1, keepdims=True)
            p = jnp.exp(s - m)
            l = jnp.sum(p, axis=-1, keepdims=True)
            o_o[rq, :] = jnp.dot(p.astype(BF16), vv, preferred_element_type=F32) / l
            l_o[rq, :] = jnp.broadcast_to(m + jnp.log(l), (BLOCK, HEAD_DIM))

        _attn_units(unit, d, J)

    ospec = pl.BlockSpec((R, HEAD_DIM), lambda h, n: (n, h))
    shp = jax.ShapeDtypeStruct((S, GW), F32)
    return _pcall(
        body, name=name, grid=(HEADS_PER_GROUP, nblk), in_specs=_attn_specs(gi, R),
        out_specs=(ospec, ospec), out_shape=(shp, shp),
        compiler_params=_cparams("parallel", "arbitrary"),
    )(qn, kn, kn, zq, zq)


def attn_bwd(qn, kn, zq, do, lse, cc, gi, *, name):
    S = qn.shape[0]
    d = DILATIONS[gi]
    R, J, nblk = _attn_geometry(d, S)
    scale = HEAD_DIM ** -0.5
    GW = HEADS_PER_GROUP * HEAD_DIM

    def body(q_r, kc_r, kp_r, vc_r, vp_r, do_r, l_r, c_r, dq_o, dkc_o, dkp_o, dvc_o, dvp_o):
        band, no_prev = _attn_masks(pl.program_id(1))

        def unit(j, r):
            rq = _sub_rows(j, r, d)
            q = q_r[rq, :].astype(BF16)
            kk, vv = _attn_keys(kc_r, kp_r, vc_r, vp_r, j, r, d, J)
            s_mask = band if j > 0 else no_prev
            dob = do_r[rq, :].astype(BF16)
            lv = l_r[rq, :]
            cv = c_r[rq, :]
            lv2 = jnp.concatenate([lv, lv], axis=1)
            cv2 = jnp.concatenate([cv, cv], axis=1)
            p = jnp.exp(_dot_nt(q, kk) * scale + s_mask - lv2)
            ds = (p * (_dot_nt(dob, vv) + cv2)).astype(BF16)
            dq_o[rq, :] = jnp.dot(ds, kk, preferred_element_type=F32) * scale
            dk2 = _dot_tn(ds, q) * scale
            dv2 = _dot_tn(p.astype(BF16), dob)
            dkp_o[rq, :] = dk2[0:BLOCK]
            dkc_o[rq, :] = dk2[BLOCK:2 * BLOCK]
            dvp_o[rq, :] = dv2[0:BLOCK]
            dvc_o[rq, :] = dv2[BLOCK:2 * BLOCK]

        _attn_units(unit, d, J)

    ospec = pl.BlockSpec((R, HEAD_DIM), lambda h, n: (n, h))
    shp = jax.ShapeDtypeStruct((S, GW), F32)
    return _pcall(
        body, name=name, grid=(HEADS_PER_GROUP, nblk),
        in_specs=_attn_specs(gi, R) + [ospec, ospec, ospec],
        out_specs=(ospec,) * 5, out_shape=(shp,) * 5,
        compiler_params=_cparams("parallel", "arbitrary"),
    )(qn, kn, kn, zq, zq, do, lse, cc)


def combine_fwd(o, lse, *, name):
    S, GW = o[0].shape
    T = ROW_TILE

    def body(o0, o1, o2, l0, l1, l2, a_o):
        m = jnp.maximum(jnp.maximum(l0[...], l1[...]), l2[...])
        e0, e1, e2 = jnp.exp(l0[...] - m), jnp.exp(l1[...] - m), jnp.exp(l2[...] - m)
        a_o[...] = ((e0 * o0[...] + e1 * o1[...] + e2 * o2[...]) / (e0 + e1 + e2)).astype(BF16)

    return _pcall(
        body, name=name, grid=(S // T,), in_specs=[_rows(T, GW)] * 6, out_specs=_rows(T, GW),
        out_shape=jax.ShapeDtypeStruct((S, GW), BF16), compiler_params=_cparams("parallel"),
    )(*o, *lse)


def combine_bwd(d_attn, o, lse, *, name):
    S, GW = d_attn.shape
    T = ROW_TILE

    def body(da_r, o0, o1, o2, l0, l1, l2, d0, d1, d2, c0, c1, c2):
        for r0 in range(0, T, GLUE_RC):
            rows = slice(r0, r0 + GLUE_RC)
            for hh in range(HEADS_PER_GROUP):
                cols = slice(hh * HEAD_DIM, (hh + 1) * HEAD_DIM)
                lv = [l_r[rows, cols] for l_r in (l0, l1, l2)]
                m = jnp.maximum(jnp.maximum(lv[0], lv[1]), lv[2])
                e = [jnp.exp(v - m) for v in lv]
                inv = 1.0 / (e[0] + e[1] + e[2])
                w = [e_g * inv for e_g in e]
                da = da_r[rows, cols]
                attn = w[0] * o0[rows, cols] + w[1] * o1[rows, cols] + w[2] * o2[rows, cols]
                a_h = jnp.sum(da * attn, axis=-1, keepdims=True)
                for w_g, d_o, c_o in zip(w, (d0, d1, d2), (c0, c1, c2)):
                    d_o[rows, cols] = w_g * da
                    c_o[rows, cols] = -w_g * a_h

    shp = jax.ShapeDtypeStruct((S, GW), F32)
    return _pcall(
        body, name=name, grid=(S // T,), in_specs=[_rows(T, GW)] * 7, out_specs=(_rows(T, GW),) * 6,
        out_shape=(shp,) * 6, compiler_params=_cparams("parallel"),
    )(d_attn, *o, *lse)


def _halo_prev(T, H, C, col):
    k = T // H
    return pl.BlockSpec((H, C), lambda i: (jnp.maximum(i * k - 1, 0), col))


def _halo_next(T, H, C, col, n_rows):
    k = T // H
    last = n_rows // H - 1
    return pl.BlockSpec((H, C), lambda i: (jnp.minimum((i + 1) * k, last), col))


CONV_RC = 64
SUBLANES = 8


def _tap_groups(offs):
    groups = {}
    for k, off in offs:
        groups.setdefault(off % SUBLANES, []).append((k, off))
    return [taps for _, taps in sorted(groups.items())]


def _for_taps(src, r0, lanes, offs, fn):
    for taps in _tap_groups(offs):
        lo = min(off for _, off in taps)
        hi = max(off for _, off in taps)
        sb = src[r0 + lo:r0 + hi + CONV_RC, lanes]
        for k, off in taps:
            fn(k, sb[off - lo:off - lo + CONV_RC])


def _dwconv(src, w_r, offs, T, C, bias_r, dst):
    for rc in range(T // CONV_RC):
        for cc in range(C // LANE):
            lanes = slice(cc * LANE, (cc + 1) * LANE)
            r0 = rc * CONV_RC
            acc = None if bias_r is None else jnp.zeros((CONV_RC, LANE), F32) + bias_r[:, lanes]
            for taps in _tap_groups(offs):
                lo = min(off for _, off in taps)
                hi = max(off for _, off in taps)
                sb = src[r0 + lo:r0 + hi + CONV_RC, lanes]
                g_acc = None
                for k, off in taps:
                    term = w_r[k:k + 1, lanes] * sb[off - lo:off - lo + CONV_RC]
                    g_acc = term if g_acc is None else g_acc + term
                acc = g_acc if acc is None else acc + g_acc
            dst[r0:r0 + CONV_RC, lanes] = acc


def _fill_glu(cv, cg, hv, hg, ubuf, i):
    T = cv.shape[0]
    live = jnp.where(i > 0, 1.0, 0.0)
    ubuf[0:CONV_HALO, :] = live * (hv[...] * _sig(hg[...]))
    ubuf[CONV_HALO:CONV_HALO + T, :] = cv[...] * _sig(cg[...])


_CONV_FWD_OFFS = [(k, CONV_HALO - (CONV_K - 1) + k) for k in range(CONV_K)]
_CONV_BWD_OFFS = [(k, CONV_K - 1 - k) for k in range(CONV_K)]


def convb_fwd(zc, w, b, g_ln, b_ln, *, name):
    S = zc.shape[0]
    C = zc.shape[1] // 2
    T = ROW_TILE

    def body(cv, cg, hv, hg, w_r, b_r, g_r, bl_r, u_o, y_o, ubuf):
        _fill_glu(cv, cg, hv, hg, ubuf, pl.program_id(0))
        _dwconv(ubuf, w_r, _CONV_FWD_OFFS, T, C, b_r, y_o)
        y = y_o[...]
        mu = jnp.mean(y, axis=-1, keepdims=True)
        yc = y - mu
        rs = lax.rsqrt(jnp.mean(yc * yc, axis=-1, keepdims=True) + EPS)
        v = yc * rs * g_r[...] + bl_r[...]
        u_o[...] = (v * _sig(v)).astype(BF16)

    vec = _full((1, C))
    return _pcall(
        body, name=name, grid=(S // T,),
        in_specs=[_rows(T, C, 0), _rows(T, C, 1), _halo_prev(T, CONV_HALO, C, 0), _halo_prev(T, CONV_HALO, C, 1),
                  _full((CONV_K, C)), vec, vec, vec],
        out_specs=(_rows(T, C), _rows(T, C)),
        out_shape=(jax.ShapeDtypeStruct((S, C), BF16), jax.ShapeDtypeStruct((S, C), F32)),
        scratch_shapes=[pltpu.VMEM((CONV_HALO + T, C), F32)],
        compiler_params=_cparams("parallel"),
    )(zc, zc, zc, zc, w, b, g_ln, b_ln)


def convb_bwd1(d_u2, y_conv, zc, g_ln, b_ln, *, name):
    S = zc.shape[0]
    C = zc.shape[1] // 2
    T = ROW_TILE

    def body(du_r, y_r, cv, cg, hv, hg, g_r, bl_r, dy_o, dw_o, db_o, dg_o, dbl_o, ubuf):
        i = pl.program_id(0)
        _fill_glu(cv, cg, hv, hg, ubuf, i)
        y = y_r[...]
        mu = jnp.mean(y, axis=-1, keepdims=True)
        yc = y - mu
        rs = lax.rsqrt(jnp.mean(yc * yc, axis=-1, keepdims=True) + EPS)
        yn = yc * rs
        v = yn * g_r[...] + bl_r[...]
        sg = _sig(v)
        dv = du_r[...] * (sg * (1.0 + v * (1.0 - sg)))
        dyn = dv * g_r[...]
        dy = rs * (dyn - jnp.mean(dyn, axis=-1, keepdims=True) - yn * jnp.mean(dyn * yn, axis=-1, keepdims=True))
        dy_o[...] = dy
        _acc_rows(dg_o, jnp.sum(dv * yn, axis=0, keepdims=True), i)
        _acc_rows(dbl_o, jnp.sum(dv, axis=0, keepdims=True), i)
        _acc_rows(db_o, jnp.sum(dy, axis=0, keepdims=True), i)

        @pl.when(i == 0)
        def _():
            dw_o[...] = jnp.zeros_like(dw_o)
        for cc in range(C // LANE):
            lanes = slice(cc * LANE, (cc + 1) * LANE)
            parts = [jnp.zeros((SUBLANES, LANE), F32) for _ in range(CONV_K)]
            for rc in range(T // CONV_RC):
                r0 = rc * CONV_RC
                dyc = dy_o[r0:r0 + CONV_RC, lanes]

                def tap(k, chunk, parts=parts, dyc=dyc):
                    prod = (dyc * chunk).reshape(CONV_RC // SUBLANES, SUBLANES, LANE)
                    parts[k] = parts[k] + jnp.sum(prod, axis=0)

                _for_taps(ubuf, r0, lanes, _CONV_FWD_OFFS, tap)
            for k in range(CONV_K):
                dw_o[k * SUBLANES:(k + 1) * SUBLANES, lanes] += parts[k]

    vec = _full((1, C))
    part = jax.ShapeDtypeStruct((8, C), F32)
    return _pcall(
        body, name=name, grid=(S // T,),
        in_specs=[_rows(T, C), _rows(T, C), _rows(T, C, 0), _rows(T, C, 1), _halo_prev(T, CONV_HALO, C, 0),
                  _halo_prev(T, CONV_HALO, C, 1), vec, vec],
        out_specs=(_rows(T, C), _full((CONV_K * SUBLANES, C)), _full((8, C)), _full((8, C)), _full((8, C))),
        out_shape=(jax.ShapeDtypeStruct((S, C), F32), jax.ShapeDtypeStruct((CONV_K * SUBLANES, C), F32),
                   part, part, part),
        scratch_shapes=[pltpu.VMEM((CONV_HALO + T, C), F32)],
        compiler_params=_cparams("arbitrary"),
    )(d_u2, y_conv, zc, zc, zc, zc, g_ln, b_ln)


def convb_bwd2(dy, zc, w, *, name):
    S = zc.shape[0]
    C = zc.shape[1] // 2
    T = ROW_TILE
    nblk = S // T

    def body(dy_r, dyn_r, cv, cg, w_r, dz_o, dbuf, dubuf):
        i = pl.program_id(0)
        live = jnp.where(i < nblk - 1, 1.0, 0.0)
        dbuf[0:T, :] = dy_r[...]
        dbuf[T:T + CONV_HALO, :] = live * dyn_r[...]
        _dwconv(dbuf, w_r, _CONV_BWD_OFFS, T, C, None, dubuf)
        du = dubuf[...]
        sg = _sig(cg[...])
        dz_o[:, 0:C] = (du * sg).astype(BF16)
        dz_o[:, C:2 * C] = (du * cv[...] * sg * (1.0 - sg)).astype(BF16)

    return _pcall(
        body, name=name, grid=(nblk,),
        in_specs=[_rows(T, C), _halo_next(T, CONV_HALO, C, 0, S), _rows(T, C, 0), _rows(T, C, 1), _full((CONV_K, C))],
        out_specs=_rows(T, 2 * C), out_shape=jax.ShapeDtypeStruct((S, 2 * C), BF16),
        scratch_shapes=[pltpu.VMEM((T + CONV_HALO, C), F32), pltpu.VMEM((T, C), F32)],
        compiler_params=_cparams("parallel"),
    )(dy, dy, zc, zc, w)


def merge_fwd(y_a, y_b, zg, *, name):
    S, D = y_a.shape
    T = min(STREAM_TILE, S)

    def body(a_r, b_r, ga_r, gb_r, m_o):
        ga, gb = ga_r[...].astype(F32), gb_r[...].astype(F32)
        m_o[...] = (_sig(ga) * a_r[...].astype(F32) + _sig(gb) * b_r[...].astype(F32)).astype(BF16)

    return _pcall(
        body, name=name, grid=(S // T,),
        in_specs=[_rows(T, D), _rows(T, D), _rows(T, D, 0), _rows(T, D, 1)],
        out_specs=_rows(T, D), out_shape=jax.ShapeDtypeStruct((S, D), BF16),
        compiler_params=_cparams("parallel"),
    )(y_a, y_b, zg, zg)


def merge_bwd(d_m, y_a, y_b, zg, *, name):
    S, D = y_a.shape
    T = ROW_TILE

    def body(dm_r, a_r, b_r, ga_r, gb_r, da_o, db_o, dz_o):
        dm = dm_r[...]
        sa, sb = _sig(ga_r[...].astype(F32)), _sig(gb_r[...].astype(F32))
        da_o[...] = (dm * sa).astype(BF16)
        db_o[...] = (dm * sb).astype(BF16)
        dz_o[:, 0:D] = (dm * a_r[...].astype(F32) * sa * (1.0 - sa)).astype(BF16)
        dz_o[:, D:2 * D] = (dm * b_r[...].astype(F32) * sb * (1.0 - sb)).astype(BF16)

    shp = jax.ShapeDtypeStruct((S, D), BF16)
    return _pcall(
        body, name=name, grid=(S // T,),
        in_specs=[_rows(T, D), _rows(T, D), _rows(T, D), _rows(T, D, 0), _rows(T, D, 1)],
        out_specs=(_rows(T, D), _rows(T, D), _rows(T, 2 * D)),
        out_shape=(shp, shp, jax.ShapeDtypeStruct((S, 2 * D), BF16)),
        compiler_params=_cparams("parallel"),
    )(d_m, y_a, y_b, zg, zg)


FFN_RC = 64


def _ffn_chunks(T, F):
    for cc in range(F // LANE):
        for rc in range(T // FFN_RC):
            yield rc * FFN_RC, slice(rc * FFN_RC, (rc + 1) * FFN_RC), slice(cc * LANE, (cc + 1) * LANE)


def _ffn_fill(g_r, hg_r, gbuf, i):
    T = g_r.shape[0]
    live = jnp.where(i > 0, 1.0, 0.0)
    gbuf[0:FFN_HALO, :] = live * hg_r[...].astype(F32)
    gbuf[FFN_HALO:FFN_HALO + T, :] = g_r[...].astype(F32)


def _ffn_gate_chunk(gbuf, w_r, b_r, r0, lanes, n=FFN_RC):
    taps = [gbuf[r0 + FFN_HALO - (FFN_K - 1) + k:r0 + FFN_HALO - (FFN_K - 1) + k + n, lanes]
            for k in range(FFN_K)]
    gp = b_r[:, lanes] + w_r[0:1, lanes] * taps[0]
    for k in range(1, FFN_K):
        gp = gp + w_r[k:k + 1, lanes] * taps[k]
    return gp, taps


def _sum8(v):
    return jnp.sum(v.reshape(v.shape[0] // SUBLANES, SUBLANES, v.shape[1]), axis=0)


def ffn_act_fwd(gu, w, b, *, name):
    S = gu.shape[0]
    F = gu.shape[1] // 2
    T = ROW_TILE // 2

    def body(g_r, u_r, hg_r, w_r, b_r, a_o, gbuf):
        _ffn_fill(g_r, hg_r, gbuf, pl.program_id(0))
        for r0, rows, lanes in _ffn_chunks(T, F):
            gp, _ = _ffn_gate_chunk(gbuf, w_r, b_r, r0, lanes)
            a_o[rows, lanes] = (gp * _sig(gp) * u_r[rows, lanes].astype(F32)).astype(BF16)

    return _pcall(
        body, name=name, grid=(S // T,),
        in_specs=[_rows(T, F, 0), _rows(T, F, 1), _halo_prev(T, FFN_HALO, F, 0), _full((FFN_K, F)), _full((1, F))],
        out_specs=_rows(T, F), out_shape=jax.ShapeDtypeStruct((S, F), BF16),
        scratch_shapes=[pltpu.VMEM((FFN_HALO + T, F), F32)],
        compiler_params=_cparams("parallel"),
    )(gu, gu, gu, w, b)


def ffn_act_bwd(d_a, gu, w, b, *, name):
    S = gu.shape[0]
    F = gu.shape[1] // 2
    T = ROW_TILE // 2
    H = FFN_HALO
    nblk = S // T

    def dgp_of(gp, da, u):
        sg = _sig(gp)
        return da * u * (sg * (1.0 + gp * (1.0 - sg))), sg

    def body(da_r, dan_r, g_r, gp_r, gn_r, u_r, un_r, w_r, b_r, o_o, dw_o, db_o, gbuf, dbuf):
        i = pl.program_id(0)
        gbuf[0:H, :] = jnp.where(i > 0, 1.0, 0.0) * gp_r[...].astype(F32)
        gbuf[H:H + T, :] = g_r[...].astype(F32)
        gbuf[H + T:H + T + H, :] = gn_r[...].astype(F32)

        @pl.when(i == 0)
        def _():
            dw_o[...] = jnp.zeros_like(dw_o)
            db_o[...] = jnp.zeros_like(db_o)

        live_n = jnp.where(i < nblk - 1, 1.0, 0.0)
        sums = None
        for r0, rows, lanes in _ffn_chunks(T, F):
            if r0 == 0:
                sums = [jnp.zeros((SUBLANES, LANE), F32) for _ in range(FFN_K + 1)]
                gp_h, _ = _ffn_gate_chunk(gbuf, w_r, b_r, T, lanes, n=H)
                dgp_h, _ = dgp_of(gp_h, live_n * dan_r[:, lanes].astype(F32), un_r[:, lanes].astype(F32))
                dbuf[T:T + H, lanes] = dgp_h
            gp, taps = _ffn_gate_chunk(gbuf, w_r, b_r, r0, lanes)
            da = da_r[rows, lanes].astype(F32)
            dgp, sg = dgp_of(gp, da, u_r[rows, lanes].astype(F32))
            o_o[rows, F + lanes.start:F + lanes.stop] = (da * gp * sg).astype(BF16)
            dbuf[rows, lanes] = dgp
            sums[FFN_K] = sums[FFN_K] + _sum8(dgp)
            for k in range(FFN_K):
                sums[k] = sums[k] + _sum8(dgp * taps[k])
            if r0 + FFN_RC == T:
                db_o[:, lanes] += sums[FFN_K]
                for k in range(FFN_K):
                    dw_o[k * SUBLANES:(k + 1) * SUBLANES, lanes] += sums[k]
        for r0, rows, lanes in _ffn_chunks(T, F):
            dg = w_r[0:1, lanes] * dbuf[r0 + FFN_K - 1:r0 + FFN_K - 1 + FFN_RC, lanes]
            for k in range(1, FFN_K):
                off = r0 + FFN_K - 1 - k
                dg = dg + w_r[k:k + 1, lanes] * dbuf[off:off + FFN_RC, lanes]
            o_o[rows, lanes] = dg.astype(BF16)

    return _pcall(
        body, name=name, grid=(nblk,),
        in_specs=[_rows(T, F), _halo_next(T, H, F, 0, S), _rows(T, F, 0), _halo_prev(T, H, F, 0),
                  _halo_next(T, H, F, 0, S), _rows(T, F, 1), _halo_next(T, H, F, 1, S),
                  _full((FFN_K, F)), _full((1, F))],
        out_specs=(_rows(T, 2 * F), _full((FFN_K * SUBLANES, F)), _full((SUBLANES, F))),
        out_shape=(jax.ShapeDtypeStruct((S, 2 * F), BF16),
                   jax.ShapeDtypeStruct((FFN_K * SUBLANES, F), F32), jax.ShapeDtypeStruct((SUBLANES, F), F32)),
        scratch_shapes=[pltpu.VMEM((H + T + H, F), F32), pltpu.VMEM((T + H, F), F32)],
        compiler_params=_cparams("arbitrary"),
    )(d_a, d_a, gu, gu, gu, gu, gu, w, b)


def _row_tile(R, target=512):
    if R <= target:
        return R
    for t in range(target, 7, -8):
        if R % t == 0:
            return t
    return R


def sum_slots(land, *, name):
    _, R, C = land.shape
    T = _row_tile(R)

    def body(l_r, o_o):
        acc = l_r[0].astype(F32)
        for q in range(1, N_DEV):
            acc = acc + l_r[q].astype(F32)
        o_o[...] = acc

    return _pcall(
        body, name=name, grid=(R // T,),
        in_specs=[pl.BlockSpec((N_DEV, T, C), lambda i: (0, i, 0))],
        out_specs=_rows(T, C), out_shape=jax.ShapeDtypeStruct((R, C), F32),
        compiler_params=_cparams("parallel"),
    )(land)


def adamw(w, g, m, v, *, name):
    shape = w.shape
    C = shape[-1]
    R = math.prod(shape[:-1])
    w2, g2, m2, v2 = (t.reshape(R, C) for t in (w, g, m, v))
    T = _row_tile(R)
    c1 = 1.0 - ADAM_B1 ** ADAM_STEP
    c2 = 1.0 - ADAM_B2 ** ADAM_STEP

    def body(w_r, g_r, m_r, v_r, d_o, m_o, v_o):
        gv = g_r[...]
        mn = ADAM_B1 * m_r[...] + (1.0 - ADAM_B1) * gv
        vn = ADAM_B2 * v_r[...] + (1.0 - ADAM_B2) * (gv * gv)
        m_o[...] = mn
        v_o[...] = vn
        d_o[...] = -ADAM_LR * ((mn / c1) / (jnp.sqrt(vn / c2) + ADAM_EPS) + ADAM_WD * w_r[...])

    shp = jax.ShapeDtypeStruct((R, C), F32)
    d, mn, vn = _pcall(
        body, name=name, grid=(R // T,), in_specs=[_rows(T, C)] * 4, out_specs=(_rows(T, C),) * 3,
        out_shape=(shp,) * 3, compiler_params=_cparams("parallel"),
    )(w2, g2, m2, v2)
    return d.reshape(shape), mn.reshape(shape), vn.reshape(shape)


def _my_pos():
    return lax.axis_index("x"), lax.axis_index("y"), lax.axis_index("c")


class _Exchange:
    def __init__(self, inputs, out_shapes, sems, start, finish):
        self.inputs, self.out_shapes, self.sems, self.start, self.finish = inputs, out_shapes, sems, start, finish


def gather_exchange(shards):
    n = len(shards)

    def plan(ins, outs, sems):
        send_sems, recv_sems, local_sems = sems
        x, y, c = _my_pos()
        me, sibling = (x, y, c), (x, y, 1 - c)
        chips = [(1 - x, y), (x, 1 - y), (1 - x, 1 - y)]

        def slot(i, p):
            return outs[i].at[4 * p[0] + 2 * p[1] + p[2]]

        def copy(k, i, block, to, src=None):
            return pltpu.make_async_remote_copy(
                src_ref=slot(i, block) if src is None else src, dst_ref=slot(i, block),
                send_sem=send_sems.at[k, i], recv_sem=recv_sems.at[k, i], device_id=to, device_id_type=MESH)

        mine = [pltpu.make_async_copy(ins[i], slot(i, me), local_sems.at[i]) for i in range(n)]
        first = []
        for i in range(n):
            first.append(copy(0, i, me, sibling, src=ins[i]))
            first += [copy(1 + j, i, me, (*chip, c), src=ins[i]) for j, chip in enumerate(chips)]
        return me, sibling, chips, c, copy, mine, first

    def start(ins, outs, sems):
        _, _, _, _, _, mine, first = plan(ins, outs, sems)
        for cp in mine + first:
            cp.start()

    def finish(ins, outs, sems):
        me, sibling, chips, c, copy, mine, first = plan(ins, outs, sems)
        passed = []
        for j, chip in enumerate(chips):
            for i in range(n):
                copy(1 + j, i, (*chip, c), me).wait_recv()
                cp = copy(4 + j, i, (*chip, c), sibling)
                cp.start()
                passed.append(cp)
        for i in range(n):
            copy(0, i, sibling, me).wait_recv()
            for j, chip in enumerate(chips):
                copy(4 + j, i, (*chip, 1 - c), me).wait_recv()
        for cp in first + passed:
            cp.wait_send()
        for cp in mine:
            cp.wait()

    outs = [jax.ShapeDtypeStruct((N_DEV,) + s.shape, s.dtype) for s in shards]
    sems = [pltpu.SemaphoreType.DMA((7, n)), pltpu.SemaphoreType.DMA((7, n)), pltpu.SemaphoreType.DMA((n,))]
    return _Exchange(list(shards), outs, sems, start, finish)


def scatter_exchange(gs):
    n = len(gs)

    def plan(ins, outs, sems):
        send_sems, recv_sems, local_sems = sems
        x, y, c = _my_pos()
        me_id = 4 * x + 2 * y + c
        mine = [pltpu.make_async_copy(ins[i].at[me_id], outs[i].at[me_id], local_sems.at[i]) for i in range(n)]
        sends, recvs = [], []
        for msk in range(1, N_DEV):
            px = 1 - x if msk & 4 else x
            py = 1 - y if msk & 2 else y
            pc = 1 - c if msk & 1 else c
            pid = 4 * px + 2 * py + pc
            for i in range(n):
                sends.append(pltpu.make_async_remote_copy(
                    src_ref=ins[i].at[pid], dst_ref=outs[i].at[me_id],
                    send_sem=send_sems.at[msk - 1, i], recv_sem=recv_sems.at[msk - 1, i],
                    device_id=(px, py, pc), device_id_type=MESH))
                recvs.append(pltpu.make_async_remote_copy(
                    src_ref=ins[i].at[pid], dst_ref=outs[i].at[pid],
                    send_sem=send_sems.at[msk - 1, i], recv_sem=recv_sems.at[msk - 1, i],
                    device_id=(px, py, pc), device_id_type=MESH))
        return mine, sends, recvs

    def start(ins, outs, sems):
        mine, sends, _ = plan(ins, outs, sems)
        for cp in mine + sends:
            cp.start()

    def finish(ins, outs, sems):
        mine, sends, recvs = plan(ins, outs, sems)
        for rv in recvs:
            rv.wait_recv()
        for cp in sends:
            cp.wait_send()
        for cp in mine:
            cp.wait()

    outs = [jax.ShapeDtypeStruct(g.shape, g.dtype) for g in gs]
    sems = [pltpu.SemaphoreType.DMA((7, n)), pltpu.SemaphoreType.DMA((7, n)), pltpu.SemaphoreType.DMA((n,))]
    return _Exchange(list(gs), outs, sems, start, finish)


def run_exchange(ex, *, name):
    HBM = pl.BlockSpec(memory_space=pl.ANY)
    n_in, n_out = len(ex.inputs), len(ex.out_shapes)

    def body(*refs):
        ins, outs, sems = refs[:n_in], refs[n_in:n_in + n_out], refs[n_in + n_out:]
        ex.start(ins, outs, sems)
        ex.finish(ins, outs, sems)

    return list(_pcall(body, name=name, in_specs=[HBM] * n_in, out_specs=tuple([HBM] * n_out),
                       out_shape=tuple(ex.out_shapes), scratch_shapes=ex.sems)(*ex.inputs))


def all_gather(xs, *, name):
    outs = run_exchange(gather_exchange([xs[l] for l in range(xs.shape[0])]), name=name)
    return jnp.stack(outs)


def _slots(g):
    return g.reshape(N_DEV, g.shape[0] // N_DEV, g.shape[1])


def _rope_tables(positions):
    half = ROT_DIM // 2
    inv_freq = ROPE_THETA ** (-jnp.arange(0, ROT_DIM, 2, dtype=F32) / ROT_DIM)
    ang = positions.astype(F32)[:, None] * inv_freq
    cos, sin = jnp.cos(ang), jnp.sin(ang)
    S = positions.shape[0]
    c_t = jnp.concatenate([cos, cos, jnp.ones((S, LANE - ROT_DIM), F32)], axis=1)
    s1_t = jnp.concatenate([-sin, jnp.zeros((S, LANE - half), F32)], axis=1)
    s2_t = jnp.concatenate([jnp.zeros((S, half), F32), sin, jnp.zeros((S, LANE - ROT_DIM), F32)], axis=1)
    return c_t, s1_t, s2_t


def _row(v):
    return v.reshape(1, -1)


def kernel(x, c, positions, w_ada, b_ada, g_norm1, w_in, g_q, g_k, w_attn_proj, w_conv_dw, b_conv_dw, g_conv_ln, b_conv_ln, w_conv_out, w_o, g_norm2, w_ffn_in, w_ffn_dw, b_ffn_dw, w_ffn_down, loss_target, m_w_ada, m_b_ada, m_g_norm1, m_w_in, m_g_q, m_g_k, m_w_attn_proj, m_w_conv_dw, m_b_conv_dw, m_g_conv_ln, m_b_conv_ln, m_w_conv_out, m_w_o, m_g_norm2, m_w_ffn_in, m_w_ffn_dw, m_b_ffn_dw, m_w_ffn_down, v_w_ada, v_b_ada, v_g_norm1, v_w_in, v_g_q, v_g_k, v_w_attn_proj, v_w_conv_dw, v_b_conv_dw, v_g_conv_ln, v_b_conv_ln, v_w_conv_out, v_w_o, v_g_norm2, v_w_ffn_in, v_w_ffn_dw, v_b_ffn_dw, v_w_ffn_down):
    L = w_in.shape[0]
    S, D = x.shape[1], x.shape[2]
    FF = w_ffn_down.shape[1] * N_DEV
    xi, yi, ci = _my_pos()
    me = 4 * xi + 2 * yi + ci
    x0 = x[0]
    tabs = _rope_tables(positions[0])

    c_act = c * _sig(c)
    c_all = all_gather(jnp.pad(c_act, ((0, 7), (0, 0)))[None], name="ag_c")[0][:, 0, :]
    c_all16 = jnp.pad(c_all, ((0, 8), (0, 0)))
    m_part = jnp.stack([mm(c_all16, w_ada[l], "nn", name="mod_mm") for l in range(L)])
    m_all = all_gather(m_part, name="ag_mod")
    mod = lax.dynamic_index_in_dim(m_all, me, axis=2, keepdims=False).reshape(L, 6 * D) + b_ada
    mod = mod.reshape(L, 6, 1, D)

    sh_in = jnp.transpose(w_in, (0, 2, 1)).astype(BF16)
    sh_fi = jnp.transpose(w_ffn_in, (0, 2, 1)).astype(BF16)
    sh_ap = jnp.transpose(w_attn_proj, (0, 2, 1)).astype(BF16)
    sh_co, sh_oo, sh_dn = w_conv_out.astype(BF16), w_o.astype(BF16), w_ffn_down.astype(BF16)

    def rowcat(g):
        return g.reshape(N_DEV * g.shape[1], g.shape[2])

    wt_in, wt_fi, wt_ap, w_co, w_oo, w_dn = ([None] * L for _ in range(6))
    wt_in[0] = rowcat(run_exchange(gather_exchange([sh_in[0]]), name="ag_w0")[0])
    cdw = all_gather(jnp.pad(w_conv_dw, ((0, 0), (0, 1), (0, 0))), name="ag_cdw")
    cdw = jnp.transpose(cdw, (0, 2, 1, 3)).reshape(L, 32, D)[:, :CONV_K]
    fsh = w_ffn_dw.shape[2]
    fpad = -fsh % LANE
    fdw = all_gather(jnp.pad(w_ffn_dw, ((0, 0), (0, 8 - FFN_K), (0, fpad))), name="ag_fdw")
    fdw = jnp.transpose(fdw[:, :, :FFN_K, :fsh], (0, 2, 1, 3)).reshape(L, FFN_K, FF)

    QKV, CW = 3 * ATTN_W, 2 * D
    seg = ((0, QKV), (QKV, CW), (QKV + CW, 2 * D))

    saved = []
    x_prev, delta, gt_prev = x0, None, None
    for l in range(L):
        sh1, sc1, gt1, sh2, sc2, gt2 = (mod[l, i] for i in range(6))
        x_l, h = norm_mod_fwd(x_prev, delta, gt_prev, _row(g_norm1[l]), sc1, sh1, name="norm_fwd")
        nxt = l + 1 < L
        if l == 0:
            zq, got = mm(h, wt_in[0], "nt", b_off=0, b_len=QKV, name="z_mm_ag0",
                         comm=gather_exchange([sh_fi[0], sh_ap[0], sh_co[0], sh_oo[0], sh_dn[0]]))
            wt_fi[0], wt_ap[0], w_co[0], w_oo[0], w_dn[0] = (rowcat(g) for g in got)
        elif nxt:
            zq, got = mm(h, wt_in[l], "nt", b_off=0, b_len=QKV, name="z_mm_ag", comm=gather_exchange([sh_in[l + 1]]))
            wt_in[l + 1] = rowcat(got[0])
        else:
            zq = mm(h, wt_in[l], "nt", b_off=0, b_len=QKV, name="z_mm")
        if l == 0 and nxt:
            zc, got = mm(h, wt_in[0], "nt", b_off=seg[1][0], b_len=seg[1][1], name="zc_mm_ag",
                         comm=gather_exchange([sh_in[1]]))
            wt_in[1] = rowcat(got[0])
        else:
            zc = mm(h, wt_in[l], "nt", b_off=seg[1][0], b_len=seg[1][1], name="z_mm")
        zg = mm(h, wt_in[l], "nt", b_off=seg[2][0], b_len=seg[2][1], out_dtype=BF16, name="zg_mm")
        gq, gk = _row(g_q[l]), _row(g_k[l])
        qn, kn = qk_prep_fwd(zq, gq, gk, tabs, name="qk_prep")
        o_g, lse_g = [], []
        for gi in range(3):
            o_i, l_i = attn_fwd(qn, kn, zq, gi, name="attn_fwd%d" % gi)
            o_g.append(o_i)
            lse_g.append(l_i)
        attn = combine_fwd(o_g, lse_g, name="combine_fwd")
        y_a = mm(attn, wt_ap[l], "nt", out_dtype=BF16, name="ya_mm")
        cw, cb = cdw[l], _row(b_conv_dw[l])
        cg, cbl = _row(g_conv_ln[l]), _row(b_conv_ln[l])
        u2, y_conv = convb_fwd(zc, cw, cb, cg, cbl, name="convb_fwd")
        y_b = mm(u2, w_co[l], "nn", out_dtype=BF16, name="yb_mm")
        merged = merge_fwd(y_a, y_b, zg, name="merge_fwd")
        mo = mm(merged, w_oo[l], "nn", name="mo_mm")
        x_mid, h2 = norm_mod_fwd(x_l, mo, gt1, _row(g_norm2[l]), sc2, sh2, name="norm_fwd")
        if nxt:
            gu, got = mm(h2, wt_fi[l], "nt", out_dtype=BF16, name="gu_mm_ag", comm=gather_exchange([sh_fi[l + 1]]))
            wt_fi[l + 1] = rowcat(got[0])
        else:
            gu = mm(h2, wt_fi[l], "nt", out_dtype=BF16, name="gu_mm")
        fw, fb = fdw[l], _row(b_ffn_dw[l])
        act = ffn_act_fwd(gu, fw, fb, name="ffn_act")
        if nxt:
            ffo, got = mm(act, w_dn[l], "nn", name="ffo_mm_ag",
                          comm=gather_exchange([sh_ap[l + 1], sh_co[l + 1], sh_oo[l + 1], sh_dn[l + 1]]))
            wt_ap[l + 1], w_co[l + 1], w_oo[l + 1], w_dn[l + 1] = (rowcat(g) for g in got)
        else:
            ffo = mm(act, w_dn[l], "nn", name="ffo_mm")
        saved.append(dict(x=x_l, h=h, zq=zq, zc=zc, zg=zg, qn=qn, kn=kn, o=o_g, lse=lse_g, attn=attn, y_a=y_a,
                          u2=u2, y_conv=y_conv, y_b=y_b, merged=merged, mo=mo, x_mid=x_mid, h2=h2, gu=gu, act=act, ffo=ffo))
        x_prev, delta, gt_prev = x_mid, ffo, gt2

    dx, lpart, d_ffo, p_gt2 = loss_head(x_prev, delta, gt_prev, loss_target[0], name="loss_head")
    loss = lax.psum(0.5 / D * jnp.sum(lpart[0]), ("x", "y", "c"))

    land = {k: [None] * L for k in ("in", "fi", "ap", "co", "o", "dn")}
    small_rows = []
    for l in reversed(range(L)):
        sv = saved[l]
        sh1, sc1, gt1, sh2, sc2, gt2 = (mod[l, i] for i in range(6))
        d_act = mm(d_ffo, w_dn[l], "nt", out_dtype=BF16, name="dact_mm")
        g_dn = mm(sv["act"], d_ffo, "tn", out_dtype=BF16, name="dwdn_mm")
        fw, fb = fdw[l], _row(b_ffn_dw[l])
        dgu, p_fw, p_fb = ffn_act_bwd(d_act, sv["gu"], fw, fb, name="ffn_bwd")
        dh2, got = mm(dgu, wt_fi[l], "nn", name="dh2_mm_rs", comm=scatter_exchange([_slots(g_dn)]))
        land["dn"][l] = got[0]
        g_fi = mm(dgu, sv["h2"], "tn", out_dtype=BF16, name="dwfi_mm")
        dx, p_g2, p_sc2, p_sh2, d_mo, p_gt1 = norm_mod_bwd(sv["x_mid"], dh2, _row(g_norm2[l]), sc2, sh2, dx,
                                                           (sv["mo"], gt1), name="norm_bwd")
        d_merged = mm(d_mo, w_oo[l], "nt", name="dmerged_mm")
        g_o = mm(sv["merged"], d_mo, "tn", out_dtype=BF16, name="dwo_mm")
        d_ya, d_yb, dzg = merge_bwd(d_merged, sv["y_a"], sv["y_b"], sv["zg"], name="merge_bwd")
        d_attn = mm(d_ya, wt_ap[l], "nn", name="dattn_mm")
        g_ap = mm(d_ya, sv["attn"], "tn", out_dtype=BF16, name="dwap_mm")
        d_u2 = mm(d_yb, w_co[l], "nt", name="du2_mm")
        g_co = mm(sv["u2"], d_yb, "tn", out_dtype=BF16, name="dwco_mm")
        cw, cb = cdw[l], _row(b_conv_dw[l])
        cg, cbl = _row(g_conv_ln[l]), _row(b_conv_ln[l])
        dy, p_cw, p_cb, p_cg, p_cbl = convb_bwd1(d_u2, sv["y_conv"], sv["zc"], cg, cbl, name="convb_bwd1")
        dzc = convb_bwd2(dy, sv["zc"], cw, name="convb_bwd2")
        dd = combine_bwd(d_attn, sv["o"], sv["lse"], name="combine_bwd")
        do_g, cc_g = dd[:3], dd[3:]
        parts = [attn_bwd(sv["qn"], sv["kn"], sv["zq"], do_g[gi], sv["lse"][gi], cc_g[gi], gi,
                          name="attn_bwd%d" % gi) for gi in range(3)]
        gq, gk = _row(g_q[l]), _row(g_k[l])
        dzq, p_gq, p_gk = attn_bwd_post(sv["zq"], gq, gk, tabs, *[[p[i] for p in parts] for i in range(5)],
                                        name="attn_post")
        g_in_q, got = mm(dzq, sv["h"], "tn", out_dtype=BF16, name="dwin_mm_rs",
                         comm=scatter_exchange([_slots(g_fi), _slots(g_o), _slots(g_ap), _slots(g_co)]))
        land["fi"][l], land["o"][l], land["ap"][l], land["co"][l] = got
        g_in = jnp.concatenate([g_in_q] + [mm(dz_s, sv["h"], "tn", out_dtype=BF16, name="dwin_mm")
                                           for dz_s in (dzc, dzg)], axis=0)
        dh, got = mm_kcat([dzq, dzc, dzg], wt_in[l], name="dh_mm_rs", comm=scatter_exchange([_slots(g_in)]))
        land["in"][l] = got[0]
        if l > 0:
            dx, p_g1, p_sc1, p_sh1, d_ffo_prev, p_gt2_prev = norm_mod_bwd(
                sv["x"], dh, _row(g_norm1[l]), sc1, sh1, dx, (saved[l - 1]["ffo"], mod[l - 1, 5]), name="norm_bwd")
        else:
            dx, p_g1, p_sc1, p_sh1 = norm_mod_bwd(sv["x"], dh, _row(g_norm1[l]), sc1, sh1, dx, name="norm_bwd")

        def row1k(p):
            v = p[0]
            pad = -v.shape[0] % D
            return jnp.pad(v, (0, pad)).reshape(-1, D)

        rows = [row1k(p) for p in (p_sh1, p_sc1, p_gt1, p_sh2, p_sc2, p_gt2, p_g1, p_g2)]
        rows.append(row1k(jnp.concatenate([p_gq, p_gk], axis=1)))
        rows += [row1k(p) for p in (p_cb, p_cg, p_cbl, jnp.sum(p_fb, axis=0, keepdims=True))]
        rows.append(jnp.sum(p_cw.reshape(CONV_K, SUBLANES, D), axis=1))
        rows += [row1k(jnp.sum(p_fw[k * SUBLANES:(k + 1) * SUBLANES], axis=0, keepdims=True))
                 for k in range(FFN_K)]
        blk = jnp.concatenate(rows, axis=0)
        small_rows.append(jnp.pad(blk, ((0, -blk.shape[0] % 8), (0, 0))))
        if l > 0:
            d_ffo, p_gt2 = d_ffo_prev, p_gt2_prev
    small_rows = small_rows[::-1]
    n_small = small_rows[0].shape[0]
    ff_rows = -(-FF // D)

    small = jnp.concatenate(small_rows, axis=0)[None]
    small_all = all_gather(small, name="ag_small")[0]
    small_sum = sum_slots(small_all, name="sum_small").reshape(L, n_small, D)
    small_all = small_all.reshape(N_DEV, L, n_small, D)

    g_b_ada = small_sum[:, 0:6].reshape(L, 6 * D)
    g_g1, g_g2 = small_sum[:, 6], small_sum[:, 7]
    g_gq, g_gk = small_sum[:, 8, 0:LANE], small_sum[:, 8, LANE:2 * LANE]
    g_cb, g_cg, g_cbl = small_sum[:, 9], small_sum[:, 10], small_sum[:, 11]
    r0 = 12
    g_fb = small_sum[:, r0:r0 + ff_rows].reshape(L, -1)[:, :FF]
    r0 += ff_rows
    g_cw_full = small_sum[:, r0:r0 + CONV_K]
    r0 += CONV_K
    g_fw_full = small_sum[:, r0:r0 + FFN_K * ff_rows].reshape(L, FFN_K, -1)[:, :, :FF]
    csh = w_conv_dw.shape[2]
    g_cw = lax.dynamic_slice_in_dim(g_cw_full, me * csh, csh, axis=2)
    g_fw = lax.dynamic_slice_in_dim(g_fw_full, me * fsh, fsh, axis=2)

    ash = w_ada.shape[2]
    dmod_all = small_all[:, :, 0:6].reshape(N_DEV, L, 6 * D)
    dmod_mine = lax.dynamic_slice_in_dim(dmod_all, me * ash, ash, axis=2)
    g_w_ada = jnp.stack([mm(c_all16, jnp.pad(dmod_mine[:, l], ((0, 8), (0, 0))), "tn", name="dwada_mm")
                         for l in range(L)])

    def reduced(key, transposed):
        out = jnp.stack([sum_slots(slots, name="sum_" + key) for slots in land[key]])
        return jnp.transpose(out, (0, 2, 1)) if transposed else out

    g_w_in = reduced("in", True)
    g_w_fi = reduced("fi", True)
    g_w_ap = reduced("ap", True)
    g_w_co = reduced("co", False)
    g_w_o = reduced("o", False)
    g_w_dn = reduced("dn", False)

    grads = [g_w_ada, g_b_ada, g_g1, g_w_in, g_gq, g_gk, g_w_ap, g_cw, g_cb, g_cg, g_cbl, g_w_co, g_w_o, g_g2,
             g_w_fi, g_fw, g_fb, g_w_dn]
    ws = [w_ada, b_ada, g_norm1, w_in, g_q, g_k, w_attn_proj, w_conv_dw, b_conv_dw, g_conv_ln, b_conv_ln,
          w_conv_out, w_o, g_norm2, w_ffn_in, w_ffn_dw, b_ffn_dw, w_ffn_down]
    ms = [m_w_ada, m_b_ada, m_g_norm1, m_w_in, m_g_q, m_g_k, m_w_attn_proj, m_w_conv_dw, m_b_conv_dw, m_g_conv_ln,
          m_b_conv_ln, m_w_conv_out, m_w_o, m_g_norm2, m_w_ffn_in, m_w_ffn_dw, m_b_ffn_dw, m_w_ffn_down]
    vs = [v_w_ada, v_b_ada, v_g_norm1, v_w_in, v_g_q, v_g_k, v_w_attn_proj, v_w_conv_dw, v_b_conv_dw, v_g_conv_ln,
          v_b_conv_ln, v_w_conv_out, v_w_o, v_g_norm2, v_w_ffn_in, v_w_ffn_dw, v_b_ffn_dw, v_w_ffn_down]
    deltas, new_m, new_v = [], [], []
    for w_i, g_i, m_i, v_i in zip(ws, grads, ms, vs):
        d_i, mn_i, vn_i = adamw(w_i, g_i, m_i, v_i, name="adamw")
        deltas.append(d_i)
        new_m.append(mn_i)
        new_v.append(vn_i)
    return (loss, dx[None], *grads, *deltas, *new_m, *new_v)
```
